```python
import jax, jax.numpy as jnp
from jax import lax
import numpy as np

D_MODEL = 1024
BATCH = 1
SEQ = 16384
DEPTH = 2
DEC_BATCH = 32
DEC_SEQ = 4
PAST_LEN = 16384
PAGE_SIZE = 128

N_EVEN = (DEPTH + 1) // 2
N_ODD = DEPTH // 2
ALPHA = (2.0 * DEPTH) ** 0.25
BETA = (8.0 * DEPTH) ** -0.25
LN_EPS = 1e-5

A_WIDTH = D_MODEL // 2
CONV_W = 31

B_WIDTH = D_MODEL // 2
B_HEAD = 64
B_HEADS = B_WIDTH // B_HEAD
DECAY_LORA = 32
A_LORA = 32
LNX_EPS = 64e-5

C_HEAD = 64
C_HEADS = D_MODEL // 128
C_GROUPS = ((128, 1), (512, 4), (2048, 16))
N_G = len(C_GROUPS)
C_GW = C_HEADS * C_HEAD

EVEN_SHIFT = 3 * B_WIDTH + DECAY_LORA + A_LORA
EVEN_COLS = EVEN_SHIFT + 3 * A_WIDTH + B_WIDTH
EVEN_SHIFT_SPLITS = [B_WIDTH, 2 * B_WIDTH, 3 * B_WIDTH, 3 * B_WIDTH + DECAY_LORA]
EVEN_REST_SPLITS = [A_WIDTH, 2 * A_WIDTH, 3 * A_WIDTH]
ODD_COLS = 3 * N_G * C_GW + C_GW
ODD_SPLITS = [N_G * C_GW, 2 * N_G * C_GW, 3 * N_G * C_GW]

kernel_name = 'hybrid_conformer_rwkv7_dilated_attn_decoder_step'


def layer_norm(x, g, b, eps=LN_EPS):
    xf = x.astype(jnp.float32)
    mu = xf.mean(-1, keepdims=True)
    var = jnp.square(xf - mu).mean(-1, keepdims=True)
    return ((xf - mu) * lax.rsqrt(var + eps) * g + b).astype(x.dtype)


def wkv_step(S, inp):
    r, w, k, v, kk, a = inp
    sa = jnp.einsum('bhij,bhj->bhi', S, -kk)
    S = S * w[:, :, None, :] + sa[..., :, None] * (kk * a)[..., None, :] + v[..., :, None] * k[..., None, :]
    return S, jnp.einsum('bhij,bhj->bhi', S, r)


def even_mixer(x, conv_buf, shift_prev, wkv, w_in, conv_w, conv_b, cln_g, cln_b, mu, w0, w2, a0, a2,
               k_k, k_a, r_k, lnx_g, lnx_b, w_out):
    Bn, T, _ = x.shape
    f32 = jnp.float32
    p = x @ w_in
    p_sh, p_rest = p[..., :EVEN_SHIFT], p[..., EVEN_SHIFT:]
    p0 = shift_prev.astype(x.dtype) @ w_in[:, :EVEN_SHIFT]
    p_prev = jnp.concatenate([p0[:, None], p_sh[:, :-1]], axis=1)
    p_sh = p_sh + (p_prev - p_sh) * mu
    r, k, v, wd, ad = jnp.split(p_sh, EVEN_SHIFT_SPLITS, axis=-1)
    a_val, a_glu, a_gate, b_gate = jnp.split(p_rest, EVEN_REST_SPLITS, axis=-1)

    u = a_val * jax.nn.sigmoid(a_glu)
    u_ext = jnp.concatenate([conv_buf.astype(u.dtype), u], axis=1)
    c = lax.conv_general_dilated(u_ext, conv_w[:, None, :].astype(u.dtype), window_strides=(1,),
                                 padding='VALID', dimension_numbers=('NWC', 'WIO', 'NWC'),
                                 feature_group_count=A_WIDTH) + conv_b
    new_conv = u_ext[:, -(CONV_W - 1):]
    y_a = jax.nn.silu(layer_norm(c, cln_g, cln_b)) * jax.nn.silu(a_gate)

    heads = lambda z: z.astype(f32).reshape(Bn, T, B_HEADS, B_HEAD)
    w_log = -jax.nn.softplus(-(w0 + jnp.tanh(wd) @ w2).astype(f32)) - 0.5
    decay = jnp.exp(-jnp.exp(w_log))
    a = jax.nn.sigmoid((a0 + ad @ a2).astype(f32))
    kk = heads(k * k_k)
    kk = kk / jnp.maximum(jnp.sqrt(jnp.sum(kk * kk, -1, keepdims=True)), 1e-12)
    k_mod = k.astype(f32) * (1.0 + (a - 1.0) * k_a)
    rh, wh, kh, vh, ah = heads(r), heads(decay), heads(k_mod), heads(v), heads(a)
    xs = tuple(jnp.moveaxis(z, 1, 0) for z in (rh, wh, kh, vh, kk, ah))
    S_final, o = lax.scan(wkv_step, wkv.astype(f32), xs)
    o = jnp.moveaxis(o, 0, 1)
    om = o.mean(-1, keepdims=True)
    ov = jnp.square(o - om).mean(-1, keepdims=True)
    o = ((o - om) * lax.rsqrt(ov + LNX_EPS)).reshape(Bn, T, B_WIDTH) * lnx_g + lnx_b
    bonus = (jnp.sum(rh * kh * r_k, -1, keepdims=True) * vh).reshape(Bn, T, B_WIDTH)
    y_b = ((o + bonus) * jax.nn.silu(b_gate.astype(f32))).astype(x.dtype)

    y = jnp.concatenate([y_a, y_b], axis=-1) @ w_out
    return y, new_conv, x[:, -1], S_final


def band_dilated_attention(q, k, v, window, dil):
    Bn, T, H, hd = q.shape
    span = window // dil
    L = T // dil
    nb = -(-L // span)
    Lp = nb * span

    def classes(z):
        z = z.reshape(Bn, L, dil, H, hd).transpose(0, 2, 1, 3, 4).reshape(Bn * dil, L, H, hd)
        z = jnp.pad(z, ((0, 0), (0, Lp - L), (0, 0), (0, 0)))
        return z.reshape(Bn * dil, nb, span, H, hd)

    def band(z):
        prev = jnp.pad(z[:, :-1], ((0, 0), (1, 0), (0, 0), (0, 0), (0, 0)))
        return jnp.concatenate([prev, z], axis=2)

    qb = classes(q)
    kb, vb = band(classes(k)), band(classes(v))
    s = jnp.einsum('nbqhd,nbkhd->nbhqk', qb, kb).astype(jnp.float32) * (hd ** -0.5)
    qi = np.arange(span)[:, None]
    ki = np.arange(2 * span)[None, :]
    in_band = (ki >= qi) & (ki <= qi + span)
    mask = np.where(np.arange(nb)[:, None, None] == 0, in_band & (ki >= span), in_band)
    s = jnp.where(mask[None, :, None], s, -jnp.inf)
    lse = jax.nn.logsumexp(s, axis=-1)
    pr = jnp.exp(s - lse[..., None])
    o = jnp.einsum('nbhqk,nbkhd->nbqhd', pr, vb.astype(jnp.float32))
    o = o.reshape(Bn, dil, Lp, H, hd)[:, :, :L].transpose(0, 2, 1, 3, 4).reshape(Bn, T, H, hd)
    lse = lse.transpose(0, 1, 3, 2).reshape(Bn, dil, Lp, H)[:, :, :L].transpose(0, 2, 1, 3).reshape(Bn, T, H)
    return o, lse


def gathered_dilated_attention(q, kv_ext, window, dil, n_buf):
    S, hd = q.shape[1], q.shape[-1]
    span = window // dil
    idx = n_buf + np.arange(S)[:, None] - dil * np.arange(span + 1)[None, :]
    valid = idx >= 0
    g = kv_ext[:, np.maximum(idx, 0)]
    s = jnp.einsum('bshd,bsjhd->bshj', q, g[:, :, :, 0]).astype(jnp.float32) * (hd ** -0.5)
    s = jnp.where(valid[None, :, None, :], s, -jnp.inf)
    lse = jax.nn.logsumexp(s, axis=-1)
    pr = jnp.exp(s - lse[..., None])
    o = jnp.einsum('bshj,bsjhd->bshd', pr, g[:, :, :, 1].astype(jnp.float32))
    return o, lse


def odd_mixer(x, bufs, w_in, w_out):
    Bn, T, _ = x.shape
    p = x @ w_in
    q, k, v, gate = jnp.split(p, ODD_SPLITS, axis=-1)
    q = q.reshape(Bn, T, N_G, C_HEADS, C_HEAD)
    k = k.reshape(Bn, T, N_G, C_HEADS, C_HEAD)
    v = v.reshape(Bn, T, N_G, C_HEADS, C_HEAD)
    outs, lses, new = [], [], []
    for gi, (win, dil) in enumerate(C_GROUPS):
        kv = jnp.stack([k[:, :, gi], v[:, :, gi]], axis=2)
        if bufs is None:
            o, lse = band_dilated_attention(q[:, :, gi], k[:, :, gi], v[:, :, gi], win, dil)
            new.append(kv[:, -min(win, T):])
        else:
            buf = bufs[gi].astype(kv.dtype)
            n_buf = buf.shape[1]
            ext = jnp.concatenate([buf, kv], axis=1)
            o, lse = gathered_dilated_attention(q[:, :, gi], ext, win, dil, n_buf)
            new.append(ext[:, -n_buf:])
        outs.append(o)
        lses.append(lse)
    wts = jax.nn.softmax(jnp.stack(lses, 0), axis=0)
    o = jnp.einsum('gbth,gbthd->bthd', wts, jnp.stack(outs, 0))
    y = (o.reshape(Bn, T, C_GW) * jax.nn.silu(gate.astype(jnp.float32))).astype(x.dtype) @ w_out
    return y, new


def setup_inputs(seed: int = 0) -> dict:
    key = jax.random.key(seed)
    ks = iter(jax.random.split(key, 40))
    nrm = lambda shape, scale=1.0: scale * jax.random.normal(next(ks), shape, jnp.float32)
    lb = [min(w, PAST_LEN) for w, _ in C_GROUPS]
    return {
        'x_prompt': nrm((BATCH, SEQ, D_MODEL)),
        'x_sample': nrm((DEC_BATCH, DEC_SEQ, D_MODEL)),
        'state_conv': nrm((N_EVEN, DEC_BATCH, CONV_W - 1, A_WIDTH), 0.5),
        'state_shift': nrm((N_EVEN, DEC_BATCH, D_MODEL)),
        'state_wkv': nrm((N_EVEN, DEC_BATCH, B_HEADS, B_HEAD, B_HEAD), 0.3),
        'cache_kv_w128': nrm((N_ODD, DEC_BATCH, lb[0], 2, C_HEADS, C_HEAD)),
        'cache_kv_w512': nrm((N_ODD, DEC_BATCH, lb[1], 2, C_HEADS, C_HEAD)),
        'cache_kv_w2048': nrm((N_ODD, DEC_BATCH, lb[2], 2, C_HEADS, C_HEAD)),
        'w_in_even': nrm((N_EVEN, D_MODEL, EVEN_COLS), D_MODEL ** -0.5),
        'conv_w': nrm((N_EVEN, CONV_W, A_WIDTH), CONV_W ** -0.5),
        'conv_b': nrm((N_EVEN, A_WIDTH), 0.02),
        'conv_ln_g': 1.0 + nrm((N_EVEN, A_WIDTH), 0.05),
        'conv_ln_b': nrm((N_EVEN, A_WIDTH), 0.02),
        'mu_shift': jax.random.uniform(next(ks), (N_EVEN, EVEN_SHIFT), jnp.float32),
        'w0': jnp.linspace(-6.0, -1.0, B_WIDTH, dtype=jnp.float32) + nrm((N_EVEN, B_WIDTH), 0.1),
        'w2': nrm((N_EVEN, DECAY_LORA, B_WIDTH), 0.1 * DECAY_LORA ** -0.5),
        'a0': nrm((N_EVEN, B_WIDTH), 0.1),
        'a2': nrm((N_EVEN, A_LORA, B_WIDTH), 0.5 * A_LORA ** -0.5),
        'k_k': 0.85 + nrm((N_EVEN, B_WIDTH), 0.05),
        'k_a': 1.0 + nrm((N_EVEN, B_WIDTH), 0.05),
        'r_k': nrm((N_EVEN, B_HEADS, B_HEAD), 0.1),
        'lnx_g': 1.0 + nrm((N_EVEN, B_WIDTH), 0.05),
        'lnx_b': nrm((N_EVEN, B_WIDTH), 0.02),
        'w_out_even': nrm((N_EVEN, A_WIDTH + B_WIDTH, D_MODEL), BETA * (A_WIDTH + B_WIDTH) ** -0.5),
        'w_in_odd': nrm((N_ODD, D_MODEL, ODD_COLS), D_MODEL ** -0.5),
        'w_out_odd': nrm((N_ODD, C_GW, D_MODEL), BETA * C_GW ** -0.5),
        'ln_g': 1.0 + nrm((DEPTH, D_MODEL), 0.05),
        'ln_b': nrm((DEPTH, D_MODEL), 0.02),
    }


def reference(x_prompt, x_sample, state_conv, state_shift, state_wkv, cache_kv_w128, cache_kv_w512,
              cache_kv_w2048, w_in_even, conv_w, conv_b, conv_ln_g, conv_ln_b, mu_shift, w0, w2, a0, a2,
              k_k, k_a, r_k, lnx_g, lnx_b, w_out_even, w_in_odd, w_out_odd, ln_g, ln_b):
    xp, xs = x_prompt, x_sample
    nbp = xp.shape[0]
    caches = (cache_kv_w128, cache_kv_w512, cache_kv_w2048)
    conv_p, conv_s, shift_p, shift_s, wkv_p, wkv_s = [], [], [], [], [], []
    kv_p = [[] for _ in C_GROUPS]
    kv_s = [[] for _ in C_GROUPS]
    for l in range(DEPTH):
        i = l // 2
        if l % 2 == 0:
            ev = (w_in_even[i], conv_w[i], conv_b[i], conv_ln_g[i], conv_ln_b[i], mu_shift[i], w0[i], w2[i],
                  a0[i], a2[i], k_k[i], k_a[i], r_k[i], lnx_g[i], lnx_b[i], w_out_even[i])
            yp, cp, sp, wp = even_mixer(xp, jnp.zeros((nbp, CONV_W - 1, A_WIDTH), xp.dtype),
                                        jnp.zeros((nbp, D_MODEL), xp.dtype),
                                        jnp.zeros((nbp, B_HEADS, B_HEAD, B_HEAD), jnp.float32), *ev)
            ys, cs, ss, ws = even_mixer(xs, state_conv[i], state_shift[i], state_wkv[i], *ev)
            conv_p.append(cp); conv_s.append(cs)
            shift_p.append(sp); shift_s.append(ss)
            wkv_p.append(wp); wkv_s.append(ws)
        else:
            yp, newp = odd_mixer(xp, None, w_in_odd[i], w_out_odd[i])
            ys, news = odd_mixer(xs, tuple(c[i] for c in caches), w_in_odd[i], w_out_odd[i])
            for gi in range(N_G):
                kv_p[gi].append(newp[gi]); kv_s[gi].append(news[gi])
        xp = layer_norm(ALPHA * xp + yp, ln_g[l], ln_b[l])
        xs = layer_norm(ALPHA * xs + ys, ln_g[l], ln_b[l])
    return (xp, xs,
            jnp.stack(conv_p), jnp.stack(conv_s),
            jnp.stack(shift_p), jnp.stack(shift_s),
            jnp.stack(wkv_p), jnp.stack(wkv_s),
            jnp.stack(kv_p[0]), jnp.stack(kv_s[0]),
            jnp.stack(kv_p[1]), jnp.stack(kv_s[1]),
            jnp.stack(kv_p[2]), jnp.stack(kv_s[2]))
```

```python
import functools
import math

import jax
import jax.numpy as jnp
from jax import lax
from jax.experimental import pallas as pl
from jax.experimental.pallas import tpu as pltpu

F32 = jnp.float32
BF16 = jnp.bfloat16
HIGHEST = lax.Precision.HIGHEST

LN_EPS = 1e-5
LNX_EPS = 64e-5
HEAD = 64
LORA_PAD = 128
CONV_W = 31
CONV_HIST = 32
C_GROUPS = ((128, 1), (512, 4), (2048, 16))
WKV_CHUNK = 64
VMEM_LIMIT = 56 * 1024 * 1024


def _cparams(*sem):
    return pltpu.CompilerParams(dimension_semantics=sem or None, vmem_limit_bytes=VMEM_LIMIT)


def _dot(a, b):
    return jnp.dot(a.astype(BF16), b.astype(BF16), preferred_element_type=F32)


def _dot_nt(a, b):
    return lax.dot_general(a.astype(BF16), b.astype(BF16), (((1,), (1,)), ((), ())), preferred_element_type=F32)


def _dot_tn(a, b):
    return lax.dot_general(a.astype(BF16), b.astype(BF16), (((0,), (0,)), ((), ())), preferred_element_type=F32)


def _dot_hi(a, b):
    return jnp.dot(a, b, precision=HIGHEST, preferred_element_type=F32)


def _split_dot(a, ones_bf16):
    ah = a.astype(BF16)
    al = (a - ah.astype(F32)).astype(BF16)
    return (jnp.dot(ah, ones_bf16, preferred_element_type=F32) +
            jnp.dot(al, ones_bf16, preferred_element_type=F32))


def _head_ones(n):
    r = lax.broadcasted_iota(jnp.int32, (n, n), 0) >> 6
    c = lax.broadcasted_iota(jnp.int32, (n, n), 1) >> 6
    return jnp.where(r == c, 1.0, 0.0).astype(BF16)


def _sigmoid(x):
    return 1.0 / (1.0 + jnp.exp(-x))


def _silu(x):
    return x * _sigmoid(x)


def _softplus(y):
    return jnp.maximum(y, 0.0) + jnp.log(1.0 + jnp.exp(-jnp.abs(y)))


def _layer_norm(x, g, b, eps):
    mu = jnp.mean(x, axis=-1, keepdims=True)
    xc = x - mu
    var = jnp.mean(xc * xc, axis=-1, keepdims=True)
    return xc * lax.rsqrt(var + eps) * g + b


def _even_mix(sh, b_width, prm, hsum):
    w0, w2p, a0, a2p, k_k, k_a, r_k = prm
    r = sh[:, 0:b_width]
    k = sh[:, b_width:2 * b_width]
    v = sh[:, 2 * b_width:3 * b_width]
    lora = sh[:, 3 * b_width:3 * b_width + LORA_PAD]
    w_log = -_softplus(-(w0 + _dot(jnp.tanh(lora), w2p))) - 0.5
    lw = -jnp.exp(w_log)
    a = _sigmoid(a0 + _dot(lora, a2p))
    kk = k * k_k
    nrm = jnp.sqrt(_split_dot(kk * kk, hsum))
    kkn = kk / jnp.maximum(nrm, 1e-12)
    km = k * (1.0 + (a - 1.0) * k_a)
    bonus = _split_dot(r * km * r_k, hsum) * v
    return r, lw, km, v, kkn, kkn * a, bonus


def _conv_ln_gate(u_ref, row0, n, cw_ref, cb, clg, clb, a_gate):
    acc = jnp.zeros((n, cb.shape[-1]), F32) + cb
    for j in range(CONV_W):
        acc = acc + cw_ref[j:j + 1, :] * u_ref[pl.ds(row0 + (CONV_HIST - (CONV_W - 1)) + j, n), :]
    return _silu(_layer_norm(acc, clg, clb, LN_EPS)) * _silu(a_gate)


def _even_front_prompt_body(x_ref, w_ref, mu_ref, cw_ref, cb_ref, clg_ref, clb_ref, w0_ref, w2_ref, a0_ref, a2_ref,
                            kk_ref, ka_ref, rk_ref,
                            ya_ref, bg_ref, r_ref, lw_ref, km_ref, v_ref, kkn_ref, ab_ref, bon_ref, conv_ref,
                            p_scr, u_scr, *, tm, bw, sub):
    i = pl.program_id(0)
    shw = 3 * bw + LORA_PAD

    @pl.when(i == 0)
    def _():
        p_scr[0:8, :] = jnp.zeros((8, shw), F32)
        u_scr[0:CONV_HIST, :] = jnp.zeros((CONV_HIST, bw), F32)

    p = jnp.dot(x_ref[...].astype(BF16), w_ref[...], preferred_element_type=F32)
    cur = p[:, :shw]
    p_scr[8:8 + tm, :] = cur
    prev = p_scr[pl.ds(7, tm), :]
    sh = cur + (prev - cur) * mu_ref[...]
    p_scr[7:8, :] = cur[tm - 1:tm, :]

    hsum = _head_ones(bw)
    prm = (w0_ref[...], w2_ref[...], a0_ref[...], a2_ref[...], kk_ref[...], ka_ref[...], rk_ref[...])
    r, lw, km, v, kkn, ab, bonus = _even_mix(sh, bw, prm, hsum)
    r_ref[...] = r
    lw_ref[...] = lw
    km_ref[...] = km
    v_ref[...] = v
    kkn_ref[...] = kkn
    ab_ref[...] = ab
    bon_ref[...] = bonus

    a_val = p[:, shw:shw + bw]
    a_glu = p[:, shw + bw:shw + 2 * bw]
    a_gate = p[:, shw + 2 * bw:shw + 3 * bw]
    b_gate = p[:, shw + 3 * bw:shw + 4 * bw]
    bg_ref[...] = _silu(b_gate)
    u_scr[CONV_HIST:CONV_HIST + tm, :] = a_val * _sigmoid(a_glu)
    cb, clg, clb = cb_ref[...], clg_ref[...], clb_ref[...]
    for s in range(tm // sub):
        ya_ref[s * sub:(s + 1) * sub, :] = _conv_ln_gate(u_scr, s * sub, sub, cw_ref, cb, clg, clb,
                                                         a_gate[s * sub:(s + 1) * sub, :])
    tail = u_scr[tm:tm + CONV_HIST, :]
    conv_ref[...] = tail
    u_scr[0:CONV_HIST, :] = tail


def _even_front_prompt(x, w_re, mu_re, cw, small, *, tm=256):
    t, d = x.shape
    bw = small[0].shape[-1]
    shw = 3 * bw + LORA_PAD
    ncol = w_re.shape[1]
    row = lambda i: (i, 0)
    fix = lambda i: (0, 0)
    vec = pl.BlockSpec((1, bw), fix)
    in_specs = [pl.BlockSpec((tm, d), row), pl.BlockSpec((d, ncol), fix), pl.BlockSpec((1, shw), fix),
                pl.BlockSpec((CONV_HIST, bw), fix), vec, vec, vec, vec, pl.BlockSpec((LORA_PAD, bw), fix), vec,
                pl.BlockSpec((LORA_PAD, bw), fix), vec, vec, vec]
    out = jax.ShapeDtypeStruct((t, bw), F32)
    outs = [out] * 9 + [jax.ShapeDtypeStruct((CONV_HIST, bw), F32)]
    out_specs = [pl.BlockSpec((tm, bw), row)] * 9 + [pl.BlockSpec((CONV_HIST, bw), fix)]
    body = functools.partial(_even_front_prompt_body, tm=tm, bw=bw, sub=32)
    return pl.pallas_call(
        body, out_shape=outs, grid=(t // tm,), in_specs=in_specs, out_specs=out_specs,
        scratch_shapes=[pltpu.VMEM((8 + tm, shw), F32), pltpu.VMEM((CONV_HIST + tm, bw), F32)],
        compiler_params=_cparams("arbitrary"), name="even_front_prompt")(x, w_re, mu_re, cw, *small)


def _even_front_sample_body(x_ref, hist_ref, w_ref, mu_ref, cw_ref, cb_ref, clg_ref, clb_ref, w0_ref, w2_ref, a0_ref,
                            a2_ref, kk_ref, ka_ref, rk_ref,
                            ya_ref, bg_ref, r_ref, lw_ref, km_ref, v_ref, kkn_ref, ab_ref, bon_ref, conv_ref,
                            u_scr, *, nb, ns, bw):
    n = nb * ns
    shw = 3 * bw + LORA_PAD
    nh = CONV_W - 1
    p = jnp.dot(x_ref[...].astype(BF16), w_ref[...], preferred_element_type=F32)
    cur = p[nb:, :shw]
    prev = p[:n, :shw]
    sh = cur + (prev - cur) * mu_ref[...]
    hsum = _head_ones(bw)
    prm = (w0_ref[...], w2_ref[...], a0_ref[...], a2_ref[...], kk_ref[...], ka_ref[...], rk_ref[...])
    r, lw, km, v, kkn, ab, bonus = _even_mix(sh, bw, prm, hsum)
    r_ref[...] = r
    lw_ref[...] = lw
    km_ref[...] = km
    v_ref[...] = v
    kkn_ref[...] = kkn
    ab_ref[...] = ab
    bon_ref[...] = bonus
    rest = p[nb:, shw:]
    a_val, a_glu = rest[:, 0:bw], rest[:, bw:2 * bw]
    a_gate, b_gate = rest[:, 2 * bw:3 * bw], rest[:, 3 * bw:4 * bw]
    bg_ref[...] = _silu(b_gate)
    u_scr[0:nh * nb, :] = hist_ref[...]
    u_scr[nh * nb:(nh + ns) * nb, :] = a_val * _sigmoid(a_glu)
    cb, clg, clb = cb_ref[...], clg_ref[...], clb_ref[...]
    for s in range(ns):
        acc = jnp.zeros((nb, bw), F32) + cb
        for j in range(CONV_W):
            acc = acc + cw_ref[j:j + 1, :] * u_scr[(s + j) * nb:(s + j + 1) * nb, :]
        ya_ref[s * nb:(s + 1) * nb, :] = (_silu(_layer_norm(acc, clg, clb, LN_EPS)) *
                                           _silu(a_gate[s * nb:(s + 1) * nb, :]))
    conv_ref[...] = u_scr[ns * nb:(ns + nh) * nb, :]


def _even_front_sample(x_in, hist, w_re, mu_re, cw, small, *, nb, ns):
    bw = small[0].shape[-1]
    n = nb * ns
    nh = CONV_W - 1
    out = jax.ShapeDtypeStruct((n, bw), F32)
    outs = [out] * 9 + [jax.ShapeDtypeStruct((nh * nb, bw), F32)]
    body = functools.partial(_even_front_sample_body, nb=nb, ns=ns, bw=bw)
    return pl.pallas_call(
        body, out_shape=outs, scratch_shapes=[pltpu.VMEM(((nh + ns) * nb, bw), F32)],
        compiler_params=_cparams(), name="even_front_sample")(x_in, hist, w_re, mu_re, cw, *small)


def _wkv_block_terms(r, lw, km, v, kkn, ab, blk, n_double):
    n, width = r.shape
    ri = lax.broadcasted_iota(jnp.int32, (n, n), 0)
    ci = lax.broadcasted_iota(jnp.int32, (n, n), 1)
    sh = blk.bit_length() - 1
    same = (ri >> sh) == (ci >> sh) if blk != n else None
    incl = ri >= ci
    strict = ri > ci
    if same is not None:
        incl = incl & same
        strict = strict & same
    cum = _dot_hi(jnp.where(incl, 1.0, 0.0), lw)
    if same is None:
        tot = cum[n - 1:n, :]
    else:
        tot = _dot_hi(jnp.where(same, 1.0, 0.0), lw)
    up = jnp.exp(cum)
    dn = jnp.exp(-cum)
    at = -kkn * jnp.exp(cum - lw)
    bt = ab * dn
    kt = km * dn
    rt = r * up
    rem = jnp.exp(tot - cum)
    bc = ab * rem
    kc = km * rem
    wc = jnp.exp(tot)
    eye = jnp.where(ri == ci, 1.0, 0.0)
    heads = []
    for h in range(width // HEAD):
        sl = slice(h * HEAD, (h + 1) * HEAD)
        a_h, b_h, k_h, r_h, v_h = at[:, sl], bt[:, sl], kt[:, sl], rt[:, sl], v[:, sl]
        g = _dot_nt(jnp.concatenate([a_h, r_h], axis=0), jnp.concatenate([b_h, k_h], axis=0))
        l_ab = jnp.where(strict, g[:n, :n], 0.0)
        l_ak = jnp.where(strict, g[:n, n:], 0.0)
        l_rb = jnp.where(incl, g[n:, :n], 0.0)
        l_rk = jnp.where(incl, g[n:, n:], 0.0)
        x = l_ab
        tinv = eye + l_ab
        for _ in range(n_double):
            x = _dot_hi(x, x)
            tinv = tinv + _dot_hi(tinv, x)
        ahat = _dot_hi(tinv, a_h)
        uhat = _dot_hi(tinv, _dot(l_ak, v_h))
        qhat = r_h + _dot(l_rb, ahat)
        ohat = _dot(l_rb, uhat) + _dot(l_rk, v_h)
        heads.append((ahat, uhat, qhat, ohat))
    return heads, bc, kc, wc


def _wkv_prompt_body(r_ref, lw_ref, km_ref, v_ref, kkn_ref, ab_ref, s0_ref, o_ref, sout_ref, s_scr, *, c):
    i = pl.program_id(0)

    @pl.when(i == 0)
    def _():
        s_scr[...] = s0_ref[...]

    v = v_ref[...]
    heads, bc, kc, wc = _wkv_block_terms(r_ref[...], lw_ref[...], km_ref[...], v, kkn_ref[...], ab_ref[...], c, 5)
    for h, (ahat, uhat, qhat, ohat) in enumerate(heads):
        sl = slice(h * HEAD, (h + 1) * HEAD)
        s = s_scr[h]
        mm = _dot_nt(jnp.concatenate([ahat, qhat], axis=0), s)
        u = mm[:c] + uhat
        o_ref[:, sl] = mm[c:] + ohat
        s_scr[h] = s * wc[:, sl] + _dot_tn(u, bc[:, sl]) + _dot_tn(v[:, sl], kc[:, sl])

    @pl.when(i == pl.num_programs(0) - 1)
    def _():
        sout_ref[...] = s_scr[...]


def _wkv_prompt(r, lw, km, v, kkn, ab, s0):
    t, bw = r.shape
    c = WKV_CHUNK
    nh = bw // HEAD
    row = pl.BlockSpec((c, bw), lambda i: (i, 0))
    st = pl.BlockSpec((nh, HEAD, HEAD), lambda i: (0, 0, 0))
    return pl.pallas_call(
        functools.partial(_wkv_prompt_body, c=c),
        out_shape=[jax.ShapeDtypeStruct((t, bw), F32), jax.ShapeDtypeStruct((nh, HEAD, HEAD), F32)],
        grid=(t // c,), in_specs=[row] * 6 + [st], out_specs=[row, st],
        scratch_shapes=[pltpu.VMEM((nh, HEAD, HEAD), F32)],
        compiler_params=_cparams("arbitrary"), name="wkv_prompt")(r, lw, km, v, kkn, ab, s0)


def _wkv_sample_body(r_ref, lw_ref, km_ref, v_ref, kkn_ref, ab_ref, s0_ref, o_ref, sout_ref,
                     ah_scr, uh_scr, qh_scr, oh_scr, bc_scr, kc_scr, wc_scr, *, nb, blk):
    v = v_ref[...]
    heads, bc, kc, wc = _wkv_block_terms(r_ref[...], lw_ref[...], km_ref[...], v, kkn_ref[...], ab_ref[...], blk, 1)
    bc_scr[...] = bc
    kc_scr[...] = kc
    wc_scr[...] = wc
    for h, (ahat, uhat, qhat, ohat) in enumerate(heads):
        sl = slice(h * HEAD, (h + 1) * HEAD)
        ah_scr[:, sl] = ahat
        uh_scr[:, sl] = uhat
        qh_scr[:, sl] = qhat
        oh_scr[:, sl] = ohat
    nh = len(heads)

    def per_seq(b, carry):
        rows = pl.ds(pl.multiple_of(b * blk, blk), blk)
        for h in range(nh):
            sl = slice(h * HEAD, (h + 1) * HEAD)
            s = s0_ref[b, h]
            mm = _dot_nt(jnp.concatenate([ah_scr[rows, sl], qh_scr[rows, sl]], axis=0), s)
            u = mm[:blk] + uh_scr[rows, sl]
            o_ref[rows, sl] = mm[blk:] + oh_scr[rows, sl]
            wrow = wc_scr[rows, sl][0:1, :]
            sout_ref[b, h] = s * wrow + _dot_tn(u, bc_scr[rows, sl]) + _dot_tn(v_ref[rows, sl], kc_scr[rows, sl])
        return carry

    lax.fori_loop(0, nb, per_seq, 0)


def _wkv_sample(r, lw, km, v, kkn, ab, s0, *, blk):
    n, bw = r.shape
    nb = n // blk
    scr = pltpu.VMEM((n, bw), F32)
    return pl.pallas_call(
        functools.partial(_wkv_sample_body, nb=nb, blk=blk),
        out_shape=[jax.ShapeDtypeStruct((n, bw), F32), jax.ShapeDtypeStruct(s0.shape, F32)],
        scratch_shapes=[scr] * 7, compiler_params=_cparams(), name="wkv_sample")(r, lw, km, v, kkn, ab, s0)


def _even_back_body(o_ref, bon_ref, bg_ref, ya_ref, x_ref, lg_ref, lb_ref, w_ref, g_ref, b_ref, y_ref, *, alpha, bw):
    hsum = _head_ones(bw)
    o = o_ref[...]
    om = _split_dot(o, hsum) * (1.0 / HEAD)
    oc = o - om
    ov = _split_dot(oc * oc, hsum) * (1.0 / HEAD)
    on = oc * lax.rsqrt(ov + LNX_EPS) * lg_ref[...] + lb_ref[...]
    yb = (on + bon_ref[...]) * bg_ref[...]
    y = _dot(ya_ref[...], w_ref[0:bw, :]) + _dot(yb, w_ref[bw:2 * bw, :])
    y_ref[...] = _layer_norm(alpha * x_ref[...] + y, g_ref[...], b_ref[...], LN_EPS)


def _even_back(o, bonus, bg, ya, x, lnx_g, lnx_b, w_out, ln_g, ln_b, *, alpha, tm):
    t, d = x.shape
    bw = o.shape[-1]
    row = lambda i: (i, 0)
    fix = lambda i: (0, 0)
    half = pl.BlockSpec((tm, bw), row)
    in_specs = [half] * 4 + [pl.BlockSpec((tm, d), row), pl.BlockSpec((1, bw), fix), pl.BlockSpec((1, bw), fix),
                             pl.BlockSpec((2 * bw, d), fix), pl.BlockSpec((1, d), fix), pl.BlockSpec((1, d), fix)]
    return pl.pallas_call(
        functools.partial(_even_back_body, alpha=alpha, bw=bw),
        out_shape=jax.ShapeDtypeStruct((t, d), F32), grid=(t // tm,), in_specs=in_specs,
        out_specs=pl.BlockSpec((tm, d), row), compiler_params=_cparams("parallel"),
        name="even_back")(o, bonus, bg, ya, x, lnx_g, lnx_b, w_out, ln_g, ln_b)


def _odd_front_body(x_ref, w_ref, *refs, tm, gw, dils, scale):
    ng = len(dils)
    q_refs, k_refs, v_refs = refs[0:ng], refs[ng:2 * ng], refs[2 * ng:3 * ng]
    gate_ref, kf_ref, vf_ref, p_scr = refs[3 * ng:3 * ng + 4]
    p = jnp.dot(x_ref[...].astype(BF16), w_ref[...], preferred_element_type=F32)
    qkv = 3 * ng * gw
    lt = p_scr.shape[-1]
    for cb in range(qkv // lt):
        p_scr[cb] = p[:, cb * lt:(cb + 1) * lt]
    gate_ref[...] = _silu(p[:, qkv:qkv + gw])
    kf_ref[...] = p[:, ng * gw:2 * ng * gw]
    vf_ref[...] = p[:, 2 * ng * gw:3 * ng * gw]
    for g, d in enumerate(dils):
        n = tm // d
        for c in range(d):
            rows = pl.ds(c, n, stride=d) if d > 1 else pl.ds(0, n)
            for lb in range(gw // lt):
                ls = slice(lb * lt, (lb + 1) * lt)
                q_refs[g][c, :, ls] = (p_scr[(g * gw) // lt + lb, rows, :] * scale).astype(BF16)
                k_refs[g][c, :, ls] = p_scr[((ng + g) * gw) // lt + lb, rows, :].astype(BF16)
                v_refs[g][c, :, ls] = p_scr[((2 * ng + g) * gw) // lt + lb, rows, :].astype(BF16)


def _odd_front(x, w, *, tm, dils, gw):
    t, d_model = x.shape
    ng = len(dils)
    ncol = w.shape[1]
    row = lambda i: (i, 0)
    cls_specs = [pl.BlockSpec((d, tm // d, gw), lambda i: (0, i, 0)) for d in dils]
    cls_shapes = [jax.ShapeDtypeStruct((d, t // d, gw), BF16) for d in dils]
    outs = cls_shapes * 3 + [jax.ShapeDtypeStruct((t, gw), F32), jax.ShapeDtypeStruct((t, ng * gw), F32),
                             jax.ShapeDtypeStruct((t, ng * gw), F32)]
    out_specs = cls_specs * 3 + [pl.BlockSpec((tm, gw), row), pl.BlockSpec((tm, ng * gw), row),
                                 pl.BlockSpec((tm, ng * gw), row)]
    body = functools.partial(_odd_front_body, tm=tm, gw=gw, dils=dils, scale=HEAD ** -0.5)
    res = pl.pallas_call(
        body, out_shape=outs, grid=(t // tm,),
        in_specs=[pl.BlockSpec((tm, d_model), row), pl.BlockSpec((d_model, ncol), lambda i: (0, 0))],
        out_specs=out_specs, scratch_shapes=[pltpu.VMEM((3 * ng * gw // 128, tm, 128), F32)],
        compiler_params=_cparams("parallel"), name="odd_front")(x, w)
    return res[0:ng], res[ng:2 * ng], res[2 * ng:3 * ng], res[3 * ng], res[3 * ng + 1], res[3 * ng + 2]


def _band_attn_body(q_ref, kc_ref, kp_ref, vc_ref, vp_ref, o_ref, lse_ref, *, bq, span, nb_class):
    j = pl.program_id(0)
    first_lo = jnp.where(((j * bq) % nb_class) == 0, span, 0)
    qi = lax.broadcasted_iota(jnp.int32, (span, 2 * span), 0)
    ki = lax.broadcasted_iota(jnp.int32, (span, 2 * span), 1)
    in_band = (ki >= qi) & (ki <= qi + span)
    lane = lax.broadcasted_iota(jnp.int32, (span, 2 * HEAD), 1)
    left = lane < HEAD
    gw = q_ref.shape[-1]
    for b in range(bq):
        mask = in_band & (ki >= first_lo) if b == 0 else in_band
        for pr in range(gw // (2 * HEAD)):
            ls = slice(pr * 2 * HEAD, (pr + 1) * 2 * HEAD)
            q2 = q_ref[b * span:(b + 1) * span, ls]
            if b == 0:
                k2 = jnp.concatenate([kp_ref[:, ls], kc_ref[0:span, ls]], axis=0)
                v2 = jnp.concatenate([vp_ref[:, ls], vc_ref[0:span, ls]], axis=0)
            else:
                k2 = kc_ref[(b - 1) * span:(b + 1) * span, ls]
                v2 = vc_ref[(b - 1) * span:(b + 1) * span, ls]
            outs, lses = [], []
            for sel in (left, jnp.logical_not(left)):
                qh = jnp.where(sel, q2, jnp.zeros_like(q2))
                s = lax.dot_general(qh, k2, (((1,), (1,)), ((), ())), preferred_element_type=F32)
                s = jnp.where(mask, s, -jnp.inf)
                m = jnp.max(s, axis=-1, keepdims=True)
                p = jnp.exp(s - m)
                l = jnp.sum(p, axis=-1, keepdims=True)
                outs.append(jnp.dot(p.astype(BF16), v2, preferred_element_type=F32) / l)
                lses.append(m + jnp.log(l))
            o_ref[b * span:(b + 1) * span, ls] = jnp.where(left, outs[0], outs[1])
            lse_ref[b * span:(b + 1) * span, ls] = jnp.where(left, lses[0], lses[1])


def _band_attn(q, k, v, *, span, nb_class, bq=4):
    t, gw = q.shape
    cur = pl.BlockSpec((bq * span, gw), lambda j: (j, 0))
    prev = pl.BlockSpec((span, gw), lambda j: (jnp.maximum(j * bq - 1, 0), 0))
    out = jax.ShapeDtypeStruct((t, gw), F32)
    return pl.pallas_call(
        functools.partial(_band_attn_body, bq=bq, span=span, nb_class=nb_class),
        out_shape=[out, out], grid=(t // (bq * span),), in_specs=[cur, cur, prev, cur, prev],
        out_specs=[cur, cur], compiler_params=_cparams("parallel"), name="band_attn")(q, k, k, v, v)


def _cache_attn_body(q_ref, kn_ref, vn_ref, c_ref, o_ref, lse_ref, cout_ref, k_scr, v_scr, *, w, d, window, ns, gw):
    pad = 16
    nh = gw // HEAD
    kn, vn = kn_ref[0], vn_ref[0]
    zpad = jnp.zeros((pad - ns, gw), F32)
    k_scr[0:w, :] = c_ref[0, :, 0:gw].astype(BF16)
    v_scr[0:w, :] = c_ref[0, :, gw:2 * gw].astype(BF16)
    k_scr[w:w + pad, :] = jnp.concatenate([kn, zpad], axis=0).astype(BF16)
    v_scr[w:w + pad, :] = jnp.concatenate([vn, zpad], axis=0).astype(BF16)
    rp = 8
    q8 = jnp.concatenate([q_ref[0], jnp.zeros((rp - ns, gw), F32)], axis=0)
    qm = jnp.concatenate([q8] * nh, axis=0)
    rowh = lax.broadcasted_iota(jnp.int32, (nh * rp, gw), 0) >> 3
    laneh = lax.broadcasted_iota(jnp.int32, (nh * rp, gw), 1) >> 6
    own = rowh == laneh
    qm = jnp.where(own, qm, 0.0).astype(BF16)
    s = lax.dot_general(qm, k_scr[...], (((1,), (1,)), ((), ())), preferred_element_type=F32)
    srow = lax.broadcasted_iota(jnp.int32, s.shape, 0) & (rp - 1)
    col = lax.broadcasted_iota(jnp.int32, s.shape, 1)
    dist = w + srow - col
    valid = (dist >= 0) & (dist <= window) & ((dist & (d - 1)) == 0) & (srow < ns)
    valid = valid | ((srow >= ns) & (col == 0))
    s = jnp.where(valid, s, -jnp.inf)
    m = jnp.max(s, axis=-1, keepdims=True)
    p = jnp.exp(s - m)
    l = jnp.sum(p, axis=-1, keepdims=True)
    o = jnp.dot(p.astype(BF16), v_scr[...], preferred_element_type=F32) / l
    lse = jnp.broadcast_to(m + jnp.log(l), o.shape)
    o = jnp.where(own, o, 0.0)
    lse = jnp.where(own, lse, 0.0)
    o8 = o[0:rp]
    l8 = lse[0:rp]
    for h in range(1, nh):
        o8 = o8 + o[h * rp:(h + 1) * rp]
        l8 = l8 + lse[h * rp:(h + 1) * rp]
    o_ref[0] = o8[0:ns]
    lse_ref[0] = l8[0:ns]
    cout_ref[0, 0:w - ns, :] = c_ref[0, ns:w, :]
    cout_ref[0, w - ns:w, 0:gw] = kn
    cout_ref[0, w - ns:w, gw:2 * gw] = vn


def _cache_attn(q, kn, vn, cache, *, d, window):
    nbat, ns, gw = q.shape
    w = cache.shape[1]
    new = pl.BlockSpec((1, ns, gw), lambda b: (b, 0, 0))
    cb = pl.BlockSpec((1, w, 2 * gw), lambda b: (b, 0, 0))
    out = jax.ShapeDtypeStruct((nbat, ns, gw), F32)
    return pl.pallas_call(
        functools.partial(_cache_attn_body, w=w, d=d, window=window, ns=ns, gw=gw),
        out_shape=[out, out, jax.ShapeDtypeStruct(cache.shape, F32)], grid=(nbat,),
        in_specs=[new, new, new, cb], out_specs=[new, new, cb],
        scratch_shapes=[pltpu.VMEM((w + 16, gw), BF16), pltpu.VMEM((w + 16, gw), BF16)],
        compiler_params=_cparams("parallel"), name="cache_attn")(q, kn, vn, cache)


def _odd_back_body(*refs, tm, dils, alpha):
    ng = len(dils)
    o_refs, l_refs = refs[0:ng], refs[ng:2 * ng]
    gate_ref, x_ref, w_ref, g_ref, b_ref, y_ref = refs[2 * ng:2 * ng + 6]
    scr = refs[2 * ng + 6:]
    os_, ls_ = [], []
    for g, d in enumerate(dils):
        if d == 1:
            os_.append(o_refs[g][0])
            ls_.append(l_refs[g][0])
        else:
            n = tm // d
            nlb, _, lt = scr[2 * g].shape
            for c in range(d):
                for lb in range(nlb):
                    scr[2 * g][lb, pl.ds(c, n, stride=d), :] = o_refs[g][c, :, lb * lt:(lb + 1) * lt]
                    scr[2 * g + 1][lb, pl.ds(c, n, stride=d), :] = l_refs[g][c, :, lb * lt:(lb + 1) * lt]
            os_.append(jnp.concatenate([scr[2 * g][lb] for lb in range(nlb)], axis=1))
            ls_.append(jnp.concatenate([scr[2 * g + 1][lb] for lb in range(nlb)], axis=1))
    m = ls_[0]
    for l in ls_[1:]:
        m = jnp.maximum(m, l)
    es = [jnp.exp(l - m) for l in ls_]
    den = es[0]
    for e in es[1:]:
        den = den + e
    o = es[0] * os_[0]
    for e, og in zip(es[1:], os_[1:]):
        o = o + e * og
    o = o / den
    y = _dot(o * gate_ref[...], w_ref[...])
    y_ref[...] = _layer_norm(alpha * x_ref[...] + y, g_ref[...], b_ref[...], LN_EPS)


def _odd_back(os_, ls_, gate, x, w_out, ln_g, ln_b, *, dils, alpha, tm):
    t, d_model = x.shape
    gw = gate.shape[-1]
    row = lambda i: (i, 0)
    fix = lambda i: (0, 0)
    cls_specs = [pl.BlockSpec((d, tm // d, gw), lambda i: (0, i, 0)) for d in dils]
    in_specs = cls_specs * 2 + [pl.BlockSpec((tm, gw), row), pl.BlockSpec((tm, d_model), row),
                                pl.BlockSpec((gw, d_model), fix), pl.BlockSpec((1, d_model), fix),
                                pl.BlockSpec((1, d_model), fix)]
    return pl.pallas_call(
        functools.partial(_odd_back_body, tm=tm, dils=dils, alpha=alpha),
        out_shape=jax.ShapeDtypeStruct((t, d_model), F32), grid=(t // tm,), in_specs=in_specs,
        out_specs=pl.BlockSpec((tm, d_model), row),
        scratch_shapes=[pltpu.VMEM((gw // 128, tm, 128), F32)] * (2 * len(dils)),
        compiler_params=_cparams("parallel"), name="odd_back")(*os_, *ls_, gate, x, w_out, ln_g, ln_b)


def kernel(x_prompt, x_sample, state_conv, state_shift, state_wkv, cache_kv_w128, cache_kv_w512, cache_kv_w2048,
           w_in_even, conv_w, conv_b, conv_ln_g, conv_ln_b, mu_shift, w0, w2, a0, a2, k_k, k_a, r_k, lnx_g, lnx_b,
           w_out_even, w_in_odd, w_out_odd, ln_g, ln_b):
    nbp, seq, d_model = x_prompt.shape
    nbs, ns, _ = x_sample.shape
    assert nbp == 1
    depth = ln_g.shape[0]
    assert depth == 2 and w_in_even.shape[0] == 1 and w_in_odd.shape[0] == 1
    alpha = (2.0 * depth) ** 0.25
    bw = w0.shape[-1]
    lora_d, lora_a = w2.shape[1], a2.shape[1]
    nh = bw // HEAD
    nhist = CONV_W - 1
    caches = (cache_kv_w128, cache_kv_w512, cache_kv_w2048)
    dils = tuple(d for _, d in C_GROUPS)
    gw = w_out_odd.shape[1]
    ng = len(C_GROUPS)

    wi = w_in_even[0]
    sh_cols = 3 * bw + lora_d + lora_a
    zpad = jnp.zeros((d_model, LORA_PAD - lora_d - lora_a), F32)
    w_re = jnp.concatenate([wi[:, :sh_cols], zpad, wi[:, sh_cols:]], axis=1).astype(BF16)
    mu_re = jnp.concatenate([mu_shift[0], jnp.zeros((LORA_PAD - lora_d - lora_a,), F32)])[None, :]
    w2p = jnp.zeros((LORA_PAD, bw), F32).at[0:lora_d].set(w2[0])
    a2p = jnp.zeros((LORA_PAD, bw), F32).at[lora_d:lora_d + lora_a].set(a2[0])
    cw = jnp.concatenate([conv_w[0], jnp.zeros((CONV_HIST - CONV_W, bw), F32)], axis=0)
    vec = lambda z: z.reshape(1, -1)
    small = (vec(conv_b[0]), vec(conv_ln_g[0]), vec(conv_ln_b[0]), vec(w0[0]), w2p, vec(a0[0]), a2p,
             vec(k_k[0]), vec(k_a[0]), vec(r_k[0]))
    w_out_e = w_out_even[0].astype(BF16)
    w_in_o = w_in_odd[0].astype(BF16)
    w_out_o = w_out_odd[0].astype(BF16)
    lng = [vec(ln_g[l]) for l in range(depth)]
    lnb = [vec(ln_b[l]) for l in range(depth)]

    xp = x_prompt[0]
    ya, bg, r, lw, km, v, kkn, ab, bonus, conv_tail = _even_front_prompt(xp, w_re, mu_re, cw, small)
    o_wkv, wkv_p = _wkv_prompt(r, lw, km, v, kkn, ab, jnp.zeros((nh, HEAD, HEAD), F32))
    xp1 = _even_back(o_wkv, bonus, bg, ya, xp, vec(lnx_g[0]), vec(lnx_b[0]), w_out_e, lng[0], lnb[0],
                     alpha=alpha, tm=256)
    conv_p = conv_tail[CONV_HIST - nhist:][None, None]
    shift_p = x_prompt[:, -1][None]

    xs_t = jnp.transpose(x_sample, (1, 0, 2)).reshape(ns * nbs, d_model)
    x_in = jnp.concatenate([state_shift[0], xs_t], axis=0)
    hist = jnp.transpose(state_conv[0], (1, 0, 2)).reshape(nhist * nbs, bw)
    res = _even_front_sample(x_in, hist, w_re, mu_re, cw, small, nb=nbs, ns=ns)
    ya_s, bg_s, bon_s, conv_s_t = res[0], res[1], res[8], res[9]
    blk = 8

    def seq_major_pad(z):
        z = jnp.transpose(z.reshape(ns, nbs, bw), (1, 0, 2))
        return jnp.pad(z, ((0, 0), (0, blk - ns), (0, 0))).reshape(nbs * blk, bw)

    wkv_in = [seq_major_pad(z) for z in res[2:8]]
    o_s_pad, wkv_s = _wkv_sample(*wkv_in, state_wkv[0], blk=blk)
    o_s = jnp.transpose(o_s_pad.reshape(nbs, blk, bw)[:, :ns], (1, 0, 2)).reshape(ns * nbs, bw)
    xs1_t = _even_back(o_s, bon_s, bg_s, ya_s, xs_t, vec(lnx_g[0]), vec(lnx_b[0]), w_out_e, lng[0], lnb[0],
                       alpha=alpha, tm=ns * nbs)
    conv_s = jnp.transpose(conv_s_t.reshape(nhist, nbs, bw), (1, 0, 2))[None]
    shift_s = x_sample[:, -1][None]
    xs1 = jnp.transpose(xs1_t.reshape(ns, nbs, d_model), (1, 0, 2)).reshape(nbs * ns, d_model)

    tm1 = 256
    qc, kc, vc, gate_p, kf, vf = _odd_front(xp1, w_in_o, tm=tm1, dils=dils, gw=gw)
    os_, ls_ = [], []
    for g, (win, d) in enumerate(C_GROUPS):
        span = win // d
        flat = lambda z: z.reshape(seq, gw)
        o_g, l_g = _band_attn(flat(qc[g]), flat(kc[g]), flat(vc[g]), span=span, nb_class=(seq // d) // span)
        os_.append(o_g.reshape(d, seq // d, gw))
        ls_.append(l_g.reshape(d, seq // d, gw))
    xp2 = _odd_back(os_, ls_, gate_p, xp1, w_out_o, lng[1], lnb[1], dils=dils, alpha=alpha, tm=tm1)
    kv_p = []
    for g, (win, d) in enumerate(C_GROUPS):
        n = min(win, seq)
        kg = kf[seq - n:, g * gw:(g + 1) * gw].reshape(n, gw // HEAD, HEAD)
        vg = vf[seq - n:, g * gw:(g + 1) * gw].reshape(n, gw // HEAD, HEAD)
        kv_p.append(jnp.stack([kg, vg], axis=1)[None, None])

    ones = (1,) * ng
    qs, _, _, gate_s, kf_s, vf_s = _odd_front(xs1, w_in_o, tm=nbs * ns, dils=ones, gw=gw)
    os_s, ls_s, kv_s_out = [], [], []
    for g, (win, d) in enumerate(C_GROUPS):
        cache = caches[g][0]
        w = cache.shape[1]
        q_g = qs[g].astype(F32).reshape(nbs, ns, gw)
        kn = kf_s[:, g * gw:(g + 1) * gw].reshape(nbs, ns, gw)
        vn = vf_s[:, g * gw:(g + 1) * gw].reshape(nbs, ns, gw)
        o_g, l_g, cnew = _cache_attn(q_g, kn, vn, cache.reshape(nbs, w, 2 * gw), d=d, window=win)
        os_s.append(o_g.reshape(1, nbs * ns, gw))
        ls_s.append(l_g.reshape(1, nbs * ns, gw))
        kv_s_out.append(cnew.reshape(nbs, w, 2, gw // HEAD, HEAD)[None])
    xs2 = _odd_back(os_s, ls_s, gate_s, xs1, w_out_o, lng[1], lnb[1], dils=ones, alpha=alpha, tm=nbs * ns)

    return (xp2[None], xs2.reshape(nbs, ns, d_model), conv_p, conv_s, shift_p, shift_s,
            wkv_p[None, None], wkv_s[None],
            kv_p[0], kv_s_out[0], kv_p[1], kv_s_out[1], kv_p[2], kv_s_out[2])
```

```python
import functools
import math

import jax
import jax.numpy as jnp
from jax import lax
from jax.experimental import pallas as pl
from jax.experimental.pallas import tpu as pltpu

F32 = jnp.float32
BF16 = jnp.bfloat16
HIGHEST = lax.Precision.HIGHEST

LN_EPS = 1e-5
LNX_EPS = 64e-5
HEAD = 64
LORA_PAD = 128
CONV_W = 31
CONV_HIST = 32
C_GROUPS = ((128, 1), (512, 4), (2048, 16))
WKV_CHUNK = 64
VMEM_LIMIT = 56 * 1024 * 1024


def _cparams(*sem):
    return pltpu.CompilerParams(dimension_semantics=sem or None, vmem_limit_bytes=VMEM_LIMIT)


def _dot(a, b):
    return jnp.dot(a.astype(BF16), b.astype(BF16), preferred_element_type=F32)


def _dot_nt(a, b):
    return lax.dot_general(a.astype(BF16), b.astype(BF16), (((1,), (1,)), ((), ())), preferred_element_type=F32)


def _dot_tn(a, b):
    return lax.dot_general(a.astype(BF16), b.astype(BF16), (((0,), (0,)), ((), ())), preferred_element_type=F32)


def _dot_hi(a, b):
    return jnp.dot(a, b, precision=HIGHEST, preferred_element_type=F32)


def _split_dot(a, ones_bf16):
    ah = a.astype(BF16)
    al = (a - ah.astype(F32)).astype(BF16)
    return (jnp.dot(ah, ones_bf16, preferred_element_type=F32) +
            jnp.dot(al, ones_bf16, preferred_element_type=F32))


def _head_ones(n):
    r = lax.broadcasted_iota(jnp.int32, (n, n), 0) >> 6
    c = lax.broadcasted_iota(jnp.int32, (n, n), 1) >> 6
    return jnp.where(r == c, 1.0, 0.0).astype(BF16)


def _sigmoid(x):
    return 1.0 / (1.0 + jnp.exp(-x))


def _silu(x):
    return x * _sigmoid(x)


def _softplus(y):
    return jnp.maximum(y, 0.0) + jnp.log(1.0 + jnp.exp(-jnp.abs(y)))


def _layer_norm(x, g, b, eps):
    mu = jnp.mean(x, axis=-1, keepdims=True)
    xc = x - mu
    var = jnp.mean(xc * xc, axis=-1, keepdims=True)
    return xc * lax.rsqrt(var + eps) * g + b


def _even_mix(sh, b_width, prm, hsum):
    w0, w2p, a0, a2p, k_k, k_a, r_k = prm
    r = sh[:, 0:b_width]
    k = sh[:, b_width:2 * b_width]
    v = sh[:, 2 * b_width:3 * b_width]
    lora = sh[:, 3 * b_width:3 * b_width + LORA_PAD]
    w_log = -_softplus(-(w0 + _dot(jnp.tanh(lora), w2p))) - 0.5
    lw = -jnp.exp(w_log)
    a = _sigmoid(a0 + _dot(lora, a2p))
    kk = k * k_k
    nrm = jnp.sqrt(_split_dot(kk * kk, hsum))
    kkn = kk / jnp.maximum(nrm, 1e-12)
    km = k * (1.0 + (a - 1.0) * k_a)
    bonus = _split_dot(r * km * r_k, hsum) * v
    return r, lw, km, v, kkn, kkn * a, bonus


def _conv_ln_gate(u_ref, row0, n, cw_ref, cb, clg, clb, a_gate):
    acc = jnp.zeros((n, cb.shape[-1]), F32) + cb
    for j in range(CONV_W):
        acc = acc + cw_ref[j:j + 1, :] * u_ref[pl.ds(row0 + (CONV_HIST - (CONV_W - 1)) + j, n), :]
    return _silu(_layer_norm(acc, clg, clb, LN_EPS)) * _silu(a_gate)


def _even_front_prompt_body(x_ref, w_ref, mu_ref, cw_ref, cb_ref, clg_ref, clb_ref, w0_ref, w2_ref, a0_ref, a2_ref,
                            kk_ref, ka_ref, rk_ref,
                            ya_ref, bg_ref, r_ref, lw_ref, km_ref, v_ref, kkn_ref, ab_ref, bon_ref, conv_ref,
                            p_scr, u_scr, *, tm, bw, sub):
    i = pl.program_id(0)
    shw = 3 * bw + LORA_PAD

    @pl.when(i == 0)
    def _():
        p_scr[0:8, :] = jnp.zeros((8, shw), F32)
        u_scr[0:CONV_HIST, :] = jnp.zeros((CONV_HIST, bw), F32)

    p = jnp.dot(x_ref[...].astype(BF16), w_ref[...], preferred_element_type=F32)
    cur = p[:, :shw]
    p_scr[8:8 + tm, :] = cur
    prev = p_scr[pl.ds(7, tm), :]
    sh = cur + (prev - cur) * mu_ref[...]
    p_scr[7:8, :] = cur[tm - 1:tm, :]

    hsum = _head_ones(bw)
    prm = (w0_ref[...], w2_ref[...], a0_ref[...], a2_ref[...], kk_ref[...], ka_ref[...], rk_ref[...])
    r, lw, km, v, kkn, ab, bonus = _even_mix(sh, bw, prm, hsum)
    r_ref[...] = r
    lw_ref[...] = lw
    km_ref[...] = km
    v_ref[...] = v
    kkn_ref[...] = kkn
    ab_ref[...] = ab
    bon_ref[...] = bonus

    a_val = p[:, shw:shw + bw]
    a_glu = p[:, shw + bw:shw + 2 * bw]
    a_gate = p[:, shw + 2 * bw:shw + 3 * bw]
    b_gate = p[:, shw + 3 * bw:shw + 4 * bw]
    bg_ref[...] = _silu(b_gate)
    u_scr[CONV_HIST:CONV_HIST + tm, :] = a_val * _sigmoid(a_glu)
    cb, clg, clb = cb_ref[...], clg_ref[...], clb_ref[...]
    for s in range(tm // sub):
        ya_ref[s * sub:(s + 1) * sub, :] = _conv_ln_gate(u_scr, s * sub, sub, cw_ref, cb, clg, clb,
                                                         a_gate[s * sub:(s + 1) * sub, :])
    tail = u_scr[tm:tm + CONV_HIST, :]
    conv_ref[...] = tail
    u_scr[0:CONV_HIST, :] = tail


def _even_front_prompt(x, w_re, mu_re, cw, small, *, tm=256):
    t, d = x.shape
    bw = small[0].shape[-1]
    shw = 3 * bw + LORA_PAD
    ncol = w_re.shape[1]
    row = lambda i: (i, 0)
    fix = lambda i: (0, 0)
    vec = pl.BlockSpec((1, bw), fix)
    in_specs = [pl.BlockSpec((tm, d), row), pl.BlockSpec((d, ncol), fix), pl.BlockSpec((1, shw), fix),
                pl.BlockSpec((CONV_HIST, bw), fix), vec, vec, vec, vec, pl.BlockSpec((LORA_PAD, bw), fix), vec,
                pl.BlockSpec((LORA_PAD, bw), fix), vec, vec, vec]
    out = jax.ShapeDtypeStruct((t, bw), F32)
    outs = [out] * 9 + [jax.ShapeDtypeStruct((CONV_HIST, bw), F32)]
    out_specs = [pl.BlockSpec((tm, bw), row)] * 9 + [pl.BlockSpec((CONV_HIST, bw), fix)]
    body = functools.partial(_even_front_prompt_body, tm=tm, bw=bw, sub=32)
    return pl.pallas_call(
        body, out_shape=outs, grid=(t // tm,), in_specs=in_specs, out_specs=out_specs,
        scratch_shapes=[pltpu.VMEM((8 + tm, shw), F32), pltpu.VMEM((CONV_HIST + tm, bw), F32)],
        compiler_params=_cparams("arbitrary"), name="even_front_prompt")(x, w_re, mu_re, cw, *small)


def _even_front_sample_body(x_ref, hist_ref, w_ref, mu_ref, cw_ref, cb_ref, clg_ref, clb_ref, w0_ref, w2_ref, a0_ref,
                            a2_ref, kk_ref, ka_ref, rk_ref,
                            ya_ref, bg_ref, r_ref, lw_ref, km_ref, v_ref, kkn_ref, ab_ref, bon_ref, conv_ref,
                            u_scr, *, nb, ns, bw):
    n = nb * ns
    shw = 3 * bw + LORA_PAD
    nh = CONV_W - 1
    p = jnp.dot(x_ref[...].astype(BF16), w_ref[...], preferred_element_type=F32)
    cur = p[nb:, :shw]
    prev = p[:n, :shw]
    sh = cur + (prev - cur) * mu_ref[...]
    hsum = _head_ones(bw)
    prm = (w0_ref[...], w2_ref[...], a0_ref[...], a2_ref[...], kk_ref[...], ka_ref[...], rk_ref[...])
    r, lw, km, v, kkn, ab, bonus = _even_mix(sh, bw, prm, hsum)
    r_ref[...] = r
    lw_ref[...] = lw
    km_ref[...] = km
    v_ref[...] = v
    kkn_ref[...] = kkn
    ab_ref[...] = ab
    bon_ref[...] = bonus
    rest = p[nb:, shw:]
    a_val, a_glu = rest[:, 0:bw], rest[:, bw:2 * bw]
    a_gate, b_gate = rest[:, 2 * bw:3 * bw], rest[:, 3 * bw:4 * bw]
    bg_ref[...] = _silu(b_gate)
    u_scr[0:nh * nb, :] = hist_ref[...]
    u_scr[nh * nb:(nh + ns) * nb, :] = a_val * _sigmoid(a_glu)
    cb, clg, clb = cb_ref[...], clg_ref[...], clb_ref[...]
    for s in range(ns):
        acc = jnp.zeros((nb, bw), F32) + cb
        for j in range(CONV_W):
            acc = acc + cw_ref[j:j + 1, :] * u_scr[(s + j) * nb:(s + j + 1) * nb, :]
        ya_ref[s * nb:(s + 1) * nb, :] = (_silu(_layer_norm(acc, clg, clb, LN_EPS)) *
                                           _silu(a_gate[s * nb:(s + 1) * nb, :]))
    conv_ref[...] = u_scr[ns * nb:(ns + nh) * nb, :]


def _even_front_sample(x_in, hist, w_re, mu_re, cw, small, *, nb, ns):
    bw = small[0].shape[-1]
    n = nb * ns
    nh = CONV_W - 1
    out = jax.ShapeDtypeStruct((n, bw), F32)
    outs = [out] * 9 + [jax.ShapeDtypeStruct((nh * nb, bw), F32)]
    body = functools.partial(_even_front_sample_body, nb=nb, ns=ns, bw=bw)
    return pl.pallas_call(
        body, out_shape=outs, scratch_shapes=[pltpu.VMEM(((nh + ns) * nb, bw), F32)],
        compiler_params=_cparams(), name="even_front_sample")(x_in, hist, w_re, mu_re, cw, *small)


def _wkv_block_terms(r, lw, km, v, kkn, ab, blk):
    n, width = r.shape
    nh = width // HEAD
    ri = lax.broadcasted_iota(jnp.int32, (n, n), 0)
    ci = lax.broadcasted_iota(jnp.int32, (n, n), 1)
    same = _same_block(ri, ci, blk)
    incl = (ri >= ci) & same
    strict = (ri > ci) & same
    cum = _split3_dot(jnp.where(incl, 1.0, 0.0).astype(BF16), lw)
    tot = _split3_dot(jnp.where(same, 1.0, 0.0).astype(BF16), lw)
    dn = jnp.exp(-cum)
    at = -kkn * jnp.exp(cum - lw)
    bt = ab * dn
    kt = km * dn
    rt = r * jnp.exp(cum)
    rem = jnp.exp(tot - cum)
    bc = ab * rem
    kc = km * rem
    wc = jnp.exp(tot)
    hs = [slice(h * HEAD, (h + 1) * HEAD) for h in range(nh)]
    gs = [_dot_nt(jnp.concatenate([at[:, sl], rt[:, sl]], axis=0), jnp.concatenate([bt[:, sl], kt[:, sl]], axis=0))
          for sl in hs]
    l_ab = [jnp.where(strict, g[:n, :n], 0.0) for g in gs]
    l_ak = [jnp.where(strict, g[:n, n:], 0.0) for g in gs]
    l_rb = [jnp.where(incl, g[n:, :n], 0.0) for g in gs]
    l_rk = [jnp.where(incl, g[n:, n:], 0.0) for g in gs]
    toff = _unit_lower_inverse_minus_eye(l_ab, ri, ci, blk)
    akv = [_dot(l, v[:, sl]) for l, sl in zip(l_ak, hs)]
    ahat = [at[:, sl] + _dot(t, at[:, sl]) for t, sl in zip(toff, hs)]
    uhat = [x + _dot(t, x) for t, x in zip(toff, akv)]
    qhat = [rt[:, sl] + _dot(l, a) for l, a, sl in zip(l_rb, ahat, hs)]
    ohat = [_dot(lb, u) + _dot(lk, v[:, sl]) for lb, lk, u, sl in zip(l_rb, l_rk, uhat, hs)]
    return list(zip(ahat, uhat, qhat, ohat)), bc, kc, wc


def _same_block(ri, ci, size):
    sh = size.bit_length() - 1
    return (ri >> sh) == (ci >> sh)


def _split(x):
    hi = x.astype(BF16)
    return hi, (x - hi.astype(F32)).astype(BF16)


def _dot3(a, b):
    (ah, al), (bh, bl) = a, b
    d = lambda x, y: jnp.dot(x, y, preferred_element_type=F32)
    return d(ah, bh) + (d(ah, bl) + d(al, bh))


def _split3_dot(ones_bf16, x):
    hi, lo = _split(x)
    lo2 = (x - hi.astype(F32) - lo.astype(F32)).astype(BF16)
    d = lambda y: jnp.dot(ones_bf16, y, preferred_element_type=F32)
    return d(hi) + (d(lo) + d(lo2))


def _unit_lower_inverse_minus_eye(ls, ri, ci, blk):
    base = 8
    eye = jnp.where(ri == ci, 1.0, 0.0)
    same8 = _same_block(ri, ci, base)
    l8 = [_split(jnp.where(same8, l, 0.0)) for l in ls]
    x2 = [_dot3(x, x) for x in l8]
    x2 = [_split(x) for x in x2]
    x4 = [_split(_dot3(x, x)) for x in x2]
    p = [eye + jnp.where(same8, l, 0.0) for l in ls]
    p = [q + _dot3(_split(q), x) for q, x in zip(p, x2)]
    t = [q + _dot3(_split(q), x) for q, x in zip(p, x4)]
    size = 2 * base
    while size <= blk:
        half = size.bit_length() - 2
        lower_left = _same_block(ri, ci, size) & (((ri >> half) & 1) == 1) & (((ci >> half) & 1) == 0)
        ts = [_split(x) for x in t]
        mt = [_dot3(_split(jnp.where(lower_left, l, 0.0)), x) for l, x in zip(ls, ts)]
        t = [x + _dot3(xs, _split(m)) for x, xs, m in zip(t, ts, mt)]
        size *= 2
    return [x - eye for x in t]


def _wkv_prompt_body(r_ref, lw_ref, km_ref, v_ref, kkn_ref, ab_ref, s0_ref, o_ref, sout_ref, s_scr, *, c, nch):
    i = pl.program_id(0)

    @pl.when(i == 0)
    def _():
        s_scr[...] = s0_ref[...]

    v = v_ref[...]
    heads, bc, kc, wc = _wkv_block_terms(r_ref[...], lw_ref[...], km_ref[...], v, kkn_ref[...], ab_ref[...], c)
    nh = len(heads)
    hs = [slice(h * HEAD, (h + 1) * HEAD) for h in range(nh)]
    vk = [_dot_tn(v[ch * c:(ch + 1) * c, sl], kc[ch * c:(ch + 1) * c, sl]) for ch in range(nch) for sl in hs]
    s = [s_scr[h] for h in range(nh)]
    for ch in range(nch):
        rows = slice(ch * c, (ch + 1) * c)
        mm = [_dot_nt(jnp.concatenate([ahat[rows], qhat[rows]], axis=0), s[h])
              for h, (ahat, _, qhat, _) in enumerate(heads)]
        u = [mm[h][:c] + heads[h][1][rows] for h in range(nh)]
        for h in range(nh):
            o_ref[rows, hs[h]] = mm[h][c:] + heads[h][3][rows]
        s = [s[h] * wc[ch * c:ch * c + 1, hs[h]] + _dot_tn(u[h], bc[rows, hs[h]]) + vk[ch * nh + h]
             for h in range(nh)]
    for h in range(nh):
        s_scr[h] = s[h]

    @pl.when(i == pl.num_programs(0) - 1)
    def _():
        sout_ref[...] = s_scr[...]


def _wkv_prompt(r, lw, km, v, kkn, ab, s0, *, nch=2):
    t, bw = r.shape
    c = WKV_CHUNK
    nh = bw // HEAD
    row = pl.BlockSpec((nch * c, bw), lambda i: (i, 0))
    st = pl.BlockSpec((nh, HEAD, HEAD), lambda i: (0, 0, 0))
    return pl.pallas_call(
        functools.partial(_wkv_prompt_body, c=c, nch=nch),
        out_shape=[jax.ShapeDtypeStruct((t, bw), F32), jax.ShapeDtypeStruct((nh, HEAD, HEAD), F32)],
        grid=(t // (nch * c),), in_specs=[row] * 6 + [st], out_specs=[row, st],
        scratch_shapes=[pltpu.VMEM((nh, HEAD, HEAD), F32)],
        compiler_params=_cparams("arbitrary"), name="wkv_prompt")(r, lw, km, v, kkn, ab, s0)


def _wkv_sample_body(r_ref, lw_ref, km_ref, v_ref, kkn_ref, ab_ref, s0_ref, o_ref, sout_ref,
                     ah_scr, uh_scr, qh_scr, oh_scr, bc_scr, kc_scr, wc_scr, *, nb, blk):
    v = v_ref[...]
    heads, bc, kc, wc = _wkv_block_terms(r_ref[...], lw_ref[...], km_ref[...], v, kkn_ref[...], ab_ref[...], blk)
    bc_scr[...] = bc
    kc_scr[...] = kc
    wc_scr[...] = wc
    for h, (ahat, uhat, qhat, ohat) in enumerate(heads):
        sl = slice(h * HEAD, (h + 1) * HEAD)
        ah_scr[:, sl] = ahat
        uh_scr[:, sl] = uhat
        qh_scr[:, sl] = qhat
        oh_scr[:, sl] = ohat
    nh = len(heads)

    def per_seq(b, carry):
        rows = pl.ds(pl.multiple_of(b * blk, blk), blk)
        for h in range(nh):
            sl = slice(h * HEAD, (h + 1) * HEAD)
            s = s0_ref[b, h]
            mm = _dot_nt(jnp.concatenate([ah_scr[rows, sl], qh_scr[rows, sl]], axis=0), s)
            u = mm[:blk] + uh_scr[rows, sl]
            o_ref[rows, sl] = mm[blk:] + oh_scr[rows, sl]
            wrow = wc_scr[rows, sl][0:1, :]
            sout_ref[b, h] = s * wrow + _dot_tn(u, bc_scr[rows, sl]) + _dot_tn(v_ref[rows, sl], kc_scr[rows, sl])
        return carry

    lax.fori_loop(0, nb, per_seq, 0)


def _wkv_sample(r, lw, km, v, kkn, ab, s0, *, blk):
    n, bw = r.shape
    nb = n // blk
    scr = pltpu.VMEM((n, bw), F32)
    return pl.pallas_call(
        functools.partial(_wkv_sample_body, nb=nb, blk=blk),
        out_shape=[jax.ShapeDtypeStruct((n, bw), F32), jax.ShapeDtypeStruct(s0.shape, F32)],
        scratch_shapes=[scr] * 7, compiler_params=_cparams(), name="wkv_sample")(r, lw, km, v, kkn, ab, s0)


def _even_back_body(o_ref, bon_ref, bg_ref, ya_ref, x_ref, lg_ref, lb_ref, w_ref, g_ref, b_ref, y_ref, *, alpha, bw):
    hsum = _head_ones(bw)
    o = o_ref[...]
    om = _split_dot(o, hsum) * (1.0 / HEAD)
    oc = o - om
    ov = _split_dot(oc * oc, hsum) * (1.0 / HEAD)
    on = oc * lax.rsqrt(ov + LNX_EPS) * lg_ref[...] + lb_ref[...]
    yb = (on + bon_ref[...]) * bg_ref[...]
    y = _dot(ya_ref[...], w_ref[0:bw, :]) + _dot(yb, w_ref[bw:2 * bw, :])
    y_ref[...] = _layer_norm(alpha * x_ref[...] + y, g_ref[...], b_ref[...], LN_EPS)


def _even_back(o, bonus, bg, ya, x, lnx_g, lnx_b, w_out, ln_g, ln_b, *, alpha, tm):
    t, d = x.shape
    bw = o.shape[-1]
    row = lambda i: (i, 0)
    fix = lambda i: (0, 0)
    half = pl.BlockSpec((tm, bw), row)
    in_specs = [half] * 4 + [pl.BlockSpec((tm, d), row), pl.BlockSpec((1, bw), fix), pl.BlockSpec((1, bw), fix),
                             pl.BlockSpec((2 * bw, d), fix), pl.BlockSpec((1, d), fix), pl.BlockSpec((1, d), fix)]
    return pl.pallas_call(
        functools.partial(_even_back_body, alpha=alpha, bw=bw),
        out_shape=jax.ShapeDtypeStruct((t, d), F32), grid=(t // tm,), in_specs=in_specs,
        out_specs=pl.BlockSpec((tm, d), row), compiler_params=_cparams("parallel"),
        name="even_back")(o, bonus, bg, ya, x, lnx_g, lnx_b, w_out, ln_g, ln_b)


def _odd_front_body(x_ref, w_ref, *refs, tm, gw, dils, scale):
    ng = len(dils)
    q_refs, k_refs, v_refs = refs[0:ng], refs[ng:2 * ng], refs[2 * ng:3 * ng]
    gate_ref, kf_ref, vf_ref, p_scr = refs[3 * ng:3 * ng + 4]
    p = jnp.dot(x_ref[...].astype(BF16), w_ref[...], preferred_element_type=F32)
    qkv = 3 * ng * gw
    lt = p_scr.shape[-1]
    for cb in range(qkv // lt):
        p_scr[cb] = p[:, cb * lt:(cb + 1) * lt]
    gate_ref[...] = _silu(p[:, qkv:qkv + gw])
    kf_ref[...] = p[:, ng * gw:2 * ng * gw]
    vf_ref[...] = p[:, 2 * ng * gw:3 * ng * gw]
    for g, d in enumerate(dils):
        n = tm // d
        for c in range(d):
            rows = pl.ds(c, n, stride=d) if d > 1 else pl.ds(0, n)
            for lb in range(gw // lt):
                ls = slice(lb * lt, (lb + 1) * lt)
                q_refs[g][c, :, ls] = (p_scr[(g * gw) // lt + lb, rows, :] * scale).astype(BF16)
                k_refs[g][c, :, ls] = p_scr[((ng + g) * gw) // lt + lb, rows, :].astype(BF16)
                v_refs[g][c, :, ls] = p_scr[((2 * ng + g) * gw) // lt + lb, rows, :].astype(BF16)


def _odd_front(x, w, *, tm, dils, gw):
    t, d_model = x.shape
    ng = len(dils)
    ncol = w.shape[1]
    row = lambda i: (i, 0)
    cls_specs = [pl.BlockSpec((d, tm // d, gw), lambda i: (0, i, 0)) for d in dils]
    cls_shapes = [jax.ShapeDtypeStruct((d, t // d, gw), BF16) for d in dils]
    outs = cls_shapes * 3 + [jax.ShapeDtypeStruct((t, gw), F32), jax.ShapeDtypeStruct((t, ng * gw), F32),
                             jax.ShapeDtypeStruct((t, ng * gw), F32)]
    out_specs = cls_specs * 3 + [pl.BlockSpec((tm, gw), row), pl.BlockSpec((tm, ng * gw), row),
                                 pl.BlockSpec((tm, ng * gw), row)]
    body = functools.partial(_odd_front_body, tm=tm, gw=gw, dils=dils, scale=HEAD ** -0.5)
    res = pl.pallas_call(
        body, out_shape=outs, grid=(t // tm,),
        in_specs=[pl.BlockSpec((tm, d_model), row), pl.BlockSpec((d_model, ncol), lambda i: (0, 0))],
        out_specs=out_specs, scratch_shapes=[pltpu.VMEM((3 * ng * gw // 128, tm, 128), F32)],
        compiler_params=_cparams("parallel"), name="odd_front")(x, w)
    return res[0:ng], res[ng:2 * ng], res[2 * ng:3 * ng], res[3 * ng], res[3 * ng + 1], res[3 * ng + 2]


def _band_attn_body(q_ref, kc_ref, kp_ref, vc_ref, vp_ref, o_ref, lse_ref, *, bq, span, nb_class):
    j = pl.program_id(0)
    first_lo = jnp.where(((j * bq) % nb_class) == 0, span, 0)
    qi = lax.broadcasted_iota(jnp.int32, (span, 2 * span), 0)
    ki = lax.broadcasted_iota(jnp.int32, (span, 2 * span), 1)
    in_band = (ki >= qi) & (ki <= qi + span)
    lane = lax.broadcasted_iota(jnp.int32, (span, 2 * HEAD), 1)
    left = lane < HEAD
    gw = q_ref.shape[-1]
    for b in range(bq):
        mask = in_band & (ki >= first_lo) if b == 0 else in_band
        for pr in range(gw // (2 * HEAD)):
            ls = slice(pr * 2 * HEAD, (pr + 1) * 2 * HEAD)
            q2 = q_ref[b * span:(b + 1) * span, ls]
            if b == 0:
                k2 = jnp.concatenate([kp_ref[:, ls], kc_ref[0:span, ls]], axis=0)
                v2 = jnp.concatenate([vp_ref[:, ls], vc_ref[0:span, ls]], axis=0)
            else:
                k2 = kc_ref[(b - 1) * span:(b + 1) * span, ls]
                v2 = vc_ref[(b - 1) * span:(b + 1) * span, ls]
            outs, lses = [], []
            for sel in (left, jnp.logical_not(left)):
                qh = jnp.where(sel, q2, jnp.zeros_like(q2))
                s = lax.dot_general(qh, k2, (((1,), (1,)), ((), ())), preferred_element_type=F32)
                s = jnp.where(mask, s, -jnp.inf)
                m = jnp.max(s, axis=-1, keepdims=True)
                p = jnp.exp(s - m)
                l = jnp.sum(p, axis=-1, keepdims=True)
                outs.append(jnp.dot(p.astype(BF16), v2, preferred_element_type=F32) / l)
                lses.append(m + jnp.log(l))
            o_ref[b * span:(b + 1) * span, ls] = jnp.where(left, outs[0], outs[1])
            lse_ref[b * span:(b + 1) * span, ls] = jnp.where(left, lses[0], lses[1])


def _band_attn(q, k, v, *, span, nb_class, bq=4):
    t, gw = q.shape
    cur = pl.BlockSpec((bq * span, gw), lambda j: (j, 0))
    prev = pl.BlockSpec((span, gw), lambda j: (jnp.maximum(j * bq - 1, 0), 0))
    out = jax.ShapeDtypeStruct((t, gw), F32)
    return pl.pallas_call(
        functools.partial(_band_attn_body, bq=bq, span=span, nb_class=nb_class),
        out_shape=[out, out], grid=(t // (bq * span),), in_specs=[cur, cur, prev, cur, prev],
        out_specs=[cur, cur], compiler_params=_cparams("parallel"), name="band_attn")(q, k, k, v, v)


def _cache_attn_body(q_ref, kn_ref, vn_ref, c_ref, o_ref, lse_ref, cout_ref, k_scr, v_scr, *, w, d, window, ns, gw):
    pad = 16
    nh = gw // HEAD
    kn, vn = kn_ref[0], vn_ref[0]
    zpad = jnp.zeros((pad - ns, gw), F32)
    k_scr[0:w, :] = c_ref[0, :, 0:gw].astype(BF16)
    v_scr[0:w, :] = c_ref[0, :, gw:2 * gw].astype(BF16)
    k_scr[w:w + pad, :] = jnp.concatenate([kn, zpad], axis=0).astype(BF16)
    v_scr[w:w + pad, :] = jnp.concatenate([vn, zpad], axis=0).astype(BF16)
    rp = 8
    q8 = jnp.concatenate([q_ref[0], jnp.zeros((rp - ns, gw), F32)], axis=0)
    qm = jnp.concatenate([q8] * nh, axis=0)
    rowh = lax.broadcasted_iota(jnp.int32, (nh * rp, gw), 0) >> 3
    laneh = lax.broadcasted_iota(jnp.int32, (nh * rp, gw), 1) >> 6
    own = rowh == laneh
    qm = jnp.where(own, qm, 0.0).astype(BF16)
    s = lax.dot_general(qm, k_scr[...], (((1,), (1,)), ((), ())), preferred_element_type=F32)
    srow = lax.broadcasted_iota(jnp.int32, s.shape, 0) & (rp - 1)
    col = lax.broadcasted_iota(jnp.int32, s.shape, 1)
    dist = w + srow - col
    valid = (dist >= 0) & (dist <= window) & ((dist & (d - 1)) == 0) & (srow < ns)
    valid = valid | ((srow >= ns) & (col == 0))
    s = jnp.where(valid, s, -jnp.inf)
    m = jnp.max(s, axis=-1, keepdims=True)
    p = jnp.exp(s - m)
    l = jnp.sum(p, axis=-1, keepdims=True)
    o = jnp.dot(p.astype(BF16), v_scr[...], preferred_element_type=F32) / l
    lse = jnp.broadcast_to(m + jnp.log(l), o.shape)
    o = jnp.where(own, o, 0.0)
    lse = jnp.where(own, lse, 0.0)
    o8 = o[0:rp]
    l8 = lse[0:rp]
    for h in range(1, nh):
        o8 = o8 + o[h * rp:(h + 1) * rp]
        l8 = l8 + lse[h * rp:(h + 1) * rp]
    o_ref[0] = o8[0:ns]
    lse_ref[0] = l8[0:ns]
    cout_ref[0, 0:w - ns, :] = c_ref[0, ns:w, :]
    cout_ref[0, w - ns:w, 0:gw] = kn
    cout_ref[0, w - ns:w, gw:2 * gw] = vn


def _cache_attn(q, kn, vn, cache, *, d, window):
    nbat, ns, gw = q.shape
    w = cache.shape[1]
    new = pl.BlockSpec((1, ns, gw), lambda b: (b, 0, 0))
    cb = pl.BlockSpec((1, w, 2 * gw), lambda b: (b, 0, 0))
    out = jax.ShapeDtypeStruct((nbat, ns, gw), F32)
    return pl.pallas_call(
        functools.partial(_cache_attn_body, w=w, d=d, window=window, ns=ns, gw=gw),
        out_shape=[out, out, jax.ShapeDtypeStruct(cache.shape, F32)], grid=(nbat,),
        in_specs=[new, new, new, cb], out_specs=[new, new, cb],
        scratch_shapes=[pltpu.VMEM((w + 16, gw), BF16), pltpu.VMEM((w + 16, gw), BF16)],
        compiler_params=_cparams("parallel"), name="cache_attn")(q, kn, vn, cache)


def _odd_back_body(*refs, tm, dils, alpha):
    ng = len(dils)
    o_refs, l_refs = refs[0:ng], refs[ng:2 * ng]
    gate_ref, x_ref, w_ref, g_ref, b_ref, y_ref = refs[2 * ng:2 * ng + 6]
    scr = refs[2 * ng + 6:]
    os_, ls_ = [], []
    for g, d in enumerate(dils):
        if d == 1:
            os_.append(o_refs[g][0])
            ls_.append(l_refs[g][0])
        else:
            n = tm // d
            nlb, _, lt = scr[2 * g].shape
            for c in range(d):
                for lb in range(nlb):
                    scr[2 * g][lb, pl.ds(c, n, stride=d), :] = o_refs[g][c, :, lb * lt:(lb + 1) * lt]
                    scr[2 * g + 1][lb, pl.ds(c, n, stride=d), :] = l_refs[g][c, :, lb * lt:(lb + 1) * lt]
            os_.append(jnp.concatenate([scr[2 * g][lb] for lb in range(nlb)], axis=1))
            ls_.append(jnp.concatenate([scr[2 * g + 1][lb] for lb in range(nlb)], axis=1))
    m = ls_[0]
    for l in ls_[1:]:
        m = jnp.maximum(m, l)
    es = [jnp.exp(l - m) for l in ls_]
    den = es[0]
    for e in es[1:]:
        den = den + e
    o = es[0] * os_[0]
    for e, og in zip(es[1:], os_[1:]):
        o = o + e * og
    o = o / den
    y = _dot(o * gate_ref[...], w_ref[...])
    y_ref[...] = _layer_norm(alpha * x_ref[...] + y, g_ref[...], b_ref[...], LN_EPS)


def _odd_back(os_, ls_, gate, x, w_out, ln_g, ln_b, *, dils, alpha, tm):
    t, d_model = x.shape
    gw = gate.shape[-1]
    row = lambda i: (i, 0)
    fix = lambda i: (0, 0)
    cls_specs = [pl.BlockSpec((d, tm // d, gw), lambda i: (0, i, 0)) for d in dils]
    in_specs = cls_specs * 2 + [pl.BlockSpec((tm, gw), row), pl.BlockSpec((tm, d_model), row),
                                pl.BlockSpec((gw, d_model), fix), pl.BlockSpec((1, d_model), fix),
                                pl.BlockSpec((1, d_model), fix)]
    return pl.pallas_call(
        functools.partial(_odd_back_body, tm=tm, dils=dils, alpha=alpha),
        out_shape=jax.ShapeDtypeStruct((t, d_model), F32), grid=(t // tm,), in_specs=in_specs,
        out_specs=pl.BlockSpec((tm, d_model), row),
        scratch_shapes=[pltpu.VMEM((gw // 128, tm, 128), F32)] * (2 * len(dils)),
        compiler_params=_cparams("parallel"), name="odd_back")(*os_, *ls_, gate, x, w_out, ln_g, ln_b)


def kernel(x_prompt, x_sample, state_conv, state_shift, state_wkv, cache_kv_w128, cache_kv_w512, cache_kv_w2048,
           w_in_even, conv_w, conv_b, conv_ln_g, conv_ln_b, mu_shift, w0, w2, a0, a2, k_k, k_a, r_k, lnx_g, lnx_b,
           w_out_even, w_in_odd, w_out_odd, ln_g, ln_b):
    nbp, seq, d_model = x_prompt.shape
    nbs, ns, _ = x_sample.shape
    assert nbp == 1
    depth = ln_g.shape[0]
    assert depth == 2 and w_in_even.shape[0] == 1 and w_in_odd.shape[0] == 1
    alpha = (2.0 * depth) ** 0.25
    bw = w0.shape[-1]
    lora_d, lora_a = w2.shape[1], a2.shape[1]
    nh = bw // HEAD
    nhist = CONV_W - 1
    caches = (cache_kv_w128, cache_kv_w512, cache_kv_w2048)
    dils = tuple(d for _, d in C_GROUPS)
    gw = w_out_odd.shape[1]
    ng = len(C_GROUPS)

    wi = w_in_even[0]
    sh_cols = 3 * bw + lora_d + lora_a
    zpad = jnp.zeros((d_model, LORA_PAD - lora_d - lora_a), F32)
    w_re = jnp.concatenate([wi[:, :sh_cols], zpad, wi[:, sh_cols:]], axis=1).astype(BF16)
    mu_re = jnp.concatenate([mu_shift[0], jnp.zeros((LORA_PAD - lora_d - lora_a,), F32)])[None, :]
    w2p = jnp.zeros((LORA_PAD, bw), F32).at[0:lora_d].set(w2[0])
    a2p = jnp.zeros((LORA_PAD, bw), F32).at[lora_d:lora_d + lora_a].set(a2[0])
    cw = jnp.concatenate([conv_w[0], jnp.zeros((CONV_HIST - CONV_W, bw), F32)], axis=0)
    vec = lambda z: z.reshape(1, -1)
    small = (vec(conv_b[0]), vec(conv_ln_g[0]), vec(conv_ln_b[0]), vec(w0[0]), w2p, vec(a0[0]), a2p,
             vec(k_k[0]), vec(k_a[0]), vec(r_k[0]))
    w_out_e = w_out_even[0].astype(BF16)
    w_in_o = w_in_odd[0].astype(BF16)
    w_out_o = w_out_odd[0].astype(BF16)
    lng = [vec(ln_g[l]) for l in range(depth)]
    lnb = [vec(ln_b[l]) for l in range(depth)]

    xp = x_prompt[0]
    ya, bg, r, lw, km, v, kkn, ab, bonus, conv_tail = _even_front_prompt(xp, w_re, mu_re, cw, small)
    o_wkv, wkv_p = _wkv_prompt(r, lw, km, v, kkn, ab, jnp.zeros((nh, HEAD, HEAD), F32))
    xp1 = _even_back(o_wkv, bonus, bg, ya, xp, vec(lnx_g[0]), vec(lnx_b[0]), w_out_e, lng[0], lnb[0],
                     alpha=alpha, tm=256)
    conv_p = conv_tail[CONV_HIST - nhist:][None, None]
    shift_p = x_prompt[:, -1][None]

    xs_t = jnp.transpose(x_sample, (1, 0, 2)).reshape(ns * nbs, d_model)
    x_in = jnp.concatenate([state_shift[0], xs_t], axis=0)
    hist = jnp.transpose(state_conv[0], (1, 0, 2)).reshape(nhist * nbs, bw)
    res = _even_front_sample(x_in, hist, w_re, mu_re, cw, small, nb=nbs, ns=ns)
    ya_s, bg_s, bon_s, conv_s_t = res[0], res[1], res[8], res[9]
    blk = 8

    def seq_major_pad(z):
        z = jnp.transpose(z.reshape(ns, nbs, bw), (1, 0, 2))
        return jnp.pad(z, ((0, 0), (0, blk - ns), (0, 0))).reshape(nbs * blk, bw)

    wkv_in = [seq_major_pad(z) for z in res[2:8]]
    o_s_pad, wkv_s = _wkv_sample(*wkv_in, state_wkv[0], blk=blk)
    o_s = jnp.transpose(o_s_pad.reshape(nbs, blk, bw)[:, :ns], (1, 0, 2)).reshape(ns * nbs, bw)
    xs1_t = _even_back(o_s, bon_s, bg_s, ya_s, xs_t, vec(lnx_g[0]), vec(lnx_b[0]), w_out_e, lng[0], lnb[0],
                       alpha=alpha, tm=ns * nbs)
    conv_s = jnp.transpose(conv_s_t.reshape(nhist, nbs, bw), (1, 0, 2))[None]
    shift_s = x_sample[:, -1][None]
    xs1 = jnp.transpose(xs1_t.reshape(ns, nbs, d_model), (1, 0, 2)).reshape(nbs * ns, d_model)

    tm1 = 256
    qc, kc, vc, gate_p, kf, vf = _odd_front(xp1, w_in_o, tm=tm1, dils=dils, gw=gw)
    os_, ls_ = [], []
    for g, (win, d) in enumerate(C_GROUPS):
        span = win // d
        flat = lambda z: z.reshape(seq, gw)
        o_g, l_g = _band_attn(flat(qc[g]), flat(kc[g]), flat(vc[g]), span=span, nb_class=(seq // d) // span)
        os_.append(o_g.reshape(d, seq // d, gw))
        ls_.append(l_g.reshape(d, seq // d, gw))
    xp2 = _odd_back(os_, ls_, gate_p, xp1, w_out_o, lng[1], lnb[1], dils=dils, alpha=alpha, tm=tm1)
    kv_p = []
    for g, (win, d) in enumerate(C_GROUPS):
        n = min(win, seq)
        kg = kf[seq - n:, g * gw:(g + 1) * gw].reshape(n, gw // HEAD, HEAD)
        vg = vf[seq - n:, g * gw:(g + 1) * gw].reshape(n, gw // HEAD, HEAD)
        kv_p.append(jnp.stack([kg, vg], axis=1)[None, None])

    ones = (1,) * ng
    qs, _, _, gate_s, kf_s, vf_s = _odd_front(xs1, w_in_o, tm=nbs * ns, dils=ones, gw=gw)
    os_s, ls_s, kv_s_out = [], [], []
    for g, (win, d) in enumerate(C_GROUPS):
        cache = caches[g][0]
        w = cache.shape[1]
        q_g = qs[g].astype(F32).reshape(nbs, ns, gw)
        kn = kf_s[:, g * gw:(g + 1) * gw].reshape(nbs, ns, gw)
        vn = vf_s[:, g * gw:(g + 1) * gw].reshape(nbs, ns, gw)
        o_g, l_g, cnew = _cache_attn(q_g, kn, vn, cache.reshape(nbs, w, 2 * gw), d=d, window=win)
        os_s.append(o_g.reshape(1, nbs * ns, gw))
        ls_s.append(l_g.reshape(1, nbs * ns, gw))
        kv_s_out.append(cnew.reshape(nbs, w, 2, gw // HEAD, HEAD)[None])
    xs2 = _odd_back(os_s, ls_s, gate_s, xs1, w_out_o, lng[1], lnb[1], dils=ones, alpha=alpha, tm=nbs * ns)

    return (xp2[None], xs2.reshape(nbs, ns, d_model), conv_p, conv_s, shift_p, shift_s,
            wkv_p[None, None], wkv_s[None],
            kv_p[0], kv_s_out[0], kv_p[1], kv_s_out[1], kv_p[2], kv_s_out[2])
```

```python
import functools

import jax
import jax.numpy as jnp
from jax import lax
from jax.experimental import pallas as pl
from jax.experimental.pallas import tpu as pltpu

F32 = jnp.float32
BF16 = jnp.bfloat16

LN_EPS = 1e-5
LNX_EPS = 64e-5
HEAD = 64
LORA_PAD = 128
CONV_W = 31
CONV_HIST = 32
C_GROUPS = ((128, 1), (512, 4), (2048, 16))
WKV_CHUNK = 64
VMEM_LIMIT = 56 * 1024 * 1024


def _cparams(*sem):
    return pltpu.CompilerParams(dimension_semantics=sem or None, vmem_limit_bytes=VMEM_LIMIT)


def _dot(a, b):
    return jnp.dot(a.astype(BF16), b.astype(BF16), preferred_element_type=F32)


def _dot_nt(a, b):
    return lax.dot_general(a.astype(BF16), b.astype(BF16), (((1,), (1,)), ((), ())), preferred_element_type=F32)


def _dot_tn(a, b):
    return lax.dot_general(a.astype(BF16), b.astype(BF16), (((0,), (0,)), ((), ())), preferred_element_type=F32)


def _split_dot(a, ones_bf16):
    ah = a.astype(BF16)
    al = (a - ah.astype(F32)).astype(BF16)
    return (jnp.dot(ah, ones_bf16, preferred_element_type=F32) +
            jnp.dot(al, ones_bf16, preferred_element_type=F32))


def _head_ones(n):
    r = lax.broadcasted_iota(jnp.int32, (n, n), 0) >> 6
    c = lax.broadcasted_iota(jnp.int32, (n, n), 1) >> 6
    return jnp.where(r == c, 1.0, 0.0).astype(BF16)


def _sigmoid(x):
    return 1.0 / (1.0 + jnp.exp(-x))


def _silu(x):
    return x * _sigmoid(x)


def _softplus(y):
    return jnp.maximum(y, 0.0) + jnp.log(1.0 + jnp.exp(-jnp.abs(y)))


def _layer_norm(x, g, b, eps):
    mu = jnp.mean(x, axis=-1, keepdims=True)
    xc = x - mu
    var = jnp.mean(xc * xc, axis=-1, keepdims=True)
    return xc * lax.rsqrt(var + eps) * g + b


def _even_mix(sh, b_width, prm, hsum):
    w0, w2p, a0, a2p, k_k, k_a, r_k = prm
    r = sh[:, 0:b_width]
    k = sh[:, b_width:2 * b_width]
    v = sh[:, 2 * b_width:3 * b_width]
    lora = sh[:, 3 * b_width:3 * b_width + LORA_PAD]
    w_log = -_softplus(-(w0 + _dot(jnp.tanh(lora), w2p))) - 0.5
    lw = -jnp.exp(w_log)
    a = _sigmoid(a0 + _dot(lora, a2p))
    kk = k * k_k
    nrm = jnp.sqrt(_split_dot(kk * kk, hsum))
    kkn = kk / jnp.maximum(nrm, 1e-12)
    km = k * (1.0 + (a - 1.0) * k_a)
    bonus = _split_dot(r * km * r_k, hsum) * v
    return r, lw, km, v, kkn, kkn * a, bonus


def _conv_ln_gate(u_ref, sh_ref, row0, n, cw_ref, cb, clg, clb, a_gate):
    acc = jnp.zeros((n, cb.shape[-1]), F32) + cb
    for j in range(CONV_W):
        a, b = divmod(CONV_HIST - (CONV_W - 1) + j, 8)
        win = u_ref[pl.ds(row0 + 8 * a, n), :] if b == 0 else sh_ref[b - 1, pl.ds(row0 + 8 * a, n), :]
        acc = acc + cw_ref[j:j + 1, :] * win
    return _silu(_layer_norm(acc, clg, clb, LN_EPS)) * _silu(a_gate)


def _even_front_prompt_body(x_ref, w_ref, mu_ref, cw_ref, cb_ref, clg_ref, clb_ref, w0_ref, w2_ref, a0_ref, a2_ref,
                            kk_ref, ka_ref, rk_ref,
                            ya_ref, bg_ref, r_ref, lw_ref, km_ref, v_ref, kkn_ref, ab_ref, bon_ref, conv_ref,
                            p_scr, u_scr, sh_scr, *, tm, bw, sub):
    i = pl.program_id(0)
    shw = 3 * bw + LORA_PAD

    @pl.when(i == 0)
    def _():
        p_scr[0:8, :] = jnp.zeros((8, shw), F32)
        u_scr[0:CONV_HIST, :] = jnp.zeros((CONV_HIST, bw), F32)

    p = jnp.dot(x_ref[...].astype(BF16), w_ref[...], preferred_element_type=F32)
    cur = p[:, :shw]
    p_scr[8:8 + tm, :] = cur
    prev = p_scr[pl.ds(7, tm), :]
    sh = cur + (prev - cur) * mu_ref[...]
    p_scr[7:8, :] = cur[tm - 1:tm, :]

    hsum = _head_ones(bw)
    prm = (w0_ref[...], w2_ref[...], a0_ref[...], a2_ref[...], kk_ref[...], ka_ref[...], rk_ref[...])
    r, lw, km, v, kkn, ab, bonus = _even_mix(sh, bw, prm, hsum)
    r_ref[...] = r
    lw_ref[...] = lw
    km_ref[...] = km
    v_ref[...] = v
    kkn_ref[...] = kkn
    ab_ref[...] = ab
    bon_ref[...] = bonus

    a_val = p[:, shw:shw + bw]
    a_glu = p[:, shw + bw:shw + 2 * bw]
    a_gate = p[:, shw + 2 * bw:shw + 3 * bw]
    b_gate = p[:, shw + 3 * bw:shw + 4 * bw]
    bg_ref[...] = _silu(b_gate)
    u_scr[CONV_HIST:CONV_HIST + tm, :] = a_val * _sigmoid(a_glu)
    for b in range(1, 8):
        sh_scr[b - 1] = u_scr[pl.ds(b, sh_scr.shape[1]), :]
    cb, clg, clb = cb_ref[...], clg_ref[...], clb_ref[...]
    for s in range(tm // sub):
        ya_ref[s * sub:(s + 1) * sub, :] = _conv_ln_gate(u_scr, sh_scr, s * sub, sub, cw_ref, cb, clg, clb,
                                                         a_gate[s * sub:(s + 1) * sub, :])
    tail = u_scr[tm:tm + CONV_HIST, :]
    conv_ref[...] = tail
    u_scr[0:CONV_HIST, :] = tail


def _even_front_prompt(x, w_re, mu_re, cw, small, *, tm=256):
    t, d = x.shape
    bw = small[0].shape[-1]
    shw = 3 * bw + LORA_PAD
    ncol = w_re.shape[1]
    row = lambda i: (i, 0)
    fix = lambda i: (0, 0)
    vec = pl.BlockSpec((1, bw), fix)
    in_specs = [pl.BlockSpec((tm, d), row), pl.BlockSpec((d, ncol), fix), pl.BlockSpec((1, shw), fix),
                pl.BlockSpec((CONV_HIST, bw), fix), vec, vec, vec, vec, pl.BlockSpec((LORA_PAD, bw), fix), vec,
                pl.BlockSpec((LORA_PAD, bw), fix), vec, vec, vec]
    out = jax.ShapeDtypeStruct((t, bw), F32)
    outs = [out] * 9 + [jax.ShapeDtypeStruct((CONV_HIST, bw), F32)]
    out_specs = [pl.BlockSpec((tm, bw), row)] * 9 + [pl.BlockSpec((CONV_HIST, bw), fix)]
    body = functools.partial(_even_front_prompt_body, tm=tm, bw=bw, sub=32)
    return pl.pallas_call(
        body, out_shape=outs, grid=(t // tm,), in_specs=in_specs, out_specs=out_specs,
        scratch_shapes=[pltpu.VMEM((8 + tm, shw), F32), pltpu.VMEM((CONV_HIST + tm, bw), F32),
                        pltpu.VMEM((7, CONV_HIST - 8 + tm, bw), F32)],
        compiler_params=_cparams("arbitrary"), name="even_front_prompt")(x, w_re, mu_re, cw, *small)


def _even_front_sample_body(x_ref, hist_ref, w_ref, mu_ref, cw_ref, cb_ref, clg_ref, clb_ref, w0_ref, w2_ref, a0_ref,
                            a2_ref, kk_ref, ka_ref, rk_ref,
                            ya_ref, bg_ref, r_ref, lw_ref, km_ref, v_ref, kkn_ref, ab_ref, bon_ref, conv_ref,
                            u_scr, *, nb, ns, bw):
    n = nb * ns
    shw = 3 * bw + LORA_PAD
    nh = CONV_W - 1
    p = jnp.dot(x_ref[...].astype(BF16), w_ref[...], preferred_element_type=F32)
    cur = p[nb:, :shw]
    prev = p[:n, :shw]
    sh = cur + (prev - cur) * mu_ref[...]
    hsum = _head_ones(bw)
    prm = (w0_ref[...], w2_ref[...], a0_ref[...], a2_ref[...], kk_ref[...], ka_ref[...], rk_ref[...])
    r, lw, km, v, kkn, ab, bonus = _even_mix(sh, bw, prm, hsum)
    r_ref[...] = r
    lw_ref[...] = lw
    km_ref[...] = km
    v_ref[...] = v
    kkn_ref[...] = kkn
    ab_ref[...] = ab
    bon_ref[...] = bonus
    rest = p[nb:, shw:]
    a_val, a_glu = rest[:, 0:bw], rest[:, bw:2 * bw]
    a_gate, b_gate = rest[:, 2 * bw:3 * bw], rest[:, 3 * bw:4 * bw]
    bg_ref[...] = _silu(b_gate)
    u_scr[0:nh * nb, :] = hist_ref[...]
    u_scr[nh * nb:(nh + ns) * nb, :] = a_val * _sigmoid(a_glu)
    cb, clg, clb = cb_ref[...], clg_ref[...], clb_ref[...]
    for s in range(ns):
        acc = jnp.zeros((nb, bw), F32) + cb
        for j in range(CONV_W):
            acc = acc + cw_ref[j:j + 1, :] * u_scr[(s + j) * nb:(s + j + 1) * nb, :]
        ya_ref[s * nb:(s + 1) * nb, :] = (_silu(_layer_norm(acc, clg, clb, LN_EPS)) *
                                           _silu(a_gate[s * nb:(s + 1) * nb, :]))
    conv_ref[...] = u_scr[ns * nb:(ns + nh) * nb, :]


def _even_front_sample(x_in, hist, w_re, mu_re, cw, small, *, nb, ns):
    bw = small[0].shape[-1]
    n = nb * ns
    nh = CONV_W - 1
    out = jax.ShapeDtypeStruct((n, bw), F32)
    outs = [out] * 9 + [jax.ShapeDtypeStruct((nh * nb, bw), F32)]
    body = functools.partial(_even_front_sample_body, nb=nb, ns=ns, bw=bw)
    return pl.pallas_call(
        body, out_shape=outs, scratch_shapes=[pltpu.VMEM(((nh + ns) * nb, bw), F32)],
        compiler_params=_cparams(), name="even_front_sample")(x_in, hist, w_re, mu_re, cw, *small)


def _wkv_block_terms(r, lw, km, v, kkn, ab, blk):
    n, width = r.shape
    nh = width // HEAD
    ri = lax.broadcasted_iota(jnp.int32, (n, n), 0)
    ci = lax.broadcasted_iota(jnp.int32, (n, n), 1)
    same = _same_block(ri, ci, blk)
    incl = (ri >= ci) & same
    strict = (ri > ci) & same
    cum = _split3_dot(jnp.where(incl, 1.0, 0.0).astype(BF16), lw)
    tot = _split3_dot(jnp.where(same, 1.0, 0.0).astype(BF16), lw)
    dn = jnp.exp(-cum)
    at = -kkn * jnp.exp(cum - lw)
    bt = ab * dn
    kt = km * dn
    rt = r * jnp.exp(cum)
    rem = jnp.exp(tot - cum)
    bc = ab * rem
    kc = km * rem
    wc = jnp.exp(tot)
    hs = [slice(h * HEAD, (h + 1) * HEAD) for h in range(nh)]
    gs = [_dot_nt(jnp.concatenate([at[:, sl], rt[:, sl]], axis=0), jnp.concatenate([bt[:, sl], kt[:, sl]], axis=0))
          for sl in hs]
    l_ab = [jnp.where(strict, g[:n, :n], 0.0) for g in gs]
    l_ak = [jnp.where(strict, g[:n, n:], 0.0) for g in gs]
    l_rb = [jnp.where(incl, g[n:, :n], 0.0) for g in gs]
    l_rk = [jnp.where(incl, g[n:, n:], 0.0) for g in gs]
    toff = _unit_lower_inverse_minus_eye(l_ab, ri, ci, blk)
    akv = [_dot(l, v[:, sl]) for l, sl in zip(l_ak, hs)]
    ahat = [at[:, sl] + _dot(t, at[:, sl]) for t, sl in zip(toff, hs)]
    uhat = [x + _dot(t, x) for t, x in zip(toff, akv)]
    qhat = [rt[:, sl] + _dot(l, a) for l, a, sl in zip(l_rb, ahat, hs)]
    ohat = [_dot(lb, u) + _dot(lk, v[:, sl]) for lb, lk, u, sl in zip(l_rb, l_rk, uhat, hs)]
    return list(zip(ahat, uhat, qhat, ohat)), bc, kc, wc


def _same_block(ri, ci, size):
    sh = size.bit_length() - 1
    return (ri >> sh) == (ci >> sh)


def _split(x):
    hi = x.astype(BF16)
    return hi, (x - hi.astype(F32)).astype(BF16)


def _split3_dot(ones_bf16, x):
    hi, lo = _split(x)
    lo2 = (x - hi.astype(F32) - lo.astype(F32)).astype(BF16)
    d = lambda y: jnp.dot(ones_bf16, y, preferred_element_type=F32)
    return d(hi) + (d(lo) + d(lo2))


def _unit_lower_inverse_minus_eye(ls, ri, ci, blk):
    base = 8
    eye = jnp.where(ri == ci, 1.0, 0.0)
    same8 = _same_block(ri, ci, base)
    l8 = [jnp.where(same8, l, 0.0).astype(BF16) for l in ls]
    x2 = [_dot(x, x).astype(BF16) for x in l8]
    x4 = [_dot(x, x).astype(BF16) for x in x2]
    p = [eye + jnp.where(same8, l, 0.0) for l in ls]
    p = [q + _dot(q, x) for q, x in zip(p, x2)]
    t = [q + _dot(q, x) for q, x in zip(p, x4)]
    size = 2 * base
    while size <= blk:
        half = size.bit_length() - 2
        lower_left = _same_block(ri, ci, size) & (((ri >> half) & 1) == 1) & (((ci >> half) & 1) == 0)
        tb = [x.astype(BF16) for x in t]
        mt = [_dot(jnp.where(lower_left, l, 0.0), x) for l, x in zip(ls, tb)]
        t = [x + _dot(xb, m) for x, xb, m in zip(t, tb, mt)]
        size *= 2
    return [x - eye for x in t]


def _wkv_prompt_body(r_ref, lw_ref, km_ref, v_ref, kkn_ref, ab_ref, s0_ref, o_ref, sout_ref, s_scr, *, c, nch):
    i = pl.program_id(0)

    @pl.when(i == 0)
    def _():
        s_scr[...] = s0_ref[...]

    v = v_ref[...]
    heads, bc, kc, wc = _wkv_block_terms(r_ref[...], lw_ref[...], km_ref[...], v, kkn_ref[...], ab_ref[...], c)
    nh = len(heads)
    hs = [slice(h * HEAD, (h + 1) * HEAD) for h in range(nh)]
    vk = [_dot_tn(v[ch * c:(ch + 1) * c, sl], kc[ch * c:(ch + 1) * c, sl]) for ch in range(nch) for sl in hs]
    s = [s_scr[h] for h in range(nh)]
    for ch in range(nch):
        rows = slice(ch * c, (ch + 1) * c)
        mm = [_dot_nt(jnp.concatenate([ahat[rows], qhat[rows]], axis=0), s[h])
              for h, (ahat, _, qhat, _) in enumerate(heads)]
        u = [mm[h][:c] + heads[h][1][rows] for h in range(nh)]
        for h in range(nh):
            o_ref[rows, hs[h]] = mm[h][c:] + heads[h][3][rows]
        s = [s[h] * wc[ch * c:ch * c + 1, hs[h]] + _dot_tn(u[h], bc[rows, hs[h]]) + vk[ch * nh + h]
             for h in range(nh)]
    for h in range(nh):
        s_scr[h] = s[h]

    @pl.when(i == pl.num_programs(0) - 1)
    def _():
        sout_ref[...] = s_scr[...]


def _wkv_prompt(r, lw, km, v, kkn, ab, s0, *, nch=2):
    t, bw = r.shape
    c = WKV_CHUNK
    nh = bw // HEAD
    row = pl.BlockSpec((nch * c, bw), lambda i: (i, 0))
    st = pl.BlockSpec((nh, HEAD, HEAD), lambda i: (0, 0, 0))
    return pl.pallas_call(
        functools.partial(_wkv_prompt_body, c=c, nch=nch),
        out_shape=[jax.ShapeDtypeStruct((t, bw), F32), jax.ShapeDtypeStruct((nh, HEAD, HEAD), F32)],
        grid=(t // (nch * c),), in_specs=[row] * 6 + [st], out_specs=[row, st],
        scratch_shapes=[pltpu.VMEM((nh, HEAD, HEAD), F32)],
        compiler_params=_cparams("arbitrary"), name="wkv_prompt")(r, lw, km, v, kkn, ab, s0)


def _wkv_sample_body(r_ref, lw_ref, km_ref, v_ref, kkn_ref, ab_ref, s0_ref, o_ref, sout_ref,
                     ah_scr, uh_scr, qh_scr, oh_scr, bc_scr, kc_scr, wc_scr, *, nb, blk):
    v = v_ref[...]
    heads, bc, kc, wc = _wkv_block_terms(r_ref[...], lw_ref[...], km_ref[...], v, kkn_ref[...], ab_ref[...], blk)
    bc_scr[...] = bc
    kc_scr[...] = kc
    wc_scr[...] = wc
    for h, (ahat, uhat, qhat, ohat) in enumerate(heads):
        sl = slice(h * HEAD, (h + 1) * HEAD)
        ah_scr[:, sl] = ahat
        uh_scr[:, sl] = uhat
        qh_scr[:, sl] = qhat
        oh_scr[:, sl] = ohat
    nh = len(heads)

    def per_seq(b, carry):
        rows = pl.ds(pl.multiple_of(b * blk, blk), blk)
        for h in range(nh):
            sl = slice(h * HEAD, (h + 1) * HEAD)
            s = s0_ref[b, h]
            mm = _dot_nt(jnp.concatenate([ah_scr[rows, sl], qh_scr[rows, sl]], axis=0), s)
            u = mm[:blk] + uh_scr[rows, sl]
            o_ref[rows, sl] = mm[blk:] + oh_scr[rows, sl]
            wrow = wc_scr[rows, sl][0:1, :]
            sout_ref[b, h] = s * wrow + _dot_tn(u, bc_scr[rows, sl]) + _dot_tn(v_ref[rows, sl], kc_scr[rows, sl])
        return carry

    lax.fori_loop(0, nb, per_seq, 0)


def _wkv_sample(r, lw, km, v, kkn, ab, s0, *, blk):
    n, bw = r.shape
    nb = n // blk
    scr = pltpu.VMEM((n, bw), F32)
    return pl.pallas_call(
        functools.partial(_wkv_sample_body, nb=nb, blk=blk),
        out_shape=[jax.ShapeDtypeStruct((n, bw), F32), jax.ShapeDtypeStruct(s0.shape, F32)],
        scratch_shapes=[scr] * 7, compiler_params=_cparams(), name="wkv_sample")(r, lw, km, v, kkn, ab, s0)


def _even_back_body(o_ref, bon_ref, bg_ref, ya_ref, x_ref, lg_ref, lb_ref, w_ref, g_ref, b_ref, y_ref, *, alpha, bw):
    hsum = _head_ones(bw)
    o = o_ref[...]
    om = _split_dot(o, hsum) * (1.0 / HEAD)
    oc = o - om
    ov = _split_dot(oc * oc, hsum) * (1.0 / HEAD)
    on = oc * lax.rsqrt(ov + LNX_EPS) * lg_ref[...] + lb_ref[...]
    yb = (on + bon_ref[...]) * bg_ref[...]
    y = _dot(ya_ref[...], w_ref[0:bw, :]) + _dot(yb, w_ref[bw:2 * bw, :])
    y_ref[...] = _layer_norm(alpha * x_ref[...] + y, g_ref[...], b_ref[...], LN_EPS)


def _even_back(o, bonus, bg, ya, x, lnx_g, lnx_b, w_out, ln_g, ln_b, *, alpha, tm):
    t, d = x.shape
    bw = o.shape[-1]
    row = lambda i: (i, 0)
    fix = lambda i: (0, 0)
    half = pl.BlockSpec((tm, bw), row)
    in_specs = [half] * 4 + [pl.BlockSpec((tm, d), row), pl.BlockSpec((1, bw), fix), pl.BlockSpec((1, bw), fix),
                             pl.BlockSpec((2 * bw, d), fix), pl.BlockSpec((1, d), fix), pl.BlockSpec((1, d), fix)]
    return pl.pallas_call(
        functools.partial(_even_back_body, alpha=alpha, bw=bw),
        out_shape=jax.ShapeDtypeStruct((t, d), F32), grid=(t // tm,), in_specs=in_specs,
        out_specs=pl.BlockSpec((tm, d), row), compiler_params=_cparams("parallel"),
        name="even_back")(o, bonus, bg, ya, x, lnx_g, lnx_b, w_out, ln_g, ln_b)


def _odd_front_body(x_ref, w_ref, *refs, tm, gw, dils, scale):
    ng = len(dils)
    q_refs, k_refs, v_refs = refs[0:ng], refs[ng:2 * ng], refs[2 * ng:3 * ng]
    gate_ref, kf_ref, vf_ref, p_scr = refs[3 * ng:3 * ng + 4]
    p = jnp.dot(x_ref[...].astype(BF16), w_ref[...], preferred_element_type=F32)
    qkv = 3 * ng * gw
    lt = p_scr.shape[-1]
    for cb in range(qkv // lt):
        p_scr[cb] = p[:, cb * lt:(cb + 1) * lt]
    gate_ref[...] = _silu(p[:, qkv:qkv + gw])
    kf_ref[...] = p[:, ng * gw:2 * ng * gw]
    vf_ref[...] = p[:, 2 * ng * gw:3 * ng * gw]
    for g, d in enumerate(dils):
        n = tm // d
        for c in range(d):
            rows = pl.ds(c, n, stride=d) if d > 1 else pl.ds(0, n)
            for lb in range(gw // lt):
                ls = slice(lb * lt, (lb + 1) * lt)
                q_refs[g][c, :, ls] = (p_scr[(g * gw) // lt + lb, rows, :] * scale).astype(BF16)
                k_refs[g][c, :, ls] = p_scr[((ng + g) * gw) // lt + lb, rows, :].astype(BF16)
                v_refs[g][c, :, ls] = p_scr[((2 * ng + g) * gw) // lt + lb, rows, :].astype(BF16)


def _odd_front(x, w, *, tm, dils, gw):
    t, d_model = x.shape
    ng = len(dils)
    ncol = w.shape[1]
    row = lambda i: (i, 0)
    cls_specs = [pl.BlockSpec((d, tm // d, gw), lambda i: (0, i, 0)) for d in dils]
    cls_shapes = [jax.ShapeDtypeStruct((d, t // d, gw), BF16) for d in dils]
    outs = cls_shapes * 3 + [jax.ShapeDtypeStruct((t, gw), F32), jax.ShapeDtypeStruct((t, ng * gw), F32),
                             jax.ShapeDtypeStruct((t, ng * gw), F32)]
    out_specs = cls_specs * 3 + [pl.BlockSpec((tm, gw), row), pl.BlockSpec((tm, ng * gw), row),
                                 pl.BlockSpec((tm, ng * gw), row)]
    body = functools.partial(_odd_front_body, tm=tm, gw=gw, dils=dils, scale=HEAD ** -0.5)
    res = pl.pallas_call(
        body, out_shape=outs, grid=(t // tm,),
        in_specs=[pl.BlockSpec((tm, d_model), row), pl.BlockSpec((d_model, ncol), lambda i: (0, 0))],
        out_specs=out_specs, scratch_shapes=[pltpu.VMEM((3 * ng * gw // 128, tm, 128), F32)],
        compiler_params=_cparams("parallel"), name="odd_front")(x, w)
    return res[0:ng], res[ng:2 * ng], res[2 * ng:3 * ng], res[3 * ng], res[3 * ng + 1], res[3 * ng + 2]


def _band_attn_body(q_ref, kc_ref, kp_ref, vc_ref, vp_ref, o_ref, lse_ref, *, bq, span, nb_class):
    j = pl.program_id(0)
    first_lo = jnp.where(((j * bq) % nb_class) == 0, span, 0)
    qi = lax.broadcasted_iota(jnp.int32, (span, 2 * span), 0)
    ki = lax.broadcasted_iota(jnp.int32, (span, 2 * span), 1)
    in_band = (ki >= qi) & (ki <= qi + span)
    lane = lax.broadcasted_iota(jnp.int32, (span, 2 * HEAD), 1)
    left = lane < HEAD
    gw = q_ref.shape[-1]
    for b in range(bq):
        mask = in_band & (ki >= first_lo) if b == 0 else in_band
        for pr in range(gw // (2 * HEAD)):
            ls = slice(pr * 2 * HEAD, (pr + 1) * 2 * HEAD)
            q2 = q_ref[b * span:(b + 1) * span, ls]
            if b == 0:
                k2 = jnp.concatenate([kp_ref[:, ls], kc_ref[0:span, ls]], axis=0)
                v2 = jnp.concatenate([vp_ref[:, ls], vc_ref[0:span, ls]], axis=0)
            else:
                k2 = kc_ref[(b - 1) * span:(b + 1) * span, ls]
                v2 = vc_ref[(b - 1) * span:(b + 1) * span, ls]
            outs, lses = [], []
            for sel in (left, jnp.logical_not(left)):
                qh = jnp.where(sel, q2, jnp.zeros_like(q2))
                s = lax.dot_general(qh, k2, (((1,), (1,)), ((), ())), preferred_element_type=F32)
                s = jnp.where(mask, s, -jnp.inf)
                m = jnp.max(s, axis=-1, keepdims=True)
                p = jnp.exp(s - m)
                l = jnp.sum(p, axis=-1, keepdims=True)
                outs.append(jnp.dot(p.astype(BF16), v2, preferred_element_type=F32) / l)
                lses.append(m + jnp.log(l))
            o_ref[b * span:(b + 1) * span, ls] = jnp.where(left, outs[0], outs[1])
            lse_ref[b * span:(b + 1) * span, ls] = jnp.where(left, lses[0], lses[1])


def _band_attn(q, k, v, *, span, nb_class, bq=4):
    t, gw = q.shape
    cur = pl.BlockSpec((bq * span, gw), lambda j: (j, 0))
    prev = pl.BlockSpec((span, gw), lambda j: (jnp.maximum(j * bq - 1, 0), 0))
    out = jax.ShapeDtypeStruct((t, gw), F32)
    return pl.pallas_call(
        functools.partial(_band_attn_body, bq=bq, span=span, nb_class=nb_class),
        out_shape=[out, out], grid=(t // (bq * span),), in_specs=[cur, cur, prev, cur, prev],
        out_specs=[cur, cur], compiler_params=_cparams("parallel"), name="band_attn")(q, k, k, v, v)


def _cache_attn_body(q_ref, kn_ref, vn_ref, nt_ref, c_ref, o_ref, lse_ref, cout_ref, *, w, d, window, ns, gw, rchunk):
    nh = gw // HEAD
    rp = 8
    zrow = jnp.zeros((rp - ns, gw), F32)
    q8 = jnp.concatenate([q_ref[0], zrow], axis=0)
    kn8 = jnp.concatenate([kn_ref[0], zrow], axis=0).astype(BF16)
    vn8 = jnp.concatenate([vn_ref[0], zrow], axis=0).astype(BF16)
    qm = jnp.concatenate([q8] * nh, axis=0)
    rowh = lax.broadcasted_iota(jnp.int32, (nh * rp, gw), 0) >> 3
    laneh = lax.broadcasted_iota(jnp.int32, (nh * rp, gw), 1) >> 6
    own = rowh == laneh
    qm = jnp.where(own, qm, 0.0).astype(BF16)
    s_c = jnp.dot(qm, c_ref[0, 0:gw, :].astype(BF16), preferred_element_type=F32)
    s_n = lax.dot_general(qm, kn8, (((1,), (1,)), ((), ())), preferred_element_type=F32)
    srow = lax.broadcasted_iota(jnp.int32, s_c.shape, 0) & (rp - 1)
    dist = w + srow - lax.broadcasted_iota(jnp.int32, s_c.shape, 1)
    valid_c = (dist <= window) & ((dist & (d - 1)) == 0) & (srow < ns)
    srow_n = lax.broadcasted_iota(jnp.int32, s_n.shape, 0) & (rp - 1)
    col_n = lax.broadcasted_iota(jnp.int32, s_n.shape, 1)
    dist_n = srow_n - col_n
    valid_n = (dist_n >= 0) & ((dist_n & (d - 1)) == 0) & (srow_n < ns)
    valid_n = valid_n | ((srow_n >= ns) & (col_n == 0))
    s_c = jnp.where(valid_c, s_c, -jnp.inf)
    s_n = jnp.where(valid_n, s_n, -jnp.inf)
    m = jnp.maximum(jnp.max(s_c, axis=-1, keepdims=True), jnp.max(s_n, axis=-1, keepdims=True))
    p_c = jnp.exp(s_c - m)
    p_n = jnp.exp(s_n - m)
    l = jnp.sum(p_c, axis=-1, keepdims=True) + jnp.sum(p_n, axis=-1, keepdims=True)
    o = lax.dot_general(p_c.astype(BF16), c_ref[0, gw:2 * gw, :].astype(BF16), (((1,), (1,)), ((), ())),
                        preferred_element_type=F32)
    o = (o + jnp.dot(p_n.astype(BF16), vn8, preferred_element_type=F32)) / l
    lse = jnp.broadcast_to(m + jnp.log(l), o.shape)
    o = jnp.where(own, o, 0.0)
    lse = jnp.where(own, lse, 0.0)
    o8 = o[0:rp]
    l8 = lse[0:rp]
    for h in range(1, nh):
        o8 = o8 + o[h * rp:(h + 1) * rp]
        l8 = l8 + lse[h * rp:(h + 1) * rp]
    o_ref[0] = o8[0:ns]
    lse_ref[0] = l8[0:ns]
    for rb in range(2 * gw // rchunk):
        rows = slice(rb * rchunk, (rb + 1) * rchunk)
        cout_ref[0, rows, :] = jnp.concatenate([c_ref[0, rows, ns:w], nt_ref[0, rows, 0:ns]], axis=1)


def _cache_attn(q, kn, vn, new_t, cache_t, *, d, window):
    nbat, ns, gw = q.shape
    w = cache_t.shape[2]
    new = pl.BlockSpec((1, ns, gw), lambda b: (b, 0, 0))
    ntb = pl.BlockSpec((1, 2 * gw, new_t.shape[2]), lambda b: (b, 0, 0))
    cb = pl.BlockSpec((1, 2 * gw, w), lambda b: (b, 0, 0))
    out = jax.ShapeDtypeStruct((nbat, ns, gw), F32)
    return pl.pallas_call(
        functools.partial(_cache_attn_body, w=w, d=d, window=window, ns=ns, gw=gw, rchunk=64),
        out_shape=[out, out, jax.ShapeDtypeStruct(cache_t.shape, F32)], grid=(nbat,),
        in_specs=[new, new, new, ntb, cb], out_specs=[new, new, cb],
        compiler_params=_cparams("parallel"), name="cache_attn")(q, kn, vn, new_t, cache_t)


def _odd_back_body(*refs, tm, dils, alpha):
    ng = len(dils)
    o_refs, l_refs = refs[0:ng], refs[ng:2 * ng]
    gate_ref, x_ref, w_ref, g_ref, b_ref, y_ref = refs[2 * ng:2 * ng + 6]
    scr = refs[2 * ng + 6:]
    os_, ls_ = [], []
    for g, d in enumerate(dils):
        if d == 1:
            os_.append(o_refs[g][0])
            ls_.append(l_refs[g][0])
        else:
            n = tm // d
            nlb, _, lt = scr[2 * g].shape
            for c in range(d):
                for lb in range(nlb):
                    scr[2 * g][lb, pl.ds(c, n, stride=d), :] = o_refs[g][c, :, lb * lt:(lb + 1) * lt]
                    scr[2 * g + 1][lb, pl.ds(c, n, stride=d), :] = l_refs[g][c, :, lb * lt:(lb + 1) * lt]
            os_.append(jnp.concatenate([scr[2 * g][lb] for lb in range(nlb)], axis=1))
            ls_.append(jnp.concatenate([scr[2 * g + 1][lb] for lb in range(nlb)], axis=1))
    m = ls_[0]
    for l in ls_[1:]:
        m = jnp.maximum(m, l)
    es = [jnp.exp(l - m) for l in ls_]
    den = es[0]
    for e in es[1:]:
        den = den + e
    o = es[0] * os_[0]
    for e, og in zip(es[1:], os_[1:]):
        o = o + e * og
    o = o / den
    y = _dot(o * gate_ref[...], w_ref[...])
    y_ref[...] = _layer_norm(alpha * x_ref[...] + y, g_ref[...], b_ref[...], LN_EPS)


def _odd_back(os_, ls_, gate, x, w_out, ln_g, ln_b, *, dils, alpha, tm):
    t, d_model = x.shape
    gw = gate.shape[-1]
    row = lambda i: (i, 0)
    fix = lambda i: (0, 0)
    cls_specs = [pl.BlockSpec((d, tm // d, gw), lambda i: (0, i, 0)) for d in dils]
    in_specs = cls_specs * 2 + [pl.BlockSpec((tm, gw), row), pl.BlockSpec((tm, d_model), row),
                                pl.BlockSpec((gw, d_model), fix), pl.BlockSpec((1, d_model), fix),
                                pl.BlockSpec((1, d_model), fix)]
    return pl.pallas_call(
        functools.partial(_odd_back_body, tm=tm, dils=dils, alpha=alpha),
        out_shape=jax.ShapeDtypeStruct((t, d_model), F32), grid=(t // tm,), in_specs=in_specs,
        out_specs=pl.BlockSpec((tm, d_model), row),
        scratch_shapes=[pltpu.VMEM((gw // 128, tm, 128), F32)] * (2 * len(dils)),
        compiler_params=_cparams("parallel"), name="odd_back")(*os_, *ls_, gate, x, w_out, ln_g, ln_b)


def kernel(x_prompt, x_sample, state_conv, state_shift, state_wkv, cache_kv_w128, cache_kv_w512, cache_kv_w2048,
           w_in_even, conv_w, conv_b, conv_ln_g, conv_ln_b, mu_shift, w0, w2, a0, a2, k_k, k_a, r_k, lnx_g, lnx_b,
           w_out_even, w_in_odd, w_out_odd, ln_g, ln_b):
    nbp, seq, d_model = x_prompt.shape
    nbs, ns, _ = x_sample.shape
    assert nbp == 1
    depth = ln_g.shape[0]
    assert depth == 2 and w_in_even.shape[0] == 1 and w_in_odd.shape[0] == 1
    alpha = (2.0 * depth) ** 0.25
    bw = w0.shape[-1]
    lora_d, lora_a = w2.shape[1], a2.shape[1]
    nh = bw // HEAD
    nhist = CONV_W - 1
    caches = (cache_kv_w128, cache_kv_w512, cache_kv_w2048)
    dils = tuple(d for _, d in C_GROUPS)
    gw = w_out_odd.shape[1]
    ng = len(C_GROUPS)

    wi = w_in_even[0]
    sh_cols = 3 * bw + lora_d + lora_a
    zpad = jnp.zeros((d_model, LORA_PAD - lora_d - lora_a), F32)
    w_re = jnp.concatenate([wi[:, :sh_cols], zpad, wi[:, sh_cols:]], axis=1).astype(BF16)
    mu_re = jnp.concatenate([mu_shift[0], jnp.zeros((LORA_PAD - lora_d - lora_a,), F32)])[None, :]
    w2p = jnp.zeros((LORA_PAD, bw), F32).at[0:lora_d].set(w2[0])
    a2p = jnp.zeros((LORA_PAD, bw), F32).at[lora_d:lora_d + lora_a].set(a2[0])
    cw = jnp.concatenate([conv_w[0], jnp.zeros((CONV_HIST - CONV_W, bw), F32)], axis=0)
    vec = lambda z: z.reshape(1, -1)
    small = (vec(conv_b[0]), vec(conv_ln_g[0]), vec(conv_ln_b[0]), vec(w0[0]), w2p, vec(a0[0]), a2p,
             vec(k_k[0]), vec(k_a[0]), vec(r_k[0]))
    w_out_e = w_out_even[0].astype(BF16)
    w_in_o = w_in_odd[0].astype(BF16)
    w_out_o = w_out_odd[0].astype(BF16)
    lng = [vec(ln_g[l]) for l in range(depth)]
    lnb = [vec(ln_b[l]) for l in range(depth)]

    xp = x_prompt[0]
    ya, bg, r, lw, km, v, kkn, ab, bonus, conv_tail = _even_front_prompt(xp, w_re, mu_re, cw, small)
    o_wkv, wkv_p = _wkv_prompt(r, lw, km, v, kkn, ab, jnp.zeros((nh, HEAD, HEAD), F32))
    xp1 = _even_back(o_wkv, bonus, bg, ya, xp, vec(lnx_g[0]), vec(lnx_b[0]), w_out_e, lng[0], lnb[0],
                     alpha=alpha, tm=256)
    conv_p = conv_tail[CONV_HIST - nhist:][None, None]
    shift_p = x_prompt[:, -1][None]

    xs_t = jnp.transpose(x_sample, (1, 0, 2)).reshape(ns * nbs, d_model)
    x_in = jnp.concatenate([state_shift[0], xs_t], axis=0)
    hist = jnp.transpose(state_conv[0], (1, 0, 2)).reshape(nhist * nbs, bw)
    res = _even_front_sample(x_in, hist, w_re, mu_re, cw, small, nb=nbs, ns=ns)
    ya_s, bg_s, bon_s, conv_s_t = res[0], res[1], res[8], res[9]
    blk = 8

    def seq_major_pad(z):
        z = jnp.transpose(z.reshape(ns, nbs, bw), (1, 0, 2))
        return jnp.pad(z, ((0, 0), (0, blk - ns), (0, 0))).reshape(nbs * blk, bw)

    wkv_in = [seq_major_pad(z) for z in res[2:8]]
    o_s_pad, wkv_s = _wkv_sample(*wkv_in, state_wkv[0], blk=blk)
    o_s = jnp.transpose(o_s_pad.reshape(nbs, blk, bw)[:, :ns], (1, 0, 2)).reshape(ns * nbs, bw)
    xs1_t = _even_back(o_s, bon_s, bg_s, ya_s, xs_t, vec(lnx_g[0]), vec(lnx_b[0]), w_out_e, lng[0], lnb[0],
                       alpha=alpha, tm=ns * nbs)
    conv_s = jnp.transpose(conv_s_t.reshape(nhist, nbs, bw), (1, 0, 2))[None]
    shift_s = x_sample[:, -1][None]
    xs1 = jnp.transpose(xs1_t.reshape(ns, nbs, d_model), (1, 0, 2)).reshape(nbs * ns, d_model)

    tm1 = 256
    qc, kc, vc, gate_p, kf, vf = _odd_front(xp1, w_in_o, tm=tm1, dils=dils, gw=gw)
    os_, ls_ = [], []
    for g, (win, d) in enumerate(C_GROUPS):
        span = win // d
        flat = lambda z: z.reshape(seq, gw)
        o_g, l_g = _band_attn(flat(qc[g]), flat(kc[g]), flat(vc[g]), span=span, nb_class=(seq // d) // span)
        os_.append(o_g.reshape(d, seq // d, gw))
        ls_.append(l_g.reshape(d, seq // d, gw))
    xp2 = _odd_back(os_, ls_, gate_p, xp1, w_out_o, lng[1], lnb[1], dils=dils, alpha=alpha, tm=tm1)
    kv_p = []
    for g, (win, d) in enumerate(C_GROUPS):
        n = min(win, seq)
        kg = kf[seq - n:, g * gw:(g + 1) * gw].reshape(n, gw // HEAD, HEAD)
        vg = vf[seq - n:, g * gw:(g + 1) * gw].reshape(n, gw // HEAD, HEAD)
        kv_p.append(jnp.stack([kg, vg], axis=1)[None, None])

    ones = (1,) * ng
    qs, _, _, gate_s, kf_s, vf_s = _odd_front(xs1, w_in_o, tm=nbs * ns, dils=ones, gw=gw)
    os_s, ls_s, kv_s_out = [], [], []
    for g, (win, d) in enumerate(C_GROUPS):
        cache = caches[g][0]
        w = cache.shape[1]
        q_g = qs[g].astype(F32).reshape(nbs, ns, gw)
        kn = kf_s[:, g * gw:(g + 1) * gw].reshape(nbs, ns, gw)
        vn = vf_s[:, g * gw:(g + 1) * gw].reshape(nbs, ns, gw)
        cache_t = jnp.transpose(cache.reshape(nbs, w, 2 * gw), (0, 2, 1))
        new_t = jnp.transpose(jnp.concatenate([kn, vn], axis=2), (0, 2, 1))
        new_t = jnp.pad(new_t, ((0, 0), (0, 0), (0, 8 - ns)))
        o_g, l_g, cnew_t = _cache_attn(q_g, kn, vn, new_t, cache_t, d=d, window=win)
        os_s.append(o_g.reshape(1, nbs * ns, gw))
        ls_s.append(l_g.reshape(1, nbs * ns, gw))
        kv_s_out.append(jnp.transpose(cnew_t, (0, 2, 1)).reshape(nbs, w, 2, gw // HEAD, HEAD)[None])
    xs2 = _odd_back(os_s, ls_s, gate_s, xs1, w_out_o, lng[1], lnb[1], dils=ones, alpha=alpha, tm=nbs * ns)

    return (xp2[None], xs2.reshape(nbs, ns, d_model), conv_p, conv_s, shift_p, shift_s,
            wkv_p[None, None], wkv_s[None],
            kv_p[0], kv_s_out[0], kv_p[1], kv_s_out[1], kv_p[2], kv_s_out[2])
```

```python
import functools

import jax
import jax.numpy as jnp
from jax import lax
from jax.experimental import pallas as pl
from jax.experimental.pallas import tpu as pltpu

F32 = jnp.float32
BF16 = jnp.bfloat16

LN_EPS = 1e-5
LNX_EPS = 64e-5
HEAD = 64
LORA_PAD = 128
CONV_W = 31
CONV_HIST = 32
C_GROUPS = ((128, 1), (512, 4), (2048, 16))
WKV_CHUNK = 64
VMEM_LIMIT = 56 * 1024 * 1024


def _cparams(*sem):
    return pltpu.CompilerParams(dimension_semantics=sem or None, vmem_limit_bytes=VMEM_LIMIT)


def _dot(a, b):
    return jnp.dot(a.astype(BF16), b.astype(BF16), preferred_element_type=F32)


def _dot_nt(a, b):
    return lax.dot_general(a.astype(BF16), b.astype(BF16), (((1,), (1,)), ((), ())), preferred_element_type=F32)


def _dot_tn(a, b):
    return lax.dot_general(a.astype(BF16), b.astype(BF16), (((0,), (0,)), ((), ())), preferred_element_type=F32)


def _split_dot(a, ones_bf16):
    ah = a.astype(BF16)
    al = (a - ah.astype(F32)).astype(BF16)
    return (jnp.dot(ah, ones_bf16, preferred_element_type=F32) +
            jnp.dot(al, ones_bf16, preferred_element_type=F32))


def _head_ones(n):
    r = lax.broadcasted_iota(jnp.int32, (n, n), 0) >> 6
    c = lax.broadcasted_iota(jnp.int32, (n, n), 1) >> 6
    return jnp.where(r == c, 1.0, 0.0).astype(BF16)


def _sigmoid(x):
    return 1.0 / (1.0 + jnp.exp(-x))


def _silu(x):
    return x * _sigmoid(x)


def _softplus(y):
    return jnp.maximum(y, 0.0) + jnp.log(1.0 + jnp.exp(-jnp.abs(y)))


def _layer_norm(x, g, b, eps):
    mu = jnp.mean(x, axis=-1, keepdims=True)
    xc = x - mu
    var = jnp.mean(xc * xc, axis=-1, keepdims=True)
    return xc * lax.rsqrt(var + eps) * g + b


def _even_mix(sh, b_width, prm, hsum):
    w0, w2p, a0, a2p, k_k, k_a, r_k = prm
    r = sh[:, 0:b_width]
    k = sh[:, b_width:2 * b_width]
    v = sh[:, 2 * b_width:3 * b_width]
    lora = sh[:, 3 * b_width:3 * b_width + LORA_PAD]
    w_log = -_softplus(-(w0 + _dot(jnp.tanh(lora), w2p))) - 0.5
    lw = -jnp.exp(w_log)
    a = _sigmoid(a0 + _dot(lora, a2p))
    kk = k * k_k
    nrm = jnp.sqrt(_split_dot(kk * kk, hsum))
    kkn = kk / jnp.maximum(nrm, 1e-12)
    km = k * (1.0 + (a - 1.0) * k_a)
    bonus = _split_dot(r * km * r_k, hsum) * v
    return r, lw, km, v, kkn, kkn * a, bonus


def _tap_weight(cw_ref, j, n):
    return jnp.concatenate([cw_ref[8 * j:8 * j + 8, :]] * (n // 8), axis=0)


def _conv_ln_gate(u_ref, sh_ref, row0, n, cw_ref, cb, clg, clb, a_gate):
    acc = jnp.zeros((n, cb.shape[-1]), F32) + cb
    for j in range(CONV_W):
        a, b = divmod(CONV_HIST - (CONV_W - 1) + j, 8)
        win = u_ref[pl.ds(row0 + 8 * a, n), :] if b == 0 else sh_ref[b - 1, pl.ds(row0 + 8 * a, n), :]
        acc = acc + _tap_weight(cw_ref, j, n) * win
    return _silu(_layer_norm(acc, clg, clb, LN_EPS)) * _silu(a_gate)


def _even_front_prompt_body(x_ref, w_ref, mu_ref, cw_ref, cb_ref, clg_ref, clb_ref, w0_ref, w2_ref, a0_ref, a2_ref,
                            kk_ref, ka_ref, rk_ref,
                            ya_ref, bg_ref, r_ref, lw_ref, km_ref, v_ref, kkn_ref, ab_ref, bon_ref, conv_ref,
                            p_scr, u_scr, sh_scr, *, tm, bw, sub):
    i = pl.program_id(0)
    shw = 3 * bw + LORA_PAD

    @pl.when(i == 0)
    def _():
        p_scr[0:8, :] = jnp.zeros((8, shw), F32)
        u_scr[0:CONV_HIST, :] = jnp.zeros((CONV_HIST, bw), F32)

    xb = x_ref[...].astype(BF16)
    proj = lambda c0, c1: jnp.dot(xb, w_ref[:, c0:c1], preferred_element_type=F32)
    cur = proj(0, shw)
    p_scr[8:8 + tm, :] = cur
    prev = p_scr[pl.ds(7, tm), :]
    sh = cur + (prev - cur) * mu_ref[...]
    p_scr[7:8, :] = cur[tm - 1:tm, :]

    hsum = _head_ones(bw)
    prm = (w0_ref[...], w2_ref[...], a0_ref[...], a2_ref[...], kk_ref[...], ka_ref[...], rk_ref[...])
    r, lw, km, v, kkn, ab, bonus = _even_mix(sh, bw, prm, hsum)
    r_ref[...] = r
    lw_ref[...] = lw
    km_ref[...] = km
    v_ref[...] = v
    kkn_ref[...] = kkn
    ab_ref[...] = ab
    bon_ref[...] = bonus

    bg_ref[...] = _silu(proj(shw + 3 * bw, shw + 4 * bw))
    u_scr[CONV_HIST:CONV_HIST + tm, :] = proj(shw, shw + bw) * _sigmoid(proj(shw + bw, shw + 2 * bw))
    a_gate = proj(shw + 2 * bw, shw + 3 * bw)
    for b in range(1, 8):
        sh_scr[b - 1] = u_scr[pl.ds(b, sh_scr.shape[1]), :]
    cb, clg, clb = cb_ref[...], clg_ref[...], clb_ref[...]
    for s in range(tm // sub):
        ya_ref[s * sub:(s + 1) * sub, :] = _conv_ln_gate(u_scr, sh_scr, s * sub, sub, cw_ref, cb, clg, clb,
                                                         a_gate[s * sub:(s + 1) * sub, :]).astype(BF16)
    tail = u_scr[tm:tm + CONV_HIST, :]
    conv_ref[...] = tail
    u_scr[0:CONV_HIST, :] = tail


def _even_front_prompt(x, w_re, mu_re, cw, small, *, tm=256):
    t, d = x.shape
    bw = small[0].shape[-1]
    shw = 3 * bw + LORA_PAD
    ncol = w_re.shape[1]
    row = lambda i: (i, 0)
    fix = lambda i: (0, 0)
    vec = pl.BlockSpec((1, bw), fix)
    in_specs = [pl.BlockSpec((tm, d), row), pl.BlockSpec((d, ncol), fix), pl.BlockSpec((1, shw), fix),
                pl.BlockSpec((8 * CONV_W, bw), fix), vec, vec, vec, vec, pl.BlockSpec((LORA_PAD, bw), fix), vec,
                pl.BlockSpec((LORA_PAD, bw), fix), vec, vec, vec]
    out = jax.ShapeDtypeStruct((t, bw), F32)
    outs = [jax.ShapeDtypeStruct((t, bw), BF16)] + [out] * 8 + [jax.ShapeDtypeStruct((CONV_HIST, bw), F32)]
    out_specs = [pl.BlockSpec((tm, bw), row)] * 9 + [pl.BlockSpec((CONV_HIST, bw), fix)]
    body = functools.partial(_even_front_prompt_body, tm=tm, bw=bw, sub=32)
    return pl.pallas_call(
        body, out_shape=outs, grid=(t // tm,), in_specs=in_specs, out_specs=out_specs,
        scratch_shapes=[pltpu.VMEM((8 + tm, shw), F32), pltpu.VMEM((CONV_HIST + tm, bw), F32),
                        pltpu.VMEM((7, CONV_HIST - 8 + tm, bw), F32)],
        compiler_params=_cparams("arbitrary"), name="even_front_prompt")(x, w_re, mu_re, cw, *small)


def _even_front_sample_body(x_ref, hist_ref, w_ref, mu_ref, cw_ref, cb_ref, clg_ref, clb_ref, w0_ref, w2_ref, a0_ref,
                            a2_ref, kk_ref, ka_ref, rk_ref,
                            ya_ref, bg_ref, r_ref, lw_ref, km_ref, v_ref, kkn_ref, ab_ref, bon_ref, conv_ref,
                            u_scr, *, nb, ns, bw):
    n = nb * ns
    shw = 3 * bw + LORA_PAD
    nh = CONV_W - 1
    p = jnp.dot(x_ref[...].astype(BF16), w_ref[...], preferred_element_type=F32)
    cur = p[nb:, :shw]
    prev = p[:n, :shw]
    sh = cur + (prev - cur) * mu_ref[...]
    hsum = _head_ones(bw)
    prm = (w0_ref[...], w2_ref[...], a0_ref[...], a2_ref[...], kk_ref[...], ka_ref[...], rk_ref[...])
    r, lw, km, v, kkn, ab, bonus = _even_mix(sh, bw, prm, hsum)
    r_ref[...] = r
    lw_ref[...] = lw
    km_ref[...] = km
    v_ref[...] = v
    kkn_ref[...] = kkn
    ab_ref[...] = ab
    bon_ref[...] = bonus
    rest = p[nb:, shw:]
    a_val, a_glu = rest[:, 0:bw], rest[:, bw:2 * bw]
    a_gate, b_gate = rest[:, 2 * bw:3 * bw], rest[:, 3 * bw:4 * bw]
    bg_ref[...] = _silu(b_gate)
    u_scr[0:nh * nb, :] = hist_ref[...]
    u_scr[nh * nb:(nh + ns) * nb, :] = a_val * _sigmoid(a_glu)
    cb, clg, clb = cb_ref[...], clg_ref[...], clb_ref[...]
    for s in range(ns):
        acc = jnp.zeros((nb, bw), F32) + cb
        for j in range(CONV_W):
            acc = acc + _tap_weight(cw_ref, j, nb) * u_scr[(s + j) * nb:(s + j + 1) * nb, :]
        ya_ref[s * nb:(s + 1) * nb, :] = (_silu(_layer_norm(acc, clg, clb, LN_EPS)) *
                                           _silu(a_gate[s * nb:(s + 1) * nb, :])).astype(BF16)
    conv_ref[...] = u_scr[ns * nb:(ns + nh) * nb, :]


def _even_front_sample(x_in, hist, w_re, mu_re, cw, small, *, nb, ns):
    bw = small[0].shape[-1]
    n = nb * ns
    nh = CONV_W - 1
    out = jax.ShapeDtypeStruct((n, bw), F32)
    outs = [jax.ShapeDtypeStruct((n, bw), BF16)] + [out] * 8 + [jax.ShapeDtypeStruct((nh * nb, bw), F32)]
    body = functools.partial(_even_front_sample_body, nb=nb, ns=ns, bw=bw)
    return pl.pallas_call(
        body, out_shape=outs, scratch_shapes=[pltpu.VMEM(((nh + ns) * nb, bw), F32)],
        compiler_params=_cparams(), name="even_front_sample")(x_in, hist, w_re, mu_re, cw, *small)


def _wkv_block_terms(r, lw, km, v, kkn, ab, blk):
    n, width = r.shape
    nh = width // HEAD
    ri = lax.broadcasted_iota(jnp.int32, (n, n), 0)
    ci = lax.broadcasted_iota(jnp.int32, (n, n), 1)
    same = _same_block(ri, ci, blk)
    incl = (ri >= ci) & same
    strict = (ri > ci) & same
    cum = _split3_dot(jnp.where(incl, 1.0, 0.0).astype(BF16), lw)
    tot = _split3_dot(jnp.where(same, 1.0, 0.0).astype(BF16), lw)
    dn = jnp.exp(-cum)
    at = -kkn * jnp.exp(cum - lw)
    bt = ab * dn
    kt = km * dn
    rt = r * jnp.exp(cum)
    rem = jnp.exp(tot - cum)
    bc = ab * rem
    kc = km * rem
    wc = jnp.exp(tot)
    hs = [slice(h * HEAD, (h + 1) * HEAD) for h in range(nh)]
    gs = [_dot_nt(jnp.concatenate([at[:, sl], rt[:, sl]], axis=0), jnp.concatenate([bt[:, sl], kt[:, sl]], axis=0))
          for sl in hs]
    l_ab = [jnp.where(strict, g[:n, :n], 0.0) for g in gs]
    l_ak = [jnp.where(strict, g[:n, n:], 0.0) for g in gs]
    l_rb = [jnp.where(incl, g[n:, :n], 0.0) for g in gs]
    l_rk = [jnp.where(incl, g[n:, n:], 0.0) for g in gs]
    toff = _unit_lower_inverse_minus_eye(l_ab, ri, ci, blk)
    lv = [_dot(jnp.concatenate([la, lr], axis=0), v[:, sl]) for la, lr, sl in zip(l_ak, l_rk, hs)]
    ahat = [at[:, sl] + _dot(t, at[:, sl]) for t, sl in zip(toff, hs)]
    uhat = [x[:n] + _dot(t, x[:n]) for t, x in zip(toff, lv)]
    qhat = [rt[:, sl] + _dot(l, a) for l, a, sl in zip(l_rb, ahat, hs)]
    ohat = [_dot(lb, u) + x[n:] for lb, u, x in zip(l_rb, uhat, lv)]
    return list(zip(ahat, uhat, qhat, ohat)), bc, kc, wc


def _same_block(ri, ci, size):
    sh = size.bit_length() - 1
    return (ri >> sh) == (ci >> sh)


def _split(x):
    hi = x.astype(BF16)
    return hi, (x - hi.astype(F32)).astype(BF16)


def _split3_dot(ones_bf16, x):
    hi, lo = _split(x)
    lo2 = (x - hi.astype(F32) - lo.astype(F32)).astype(BF16)
    d = lambda y: jnp.dot(ones_bf16, y, preferred_element_type=F32)
    return d(hi) + (d(lo) + d(lo2))


def _unit_lower_inverse_minus_eye(ls, ri, ci, blk):
    base = 8
    eye = jnp.where(ri == ci, 1.0, 0.0)
    same8 = _same_block(ri, ci, base)
    l8 = [jnp.where(same8, l, 0.0).astype(BF16) for l in ls]
    x2 = [_dot(x, x).astype(BF16) for x in l8]
    x4 = [_dot(x, x).astype(BF16) for x in x2]
    p = [eye + jnp.where(same8, l, 0.0) for l in ls]
    p = [q + _dot(q, x) for q, x in zip(p, x2)]
    t = [q + _dot(q, x) for q, x in zip(p, x4)]
    size = 2 * base
    while size <= blk:
        half = size.bit_length() - 2
        lower_left = _same_block(ri, ci, size) & (((ri >> half) & 1) == 1) & (((ci >> half) & 1) == 0)
        tb = [x.astype(BF16) for x in t]
        mt = [_dot(jnp.where(lower_left, l, 0.0), x) for l, x in zip(ls, tb)]
        t = [x + _dot(xb, m) for x, xb, m in zip(t, tb, mt)]
        size *= 2
    return [x - eye for x in t]


def _wkv_prompt_body(r_ref, lw_ref, km_ref, v_ref, kkn_ref, ab_ref, s0_ref, o_ref, sout_ref, s_scr, *, c, nch):
    i = pl.program_id(0)

    @pl.when(i == 0)
    def _():
        s_scr[...] = s0_ref[...]

    v = v_ref[...]
    heads, bc, kc, wc = _wkv_block_terms(r_ref[...], lw_ref[...], km_ref[...], v, kkn_ref[...], ab_ref[...], c)
    nh = len(heads)
    hs = [slice(h * HEAD, (h + 1) * HEAD) for h in range(nh)]
    vk = [_dot_tn(v[ch * c:(ch + 1) * c, sl], kc[ch * c:(ch + 1) * c, sl]) for ch in range(nch) for sl in hs]
    s = [s_scr[h] for h in range(nh)]
    for ch in range(nch):
        rows = slice(ch * c, (ch + 1) * c)
        mm = [_dot_nt(jnp.concatenate([ahat[rows], qhat[rows]], axis=0), s[h])
              for h, (ahat, _, qhat, _) in enumerate(heads)]
        u = [mm[h][:c] + heads[h][1][rows] for h in range(nh)]
        for h in range(nh):
            o_ref[rows, hs[h]] = mm[h][c:] + heads[h][3][rows]
        s = [s[h] * wc[ch * c:ch * c + 1, hs[h]] + _dot_tn(u[h], bc[rows, hs[h]]) + vk[ch * nh + h]
             for h in range(nh)]
    for h in range(nh):
        s_scr[h] = s[h]

    @pl.when(i == pl.num_programs(0) - 1)
    def _():
        sout_ref[...] = s_scr[...]


def _wkv_prompt(r, lw, km, v, kkn, ab, s0, *, nch=2):
    t, bw = r.shape
    c = WKV_CHUNK
    nh = bw // HEAD
    row = pl.BlockSpec((nch * c, bw), lambda i: (i, 0))
    st = pl.BlockSpec((nh, HEAD, HEAD), lambda i: (0, 0, 0))
    return pl.pallas_call(
        functools.partial(_wkv_prompt_body, c=c, nch=nch),
        out_shape=[jax.ShapeDtypeStruct((t, bw), F32), jax.ShapeDtypeStruct((nh, HEAD, HEAD), F32)],
        grid=(t // (nch * c),), in_specs=[row] * 6 + [st], out_specs=[row, st],
        scratch_shapes=[pltpu.VMEM((nh, HEAD, HEAD), F32)],
        compiler_params=_cparams("arbitrary"), name="wkv_prompt")(r, lw, km, v, kkn, ab, s0)


def _wkv_sample_body(r_ref, lw_ref, km_ref, v_ref, kkn_ref, ab_ref, s0_ref, o_ref, sout_ref,
                     ah_scr, uh_scr, qh_scr, oh_scr, bc_scr, kc_scr, wc_scr, *, nb, blk):
    v = v_ref[...]
    heads, bc, kc, wc = _wkv_block_terms(r_ref[...], lw_ref[...], km_ref[...], v, kkn_ref[...], ab_ref[...], blk)
    bc_scr[...] = bc
    kc_scr[...] = kc
    wc_scr[...] = wc
    for h, (ahat, uhat, qhat, ohat) in enumerate(heads):
        sl = slice(h * HEAD, (h + 1) * HEAD)
        ah_scr[:, sl] = ahat
        uh_scr[:, sl] = uhat
        qh_scr[:, sl] = qhat
        oh_scr[:, sl] = ohat
    nh = len(heads)

    def per_seq(b, carry):
        rows = pl.ds(pl.multiple_of(b * blk, blk), blk)
        for h in range(nh):
            sl = slice(h * HEAD, (h + 1) * HEAD)
            s = s0_ref[b, h]
            mm = _dot_nt(jnp.concatenate([ah_scr[rows, sl], qh_scr[rows, sl]], axis=0), s)
            u = mm[:blk] + uh_scr[rows, sl]
            o_ref[rows, sl] = mm[blk:] + oh_scr[rows, sl]
            wrow = wc_scr[rows, sl][0:1, :]
            sout_ref[b, h] = s * wrow + _dot_tn(u, bc_scr[rows, sl]) + _dot_tn(v_ref[rows, sl], kc_scr[rows, sl])
        return carry

    lax.fori_loop(0, nb, per_seq, 0)


def _wkv_sample(r, lw, km, v, kkn, ab, s0, *, blk):
    n, bw = r.shape
    nb = n // blk
    scr = pltpu.VMEM((n, bw), F32)
    return pl.pallas_call(
        functools.partial(_wkv_sample_body, nb=nb, blk=blk),
        out_shape=[jax.ShapeDtypeStruct((n, bw), F32), jax.ShapeDtypeStruct(s0.shape, F32)],
        scratch_shapes=[scr] * 7, compiler_params=_cparams(), name="wkv_sample")(r, lw, km, v, kkn, ab, s0)


def _even_back_body(o_ref, bon_ref, bg_ref, ya_ref, x_ref, lg_ref, lb_ref, w_ref, g_ref, b_ref, y_ref, *, alpha, bw):
    hsum = _head_ones(bw)
    o = o_ref[...]
    om = _split_dot(o, hsum) * (1.0 / HEAD)
    oc = o - om
    ov = _split_dot(oc * oc, hsum) * (1.0 / HEAD)
    on = oc * lax.rsqrt(ov + LNX_EPS) * lg_ref[...] + lb_ref[...]
    yb = (on + bon_ref[...]) * bg_ref[...]
    y = _dot(ya_ref[...], w_ref[0:bw, :]) + _dot(yb, w_ref[bw:2 * bw, :])
    y_ref[...] = _layer_norm(alpha * x_ref[...] + y, g_ref[...], b_ref[...], LN_EPS)


def _even_back(o, bonus, bg, ya, x, lnx_g, lnx_b, w_out, ln_g, ln_b, *, alpha, tm):
    t, d = x.shape
    bw = o.shape[-1]
    row = lambda i: (i, 0)
    fix = lambda i: (0, 0)
    half = pl.BlockSpec((tm, bw), row)
    in_specs = [half] * 4 + [pl.BlockSpec((tm, d), row), pl.BlockSpec((1, bw), fix), pl.BlockSpec((1, bw), fix),
                             pl.BlockSpec((2 * bw, d), fix), pl.BlockSpec((1, d), fix), pl.BlockSpec((1, d), fix)]
    return pl.pallas_call(
        functools.partial(_even_back_body, alpha=alpha, bw=bw),
        out_shape=jax.ShapeDtypeStruct((t, d), F32), grid=(t // tm,), in_specs=in_specs,
        out_specs=pl.BlockSpec((tm, d), row), compiler_params=_cparams("parallel"),
        name="even_back")(o, bonus, bg, ya, x, lnx_g, lnx_b, w_out, ln_g, ln_b)


def _odd_front_body(x_ref, w_ref, *refs, tm, gw, dils, scale):
    ng = len(dils)
    q_refs, k_refs, v_refs = refs[0:ng], refs[ng:2 * ng], refs[2 * ng:3 * ng]
    gate_ref, p_scr = refs[3 * ng:3 * ng + 2]
    xb = x_ref[...].astype(BF16)
    lt = p_scr.shape[-1]
    nlb = gw // lt
    for part, cls_refs in enumerate((q_refs, k_refs, v_refs)):
        for g, d in enumerate(dils):
            blk = part * ng + g
            pc = jnp.dot(xb, w_ref[:, blk * gw:(blk + 1) * gw], preferred_element_type=F32)
            if part == 0:
                pc = pc * scale
            if d == 1:
                cls_refs[g][0] = pc.astype(BF16)
                continue
            slot = (blk % 2) * nlb
            for lb in range(nlb):
                p_scr[slot + lb] = pc[:, lb * lt:(lb + 1) * lt]
            n = tm // d
            for c in range(d):
                for lb in range(nlb):
                    cls_refs[g][c, :, lb * lt:(lb + 1) * lt] = (
                        p_scr[slot + lb, pl.ds(c, n, stride=d), :].astype(BF16))
    gate = jnp.dot(xb, w_ref[:, 3 * ng * gw:(3 * ng + 1) * gw], preferred_element_type=F32)
    gate_ref[...] = _silu(gate)


def _odd_front(x, w, *, tm, dils, gw):
    t, d_model = x.shape
    ng = len(dils)
    ncol = w.shape[1]
    row = lambda i: (i, 0)
    cls_specs = [pl.BlockSpec((d, tm // d, gw), lambda i: (0, i, 0)) for d in dils]
    cls_shapes = [jax.ShapeDtypeStruct((d, t // d, gw), BF16) for d in dils]
    outs = cls_shapes * 3 + [jax.ShapeDtypeStruct((t, gw), F32)]
    out_specs = cls_specs * 3 + [pl.BlockSpec((tm, gw), row)]
    body = functools.partial(_odd_front_body, tm=tm, gw=gw, dils=dils, scale=HEAD ** -0.5)
    res = pl.pallas_call(
        body, out_shape=outs, grid=(t // tm,),
        in_specs=[pl.BlockSpec((tm, d_model), row),
                  pl.BlockSpec((d_model, ncol), lambda i: (0, 0), pipeline_mode=pl.Buffered(1))],
        out_specs=out_specs, scratch_shapes=[pltpu.VMEM((2 * gw // 128, tm, 128), F32)],
        compiler_params=_cparams("parallel"), name="odd_front")(x, w)
    return res[0:ng], res[ng:2 * ng], res[2 * ng:3 * ng], res[3 * ng]


def _kv_rows_body(x_ref, w_ref, kv_ref):
    kv_ref[...] = jnp.dot(x_ref[...].astype(BF16), w_ref[...], preferred_element_type=F32)


def _kv_rows(x, w_kv, *, tm):
    t, d_model = x.shape
    ncol = w_kv.shape[1]
    return pl.pallas_call(
        _kv_rows_body, out_shape=jax.ShapeDtypeStruct((t, ncol), F32), grid=(t // tm,),
        in_specs=[pl.BlockSpec((tm, d_model), lambda i: (i, 0)), pl.BlockSpec((d_model, ncol), lambda i: (0, 0))],
        out_specs=pl.BlockSpec((tm, ncol), lambda i: (i, 0)),
        compiler_params=_cparams("parallel"), name="kv_rows")(x, w_kv)


def _band_attn_body(q_ref, kc_ref, kp_ref, vc_ref, vp_ref, o_ref, lse_ref, *, bq, span, nb_class):
    j = pl.program_id(0)
    first_lo = jnp.where(((j * bq) % nb_class) == 0, span, 0)
    qi = lax.broadcasted_iota(jnp.int32, (span, 2 * span), 0)
    ki = lax.broadcasted_iota(jnp.int32, (span, 2 * span), 1)
    in_band = (ki >= qi) & (ki <= qi + span)
    lane = lax.broadcasted_iota(jnp.int32, (span, 2 * HEAD), 1)
    left = lane < HEAD
    gw = q_ref.shape[-1]
    for b in range(bq):
        mask = in_band & (ki >= first_lo) if b == 0 else in_band
        for pr in range(gw // (2 * HEAD)):
            ls = slice(pr * 2 * HEAD, (pr + 1) * 2 * HEAD)
            q2 = q_ref[b * span:(b + 1) * span, ls]
            if b == 0:
                k2 = jnp.concatenate([kp_ref[:, ls], kc_ref[0:span, ls]], axis=0)
                v2 = jnp.concatenate([vp_ref[:, ls], vc_ref[0:span, ls]], axis=0)
            else:
                k2 = kc_ref[(b - 1) * span:(b + 1) * span, ls]
                v2 = vc_ref[(b - 1) * span:(b + 1) * span, ls]
            outs, lses = [], []
            for sel in (left, jnp.logical_not(left)):
                qh = jnp.where(sel, q2, jnp.zeros_like(q2))
                s = lax.dot_general(qh, k2, (((1,), (1,)), ((), ())), preferred_element_type=F32)
                s = jnp.where(mask, s, -jnp.inf)
                m = jnp.max(s, axis=-1, keepdims=True)
                p = jnp.exp(s - m)
                l = jnp.sum(p, axis=-1, keepdims=True)
                outs.append(jnp.dot(p.astype(BF16), v2, preferred_element_type=F32) / l)
                lses.append(m + jnp.log(l))
            o_ref[b * span:(b + 1) * span, ls] = jnp.where(left, outs[0], outs[1])
            lse_ref[b * span:(b + 1) * span, ls] = jnp.where(left, lses[0], lses[1])


def _band_attn(q, k, v, *, span, nb_class, bq=4):
    t, gw = q.shape
    cur = pl.BlockSpec((bq * span, gw), lambda j: (j, 0))
    prev = pl.BlockSpec((span, gw), lambda j: (jnp.maximum(j * bq - 1, 0), 0))
    out = jax.ShapeDtypeStruct((t, gw), F32)
    return pl.pallas_call(
        functools.partial(_band_attn_body, bq=bq, span=span, nb_class=nb_class),
        out_shape=[out, out], grid=(t // (bq * span),), in_specs=[cur, cur, prev, cur, prev],
        out_specs=[cur, cur], compiler_params=_cparams("parallel"), name="band_attn")(q, k, k, v, v)


def _cache_attn_body(q_ref, kn_ref, vn_ref, nt_ref, c_ref, o_ref, lse_ref, cout_ref, *, w, d, window, ns, gw, rchunk):
    nh = gw // HEAD
    rp = 8
    zrow = jnp.zeros((rp - ns, gw), F32)
    q8 = jnp.concatenate([q_ref[0], zrow], axis=0)
    kn8 = jnp.concatenate([kn_ref[0], zrow], axis=0).astype(BF16)
    vn8 = jnp.concatenate([vn_ref[0], zrow], axis=0).astype(BF16)
    qm = jnp.concatenate([q8] * nh, axis=0)
    rowh = lax.broadcasted_iota(jnp.int32, (nh * rp, gw), 0) >> 3
    laneh = lax.broadcasted_iota(jnp.int32, (nh * rp, gw), 1) >> 6
    own = rowh == laneh
    qm = jnp.where(own, qm, 0.0).astype(BF16)
    s_c = jnp.dot(qm, c_ref[0, 0:gw, :].astype(BF16), preferred_element_type=F32)
    s_n = lax.dot_general(qm, kn8, (((1,), (1,)), ((), ())), preferred_element_type=F32)
    srow = lax.broadcasted_iota(jnp.int32, s_c.shape, 0) & (rp - 1)
    dist = w + srow - lax.broadcasted_iota(jnp.int32, s_c.shape, 1)
    valid_c = (dist <= window) & ((dist & (d - 1)) == 0) & (srow < ns)
    srow_n = lax.broadcasted_iota(jnp.int32, s_n.shape, 0) & (rp - 1)
    col_n = lax.broadcasted_iota(jnp.int32, s_n.shape, 1)
    dist_n = srow_n - col_n
    valid_n = (dist_n >= 0) & ((dist_n & (d - 1)) == 0) & (srow_n < ns)
    valid_n = valid_n | ((srow_n >= ns) & (col_n == 0))
    s_c = jnp.where(valid_c, s_c, -jnp.inf)
    s_n = jnp.where(valid_n, s_n, -jnp.inf)
    m = jnp.maximum(jnp.max(s_c, axis=-1, keepdims=True), jnp.max(s_n, axis=-1, keepdims=True))
    p_c = jnp.exp(s_c - m)
    p_n = jnp.exp(s_n - m)
    l = jnp.sum(p_c, axis=-1, keepdims=True) + jnp.sum(p_n, axis=-1, keepdims=True)
    o = lax.dot_general(p_c.astype(BF16), c_ref[0, gw:2 * gw, :].astype(BF16), (((1,), (1,)), ((), ())),
                        preferred_element_type=F32)
    o = (o + jnp.dot(p_n.astype(BF16), vn8, preferred_element_type=F32)) / l
    lse = jnp.broadcast_to(m + jnp.log(l), o.shape)
    o = jnp.where(own, o, 0.0)
    lse = jnp.where(own, lse, 0.0)
    o8 = o[0:rp]
    l8 = lse[0:rp]
    for h in range(1, nh):
        o8 = o8 + o[h * rp:(h + 1) * rp]
        l8 = l8 + lse[h * rp:(h + 1) * rp]
    o_ref[0] = o8[0:ns]
    lse_ref[0] = l8[0:ns]
    for rb in range(2 * gw // rchunk):
        rows = slice(rb * rchunk, (rb + 1) * rchunk)
        cout_ref[0, rows, :] = jnp.concatenate([c_ref[0, rows, ns:w], nt_ref[0, rows, 0:ns]], axis=1)


def _cache_attn(q, kn, vn, new_t, cache_t, *, d, window):
    nbat, ns, gw = q.shape
    w = cache_t.shape[2]
    new = pl.BlockSpec((1, ns, gw), lambda b: (b, 0, 0))
    ntb = pl.BlockSpec((1, 2 * gw, new_t.shape[2]), lambda b: (b, 0, 0))
    cb = pl.BlockSpec((1, 2 * gw, w), lambda b: (b, 0, 0))
    out = jax.ShapeDtypeStruct((nbat, ns, gw), F32)
    return pl.pallas_call(
        functools.partial(_cache_attn_body, w=w, d=d, window=window, ns=ns, gw=gw, rchunk=64),
        out_shape=[out, out, jax.ShapeDtypeStruct(cache_t.shape, F32)], grid=(nbat,),
        in_specs=[new, new, new, ntb, cb], out_specs=[new, new, cb],
        compiler_params=_cparams("parallel"), name="cache_attn")(q, kn, vn, new_t, cache_t)


def _odd_back_body(*refs, tm, dils, alpha):
    ng = len(dils)
    o_refs, l_refs = refs[0:ng], refs[ng:2 * ng]
    gate_ref, x_ref, w_ref, g_ref, b_ref, y_ref = refs[2 * ng:2 * ng + 6]
    scr = refs[2 * ng + 6:]
    os_, ls_ = [], []
    for g, d in enumerate(dils):
        if d == 1:
            os_.append(o_refs[g][0])
            ls_.append(l_refs[g][0])
        else:
            n = tm // d
            nlb, _, lt = scr[2 * g].shape
            for c in range(d):
                for lb in range(nlb):
                    scr[2 * g][lb, pl.ds(c, n, stride=d), :] = o_refs[g][c, :, lb * lt:(lb + 1) * lt]
                    scr[2 * g + 1][lb, pl.ds(c, n, stride=d), :] = l_refs[g][c, :, lb * lt:(lb + 1) * lt]
            os_.append(jnp.concatenate([scr[2 * g][lb] for lb in range(nlb)], axis=1))
            ls_.append(jnp.concatenate([scr[2 * g + 1][lb] for lb in range(nlb)], axis=1))
    m = ls_[0]
    for l in ls_[1:]:
        m = jnp.maximum(m, l)
    es = [jnp.exp(l - m) for l in ls_]
    den = es[0]
    for e in es[1:]:
        den = den + e
    o = es[0] * os_[0]
    for e, og in zip(es[1:], os_[1:]):
        o = o + e * og
    o = o / den
    y = _dot(o * gate_ref[...], w_ref[...])
    y_ref[...] = _layer_norm(alpha * x_ref[...] + y, g_ref[...], b_ref[...], LN_EPS)


def _odd_back(os_, ls_, gate, x, w_out, ln_g, ln_b, *, dils, alpha, tm):
    t, d_model = x.shape
    gw = gate.shape[-1]
    row = lambda i: (i, 0)
    fix = lambda i: (0, 0)
    cls_specs = [pl.BlockSpec((d, tm // d, gw), lambda i: (0, i, 0)) for d in dils]
    in_specs = cls_specs * 2 + [pl.BlockSpec((tm, gw), row), pl.BlockSpec((tm, d_model), row),
                                pl.BlockSpec((gw, d_model), fix), pl.BlockSpec((1, d_model), fix),
                                pl.BlockSpec((1, d_model), fix)]
    return pl.pallas_call(
        functools.partial(_odd_back_body, tm=tm, dils=dils, alpha=alpha),
        out_shape=jax.ShapeDtypeStruct((t, d_model), F32), grid=(t // tm,), in_specs=in_specs,
        out_specs=pl.BlockSpec((tm, d_model), row),
        scratch_shapes=[pltpu.VMEM((gw // 128, tm, 128), F32)] * (2 * len(dils)),
        compiler_params=_cparams("parallel"), name="odd_back")(*os_, *ls_, gate, x, w_out, ln_g, ln_b)


def kernel(x_prompt, x_sample, state_conv, state_shift, state_wkv, cache_kv_w128, cache_kv_w512, cache_kv_w2048,
           w_in_even, conv_w, conv_b, conv_ln_g, conv_ln_b, mu_shift, w0, w2, a0, a2, k_k, k_a, r_k, lnx_g, lnx_b,
           w_out_even, w_in_odd, w_out_odd, ln_g, ln_b):
    nbp, seq, d_model = x_prompt.shape
    nbs, ns, _ = x_sample.shape
    assert nbp == 1
    depth = ln_g.shape[0]
    assert depth == 2 and w_in_even.shape[0] == 1 and w_in_odd.shape[0] == 1
    alpha = (2.0 * depth) ** 0.25
    bw = w0.shape[-1]
    lora_d, lora_a = w2.shape[1], a2.shape[1]
    nh = bw // HEAD
    nhist = CONV_W - 1
    caches = (cache_kv_w128, cache_kv_w512, cache_kv_w2048)
    dils = tuple(d for _, d in C_GROUPS)
    gw = w_out_odd.shape[1]
    ng = len(C_GROUPS)

    wi = w_in_even[0]
    sh_cols = 3 * bw + lora_d + lora_a
    zpad = jnp.zeros((d_model, LORA_PAD - lora_d - lora_a), F32)
    w_re = jnp.concatenate([wi[:, :sh_cols], zpad, wi[:, sh_cols:]], axis=1).astype(BF16)
    mu_re = jnp.concatenate([mu_shift[0], jnp.zeros((LORA_PAD - lora_d - lora_a,), F32)])[None, :]
    w2p = jnp.zeros((LORA_PAD, bw), F32).at[0:lora_d].set(w2[0])
    a2p = jnp.zeros((LORA_PAD, bw), F32).at[lora_d:lora_d + lora_a].set(a2[0])
    cw = jnp.repeat(conv_w[0], 8, axis=0)
    vec = lambda z: z.reshape(1, -1)
    small = (vec(conv_b[0]), vec(conv_ln_g[0]), vec(conv_ln_b[0]), vec(w0[0]), w2p, vec(a0[0]), a2p,
             vec(k_k[0]), vec(k_a[0]), vec(r_k[0]))
    w_out_e = w_out_even[0].astype(BF16)
    w_in_o = w_in_odd[0].astype(BF16)
    w_out_o = w_out_odd[0].astype(BF16)
    lng = [vec(ln_g[l]) for l in range(depth)]
    lnb = [vec(ln_b[l]) for l in range(depth)]

    xp = x_prompt[0]
    ya, bg, r, lw, km, v, kkn, ab, bonus, conv_tail = _even_front_prompt(xp, w_re, mu_re, cw, small)
    o_wkv, wkv_p = _wkv_prompt(r, lw, km, v, kkn, ab, jnp.zeros((nh, HEAD, HEAD), F32))
    xp1 = _even_back(o_wkv, bonus, bg, ya, xp, vec(lnx_g[0]), vec(lnx_b[0]), w_out_e, lng[0], lnb[0],
                     alpha=alpha, tm=256)
    conv_p = conv_tail[CONV_HIST - nhist:][None, None]
    shift_p = x_prompt[:, -1][None]

    xs_t = jnp.transpose(x_sample, (1, 0, 2)).reshape(ns * nbs, d_model)
    x_in = jnp.concatenate([state_shift[0], xs_t], axis=0)
    hist = jnp.transpose(state_conv[0], (1, 0, 2)).reshape(nhist * nbs, bw)
    res = _even_front_sample(x_in, hist, w_re, mu_re, cw, small, nb=nbs, ns=ns)
    ya_s, bg_s, bon_s, conv_s_t = res[0], res[1], res[8], res[9]
    blk = 8

    def seq_major_pad(z):
        z = jnp.transpose(z.reshape(ns, nbs, bw), (1, 0, 2))
        return jnp.pad(z, ((0, 0), (0, blk - ns), (0, 0))).reshape(nbs * blk, bw)

    wkv_in = [seq_major_pad(z) for z in res[2:8]]
    o_s_pad, wkv_s = _wkv_sample(*wkv_in, state_wkv[0], blk=blk)
    o_s = jnp.transpose(o_s_pad.reshape(nbs, blk, bw)[:, :ns], (1, 0, 2)).reshape(ns * nbs, bw)
    xs1_t = _even_back(o_s, bon_s, bg_s, ya_s, xs_t, vec(lnx_g[0]), vec(lnx_b[0]), w_out_e, lng[0], lnb[0],
                       alpha=alpha, tm=ns * nbs)
    conv_s = jnp.transpose(conv_s_t.reshape(nhist, nbs, bw), (1, 0, 2))[None]
    shift_s = x_sample[:, -1][None]
    xs1 = jnp.transpose(xs1_t.reshape(ns, nbs, d_model), (1, 0, 2)).reshape(nbs * ns, d_model)

    tm1 = 256
    tail = min(max(win for win, _ in C_GROUPS), seq)
    w_kv = w_in_o[:, ng * gw:3 * ng * gw]
    qc, kc, vc, gate_p = _odd_front(xp1, w_in_o, tm=1024, dils=dils, gw=gw)
    kvf = _kv_rows(xp1[seq - tail:], w_kv, tm=512)
    kf, vf = kvf[:, :ng * gw], kvf[:, ng * gw:]
    os_, ls_ = [], []
    for g, (win, d) in enumerate(C_GROUPS):
        span = win // d
        flat = lambda z: z.reshape(seq, gw)
        o_g, l_g = _band_attn(flat(qc[g]), flat(kc[g]), flat(vc[g]), span=span, nb_class=(seq // d) // span)
        os_.append(o_g.reshape(d, seq // d, gw))
        ls_.append(l_g.reshape(d, seq // d, gw))
    xp2 = _odd_back(os_, ls_, gate_p, xp1, w_out_o, lng[1], lnb[1], dils=dils, alpha=alpha, tm=tm1)
    kv_p = []
    for g, (win, d) in enumerate(C_GROUPS):
        n = min(win, seq)
        kg = kf[tail - n:, g * gw:(g + 1) * gw].reshape(n, gw // HEAD, HEAD)
        vg = vf[tail - n:, g * gw:(g + 1) * gw].reshape(n, gw // HEAD, HEAD)
        kv_p.append(jnp.stack([kg, vg], axis=1)[None, None])

    ones = (1,) * ng
    qs, _, _, gate_s = _odd_front(xs1, w_in_o, tm=nbs * ns, dils=ones, gw=gw)
    kvf_s = _kv_rows(xs1, w_kv, tm=nbs * ns)
    kf_s, vf_s = kvf_s[:, :ng * gw], kvf_s[:, ng * gw:]
    os_s, ls_s, kv_s_out = [], [], []
    for g, (win, d) in enumerate(C_GROUPS):
        cache = caches[g][0]
        w = cache.shape[1]
        q_g = qs[g].astype(F32).reshape(nbs, ns, gw)
        kn = kf_s[:, g * gw:(g + 1) * gw].reshape(nbs, ns, gw)
        vn = vf_s[:, g * gw:(g + 1) * gw].reshape(nbs, ns, gw)
        cache_t = jnp.transpose(cache.reshape(nbs, w, 2 * gw), (0, 2, 1))
        new_t = jnp.transpose(jnp.concatenate([kn, vn], axis=2), (0, 2, 1))
        new_t = jnp.pad(new_t, ((0, 0), (0, 0), (0, 8 - ns)))
        o_g, l_g, cnew_t = _cache_attn(q_g, kn, vn, new_t, cache_t, d=d, window=win)
        os_s.append(o_g.reshape(1, nbs * ns, gw))
        ls_s.append(l_g.reshape(1, nbs * ns, gw))
        kv_s_out.append(jnp.transpose(cnew_t, (0, 2, 1)).reshape(nbs, w, 2, gw // HEAD, HEAD)[None])
    xs2 = _odd_back(os_s, ls_s, gate_s, xs1, w_out_o, lng[1], lnb[1], dils=ones, alpha=alpha, tm=nbs * ns)

    return (xp2[None], xs2.reshape(nbs, ns, d_model), conv_p, conv_s, shift_p, shift_s,
            wkv_p[None, None], wkv_s[None],
            kv_p[0], kv_s_out[0], kv_p[1], kv_s_out[1], kv_p[2], kv_s_out[2])
```

```python
import functools

import jax
import jax.numpy as jnp
from jax import lax
from jax.experimental import pallas as pl
from jax.experimental.pallas import tpu as pltpu

F32 = jnp.float32
BF16 = jnp.bfloat16

LN_EPS = 1e-5
LNX_EPS = 64e-5
HEAD = 64
LORA_PAD = 128
CONV_W = 31
CONV_HIST = 32
C_GROUPS = ((128, 1), (512, 4), (2048, 16))
WKV_CHUNK = 64
VMEM_LIMIT = 56 * 1024 * 1024


def _cparams(*sem):
    return pltpu.CompilerParams(dimension_semantics=sem or None, vmem_limit_bytes=VMEM_LIMIT)


def _dot(a, b):
    return jnp.dot(a.astype(BF16), b.astype(BF16), preferred_element_type=F32)


def _dot_nt(a, b):
    return lax.dot_general(a.astype(BF16), b.astype(BF16), (((1,), (1,)), ((), ())), preferred_element_type=F32)


def _dot_tn(a, b):
    return lax.dot_general(a.astype(BF16), b.astype(BF16), (((0,), (0,)), ((), ())), preferred_element_type=F32)


def _head_sum(a, ones_bf16):
    ah = a.astype(BF16)
    al = (a - ah.astype(F32)).astype(BF16)
    lt = ones_bf16.shape[0]
    parts = []
    for lb in range(a.shape[1] // lt):
        sl = slice(lb * lt, (lb + 1) * lt)
        parts.append(jnp.dot(ah[:, sl], ones_bf16, preferred_element_type=F32) +
                     jnp.dot(al[:, sl], ones_bf16, preferred_element_type=F32))
    return jnp.concatenate(parts, axis=1)


def _head_ones():
    n = 2 * HEAD
    r = lax.broadcasted_iota(jnp.int32, (n, n), 0) >> 6
    c = lax.broadcasted_iota(jnp.int32, (n, n), 1) >> 6
    return jnp.where(r == c, 1.0, 0.0).astype(BF16)


def _sigmoid(x):
    return 1.0 / (1.0 + jnp.exp(-x))


def _silu(x):
    return x * _sigmoid(x)


def _softplus(y):
    return jnp.maximum(y, 0.0) + jnp.log(1.0 + jnp.exp(-jnp.abs(y)))


def _layer_norm(x, g, b, eps):
    mu = jnp.mean(x, axis=-1, keepdims=True)
    xc = x - mu
    var = jnp.mean(xc * xc, axis=-1, keepdims=True)
    return xc * lax.rsqrt(var + eps) * g + b


def _even_mix(r, k, v, lora, prm, hsum):
    w0, w2p, a0, a2p, k_k, k_a, r_k = prm
    w_log = -_softplus(-(w0 + _dot(jnp.tanh(lora), w2p))) - 0.5
    lw = -jnp.exp(w_log)
    a = _sigmoid(a0 + _dot(lora, a2p))
    kk = k * k_k
    nrm = jnp.sqrt(_head_sum(kk * kk, hsum))
    kkn = kk / jnp.maximum(nrm, 1e-12)
    km = k * (1.0 + (a - 1.0) * k_a)
    bonus = _head_sum(r * km * r_k, hsum) * v
    return r, lw, km, v, kkn, kkn * a, bonus


def _tap_weight(cw_ref, j, n):
    return jnp.concatenate([cw_ref[8 * j:8 * j + 8, :]] * (n // 8), axis=0)


def _conv_ln_gate(u_ref, sh_ref, row0, n, cw_ref, cb, clg, clb, a_gate):
    acc = jnp.zeros((n, cb.shape[-1]), F32) + cb
    for j in range(CONV_W):
        a, b = divmod(CONV_HIST - (CONV_W - 1) + j, 8)
        win = u_ref[pl.ds(row0 + 8 * a, n), :] if b == 0 else sh_ref[b - 1, pl.ds(row0 + 8 * a, n), :]
        acc = acc + _tap_weight(cw_ref, j, n) * win
    return _silu(_layer_norm(acc, clg, clb, LN_EPS)) * _silu(a_gate)


def _even_front_prompt_body(x_ref, w_ref, mu_ref, cw_ref, cb_ref, clg_ref, clb_ref,
                            ya_ref, bg_ref, r_ref, k_ref, v_ref, lora_ref, conv_ref,
                            p_scr, u_scr, sh_scr, *, tm, bw, sub):
    i = pl.program_id(0)
    shw = 3 * bw + LORA_PAD

    @pl.when(i == 0)
    def _():
        p_scr[0:8, :] = jnp.zeros((8, shw), F32)
        u_scr[0:CONV_HIST, :] = jnp.zeros((CONV_HIST, bw), F32)

    xb = x_ref[...].astype(BF16)
    proj = lambda c0, c1: jnp.dot(xb, w_ref[:, c0:c1], preferred_element_type=F32)
    cur = proj(0, shw)
    p_scr[8:8 + tm, :] = cur
    prev = p_scr[pl.ds(7, tm), :]
    sh = cur + (prev - cur) * mu_ref[...]
    p_scr[7:8, :] = cur[tm - 1:tm, :]
    r_ref[...] = sh[:, 0:bw]
    k_ref[...] = sh[:, bw:2 * bw]
    v_ref[...] = sh[:, 2 * bw:3 * bw]
    lora_ref[...] = sh[:, 3 * bw:shw]

    bg_ref[...] = _silu(proj(shw + 3 * bw, shw + 4 * bw))
    u_scr[CONV_HIST:CONV_HIST + tm, :] = proj(shw, shw + bw) * _sigmoid(proj(shw + bw, shw + 2 * bw))
    a_gate = proj(shw + 2 * bw, shw + 3 * bw)
    for b in range(1, 8):
        sh_scr[b - 1] = u_scr[pl.ds(b, sh_scr.shape[1]), :]
    cb, clg, clb = cb_ref[...], clg_ref[...], clb_ref[...]
    for s in range(tm // sub):
        ya_ref[s * sub:(s + 1) * sub, :] = _conv_ln_gate(u_scr, sh_scr, s * sub, sub, cw_ref, cb, clg, clb,
                                                         a_gate[s * sub:(s + 1) * sub, :]).astype(BF16)
    tail = u_scr[tm:tm + CONV_HIST, :]
    conv_ref[...] = tail
    u_scr[0:CONV_HIST, :] = tail


def _even_front_prompt(x, w_re, mu_re, cw, conv_small, *, tm=256):
    t, d = x.shape
    bw = conv_small[0].shape[-1]
    shw = 3 * bw + LORA_PAD
    ncol = w_re.shape[1]
    row = lambda i: (i, 0)
    fix = lambda i: (0, 0)
    vec = pl.BlockSpec((1, bw), fix)
    in_specs = [pl.BlockSpec((tm, d), row), pl.BlockSpec((d, ncol), fix), pl.BlockSpec((1, shw), fix),
                pl.BlockSpec((8 * CONV_W, bw), fix), vec, vec, vec]
    out = jax.ShapeDtypeStruct((t, bw), F32)
    outs = [jax.ShapeDtypeStruct((t, bw), BF16)] + [out] * 4 + [jax.ShapeDtypeStruct((t, LORA_PAD), F32),
                                                                jax.ShapeDtypeStruct((CONV_HIST, bw), F32)]
    out_specs = [pl.BlockSpec((tm, bw), row)] * 5 + [pl.BlockSpec((tm, LORA_PAD), row),
                                                     pl.BlockSpec((CONV_HIST, bw), fix)]
    body = functools.partial(_even_front_prompt_body, tm=tm, bw=bw, sub=32)
    return pl.pallas_call(
        body, out_shape=outs, grid=(t // tm,), in_specs=in_specs, out_specs=out_specs,
        scratch_shapes=[pltpu.VMEM((8 + tm, shw), F32), pltpu.VMEM((CONV_HIST + tm, bw), F32),
                        pltpu.VMEM((7, CONV_HIST - 8 + tm, bw), F32)],
        compiler_params=_cparams("arbitrary"), name="even_front_prompt")(x, w_re, mu_re, cw, *conv_small)


def _even_front_sample_body(x_ref, hist_ref, w_ref, mu_ref, cw_ref, cb_ref, clg_ref, clb_ref,
                            ya_ref, bg_ref, r_ref, k_ref, v_ref, lora_ref, conv_ref,
                            u_scr, *, nb, ns, bw):
    n = nb * ns
    shw = 3 * bw + LORA_PAD
    nh = CONV_W - 1
    p = jnp.dot(x_ref[...].astype(BF16), w_ref[...], preferred_element_type=F32)
    cur = p[nb:, :shw]
    prev = p[:n, :shw]
    sh = cur + (prev - cur) * mu_ref[...]
    r_ref[...] = sh[:, 0:bw]
    k_ref[...] = sh[:, bw:2 * bw]
    v_ref[...] = sh[:, 2 * bw:3 * bw]
    lora_ref[...] = sh[:, 3 * bw:shw]
    rest = p[nb:, shw:]
    a_val, a_glu = rest[:, 0:bw], rest[:, bw:2 * bw]
    a_gate, b_gate = rest[:, 2 * bw:3 * bw], rest[:, 3 * bw:4 * bw]
    bg_ref[...] = _silu(b_gate)
    u_scr[0:nh * nb, :] = hist_ref[...]
    u_scr[nh * nb:(nh + ns) * nb, :] = a_val * _sigmoid(a_glu)
    cb, clg, clb = cb_ref[...], clg_ref[...], clb_ref[...]
    for s in range(ns):
        acc = jnp.zeros((nb, bw), F32) + cb
        for j in range(CONV_W):
            acc = acc + _tap_weight(cw_ref, j, nb) * u_scr[(s + j) * nb:(s + j + 1) * nb, :]
        ya_ref[s * nb:(s + 1) * nb, :] = (_silu(_layer_norm(acc, clg, clb, LN_EPS)) *
                                           _silu(a_gate[s * nb:(s + 1) * nb, :])).astype(BF16)
    conv_ref[...] = u_scr[ns * nb:(ns + nh) * nb, :]


def _even_front_sample(x_in, hist, w_re, mu_re, cw, conv_small, *, nb, ns):
    bw = conv_small[0].shape[-1]
    n = nb * ns
    nh = CONV_W - 1
    out = jax.ShapeDtypeStruct((n, bw), F32)
    outs = [jax.ShapeDtypeStruct((n, bw), BF16)] + [out] * 4 + [jax.ShapeDtypeStruct((n, LORA_PAD), F32),
                                                                jax.ShapeDtypeStruct((nh * nb, bw), F32)]
    body = functools.partial(_even_front_sample_body, nb=nb, ns=ns, bw=bw)
    return pl.pallas_call(
        body, out_shape=outs, scratch_shapes=[pltpu.VMEM(((nh + ns) * nb, bw), F32)],
        compiler_params=_cparams(), name="even_front_sample")(x_in, hist, w_re, mu_re, cw, *conv_small)


def _wkv_block_terms(r, lw, km, v, kkn, ab, blk):
    n, width = r.shape
    nh = width // HEAD
    ri = lax.broadcasted_iota(jnp.int32, (n, n), 0)
    ci = lax.broadcasted_iota(jnp.int32, (n, n), 1)
    same = _same_block(ri, ci, blk)
    incl = (ri >= ci) & same
    strict = (ri > ci) & same
    cum = _split3_dot(jnp.where(incl, 1.0, 0.0).astype(BF16), lw)
    tot = _split3_dot(jnp.where(same, 1.0, 0.0).astype(BF16), lw)
    dn = jnp.exp(-cum)
    at = -kkn * jnp.exp(cum - lw)
    bt = ab * dn
    kt = km * dn
    rt = r * jnp.exp(cum)
    rem = jnp.exp(tot - cum)
    bc = ab * rem
    kc = km * rem
    wc = jnp.exp(tot)
    hs = [slice(h * HEAD, (h + 1) * HEAD) for h in range(nh)]
    gs = [_dot_nt(jnp.concatenate([at[:, sl], rt[:, sl]], axis=0), jnp.concatenate([bt[:, sl], kt[:, sl]], axis=0))
          for sl in hs]
    l_ab = [jnp.where(strict, g[:n, :n], 0.0) for g in gs]
    l_ak = [jnp.where(strict, g[:n, n:], 0.0) for g in gs]
    l_rb = [jnp.where(incl, g[n:, :n], 0.0) for g in gs]
    l_rk = [jnp.where(incl, g[n:, n:], 0.0) for g in gs]
    toff = _unit_lower_inverse_minus_eye(l_ab, ri, ci, blk)
    lv = [_dot(jnp.concatenate([la, lr], axis=0), v[:, sl]) for la, lr, sl in zip(l_ak, l_rk, hs)]
    ahat = [at[:, sl] + _dot(t, at[:, sl]) for t, sl in zip(toff, hs)]
    uhat = [x[:n] + _dot(t, x[:n]) for t, x in zip(toff, lv)]
    qhat = [rt[:, sl] + _dot(l, a) for l, a, sl in zip(l_rb, ahat, hs)]
    ohat = [_dot(lb, u) + x[n:] for lb, u, x in zip(l_rb, uhat, lv)]
    return list(zip(ahat, uhat, qhat, ohat)), bc, kc, wc


def _same_block(ri, ci, size):
    sh = size.bit_length() - 1
    return (ri >> sh) == (ci >> sh)


def _split(x):
    hi = x.astype(BF16)
    return hi, (x - hi.astype(F32)).astype(BF16)


def _split3_dot(ones_bf16, x):
    hi, lo = _split(x)
    lo2 = (x - hi.astype(F32) - lo.astype(F32)).astype(BF16)
    d = lambda y: jnp.dot(ones_bf16, y, preferred_element_type=F32)
    return d(hi) + (d(lo) + d(lo2))


def _unit_lower_inverse_minus_eye(ls, ri, ci, blk):
    base = 8
    eye = jnp.where(ri == ci, 1.0, 0.0)
    same8 = _same_block(ri, ci, base)
    l8 = [jnp.where(same8, l, 0.0).astype(BF16) for l in ls]
    x2 = [_dot(x, x).astype(BF16) for x in l8]
    x4 = [_dot(x, x).astype(BF16) for x in x2]
    p = [eye + jnp.where(same8, l, 0.0) for l in ls]
    p = [q + _dot(q, x) for q, x in zip(p, x2)]
    t = [q + _dot(q, x) for q, x in zip(p, x4)]
    size = 2 * base
    while size <= blk:
        half = size.bit_length() - 2
        lower_left = _same_block(ri, ci, size) & (((ri >> half) & 1) == 1) & (((ci >> half) & 1) == 0)
        tb = [x.astype(BF16) for x in t]
        mt = [_dot(jnp.where(lower_left, l, 0.0), x) for l, x in zip(ls, tb)]
        t = [x + _dot(xb, m) for x, xb, m in zip(t, tb, mt)]
        size *= 2
    return [x - eye for x in t]


def _wkv_prompt_body(r_ref, k_ref, v_ref, lora_ref, w0_ref, w2_ref, a0_ref, a2_ref, kk_ref, ka_ref, rk_ref, s0_ref,
                     o_ref, bon_ref, sout_ref, s_scr, *, c, nch):
    i = pl.program_id(0)

    @pl.when(i == 0)
    def _():
        s_scr[...] = s0_ref[...]

    prm = (w0_ref[...], w2_ref[...], a0_ref[...], a2_ref[...], kk_ref[...], ka_ref[...], rk_ref[...])
    r, lw, km, v, kkn, ab, bonus = _even_mix(r_ref[...], k_ref[...], v_ref[...], lora_ref[...], prm, _head_ones())
    bon_ref[...] = bonus
    heads, bc, kc, wc = _wkv_block_terms(r, lw, km, v, kkn, ab, c)
    nh = len(heads)
    hs = [slice(h * HEAD, (h + 1) * HEAD) for h in range(nh)]
    vk = [_dot_tn(v[ch * c:(ch + 1) * c, sl], kc[ch * c:(ch + 1) * c, sl]) for ch in range(nch) for sl in hs]
    s = [s_scr[h] for h in range(nh)]
    for ch in range(nch):
        rows = slice(ch * c, (ch + 1) * c)
        mm = [_dot_nt(jnp.concatenate([ahat[rows], qhat[rows]], axis=0), s[h])
              for h, (ahat, _, qhat, _) in enumerate(heads)]
        u = [mm[h][:c] + heads[h][1][rows] for h in range(nh)]
        for h in range(nh):
            o_ref[rows, hs[h]] = mm[h][c:] + heads[h][3][rows]
        s = [s[h] * wc[ch * c:ch * c + 1, hs[h]] + _dot_tn(u[h], bc[rows, hs[h]]) + vk[ch * nh + h]
             for h in range(nh)]
    for h in range(nh):
        s_scr[h] = s[h]

    @pl.when(i == pl.num_programs(0) - 1)
    def _():
        sout_ref[...] = s_scr[...]


def _mix_specs(bw):
    fix = lambda *_: (0, 0)
    vec = pl.BlockSpec((1, bw), fix)
    lora = pl.BlockSpec((LORA_PAD, bw), fix)
    return [vec, lora, vec, lora, vec, vec, vec]


def _wkv_prompt(r, k, v, lora, mix_small, s0, *, nch=2):
    t, bw = r.shape
    c = WKV_CHUNK
    nh = bw // HEAD
    row = pl.BlockSpec((nch * c, bw), lambda i: (i, 0))
    lrow = pl.BlockSpec((nch * c, LORA_PAD), lambda i: (i, 0))
    st = pl.BlockSpec((nh, HEAD, HEAD), lambda i: (0, 0, 0))
    out = jax.ShapeDtypeStruct((t, bw), F32)
    return pl.pallas_call(
        functools.partial(_wkv_prompt_body, c=c, nch=nch),
        out_shape=[out, out, jax.ShapeDtypeStruct((nh, HEAD, HEAD), F32)],
        grid=(t // (nch * c),), in_specs=[row] * 3 + [lrow] + _mix_specs(bw) + [st], out_specs=[row, row, st],
        scratch_shapes=[pltpu.VMEM((nh, HEAD, HEAD), F32)],
        compiler_params=_cparams("arbitrary"), name="wkv_prompt")(r, k, v, lora, *mix_small, s0)


def _wkv_sample_body(r_ref, k_ref, v_ref, lora_ref, w0_ref, w2_ref, a0_ref, a2_ref, kk_ref, ka_ref, rk_ref, s0_ref,
                     o_ref, bon_ref, sout_ref,
                     ah_scr, uh_scr, qh_scr, oh_scr, bc_scr, kc_scr, wc_scr, *, nb, blk, ns):
    prm = (w0_ref[...], w2_ref[...], a0_ref[...], a2_ref[...], kk_ref[...], ka_ref[...], rk_ref[...])
    r, lw, km, v, kkn, ab, bonus = _even_mix(r_ref[...], k_ref[...], v_ref[...], lora_ref[...], prm, _head_ones())
    real = (lax.broadcasted_iota(jnp.int32, lw.shape, 0) & (blk - 1)) < ns
    lw = jnp.where(real, lw, 0.0)
    bon_ref[...] = bonus
    heads, bc, kc, wc = _wkv_block_terms(r, lw, km, v, kkn, ab, blk)
    bc_scr[...] = bc
    kc_scr[...] = kc
    wc_scr[...] = wc
    for h, (ahat, uhat, qhat, ohat) in enumerate(heads):
        sl = slice(h * HEAD, (h + 1) * HEAD)
        ah_scr[:, sl] = ahat
        uh_scr[:, sl] = uhat
        qh_scr[:, sl] = qhat
        oh_scr[:, sl] = ohat
    nh = len(heads)

    def per_seq(b, carry):
        rows = pl.ds(pl.multiple_of(b * blk, blk), blk)
        for h in range(nh):
            sl = slice(h * HEAD, (h + 1) * HEAD)
            s = s0_ref[b, h]
            mm = _dot_nt(jnp.concatenate([ah_scr[rows, sl], qh_scr[rows, sl]], axis=0), s)
            u = mm[:blk] + uh_scr[rows, sl]
            o_ref[rows, sl] = mm[blk:] + oh_scr[rows, sl]
            wrow = wc_scr[rows, sl][0:1, :]
            sout_ref[b, h] = s * wrow + _dot_tn(u, bc_scr[rows, sl]) + _dot_tn(v_ref[rows, sl], kc_scr[rows, sl])
        return carry

    lax.fori_loop(0, nb, per_seq, 0)


def _wkv_sample(r, k, v, lora, mix_small, s0, *, blk, ns):
    n, bw = r.shape
    nb = n // blk
    scr = pltpu.VMEM((n, bw), F32)
    out = jax.ShapeDtypeStruct((n, bw), F32)
    return pl.pallas_call(
        functools.partial(_wkv_sample_body, nb=nb, blk=blk, ns=ns),
        out_shape=[out, out, jax.ShapeDtypeStruct(s0.shape, F32)],
        scratch_shapes=[scr] * 7, compiler_params=_cparams(), name="wkv_sample")(r, k, v, lora, *mix_small, s0)


def _even_back_rows(o, bonus, bg, ya, x, lg, lb, w_ref, g, b, hsum, *, alpha, bw):
    om = _head_sum(o, hsum) * (1.0 / HEAD)
    oc = o - om
    ov = _head_sum(oc * oc, hsum) * (1.0 / HEAD)
    on = oc * lax.rsqrt(ov + LNX_EPS) * lg + lb
    yb = (on + bonus) * bg
    y = _dot(ya, w_ref[0:bw, :]) + _dot(yb, w_ref[bw:2 * bw, :])
    return _layer_norm(alpha * x + y, g, b, LN_EPS)


def _project_class_block(xb, w_ref, blk, cls_ref, p_scr, *, tm, gw, d, scale):
    pc = jnp.dot(xb, w_ref[:, blk * gw:(blk + 1) * gw], preferred_element_type=F32)
    if scale is not None:
        pc = pc * scale
    if d == 1:
        cls_ref[0] = pc.astype(BF16)
        return
    lt = p_scr.shape[-1]
    nlb = gw // lt
    slot = (blk % 2) * nlb
    for lb in range(nlb):
        p_scr[slot + lb] = pc[:, lb * lt:(lb + 1) * lt]
    n = tm // d
    for c in range(d):
        for lb in range(nlb):
            cls_ref[c, :, lb * lt:(lb + 1) * lt] = p_scr[slot + lb, pl.ds(c, n, stride=d), :].astype(BF16)


def _mid_body(o_ref, bon_ref, bg_ref, ya_ref, x_ref, lg_ref, lb_ref, wo_ref, g_ref, b_ref, wi_ref, x1_ref, *refs,
              tm, sub, gw, bw, dils, alpha, scale):
    ng = len(dils)
    cls_refs = refs[0:3 * ng]
    gate_ref, xb_scr, p_scr = refs[3 * ng:3 * ng + 3]
    i = pl.program_id(0)
    slot = i % 2

    @pl.when(i == 0)
    def _():
        xb_scr[1] = jnp.zeros(xb_scr.shape[1:], BF16)

    xb = xb_scr[1 - slot]
    hsum = _head_ones()
    nsub = tm // sub
    nblk = 3 * ng + 1
    for blk in range(nblk):
        if blk < 3 * ng:
            g = blk % ng
            _project_class_block(xb, wi_ref, blk, cls_refs[blk], p_scr, tm=tm, gw=gw, d=dils[g],
                                 scale=scale if blk < ng else None)
        else:
            gate_ref[...] = _silu(jnp.dot(xb, wi_ref[:, blk * gw:(blk + 1) * gw], preferred_element_type=F32))
        for s in range(nsub):
            if s * nblk // nsub != blk:
                continue
            rows = slice(s * sub, (s + 1) * sub)
            x1 = _even_back_rows(o_ref[rows, :], bon_ref[rows, :], bg_ref[rows, :], ya_ref[rows, :], x_ref[rows, :],
                                 lg_ref[...], lb_ref[...], wo_ref, g_ref[...], b_ref[...], hsum, alpha=alpha, bw=bw)
            x1_ref[rows, :] = x1
            xb_scr[slot, rows, :] = x1.astype(BF16)


def _mid(o, bonus, bg, ya, x, lnx_g, lnx_b, w_out, ln_g, ln_b, w_in, *, alpha, tm, dils, gw):
    t, d_model = x.shape
    bw = o.shape[-1]
    ng = len(dils)
    nt = t // tm
    cur = lambda i: (jnp.minimum(i, nt - 1), 0)
    prev = lambda i: (jnp.maximum(i - 1, 0), 0)
    fix = lambda i: (0, 0)
    half = pl.BlockSpec((tm, bw), cur)
    once = lambda shape: pl.BlockSpec(shape, fix, pipeline_mode=pl.Buffered(1))
    in_specs = [half] * 4 + [pl.BlockSpec((tm, d_model), cur), once((1, bw)), once((1, bw)), once((2 * bw, d_model)),
                             once((1, d_model)), once((1, d_model)), once(w_in.shape)]
    cls_specs = [pl.BlockSpec((d, tm // d, gw), lambda i: (0, jnp.maximum(i - 1, 0), 0)) for d in dils]
    cls_shapes = [jax.ShapeDtypeStruct((d, t // d, gw), BF16) for d in dils]
    outs = [jax.ShapeDtypeStruct((t, d_model), F32)] + cls_shapes * 3 + [jax.ShapeDtypeStruct((t, gw), F32)]
    out_specs = [pl.BlockSpec((tm, d_model), cur)] + cls_specs * 3 + [pl.BlockSpec((tm, gw), prev)]
    body = functools.partial(_mid_body, tm=tm, sub=min(tm, 128), gw=gw, bw=bw, dils=dils, alpha=alpha,
                             scale=HEAD ** -0.5)
    res = pl.pallas_call(
        body, out_shape=outs, grid=(nt + 1,), in_specs=in_specs, out_specs=out_specs,
        scratch_shapes=[pltpu.VMEM((2, tm, d_model), BF16), pltpu.VMEM((2 * gw // 128, tm, 128), F32)],
        compiler_params=_cparams("arbitrary"), name="mid")(o, bonus, bg, ya, x, lnx_g, lnx_b, w_out, ln_g, ln_b, w_in)
    return res[0], res[1:1 + ng], res[1 + ng:1 + 2 * ng], res[1 + 2 * ng:1 + 3 * ng], res[1 + 3 * ng]


def _kv_rows_body(x_ref, w_ref, kv_ref):
    kv_ref[...] = jnp.dot(x_ref[...].astype(BF16), w_ref[...], preferred_element_type=F32)


def _kv_rows(x, w_kv, *, tm):
    t, d_model = x.shape
    ncol = w_kv.shape[1]
    return pl.pallas_call(
        _kv_rows_body, out_shape=jax.ShapeDtypeStruct((t, ncol), F32), grid=(t // tm,),
        in_specs=[pl.BlockSpec((tm, d_model), lambda i: (i, 0)), pl.BlockSpec((d_model, ncol), lambda i: (0, 0))],
        out_specs=pl.BlockSpec((tm, ncol), lambda i: (i, 0)),
        compiler_params=_cparams("parallel"), name="kv_rows")(x, w_kv)


def _band_attn_body(q_ref, kc_ref, kp_ref, vc_ref, vp_ref, o_ref, lse_ref, *, bq, span, nb_class):
    j = pl.program_id(0)
    first_lo = jnp.where(((j * bq) % nb_class) == 0, span, 0)
    qi = lax.broadcasted_iota(jnp.int32, (span, 2 * span), 0)
    ki = lax.broadcasted_iota(jnp.int32, (span, 2 * span), 1)
    in_band = (ki >= qi) & (ki <= qi + span)
    lane = lax.broadcasted_iota(jnp.int32, (span, 2 * HEAD), 1)
    left = lane < HEAD
    gw = q_ref.shape[-1]
    for b in range(bq):
        mask = in_band & (ki >= first_lo) if b == 0 else in_band
        for pr in range(gw // (2 * HEAD)):
            ls = slice(pr * 2 * HEAD, (pr + 1) * 2 * HEAD)
            q2 = q_ref[b * span:(b + 1) * span, ls]
            if b == 0:
                k2 = jnp.concatenate([kp_ref[:, ls], kc_ref[0:span, ls]], axis=0)
                v2 = jnp.concatenate([vp_ref[:, ls], vc_ref[0:span, ls]], axis=0)
            else:
                k2 = kc_ref[(b - 1) * span:(b + 1) * span, ls]
                v2 = vc_ref[(b - 1) * span:(b + 1) * span, ls]
            outs, lses = [], []
            for sel in (left, jnp.logical_not(left)):
                qh = jnp.where(sel, q2, jnp.zeros_like(q2))
                s = lax.dot_general(qh, k2, (((1,), (1,)), ((), ())), preferred_element_type=F32)
                s = jnp.where(mask, s, -jnp.inf)
                m = jnp.max(s, axis=-1, keepdims=True)
                p = jnp.exp(s - m)
                l = jnp.sum(p, axis=-1, keepdims=True)
                outs.append(jnp.dot(p.astype(BF16), v2, preferred_element_type=F32) / l)
                lses.append(m + jnp.log(l))
            o_ref[b * span:(b + 1) * span, ls] = jnp.where(left, outs[0], outs[1])
            lse_ref[b * span:(b + 1) * span, ls] = jnp.where(left, lses[0], lses[1])


def _band_attn(q, k, v, *, span, nb_class, bq=4):
    t, gw = q.shape
    cur = pl.BlockSpec((bq * span, gw), lambda j: (j, 0))
    prev = pl.BlockSpec((span, gw), lambda j: (jnp.maximum(j * bq - 1, 0), 0))
    out = jax.ShapeDtypeStruct((t, gw), F32)
    return pl.pallas_call(
        functools.partial(_band_attn_body, bq=bq, span=span, nb_class=nb_class),
        out_shape=[out, out], grid=(t // (bq * span),), in_specs=[cur, cur, prev, cur, prev],
        out_specs=[cur, cur], compiler_params=_cparams("parallel"), name="band_attn")(q, k, k, v, v)


def _cache_attn_body(q_ref, kn_ref, vn_ref, nt_ref, c_ref, o_ref, lse_ref, cout_ref, *, w, d, window, ns, gw, rchunk):
    nh = gw // HEAD
    rp = 8
    zrow = jnp.zeros((rp - ns, gw), F32)
    q8 = jnp.concatenate([q_ref[0], zrow], axis=0)
    kn8 = jnp.concatenate([kn_ref[0], zrow], axis=0).astype(BF16)
    vn8 = jnp.concatenate([vn_ref[0], zrow], axis=0).astype(BF16)
    qm = jnp.concatenate([q8] * nh, axis=0)
    rowh = lax.broadcasted_iota(jnp.int32, (nh * rp, gw), 0) >> 3
    laneh = lax.broadcasted_iota(jnp.int32, (nh * rp, gw), 1) >> 6
    own = rowh == laneh
    qm = jnp.where(own, qm, 0.0).astype(BF16)
    s_c = jnp.dot(qm, c_ref[0, 0:gw, :].astype(BF16), preferred_element_type=F32)
    s_n = lax.dot_general(qm, kn8, (((1,), (1,)), ((), ())), preferred_element_type=F32)
    srow = lax.broadcasted_iota(jnp.int32, s_c.shape, 0) & (rp - 1)
    dist = w + srow - lax.broadcasted_iota(jnp.int32, s_c.shape, 1)
    valid_c = (dist <= window) & ((dist & (d - 1)) == 0) & (srow < ns)
    srow_n = lax.broadcasted_iota(jnp.int32, s_n.shape, 0) & (rp - 1)
    col_n = lax.broadcasted_iota(jnp.int32, s_n.shape, 1)
    dist_n = srow_n - col_n
    valid_n = (dist_n >= 0) & ((dist_n & (d - 1)) == 0) & (srow_n < ns)
    valid_n = valid_n | ((srow_n >= ns) & (col_n == 0))
    s_c = jnp.where(valid_c, s_c, -jnp.inf)
    s_n = jnp.where(valid_n, s_n, -jnp.inf)
    m = jnp.maximum(jnp.max(s_c, axis=-1, keepdims=True), jnp.max(s_n, axis=-1, keepdims=True))
    p_c = jnp.exp(s_c - m)
    p_n = jnp.exp(s_n - m)
    l = jnp.sum(p_c, axis=-1, keepdims=True) + jnp.sum(p_n, axis=-1, keepdims=True)
    o = lax.dot_general(p_c.astype(BF16), c_ref[0, gw:2 * gw, :].astype(BF16), (((1,), (1,)), ((), ())),
                        preferred_element_type=F32)
    o = (o + jnp.dot(p_n.astype(BF16), vn8, preferred_element_type=F32)) / l
    lse = jnp.broadcast_to(m + jnp.log(l), o.shape)
    o = jnp.where(own, o, 0.0)
    lse = jnp.where(own, lse, 0.0)
    o8 = o[0:rp]
    l8 = lse[0:rp]
    for h in range(1, nh):
        o8 = o8 + o[h * rp:(h + 1) * rp]
        l8 = l8 + lse[h * rp:(h + 1) * rp]
    o_ref[0] = o8[0:ns]
    lse_ref[0] = l8[0:ns]
    for rb in range(2 * gw // rchunk):
        rows = slice(rb * rchunk, (rb + 1) * rchunk)
        cout_ref[0, rows, :] = jnp.concatenate([c_ref[0, rows, ns:w], nt_ref[0, rows, 0:ns]], axis=1)


def _cache_attn(q, kn, vn, new_t, cache_t, *, d, window):
    nbat, ns, gw = q.shape
    w = cache_t.shape[2]
    new = pl.BlockSpec((1, ns, gw), lambda b: (b, 0, 0))
    ntb = pl.BlockSpec((1, 2 * gw, new_t.shape[2]), lambda b: (b, 0, 0))
    cb = pl.BlockSpec((1, 2 * gw, w), lambda b: (b, 0, 0))
    out = jax.ShapeDtypeStruct((nbat, ns, gw), F32)
    return pl.pallas_call(
        functools.partial(_cache_attn_body, w=w, d=d, window=window, ns=ns, gw=gw, rchunk=64),
        out_shape=[out, out, jax.ShapeDtypeStruct(cache_t.shape, F32)], grid=(nbat,),
        in_specs=[new, new, new, ntb, cb], out_specs=[new, new, cb],
        compiler_params=_cparams("parallel"), name="cache_attn")(q, kn, vn, new_t, cache_t)


def _odd_back_body(*refs, tm, dils, alpha):
    ng = len(dils)
    o_refs, l_refs = refs[0:ng], refs[ng:2 * ng]
    gate_ref, x_ref, w_ref, g_ref, b_ref, y_ref = refs[2 * ng:2 * ng + 6]
    scr = refs[2 * ng + 6:]
    os_, ls_ = [], []
    for g, d in enumerate(dils):
        if d == 1:
            os_.append(o_refs[g][0])
            ls_.append(l_refs[g][0])
        else:
            n = tm // d
            nlb, _, lt = scr[2 * g].shape
            for c in range(d):
                for lb in range(nlb):
                    scr[2 * g][lb, pl.ds(c, n, stride=d), :] = o_refs[g][c, :, lb * lt:(lb + 1) * lt]
                    scr[2 * g + 1][lb, pl.ds(c, n, stride=d), :] = l_refs[g][c, :, lb * lt:(lb + 1) * lt]
            os_.append(jnp.concatenate([scr[2 * g][lb] for lb in range(nlb)], axis=1))
            ls_.append(jnp.concatenate([scr[2 * g + 1][lb] for lb in range(nlb)], axis=1))
    m = ls_[0]
    for l in ls_[1:]:
        m = jnp.maximum(m, l)
    es = [jnp.exp(l - m) for l in ls_]
    den = es[0]
    for e in es[1:]:
        den = den + e
    o = es[0] * os_[0]
    for e, og in zip(es[1:], os_[1:]):
        o = o + e * og
    o = o / den
    y = _dot(o * gate_ref[...], w_ref[...])
    y_ref[...] = _layer_norm(alpha * x_ref[...] + y, g_ref[...], b_ref[...], LN_EPS)


def _odd_back(os_, ls_, gate, x, w_out, ln_g, ln_b, *, dils, alpha, tm):
    t, d_model = x.shape
    gw = gate.shape[-1]
    row = lambda i: (i, 0)
    fix = lambda i: (0, 0)
    cls_specs = [pl.BlockSpec((d, tm // d, gw), lambda i: (0, i, 0)) for d in dils]
    in_specs = cls_specs * 2 + [pl.BlockSpec((tm, gw), row), pl.BlockSpec((tm, d_model), row),
                                pl.BlockSpec((gw, d_model), fix), pl.BlockSpec((1, d_model), fix),
                                pl.BlockSpec((1, d_model), fix)]
    return pl.pallas_call(
        functools.partial(_odd_back_body, tm=tm, dils=dils, alpha=alpha),
        out_shape=jax.ShapeDtypeStruct((t, d_model), F32), grid=(t // tm,), in_specs=in_specs,
        out_specs=pl.BlockSpec((tm, d_model), row),
        scratch_shapes=[pltpu.VMEM((gw // 128, tm, 128), F32)] * (2 * len(dils)),
        compiler_params=_cparams("parallel"), name="odd_back")(*os_, *ls_, gate, x, w_out, ln_g, ln_b)


def kernel(x_prompt, x_sample, state_conv, state_shift, state_wkv, cache_kv_w128, cache_kv_w512, cache_kv_w2048,
           w_in_even, conv_w, conv_b, conv_ln_g, conv_ln_b, mu_shift, w0, w2, a0, a2, k_k, k_a, r_k, lnx_g, lnx_b,
           w_out_even, w_in_odd, w_out_odd, ln_g, ln_b):
    nbp, seq, d_model = x_prompt.shape
    nbs, ns, _ = x_sample.shape
    assert nbp == 1
    depth = ln_g.shape[0]
    assert depth == 2 and w_in_even.shape[0] == 1 and w_in_odd.shape[0] == 1
    alpha = (2.0 * depth) ** 0.25
    bw = w0.shape[-1]
    lora_d, lora_a = w2.shape[1], a2.shape[1]
    nh = bw // HEAD
    nhist = CONV_W - 1
    caches = (cache_kv_w128, cache_kv_w512, cache_kv_w2048)
    dils = tuple(d for _, d in C_GROUPS)
    gw = w_out_odd.shape[1]
    ng = len(C_GROUPS)

    wi = w_in_even[0]
    sh_cols = 3 * bw + lora_d + lora_a
    zpad = jnp.zeros((d_model, LORA_PAD - lora_d - lora_a), F32)
    w_re = jnp.concatenate([wi[:, :sh_cols], zpad, wi[:, sh_cols:]], axis=1).astype(BF16)
    mu_re = jnp.concatenate([mu_shift[0], jnp.zeros((LORA_PAD - lora_d - lora_a,), F32)])[None, :]
    w2p = jnp.zeros((LORA_PAD, bw), F32).at[0:lora_d].set(w2[0])
    a2p = jnp.zeros((LORA_PAD, bw), F32).at[lora_d:lora_d + lora_a].set(a2[0])
    cw = jnp.repeat(conv_w[0], 8, axis=0)
    vec = lambda z: z.reshape(1, -1)
    conv_small = (vec(conv_b[0]), vec(conv_ln_g[0]), vec(conv_ln_b[0]))
    mix_small = (vec(w0[0]), w2p, vec(a0[0]), a2p, vec(k_k[0]), vec(k_a[0]), vec(r_k[0]))
    w_out_e = w_out_even[0].astype(BF16)
    w_in_o = w_in_odd[0].astype(BF16)
    w_out_o = w_out_odd[0].astype(BF16)
    lng = [vec(ln_g[l]) for l in range(depth)]
    lnb = [vec(ln_b[l]) for l in range(depth)]

    xp = x_prompt[0]
    ya, bg, r, k, v, lora, conv_tail = _even_front_prompt(xp, w_re, mu_re, cw, conv_small)
    o_wkv, bonus, wkv_p = _wkv_prompt(r, k, v, lora, mix_small, jnp.zeros((nh, HEAD, HEAD), F32))
    xp1, qc, kc, vc, gate_p = _mid(o_wkv, bonus, bg, ya, xp, vec(lnx_g[0]), vec(lnx_b[0]), w_out_e, lng[0], lnb[0],
                                   w_in_o, alpha=alpha, tm=512, dils=dils, gw=gw)
    conv_p = conv_tail[CONV_HIST - nhist:][None, None]
    shift_p = x_prompt[:, -1][None]

    xs_t = jnp.transpose(x_sample, (1, 0, 2)).reshape(ns * nbs, d_model)
    x_in = jnp.concatenate([state_shift[0], xs_t], axis=0)
    hist = jnp.transpose(state_conv[0], (1, 0, 2)).reshape(nhist * nbs, bw)
    res = _even_front_sample(x_in, hist, w_re, mu_re, cw, conv_small, nb=nbs, ns=ns)
    ya_s, bg_s, conv_s_t = res[0], res[1], res[6]
    blk = 8

    def seq_major_pad(z):
        z = jnp.transpose(z.reshape(ns, nbs, z.shape[-1]), (1, 0, 2))
        return jnp.pad(z, ((0, 0), (0, blk - ns), (0, 0))).reshape(nbs * blk, z.shape[-1])

    def step_major(z):
        return jnp.transpose(z.reshape(nbs, blk, z.shape[-1])[:, :ns], (1, 0, 2)).reshape(ns * nbs, z.shape[-1])

    wkv_in = [seq_major_pad(z) for z in res[2:6]]
    o_s_pad, bon_s_pad, wkv_s = _wkv_sample(*wkv_in, mix_small, state_wkv[0], blk=blk, ns=ns)
    o_s, bon_s = step_major(o_s_pad), step_major(bon_s_pad)
    ones = (1,) * ng
    xs1_t, qs_t, _, _, gate_s_t = _mid(o_s, bon_s, bg_s, ya_s, xs_t, vec(lnx_g[0]), vec(lnx_b[0]), w_out_e, lng[0],
                                       lnb[0], w_in_o, alpha=alpha, tm=ns * nbs, dils=ones, gw=gw)
    conv_s = jnp.transpose(conv_s_t.reshape(nhist, nbs, bw), (1, 0, 2))[None]
    shift_s = x_sample[:, -1][None]

    def seq_major(z):
        return jnp.transpose(z.reshape(ns, nbs, z.shape[-1]), (1, 0, 2)).reshape(nbs * ns, z.shape[-1])

    xs1 = seq_major(xs1_t)
    gate_s = seq_major(gate_s_t)
    qs = [seq_major(q[0]) for q in qs_t]

    tm1 = 256
    tail = min(max(win for win, _ in C_GROUPS), seq)
    w_kv = w_in_o[:, ng * gw:3 * ng * gw]
    kvf = _kv_rows(xp1[seq - tail:], w_kv, tm=512)
    kf, vf = kvf[:, :ng * gw], kvf[:, ng * gw:]
    os_, ls_ = [], []
    for g, (win, d) in enumerate(C_GROUPS):
        span = win // d
        flat = lambda z: z.reshape(seq, gw)
        o_g, l_g = _band_attn(flat(qc[g]), flat(kc[g]), flat(vc[g]), span=span, nb_class=(seq // d) // span)
        os_.append(o_g.reshape(d, seq // d, gw))
        ls_.append(l_g.reshape(d, seq // d, gw))
    xp2 = _odd_back(os_, ls_, gate_p, xp1, w_out_o, lng[1], lnb[1], dils=dils, alpha=alpha, tm=tm1)
    kv_p = []
    for g, (win, d) in enumerate(C_GROUPS):
        n = min(win, seq)
        kg = kf[tail - n:, g * gw:(g + 1) * gw].reshape(n, gw // HEAD, HEAD)
        vg = vf[tail - n:, g * gw:(g + 1) * gw].reshape(n, gw // HEAD, HEAD)
        kv_p.append(jnp.stack([kg, vg], axis=1)[None, None])

    kvf_s = _kv_rows(xs1, w_kv, tm=nbs * ns)
    kf_s, vf_s = kvf_s[:, :ng * gw], kvf_s[:, ng * gw:]
    os_s, ls_s, kv_s_out = [], [], []
    for g, (win, d) in enumerate(C_GROUPS):
        cache = caches[g][0]
        w = cache.shape[1]
        q_g = qs[g].astype(F32).reshape(nbs, ns, gw)
        kn = kf_s[:, g * gw:(g + 1) * gw].reshape(nbs, ns, gw)
        vn = vf_s[:, g * gw:(g + 1) * gw].reshape(nbs, ns, gw)
        cache_t = jnp.transpose(cache.reshape(nbs, w, 2 * gw), (0, 2, 1))
        new_t = jnp.transpose(jnp.concatenate([kn, vn], axis=2), (0, 2, 1))
        new_t = jnp.pad(new_t, ((0, 0), (0, 0), (0, 8 - ns)))
        o_g, l_g, cnew_t = _cache_attn(q_g, kn, vn, new_t, cache_t, d=d, window=win)
        os_s.append(o_g.reshape(1, nbs * ns, gw))
        ls_s.append(l_g.reshape(1, nbs * ns, gw))
        kv_s_out.append(jnp.transpose(cnew_t, (0, 2, 1)).reshape(nbs, w, 2, gw // HEAD, HEAD)[None])
    xs2 = _odd_back(os_s, ls_s, gate_s, xs1, w_out_o, lng[1], lnb[1], dils=ones, alpha=alpha, tm=nbs * ns)

    return (xp2[None], xs2.reshape(nbs, ns, d_model), conv_p, conv_s, shift_p, shift_s,
            wkv_p[None, None], wkv_s[None],
            kv_p[0], kv_s_out[0], kv_p[1], kv_s_out[1], kv_p[2], kv_s_out[2])
```

```python
import functools

import jax
import jax.numpy as jnp
from jax import lax
from jax.experimental import pallas as pl
from jax.experimental.pallas import tpu as pltpu

F32 = jnp.float32
BF16 = jnp.bfloat16

LN_EPS = 1e-5
LNX_EPS = 64e-5
HEAD = 64
LORA_PAD = 128
CONV_W = 31
CONV_HIST = 32
C_GROUPS = ((128, 1), (512, 4), (2048, 16))
WKV_CHUNK = 64
VMEM_LIMIT = 56 * 1024 * 1024


def _cparams(*sem):
    return pltpu.CompilerParams(dimension_semantics=sem or None, vmem_limit_bytes=VMEM_LIMIT)


def _dot(a, b):
    return jnp.dot(a.astype(BF16), b.astype(BF16), preferred_element_type=F32)


def _dot_nt(a, b):
    return lax.dot_general(a.astype(BF16), b.astype(BF16), (((1,), (1,)), ((), ())), preferred_element_type=F32)


def _dot_tn(a, b):
    return lax.dot_general(a.astype(BF16), b.astype(BF16), (((0,), (0,)), ((), ())), preferred_element_type=F32)


def _head_sum(a, ones_bf16):
    ah = a.astype(BF16)
    al = (a - ah.astype(F32)).astype(BF16)
    lt = ones_bf16.shape[0]
    parts = []
    for lb in range(a.shape[1] // lt):
        sl = slice(lb * lt, (lb + 1) * lt)
        parts.append(jnp.dot(ah[:, sl], ones_bf16, preferred_element_type=F32) +
                     jnp.dot(al[:, sl], ones_bf16, preferred_element_type=F32))
    return jnp.concatenate(parts, axis=1)


def _head_ones():
    n = 2 * HEAD
    r = lax.broadcasted_iota(jnp.int32, (n, n), 0) >> 6
    c = lax.broadcasted_iota(jnp.int32, (n, n), 1) >> 6
    return jnp.where(r == c, 1.0, 0.0).astype(BF16)


def _sigmoid(x):
    return 1.0 / (1.0 + jnp.exp(-x))


def _silu(x):
    return x * _sigmoid(x)


def _softplus(y):
    return jnp.maximum(y, 0.0) + jnp.log(1.0 + jnp.exp(-jnp.abs(y)))


def _layer_norm(x, g, b, eps):
    mu = jnp.mean(x, axis=-1, keepdims=True)
    xc = x - mu
    var = jnp.mean(xc * xc, axis=-1, keepdims=True)
    return xc * lax.rsqrt(var + eps) * g + b


def _even_mix(r, k, v, lora, prm, hsum):
    w0, w2p, a0, a2p, k_k, k_a, r_k = prm
    w_log = -_softplus(-(w0 + _dot(jnp.tanh(lora), w2p))) - 0.5
    lw = -jnp.exp(w_log)
    a = _sigmoid(a0 + _dot(lora, a2p))
    kk = k * k_k
    nrm = jnp.sqrt(_head_sum(kk * kk, hsum))
    kkn = kk / jnp.maximum(nrm, 1e-12)
    km = k * (1.0 + (a - 1.0) * k_a)
    bonus = _head_sum(r * km * r_k, hsum) * v
    return r, lw, km, v, kkn, kkn * a, bonus


def _tap_weight(cw_ref, j, n):
    return jnp.concatenate([cw_ref[8 * j:8 * j + 8, :]] * (n // 8), axis=0)


def _conv_ln_gate(u_ref, sh_ref, row0, n, cw_ref, cb, clg, clb, a_gate):
    acc = jnp.zeros((n, cb.shape[-1]), F32) + cb
    for j in range(CONV_W):
        a, b = divmod(CONV_HIST - (CONV_W - 1) + j, 8)
        win = u_ref[pl.ds(row0 + 8 * a, n), :] if b == 0 else sh_ref[b - 1, pl.ds(row0 + 8 * a, n), :]
        acc = acc + _tap_weight(cw_ref, j, n) * win
    return _silu(_layer_norm(acc, clg, clb, LN_EPS)) * _silu(a_gate)


def _even_front_prompt_body(x_ref, w_ref, mu_ref, cw_ref, cb_ref, clg_ref, clb_ref,
                            ya_ref, bg_ref, r_ref, k_ref, v_ref, lora_ref, conv_ref,
                            p_scr, u_scr, sh_scr, *, tm, bw, sub):
    i = pl.program_id(0)
    shw = 3 * bw + LORA_PAD

    @pl.when(i == 0)
    def _():
        p_scr[0:8, :] = jnp.zeros((8, shw), F32)
        u_scr[0:CONV_HIST, :] = jnp.zeros((CONV_HIST, bw), F32)

    xb = x_ref[...].astype(BF16)
    proj = lambda c0, c1: jnp.dot(xb, w_ref[:, c0:c1], preferred_element_type=F32)
    cur = proj(0, shw)
    p_scr[8:8 + tm, :] = cur
    prev = p_scr[pl.ds(7, tm), :]
    sh = cur + (prev - cur) * mu_ref[...]
    p_scr[7:8, :] = cur[tm - 1:tm, :]
    r_ref[...] = sh[:, 0:bw]
    k_ref[...] = sh[:, bw:2 * bw]
    v_ref[...] = sh[:, 2 * bw:3 * bw]
    lora_ref[...] = sh[:, 3 * bw:shw]

    bg_ref[...] = _silu(proj(shw + 3 * bw, shw + 4 * bw))
    u_scr[CONV_HIST:CONV_HIST + tm, :] = proj(shw, shw + bw) * _sigmoid(proj(shw + bw, shw + 2 * bw))
    a_gate = proj(shw + 2 * bw, shw + 3 * bw)
    for b in range(1, 8):
        sh_scr[b - 1] = u_scr[pl.ds(b, sh_scr.shape[1]), :]
    cb, clg, clb = cb_ref[...], clg_ref[...], clb_ref[...]
    for s in range(tm // sub):
        ya_ref[s * sub:(s + 1) * sub, :] = _conv_ln_gate(u_scr, sh_scr, s * sub, sub, cw_ref, cb, clg, clb,
                                                         a_gate[s * sub:(s + 1) * sub, :]).astype(BF16)
    tail = u_scr[tm:tm + CONV_HIST, :]
    conv_ref[...] = tail
    u_scr[0:CONV_HIST, :] = tail


def _even_front_prompt(x, w_re, mu_re, cw, conv_small, *, tm=256):
    t, d = x.shape
    bw = conv_small[0].shape[-1]
    shw = 3 * bw + LORA_PAD
    ncol = w_re.shape[1]
    row = lambda i: (i, 0)
    fix = lambda i: (0, 0)
    vec = pl.BlockSpec((1, bw), fix)
    in_specs = [pl.BlockSpec((tm, d), row), pl.BlockSpec((d, ncol), fix), pl.BlockSpec((1, shw), fix),
                pl.BlockSpec((8 * CONV_W, bw), fix), vec, vec, vec]
    out = jax.ShapeDtypeStruct((t, bw), F32)
    outs = [jax.ShapeDtypeStruct((t, bw), BF16)] + [out] * 4 + [jax.ShapeDtypeStruct((t, LORA_PAD), F32),
                                                                jax.ShapeDtypeStruct((CONV_HIST, bw), F32)]
    out_specs = [pl.BlockSpec((tm, bw), row)] * 5 + [pl.BlockSpec((tm, LORA_PAD), row),
                                                     pl.BlockSpec((CONV_HIST, bw), fix)]
    body = functools.partial(_even_front_prompt_body, tm=tm, bw=bw, sub=32)
    return pl.pallas_call(
        body, out_shape=outs, grid=(t // tm,), in_specs=in_specs, out_specs=out_specs,
        scratch_shapes=[pltpu.VMEM((8 + tm, shw), F32), pltpu.VMEM((CONV_HIST + tm, bw), F32),
                        pltpu.VMEM((7, CONV_HIST - 8 + tm, bw), F32)],
        compiler_params=_cparams("arbitrary"), name="even_front_prompt")(x, w_re, mu_re, cw, *conv_small)


def _even_front_sample_body(x_ref, hist_ref, w_ref, mu_ref, cw_ref, cb_ref, clg_ref, clb_ref,
                            ya_ref, bg_ref, r_ref, k_ref, v_ref, lora_ref, conv_ref,
                            u_scr, *, nb, ns, bw):
    n = nb * ns
    shw = 3 * bw + LORA_PAD
    nh = CONV_W - 1
    p = jnp.dot(x_ref[...].astype(BF16), w_ref[...], preferred_element_type=F32)
    cur = p[nb:, :shw]
    prev = p[:n, :shw]
    sh = cur + (prev - cur) * mu_ref[...]
    r_ref[...] = sh[:, 0:bw]
    k_ref[...] = sh[:, bw:2 * bw]
    v_ref[...] = sh[:, 2 * bw:3 * bw]
    lora_ref[...] = sh[:, 3 * bw:shw]
    rest = p[nb:, shw:]
    a_val, a_glu = rest[:, 0:bw], rest[:, bw:2 * bw]
    a_gate, b_gate = rest[:, 2 * bw:3 * bw], rest[:, 3 * bw:4 * bw]
    bg_ref[...] = _silu(b_gate)
    u_scr[0:nh * nb, :] = hist_ref[...]
    u_scr[nh * nb:(nh + ns) * nb, :] = a_val * _sigmoid(a_glu)
    cb, clg, clb = cb_ref[...], clg_ref[...], clb_ref[...]
    for s in range(ns):
        acc = jnp.zeros((nb, bw), F32) + cb
        for j in range(CONV_W):
            acc = acc + _tap_weight(cw_ref, j, nb) * u_scr[(s + j) * nb:(s + j + 1) * nb, :]
        ya_ref[s * nb:(s + 1) * nb, :] = (_silu(_layer_norm(acc, clg, clb, LN_EPS)) *
                                           _silu(a_gate[s * nb:(s + 1) * nb, :])).astype(BF16)
    conv_ref[...] = u_scr[ns * nb:(ns + nh) * nb, :]


def _even_front_sample(x_in, hist, w_re, mu_re, cw, conv_small, *, nb, ns):
    bw = conv_small[0].shape[-1]
    n = nb * ns
    nh = CONV_W - 1
    out = jax.ShapeDtypeStruct((n, bw), F32)
    outs = [jax.ShapeDtypeStruct((n, bw), BF16)] + [out] * 4 + [jax.ShapeDtypeStruct((n, LORA_PAD), F32),
                                                                jax.ShapeDtypeStruct((nh * nb, bw), F32)]
    body = functools.partial(_even_front_sample_body, nb=nb, ns=ns, bw=bw)
    return pl.pallas_call(
        body, out_shape=outs, scratch_shapes=[pltpu.VMEM(((nh + ns) * nb, bw), F32)],
        compiler_params=_cparams(), name="even_front_sample")(x_in, hist, w_re, mu_re, cw, *conv_small)


def _wkv_block_terms(r, lw, km, v, kkn, ab, blk, problems):
    n, width = r.shape
    nh = width // HEAD
    ri = lax.broadcasted_iota(jnp.int32, (n, n), 0)
    ci = lax.broadcasted_iota(jnp.int32, (n, n), 1)
    same = _same_block(ri, ci, blk)
    incl = (ri >= ci) & same
    strict = (ri > ci) & same
    cum = _split3_dot(jnp.where(incl, 1.0, 0.0).astype(BF16), lw)
    tot = _split3_dot(jnp.where(same, 1.0, 0.0).astype(BF16), lw)
    dn = jnp.exp(-cum)
    at = -kkn * jnp.exp(cum - lw)
    bt = ab * dn
    kt = km * dn
    rt = r * jnp.exp(cum)
    rem = jnp.exp(tot - cum)
    bc = ab * rem
    kc = km * rem
    wc = jnp.exp(tot)
    m = n // problems
    if m != n:
        ri = lax.broadcasted_iota(jnp.int32, (m, m), 0)
        ci = lax.broadcasted_iota(jnp.int32, (m, m), 1)
        same = _same_block(ri, ci, blk)
        incl = (ri >= ci) & same
        strict = (ri > ci) & same
    chains = [(slice(p * m, (p + 1) * m), slice(h * HEAD, (h + 1) * HEAD)) for p in range(problems) for h in range(nh)]
    gs = [_dot_nt(jnp.concatenate([at[rw, sl], rt[rw, sl]], axis=0), jnp.concatenate([bt[rw, sl], kt[rw, sl]], axis=0))
          for rw, sl in chains]
    l_ab = [jnp.where(strict, g[:m, :m], 0.0) for g in gs]
    l_ak = [jnp.where(strict, g[:m, m:], 0.0) for g in gs]
    l_rb = [jnp.where(incl, g[m:, :m], 0.0) for g in gs]
    l_rk = [jnp.where(incl, g[m:, m:], 0.0) for g in gs]
    toff = _unit_lower_inverse_minus_eye(l_ab, ri, ci, blk)
    lv = [_dot(jnp.concatenate([la, lr], axis=0), v[rw, sl]) for la, lr, (rw, sl) in zip(l_ak, l_rk, chains)]
    ahat = [at[rw, sl] + _dot(t, at[rw, sl]) for t, (rw, sl) in zip(toff, chains)]
    uhat = [x[:m] + _dot(t, x[:m]) for t, x in zip(toff, lv)]
    qhat = [rt[rw, sl] + _dot(l, a) for l, a, (rw, sl) in zip(l_rb, ahat, chains)]
    ohat = [_dot(lb, u) + x[m:] for lb, u, x in zip(l_rb, uhat, lv)]
    terms = list(zip(ahat, uhat, qhat, ohat))
    return [terms[p * nh:(p + 1) * nh] for p in range(problems)], bc, kc, wc


def _same_block(ri, ci, size):
    sh = size.bit_length() - 1
    return (ri >> sh) == (ci >> sh)


def _split(x):
    hi = x.astype(BF16)
    return hi, (x - hi.astype(F32)).astype(BF16)


def _split3_dot(ones_bf16, x):
    hi, lo = _split(x)
    lo2 = (x - hi.astype(F32) - lo.astype(F32)).astype(BF16)
    d = lambda y: jnp.dot(ones_bf16, y, preferred_element_type=F32)
    return d(hi) + (d(lo) + d(lo2))


def _unit_lower_inverse_minus_eye(ls, ri, ci, blk):
    base = 8
    eye = jnp.where(ri == ci, 1.0, 0.0)
    same8 = _same_block(ri, ci, base)
    l8 = [jnp.where(same8, l, 0.0).astype(BF16) for l in ls]
    x2 = [_dot(x, x).astype(BF16) for x in l8]
    x4 = [_dot(x, x).astype(BF16) for x in x2]
    p = [eye + jnp.where(same8, l, 0.0) for l in ls]
    p = [q + _dot(q, x) for q, x in zip(p, x2)]
    t = [q + _dot(q, x) for q, x in zip(p, x4)]
    size = 2 * base
    while size <= blk:
        half = size.bit_length() - 2
        lower_left = _same_block(ri, ci, size) & (((ri >> half) & 1) == 1) & (((ci >> half) & 1) == 0)
        tb = [x.astype(BF16) for x in t]
        mt = [_dot(jnp.where(lower_left, l, 0.0), x) for l, x in zip(ls, tb)]
        t = [x + _dot(xb, m) for x, xb, m in zip(t, tb, mt)]
        size *= 2
    return [x - eye for x in t]


def _wkv_prompt_body(r_ref, k_ref, v_ref, lora_ref, w0_ref, w2_ref, a0_ref, a2_ref, kk_ref, ka_ref, rk_ref, s0_ref,
                     o_ref, bon_ref, sout_ref, s_scr, *, c, nch):
    i = pl.program_id(0)

    @pl.when(i == 0)
    def _():
        s_scr[...] = s0_ref[...]

    prm = (w0_ref[...], w2_ref[...], a0_ref[...], a2_ref[...], kk_ref[...], ka_ref[...], rk_ref[...])
    r, lw, km, v, kkn, ab, bonus = _even_mix(r_ref[...], k_ref[...], v_ref[...], lora_ref[...], prm, _head_ones())
    bon_ref[...] = bonus
    terms, bc, kc, wc = _wkv_block_terms(r, lw, km, v, kkn, ab, c, nch)
    nh = len(terms[0])
    hs = [slice(h * HEAD, (h + 1) * HEAD) for h in range(nh)]
    vk = [_dot_tn(v[ch * c:(ch + 1) * c, sl], kc[ch * c:(ch + 1) * c, sl]) for ch in range(nch) for sl in hs]
    s = [s_scr[h] for h in range(nh)]
    for ch in range(nch):
        rows = slice(ch * c, (ch + 1) * c)
        heads = terms[ch]
        mm = [_dot_nt(jnp.concatenate([ahat, qhat], axis=0), s[h])
              for h, (ahat, _, qhat, _) in enumerate(heads)]
        u = [mm[h][:c] + heads[h][1] for h in range(nh)]
        for h in range(nh):
            o_ref[rows, hs[h]] = mm[h][c:] + heads[h][3]
        s = [s[h] * wc[ch * c:ch * c + 1, hs[h]] + _dot_tn(u[h], bc[rows, hs[h]]) + vk[ch * nh + h]
             for h in range(nh)]
    for h in range(nh):
        s_scr[h] = s[h]

    @pl.when(i == pl.num_programs(0) - 1)
    def _():
        sout_ref[...] = s_scr[...]


def _mix_specs(bw):
    fix = lambda *_: (0, 0)
    vec = pl.BlockSpec((1, bw), fix)
    lora = pl.BlockSpec((LORA_PAD, bw), fix)
    return [vec, lora, vec, lora, vec, vec, vec]


def _wkv_prompt(r, k, v, lora, mix_small, s0, *, nch=4):
    t, bw = r.shape
    c = WKV_CHUNK
    nh = bw // HEAD
    row = pl.BlockSpec((nch * c, bw), lambda i: (i, 0))
    lrow = pl.BlockSpec((nch * c, LORA_PAD), lambda i: (i, 0))
    st = pl.BlockSpec((nh, HEAD, HEAD), lambda i: (0, 0, 0))
    out = jax.ShapeDtypeStruct((t, bw), F32)
    return pl.pallas_call(
        functools.partial(_wkv_prompt_body, c=c, nch=nch),
        out_shape=[out, out, jax.ShapeDtypeStruct((nh, HEAD, HEAD), F32)],
        grid=(t // (nch * c),), in_specs=[row] * 3 + [lrow] + _mix_specs(bw) + [st], out_specs=[row, row, st],
        scratch_shapes=[pltpu.VMEM((nh, HEAD, HEAD), F32)],
        compiler_params=_cparams("arbitrary"), name="wkv_prompt")(r, k, v, lora, *mix_small, s0)


def _wkv_sample_body(r_ref, k_ref, v_ref, lora_ref, w0_ref, w2_ref, a0_ref, a2_ref, kk_ref, ka_ref, rk_ref, s0_ref,
                     o_ref, bon_ref, sout_ref,
                     ah_scr, uh_scr, qh_scr, oh_scr, bc_scr, kc_scr, wc_scr, *, nb, blk, ns):
    prm = (w0_ref[...], w2_ref[...], a0_ref[...], a2_ref[...], kk_ref[...], ka_ref[...], rk_ref[...])
    r, lw, km, v, kkn, ab, bonus = _even_mix(r_ref[...], k_ref[...], v_ref[...], lora_ref[...], prm, _head_ones())
    real = (lax.broadcasted_iota(jnp.int32, lw.shape, 0) & (blk - 1)) < ns
    lw = jnp.where(real, lw, 0.0)
    bon_ref[...] = bonus
    (heads,), bc, kc, wc = _wkv_block_terms(r, lw, km, v, kkn, ab, blk, 1)
    bc_scr[...] = bc
    kc_scr[...] = kc
    wc_scr[...] = wc
    for h, (ahat, uhat, qhat, ohat) in enumerate(heads):
        sl = slice(h * HEAD, (h + 1) * HEAD)
        ah_scr[:, sl] = ahat
        uh_scr[:, sl] = uhat
        qh_scr[:, sl] = qhat
        oh_scr[:, sl] = ohat
    nh = len(heads)

    def per_seq(b, carry):
        rows = pl.ds(pl.multiple_of(b * blk, blk), blk)
        for h in range(nh):
            sl = slice(h * HEAD, (h + 1) * HEAD)
            s = s0_ref[b, h]
            mm = _dot_nt(jnp.concatenate([ah_scr[rows, sl], qh_scr[rows, sl]], axis=0), s)
            u = mm[:blk] + uh_scr[rows, sl]
            o_ref[rows, sl] = mm[blk:] + oh_scr[rows, sl]
            wrow = wc_scr[rows, sl][0:1, :]
            sout_ref[b, h] = s * wrow + _dot_tn(u, bc_scr[rows, sl]) + _dot_tn(v_ref[rows, sl], kc_scr[rows, sl])
        return carry

    lax.fori_loop(0, nb, per_seq, 0)


def _wkv_sample(r, k, v, lora, mix_small, s0, *, blk, ns):
    n, bw = r.shape
    nb = n // blk
    scr = pltpu.VMEM((n, bw), F32)
    out = jax.ShapeDtypeStruct((n, bw), F32)
    return pl.pallas_call(
        functools.partial(_wkv_sample_body, nb=nb, blk=blk, ns=ns),
        out_shape=[out, out, jax.ShapeDtypeStruct(s0.shape, F32)],
        scratch_shapes=[scr] * 7, compiler_params=_cparams(), name="wkv_sample")(r, k, v, lora, *mix_small, s0)


def _even_back_rows(o, bonus, bg, ya, x, lg, lb, w_ref, g, b, hsum, *, alpha, bw):
    om = _head_sum(o, hsum) * (1.0 / HEAD)
    oc = o - om
    ov = _head_sum(oc * oc, hsum) * (1.0 / HEAD)
    on = oc * lax.rsqrt(ov + LNX_EPS) * lg + lb
    yb = (on + bonus) * bg
    y = _dot(ya, w_ref[0:bw, :]) + _dot(yb, w_ref[bw:2 * bw, :])
    return _layer_norm(alpha * x + y, g, b, LN_EPS)


def _project_class_block(xb, w_ref, blk, cls_ref, p_scr, *, tm, gw, d, scale):
    pc = jnp.dot(xb, w_ref[:, blk * gw:(blk + 1) * gw], preferred_element_type=F32)
    if scale is not None:
        pc = pc * scale
    if d == 1:
        cls_ref[0] = pc.astype(BF16)
        return
    lt = p_scr.shape[-1]
    nlb = gw // lt
    slot = (blk % 2) * nlb
    for lb in range(nlb):
        p_scr[slot + lb] = pc[:, lb * lt:(lb + 1) * lt]
    n = tm // d
    for c in range(d):
        for lb in range(nlb):
            cls_ref[c, :, lb * lt:(lb + 1) * lt] = p_scr[slot + lb, pl.ds(c, n, stride=d), :].astype(BF16)


def _mid_body(o_ref, bon_ref, bg_ref, ya_ref, x_ref, lg_ref, lb_ref, wo_ref, g_ref, b_ref, wi_ref, x1_ref, *refs,
              tm, sub, gw, bw, dils, alpha, scale):
    ng = len(dils)
    cls_refs = refs[0:3 * ng]
    gate_ref, xb_scr, p_scr = refs[3 * ng:3 * ng + 3]
    i = pl.program_id(0)
    slot = i % 2

    @pl.when(i == 0)
    def _():
        xb_scr[1] = jnp.zeros(xb_scr.shape[1:], BF16)

    xb = xb_scr[1 - slot]
    hsum = _head_ones()
    nsub = tm // sub
    nblk = 3 * ng + 1
    for blk in range(nblk):
        if blk < 3 * ng:
            g = blk % ng
            _project_class_block(xb, wi_ref, blk, cls_refs[blk], p_scr, tm=tm, gw=gw, d=dils[g],
                                 scale=scale if blk < ng else None)
        else:
            gate_ref[...] = _silu(jnp.dot(xb, wi_ref[:, blk * gw:(blk + 1) * gw], preferred_element_type=F32))
        for s in range(nsub):
            if s * nblk // nsub != blk:
                continue
            rows = slice(s * sub, (s + 1) * sub)
            x1 = _even_back_rows(o_ref[rows, :], bon_ref[rows, :], bg_ref[rows, :], ya_ref[rows, :], x_ref[rows, :],
                                 lg_ref[...], lb_ref[...], wo_ref, g_ref[...], b_ref[...], hsum, alpha=alpha, bw=bw)
            x1_ref[rows, :] = x1
            xb_scr[slot, rows, :] = x1.astype(BF16)


def _mid(o, bonus, bg, ya, x, lnx_g, lnx_b, w_out, ln_g, ln_b, w_in, *, alpha, tm, dils, gw):
    t, d_model = x.shape
    bw = o.shape[-1]
    ng = len(dils)
    nt = t // tm
    cur = lambda i: (jnp.minimum(i, nt - 1), 0)
    prev = lambda i: (jnp.maximum(i - 1, 0), 0)
    fix = lambda i: (0, 0)
    half = pl.BlockSpec((tm, bw), cur)
    once = lambda shape: pl.BlockSpec(shape, fix, pipeline_mode=pl.Buffered(1))
    in_specs = [half] * 4 + [pl.BlockSpec((tm, d_model), cur), once((1, bw)), once((1, bw)), once((2 * bw, d_model)),
                             once((1, d_model)), once((1, d_model)), once(w_in.shape)]
    cls_specs = [pl.BlockSpec((d, tm // d, gw), lambda i: (0, jnp.maximum(i - 1, 0), 0)) for d in dils]
    cls_shapes = [jax.ShapeDtypeStruct((d, t // d, gw), BF16) for d in dils]
    outs = [jax.ShapeDtypeStruct((t, d_model), F32)] + cls_shapes * 3 + [jax.ShapeDtypeStruct((t, gw), F32)]
    out_specs = [pl.BlockSpec((tm, d_model), cur)] + cls_specs * 3 + [pl.BlockSpec((tm, gw), prev)]
    body = functools.partial(_mid_body, tm=tm, sub=min(tm, 128), gw=gw, bw=bw, dils=dils, alpha=alpha,
                             scale=HEAD ** -0.5)
    res = pl.pallas_call(
        body, out_shape=outs, grid=(nt + 1,), in_specs=in_specs, out_specs=out_specs,
        scratch_shapes=[pltpu.VMEM((2, tm, d_model), BF16), pltpu.VMEM((2 * gw // 128, tm, 128), F32)],
        compiler_params=_cparams("arbitrary"), name="mid")(o, bonus, bg, ya, x, lnx_g, lnx_b, w_out, ln_g, ln_b, w_in)
    return res[0], res[1:1 + ng], res[1 + ng:1 + 2 * ng], res[1 + 2 * ng:1 + 3 * ng], res[1 + 3 * ng]


def _kv_rows_body(x_ref, wt_ref, kv_ref):
    kv_ref[...] = lax.dot_general(wt_ref[...], x_ref[...].astype(BF16), (((1,), (1,)), ((), ())),
                                  preferred_element_type=F32)


def _kv_rows_t(x, w_kv_t, *, tm, first_row):
    t = x.shape[0] - first_row
    d_model = x.shape[1]
    ncol = w_kv_t.shape[0]
    return pl.pallas_call(
        _kv_rows_body, out_shape=jax.ShapeDtypeStruct((ncol, t), F32), grid=(t // tm,),
        in_specs=[pl.BlockSpec((tm, d_model), lambda i: (i + first_row // tm, 0)),
                  pl.BlockSpec((ncol, d_model), lambda i: (0, 0))],
        out_specs=pl.BlockSpec((ncol, tm), lambda i: (0, i)),
        compiler_params=_cparams("parallel"), name="kv_rows")(x, w_kv_t)


def _band_attn_body(q_ref, kc_ref, kp_ref, vc_ref, vp_ref, o_ref, lse_ref, *, bq, span, nb_class):
    j = pl.program_id(0)
    first_lo = jnp.where(((j * bq) % nb_class) == 0, span, 0)
    qi = lax.broadcasted_iota(jnp.int32, (span, 2 * span), 0)
    ki = lax.broadcasted_iota(jnp.int32, (span, 2 * span), 1)
    in_band = (ki >= qi) & (ki <= qi + span)
    lane = lax.broadcasted_iota(jnp.int32, (span, 2 * HEAD), 1)
    left = lane < HEAD
    gw = q_ref.shape[-1]
    for b in range(bq):
        mask = in_band & (ki >= first_lo) if b == 0 else in_band
        for pr in range(gw // (2 * HEAD)):
            ls = slice(pr * 2 * HEAD, (pr + 1) * 2 * HEAD)
            q2 = q_ref[b * span:(b + 1) * span, ls]
            if b == 0:
                k2 = jnp.concatenate([kp_ref[:, ls], kc_ref[0:span, ls]], axis=0)
                v2 = jnp.concatenate([vp_ref[:, ls], vc_ref[0:span, ls]], axis=0)
            else:
                k2 = kc_ref[(b - 1) * span:(b + 1) * span, ls]
                v2 = vc_ref[(b - 1) * span:(b + 1) * span, ls]
            outs, lses = [], []
            for sel in (left, jnp.logical_not(left)):
                qh = jnp.where(sel, q2, jnp.zeros_like(q2))
                s = lax.dot_general(qh, k2, (((1,), (1,)), ((), ())), preferred_element_type=F32)
                s = jnp.where(mask, s, -jnp.inf)
                m = jnp.max(s, axis=-1, keepdims=True)
                p = jnp.exp(s - m)
                l = jnp.sum(p, axis=-1, keepdims=True)
                outs.append(jnp.dot(p.astype(BF16), v2, preferred_element_type=F32) / l)
                lses.append(m + jnp.log(l))
            o_ref[b * span:(b + 1) * span, ls] = jnp.where(left, outs[0], outs[1])
            lse_ref[b * span:(b + 1) * span, ls] = jnp.where(left, lses[0], lses[1])


def _band_attn(q, k, v, *, span, nb_class, bq=4):
    t, gw = q.shape
    cur = pl.BlockSpec((bq * span, gw), lambda j: (j, 0))
    prev = pl.BlockSpec((span, gw), lambda j: (jnp.maximum(j * bq - 1, 0), 0))
    out = jax.ShapeDtypeStruct((t, gw), F32)
    return pl.pallas_call(
        functools.partial(_band_attn_body, bq=bq, span=span, nb_class=nb_class),
        out_shape=[out, out], grid=(t // (bq * span),), in_specs=[cur, cur, prev, cur, prev],
        out_specs=[cur, cur], compiler_params=_cparams("parallel"), name="band_attn")(q, k, k, v, v)


def _cache_attn_body(q_ref, nt_ref, c_ref, o_ref, lse_ref, cout_ref, *, w, d, window, ns, gw, rchunk):
    nh = gw // HEAD
    rp = 8
    zrow = jnp.zeros((rp - ns, gw), F32)
    q8 = jnp.concatenate([q_ref[0], zrow], axis=0)
    kn_t = nt_ref[0, 0:gw, :].astype(BF16)
    vn_t = nt_ref[0, gw:2 * gw, :].astype(BF16)
    qm = jnp.concatenate([q8] * nh, axis=0)
    rowh = lax.broadcasted_iota(jnp.int32, (nh * rp, gw), 0) >> 3
    laneh = lax.broadcasted_iota(jnp.int32, (nh * rp, gw), 1) >> 6
    own = rowh == laneh
    qm = jnp.where(own, qm, 0.0).astype(BF16)
    s_c = jnp.dot(qm, c_ref[0, 0:gw, :].astype(BF16), preferred_element_type=F32)
    s_n = jnp.dot(qm, kn_t, preferred_element_type=F32)
    srow = lax.broadcasted_iota(jnp.int32, s_c.shape, 0) & (rp - 1)
    dist = w + srow - lax.broadcasted_iota(jnp.int32, s_c.shape, 1)
    valid_c = (dist <= window) & ((dist & (d - 1)) == 0) & (srow < ns)
    srow_n = lax.broadcasted_iota(jnp.int32, s_n.shape, 0) & (rp - 1)
    col_n = lax.broadcasted_iota(jnp.int32, s_n.shape, 1)
    dist_n = srow_n - col_n
    valid_n = (dist_n >= 0) & ((dist_n & (d - 1)) == 0) & (srow_n < ns)
    valid_n = valid_n | ((srow_n >= ns) & (col_n == 0))
    s_c = jnp.where(valid_c, s_c, -jnp.inf)
    s_n = jnp.where(valid_n, s_n, -jnp.inf)
    m = jnp.maximum(jnp.max(s_c, axis=-1, keepdims=True), jnp.max(s_n, axis=-1, keepdims=True))
    p_c = jnp.exp(s_c - m)
    p_n = jnp.exp(s_n - m)
    l = jnp.sum(p_c, axis=-1, keepdims=True) + jnp.sum(p_n, axis=-1, keepdims=True)
    o = lax.dot_general(p_c.astype(BF16), c_ref[0, gw:2 * gw, :].astype(BF16), (((1,), (1,)), ((), ())),
                        preferred_element_type=F32)
    o = (o + lax.dot_general(p_n.astype(BF16), vn_t, (((1,), (1,)), ((), ())), preferred_element_type=F32)) / l
    lse = jnp.broadcast_to(m + jnp.log(l), o.shape)
    o = jnp.where(own, o, 0.0)
    lse = jnp.where(own, lse, 0.0)
    o8 = o[0:rp]
    l8 = lse[0:rp]
    for h in range(1, nh):
        o8 = o8 + o[h * rp:(h + 1) * rp]
        l8 = l8 + lse[h * rp:(h + 1) * rp]
    o_ref[0] = o8[0:ns]
    lse_ref[0] = l8[0:ns]
    for rb in range(2 * gw // rchunk):
        rows = slice(rb * rchunk, (rb + 1) * rchunk)
        cout_ref[0, rows, :] = jnp.concatenate([c_ref[0, rows, ns:w], nt_ref[0, rows, 0:ns]], axis=1)


def _cache_attn(q, new_t, cache_t, *, d, window):
    nbat, ns, gw = q.shape
    w = cache_t.shape[2]
    new = pl.BlockSpec((1, ns, gw), lambda b: (b, 0, 0))
    ntb = pl.BlockSpec((1, 2 * gw, new_t.shape[2]), lambda b: (b, 0, 0))
    cb = pl.BlockSpec((1, 2 * gw, w), lambda b: (b, 0, 0))
    out = jax.ShapeDtypeStruct((nbat, ns, gw), F32)
    return pl.pallas_call(
        functools.partial(_cache_attn_body, w=w, d=d, window=window, ns=ns, gw=gw, rchunk=64),
        out_shape=[out, out, jax.ShapeDtypeStruct(cache_t.shape, F32)], grid=(nbat,),
        in_specs=[new, ntb, cb], out_specs=[new, new, cb],
        compiler_params=_cparams("parallel"), name="cache_attn")(q, new_t, cache_t)


def _odd_back_body(*refs, tm, dils, alpha):
    ng = len(dils)
    o_refs, l_refs = refs[0:ng], refs[ng:2 * ng]
    gate_ref, x_ref, w_ref, g_ref, b_ref, y_ref = refs[2 * ng:2 * ng + 6]
    scr = refs[2 * ng + 6:]
    os_, ls_ = [], []
    for g, d in enumerate(dils):
        if d == 1:
            os_.append(o_refs[g][0])
            ls_.append(l_refs[g][0])
        else:
            n = tm // d
            nlb, _, lt = scr[2 * g].shape
            for c in range(d):
                for lb in range(nlb):
                    scr[2 * g][lb, pl.ds(c, n, stride=d), :] = o_refs[g][c, :, lb * lt:(lb + 1) * lt]
                    scr[2 * g + 1][lb, pl.ds(c, n, stride=d), :] = l_refs[g][c, :, lb * lt:(lb + 1) * lt]
            os_.append(jnp.concatenate([scr[2 * g][lb] for lb in range(nlb)], axis=1))
            ls_.append(jnp.concatenate([scr[2 * g + 1][lb] for lb in range(nlb)], axis=1))
    m = ls_[0]
    for l in ls_[1:]:
        m = jnp.maximum(m, l)
    es = [jnp.exp(l - m) for l in ls_]
    den = es[0]
    for e in es[1:]:
        den = den + e
    o = es[0] * os_[0]
    for e, og in zip(es[1:], os_[1:]):
        o = o + e * og
    o = o / den
    y = _dot(o * gate_ref[...], w_ref[...])
    y_ref[...] = _layer_norm(alpha * x_ref[...] + y, g_ref[...], b_ref[...], LN_EPS)


def _odd_back(os_, ls_, gate, x, w_out, ln_g, ln_b, *, dils, alpha, tm):
    t, d_model = x.shape
    gw = gate.shape[-1]
    row = lambda i: (i, 0)
    fix = lambda i: (0, 0)
    cls_specs = [pl.BlockSpec((d, tm // d, gw), lambda i: (0, i, 0)) for d in dils]
    in_specs = cls_specs * 2 + [pl.BlockSpec((tm, gw), row), pl.BlockSpec((tm, d_model), row),
                                pl.BlockSpec((gw, d_model), fix), pl.BlockSpec((1, d_model), fix),
                                pl.BlockSpec((1, d_model), fix)]
    return pl.pallas_call(
        functools.partial(_odd_back_body, tm=tm, dils=dils, alpha=alpha),
        out_shape=jax.ShapeDtypeStruct((t, d_model), F32), grid=(t // tm,), in_specs=in_specs,
        out_specs=pl.BlockSpec((tm, d_model), row),
        scratch_shapes=[pltpu.VMEM((gw // 128, tm, 128), F32)] * (2 * len(dils)),
        compiler_params=_cparams("parallel"), name="odd_back")(*os_, *ls_, gate, x, w_out, ln_g, ln_b)


def kernel(x_prompt, x_sample, state_conv, state_shift, state_wkv, cache_kv_w128, cache_kv_w512, cache_kv_w2048,
           w_in_even, conv_w, conv_b, conv_ln_g, conv_ln_b, mu_shift, w0, w2, a0, a2, k_k, k_a, r_k, lnx_g, lnx_b,
           w_out_even, w_in_odd, w_out_odd, ln_g, ln_b):
    nbp, seq, d_model = x_prompt.shape
    nbs, ns, _ = x_sample.shape
    assert nbp == 1
    depth = ln_g.shape[0]
    assert depth == 2 and w_in_even.shape[0] == 1 and w_in_odd.shape[0] == 1
    alpha = (2.0 * depth) ** 0.25
    bw = w0.shape[-1]
    lora_d, lora_a = w2.shape[1], a2.shape[1]
    nh = bw // HEAD
    nhist = CONV_W - 1
    caches = (cache_kv_w128, cache_kv_w512, cache_kv_w2048)
    dils = tuple(d for _, d in C_GROUPS)
    gw = w_out_odd.shape[1]
    ng = len(C_GROUPS)

    wi = w_in_even[0]
    sh_cols = 3 * bw + lora_d + lora_a
    zpad = jnp.zeros((d_model, LORA_PAD - lora_d - lora_a), F32)
    w_re = jnp.concatenate([wi[:, :sh_cols], zpad, wi[:, sh_cols:]], axis=1).astype(BF16)
    mu_re = jnp.concatenate([mu_shift[0], jnp.zeros((LORA_PAD - lora_d - lora_a,), F32)])[None, :]
    w2p = jnp.zeros((LORA_PAD, bw), F32).at[0:lora_d].set(w2[0])
    a2p = jnp.zeros((LORA_PAD, bw), F32).at[lora_d:lora_d + lora_a].set(a2[0])
    cw = jnp.repeat(conv_w[0], 8, axis=0)
    vec = lambda z: z.reshape(1, -1)
    conv_small = (vec(conv_b[0]), vec(conv_ln_g[0]), vec(conv_ln_b[0]))
    mix_small = (vec(w0[0]), w2p, vec(a0[0]), a2p, vec(k_k[0]), vec(k_a[0]), vec(r_k[0]))
    w_out_e = w_out_even[0].astype(BF16)
    w_in_o = w_in_odd[0].astype(BF16)
    w_out_o = w_out_odd[0].astype(BF16)
    lng = [vec(ln_g[l]) for l in range(depth)]
    lnb = [vec(ln_b[l]) for l in range(depth)]

    xp = x_prompt[0]
    ya, bg, r, k, v, lora, conv_tail = _even_front_prompt(xp, w_re, mu_re, cw, conv_small)
    o_wkv, bonus, wkv_p = _wkv_prompt(r, k, v, lora, mix_small, jnp.zeros((nh, HEAD, HEAD), F32))
    xp1, qc, kc, vc, gate_p = _mid(o_wkv, bonus, bg, ya, xp, vec(lnx_g[0]), vec(lnx_b[0]), w_out_e, lng[0], lnb[0],
                                   w_in_o, alpha=alpha, tm=512, dils=dils, gw=gw)
    conv_p = conv_tail[CONV_HIST - nhist:][None, None]
    shift_p = x_prompt[:, -1][None]

    xs_t = jnp.transpose(x_sample, (1, 0, 2)).reshape(ns * nbs, d_model)
    x_in = jnp.concatenate([state_shift[0], xs_t], axis=0)
    hist = jnp.transpose(state_conv[0], (1, 0, 2)).reshape(nhist * nbs, bw)
    res = _even_front_sample(x_in, hist, w_re, mu_re, cw, conv_small, nb=nbs, ns=ns)
    ya_s, bg_s, conv_s_t = res[0], res[1], res[6]
    blk = 8

    def seq_major_pad(z):
        z = jnp.transpose(z.reshape(ns, nbs, z.shape[-1]), (1, 0, 2))
        return jnp.pad(z, ((0, 0), (0, blk - ns), (0, 0))).reshape(nbs * blk, z.shape[-1])

    def step_major(z):
        return jnp.transpose(z.reshape(nbs, blk, z.shape[-1])[:, :ns], (1, 0, 2)).reshape(ns * nbs, z.shape[-1])

    wkv_in = [seq_major_pad(z) for z in res[2:6]]
    o_s_pad, bon_s_pad, wkv_s = _wkv_sample(*wkv_in, mix_small, state_wkv[0], blk=blk, ns=ns)
    o_s, bon_s = step_major(o_s_pad), step_major(bon_s_pad)
    ones = (1,) * ng
    xs1_t, qs_t, _, _, gate_s_t = _mid(o_s, bon_s, bg_s, ya_s, xs_t, vec(lnx_g[0]), vec(lnx_b[0]), w_out_e, lng[0],
                                       lnb[0], w_in_o, alpha=alpha, tm=ns * nbs, dils=ones, gw=gw)
    conv_s = jnp.transpose(conv_s_t.reshape(nhist, nbs, bw), (1, 0, 2))[None]
    shift_s = x_sample[:, -1][None]

    def seq_major(z):
        return jnp.transpose(z.reshape(ns, nbs, z.shape[-1]), (1, 0, 2)).reshape(nbs * ns, z.shape[-1])

    xs1 = seq_major(xs1_t)
    gate_s = seq_major(gate_s_t)
    qs = [seq_major(q[0]) for q in qs_t]

    tm1 = 256
    tail = min(max(win for win, _ in C_GROUPS), seq)
    w_kv_t = jnp.concatenate([w_in_o[:, (part * ng + g) * gw:(part * ng + g + 1) * gw]
                              for g in range(ng) for part in (1, 2)], axis=1).T
    kvt = _kv_rows_t(xp1, w_kv_t, tm=512, first_row=seq - tail).reshape(ng, 2, gw // HEAD, HEAD, tail)
    os_, ls_ = [], []
    for g, (win, d) in enumerate(C_GROUPS):
        span = win // d
        flat = lambda z: z.reshape(seq, gw)
        o_g, l_g = _band_attn(flat(qc[g]), flat(kc[g]), flat(vc[g]), span=span, nb_class=(seq // d) // span)
        os_.append(o_g.reshape(d, seq // d, gw))
        ls_.append(l_g.reshape(d, seq // d, gw))
    xp2 = _odd_back(os_, ls_, gate_p, xp1, w_out_o, lng[1], lnb[1], dils=dils, alpha=alpha, tm=tm1)
    kv_p = []
    for g, (win, d) in enumerate(C_GROUPS):
        n = min(win, seq)
        kv_p.append(jnp.transpose(kvt[g, :, :, :, tail - n:], (3, 0, 1, 2))[None, None])

    kvt_s = _kv_rows_t(xs1, w_kv_t, tm=nbs * ns, first_row=0).reshape(ng, 2 * gw, nbs, ns)
    new_t_all = jnp.pad(jnp.transpose(kvt_s, (0, 2, 1, 3)), ((0, 0), (0, 0), (0, 0), (0, 8 - ns)))
    os_s, ls_s, kv_s_out = [], [], []
    for g, (win, d) in enumerate(C_GROUPS):
        cache = caches[g][0]
        w = cache.shape[1]
        q_g = qs[g].astype(F32).reshape(nbs, ns, gw)
        cache_t = jnp.transpose(cache.reshape(nbs, w, 2 * gw), (0, 2, 1))
        o_g, l_g, cnew_t = _cache_attn(q_g, new_t_all[g], cache_t, d=d, window=win)
        os_s.append(o_g.reshape(1, nbs * ns, gw))
        ls_s.append(l_g.reshape(1, nbs * ns, gw))
        kv_s_out.append(jnp.transpose(cnew_t, (0, 2, 1)).reshape(nbs, w, 2, gw // HEAD, HEAD)[None])
    xs2 = _odd_back(os_s, ls_s, gate_s, xs1, w_out_o, lng[1], lnb[1], dils=ones, alpha=alpha, tm=nbs * ns)

    return (xp2[None], xs2.reshape(nbs, ns, d_model), conv_p, conv_s, shift_p, shift_s,
            wkv_p[None, None], wkv_s[None],
            kv_p[0], kv_s_out[0], kv_p[1], kv_s_out[1], kv_p[2], kv_s_out[2])
```

```python
import functools

import jax
import jax.numpy as jnp
from jax import lax
from jax.experimental import pallas as pl
from jax.experimental.pallas import tpu as pltpu

F32 = jnp.float32
BF16 = jnp.bfloat16

LN_EPS = 1e-5
LNX_EPS = 64e-5
HEAD = 64
LORA_PAD = 128
CONV_W = 31
CONV_HIST = 32
C_GROUPS = ((128, 1), (512, 4), (2048, 16))
WKV_CHUNK = 64
VMEM_LIMIT = 56 * 1024 * 1024


def _cparams(*sem):
    return pltpu.CompilerParams(dimension_semantics=sem or None, vmem_limit_bytes=VMEM_LIMIT)


def _dot(a, b):
    return jnp.dot(a.astype(BF16), b.astype(BF16), preferred_element_type=F32)


def _dot_nt(a, b):
    return lax.dot_general(a.astype(BF16), b.astype(BF16), (((1,), (1,)), ((), ())), preferred_element_type=F32)


def _dot_tn(a, b):
    return lax.dot_general(a.astype(BF16), b.astype(BF16), (((0,), (0,)), ((), ())), preferred_element_type=F32)


def _head_sum(a, ones_bf16):
    ah = a.astype(BF16)
    al = (a - ah.astype(F32)).astype(BF16)
    lt = ones_bf16.shape[0]
    parts = []
    for lb in range(a.shape[1] // lt):
        sl = slice(lb * lt, (lb + 1) * lt)
        parts.append(jnp.dot(ah[:, sl], ones_bf16, preferred_element_type=F32) +
                     jnp.dot(al[:, sl], ones_bf16, preferred_element_type=F32))
    return jnp.concatenate(parts, axis=1)


def _head_ones():
    n = 2 * HEAD
    r = lax.broadcasted_iota(jnp.int32, (n, n), 0) >> 6
    c = lax.broadcasted_iota(jnp.int32, (n, n), 1) >> 6
    return jnp.where(r == c, 1.0, 0.0).astype(BF16)


def _sigmoid(x):
    return 1.0 / (1.0 + jnp.exp(-x))


def _silu(x):
    return x * _sigmoid(x)


def _softplus(y):
    return jnp.maximum(y, 0.0) + jnp.log(1.0 + jnp.exp(-jnp.abs(y)))


def _layer_norm(x, g, b, eps):
    mu = jnp.mean(x, axis=-1, keepdims=True)
    xc = x - mu
    var = jnp.mean(xc * xc, axis=-1, keepdims=True)
    return xc * lax.rsqrt(var + eps) * g + b


def _even_mix(r, k, v, lora, prm, hsum):
    w0, w2p, a0, a2p, k_k, k_a, r_k = prm
    w_log = -_softplus(-(w0 + _dot(jnp.tanh(lora), w2p))) - 0.5
    lw = -jnp.exp(w_log)
    a = _sigmoid(a0 + _dot(lora, a2p))
    kk = k * k_k
    nrm = jnp.sqrt(_head_sum(kk * kk, hsum))
    kkn = kk / jnp.maximum(nrm, 1e-12)
    km = k * (1.0 + (a - 1.0) * k_a)
    bonus = _head_sum(r * km * r_k, hsum) * v
    return r, lw, km, v, kkn, kkn * a, bonus


def _tap_weight(cw_ref, j, n):
    return jnp.concatenate([cw_ref[8 * j:8 * j + 8, :]] * (n // 8), axis=0)


def _conv_ln_gate(u_ref, sh_ref, row0, n, cw_ref, cb, clg, clb, a_gate):
    acc = jnp.zeros((n, cb.shape[-1]), F32) + cb
    for j in range(CONV_W):
        a, b = divmod(CONV_HIST - (CONV_W - 1) + j, 8)
        win = u_ref[pl.ds(row0 + 8 * a, n), :] if b == 0 else sh_ref[b - 1, pl.ds(row0 + 8 * a, n), :]
        acc = acc + _tap_weight(cw_ref, j, n) * win
    return _silu(_layer_norm(acc, clg, clb, LN_EPS)) * _silu(a_gate)


def _even_front_prompt_body(x_ref, w_ref, mu_ref, cw_ref, cb_ref, clg_ref, clb_ref,
                            ya_ref, bg_ref, r_ref, k_ref, v_ref, lora_ref, conv_ref,
                            p_scr, u_scr, sh_scr, *, tm, bw, sub):
    i = pl.program_id(0)
    shw = 3 * bw + LORA_PAD

    @pl.when(i == 0)
    def _():
        p_scr[0:8, :] = jnp.zeros((8, shw), F32)
        u_scr[0:CONV_HIST, :] = jnp.zeros((CONV_HIST, bw), F32)

    xb = x_ref[...].astype(BF16)
    proj = lambda c0, c1: jnp.dot(xb, w_ref[:, c0:c1], preferred_element_type=F32)
    cur = proj(0, shw)
    p_scr[8:8 + tm, :] = cur
    prev = p_scr[pl.ds(7, tm), :]
    sh = cur + (prev - cur) * mu_ref[...]
    p_scr[7:8, :] = cur[tm - 1:tm, :]
    r_ref[...] = sh[:, 0:bw]
    k_ref[...] = sh[:, bw:2 * bw]
    v_ref[...] = sh[:, 2 * bw:3 * bw]
    lora_ref[...] = sh[:, 3 * bw:shw]

    bg_ref[...] = _silu(proj(shw + 3 * bw, shw + 4 * bw))
    u_scr[CONV_HIST:CONV_HIST + tm, :] = proj(shw, shw + bw) * _sigmoid(proj(shw + bw, shw + 2 * bw))
    a_gate = proj(shw + 2 * bw, shw + 3 * bw)
    for b in range(1, 8):
        sh_scr[b - 1] = u_scr[pl.ds(b, sh_scr.shape[1]), :]
    cb, clg, clb = cb_ref[...], clg_ref[...], clb_ref[...]
    for s in range(tm // sub):
        ya_ref[s * sub:(s + 1) * sub, :] = _conv_ln_gate(u_scr, sh_scr, s * sub, sub, cw_ref, cb, clg, clb,
                                                         a_gate[s * sub:(s + 1) * sub, :]).astype(BF16)
    tail = u_scr[tm:tm + CONV_HIST, :]
    conv_ref[...] = tail
    u_scr[0:CONV_HIST, :] = tail


def _even_front_prompt(x, w_re, mu_re, cw, conv_small, *, tm=256):
    t, d = x.shape
    bw = conv_small[0].shape[-1]
    shw = 3 * bw + LORA_PAD
    ncol = w_re.shape[1]
    row = lambda i: (i, 0)
    fix = lambda i: (0, 0)
    vec = pl.BlockSpec((1, bw), fix)
    in_specs = [pl.BlockSpec((tm, d), row), pl.BlockSpec((d, ncol), fix), pl.BlockSpec((1, shw), fix),
                pl.BlockSpec((8 * CONV_W, bw), fix), vec, vec, vec]
    out = jax.ShapeDtypeStruct((t, bw), F32)
    outs = [jax.ShapeDtypeStruct((t, bw), BF16)] + [out] * 4 + [jax.ShapeDtypeStruct((t, LORA_PAD), F32),
                                                                jax.ShapeDtypeStruct((CONV_HIST, bw), F32)]
    out_specs = [pl.BlockSpec((tm, bw), row)] * 5 + [pl.BlockSpec((tm, LORA_PAD), row),
                                                     pl.BlockSpec((CONV_HIST, bw), fix)]
    body = functools.partial(_even_front_prompt_body, tm=tm, bw=bw, sub=32)
    return pl.pallas_call(
        body, out_shape=outs, grid=(t // tm,), in_specs=in_specs, out_specs=out_specs,
        scratch_shapes=[pltpu.VMEM((8 + tm, shw), F32), pltpu.VMEM((CONV_HIST + tm, bw), F32),
                        pltpu.VMEM((7, CONV_HIST - 8 + tm, bw), F32)],
        compiler_params=_cparams("arbitrary"), name="even_front_prompt")(x, w_re, mu_re, cw, *conv_small)


def _even_front_sample_body(x_ref, hist_ref, w_ref, mu_ref, cw_ref, cb_ref, clg_ref, clb_ref,
                            ya_ref, bg_ref, r_ref, k_ref, v_ref, lora_ref, conv_ref,
                            u_scr, *, nb, ns, bw):
    n = nb * ns
    shw = 3 * bw + LORA_PAD
    nh = CONV_W - 1
    p = jnp.dot(x_ref[...].astype(BF16), w_ref[...], preferred_element_type=F32)
    cur = p[nb:, :shw]
    prev = p[:n, :shw]
    sh = cur + (prev - cur) * mu_ref[...]
    r_ref[...] = sh[:, 0:bw]
    k_ref[...] = sh[:, bw:2 * bw]
    v_ref[...] = sh[:, 2 * bw:3 * bw]
    lora_ref[...] = sh[:, 3 * bw:shw]
    rest = p[nb:, shw:]
    a_val, a_glu = rest[:, 0:bw], rest[:, bw:2 * bw]
    a_gate, b_gate = rest[:, 2 * bw:3 * bw], rest[:, 3 * bw:4 * bw]
    bg_ref[...] = _silu(b_gate)
    u_scr[0:nh * nb, :] = hist_ref[...]
    u_scr[nh * nb:(nh + ns) * nb, :] = a_val * _sigmoid(a_glu)
    cb, clg, clb = cb_ref[...], clg_ref[...], clb_ref[...]
    for s in range(ns):
        acc = jnp.zeros((nb, bw), F32) + cb
        for j in range(CONV_W):
            acc = acc + _tap_weight(cw_ref, j, nb) * u_scr[(s + j) * nb:(s + j + 1) * nb, :]
        ya_ref[s * nb:(s + 1) * nb, :] = (_silu(_layer_norm(acc, clg, clb, LN_EPS)) *
                                           _silu(a_gate[s * nb:(s + 1) * nb, :])).astype(BF16)
    conv_ref[...] = u_scr[ns * nb:(ns + nh) * nb, :]


def _even_front_sample(x_in, hist, w_re, mu_re, cw, conv_small, *, nb, ns):
    bw = conv_small[0].shape[-1]
    n = nb * ns
    nh = CONV_W - 1
    out = jax.ShapeDtypeStruct((n, bw), F32)
    outs = [jax.ShapeDtypeStruct((n, bw), BF16)] + [out] * 4 + [jax.ShapeDtypeStruct((n, LORA_PAD), F32),
                                                                jax.ShapeDtypeStruct((nh * nb, bw), F32)]
    body = functools.partial(_even_front_sample_body, nb=nb, ns=ns, bw=bw)
    return pl.pallas_call(
        body, out_shape=outs, scratch_shapes=[pltpu.VMEM(((nh + ns) * nb, bw), F32)],
        compiler_params=_cparams(), name="even_front_sample")(x_in, hist, w_re, mu_re, cw, *conv_small)


def _wkv_block_terms(r, lw, km, v, kkn, ab, blk, problems):
    n, width = r.shape
    nh = width // HEAD
    ri = lax.broadcasted_iota(jnp.int32, (n, n), 0)
    ci = lax.broadcasted_iota(jnp.int32, (n, n), 1)
    same = _same_block(ri, ci, blk)
    incl = (ri >= ci) & same
    strict = (ri > ci) & same
    cum = _split3_dot(jnp.where(incl, 1.0, 0.0).astype(BF16), lw)
    tot = _split3_dot(jnp.where(same, 1.0, 0.0).astype(BF16), lw)
    dn = jnp.exp(-cum)
    at = -kkn * jnp.exp(cum - lw)
    bt = ab * dn
    kt = km * dn
    rt = r * jnp.exp(cum)
    rem = jnp.exp(tot - cum)
    bc = ab * rem
    kc = km * rem
    wc = jnp.exp(tot)
    m = n // problems
    if m != n:
        ri = lax.broadcasted_iota(jnp.int32, (m, m), 0)
        ci = lax.broadcasted_iota(jnp.int32, (m, m), 1)
        same = _same_block(ri, ci, blk)
        incl = (ri >= ci) & same
        strict = (ri > ci) & same
    chains = [(slice(p * m, (p + 1) * m), slice(h * HEAD, (h + 1) * HEAD)) for p in range(problems) for h in range(nh)]
    gs = [_dot_nt(jnp.concatenate([at[rw, sl], rt[rw, sl]], axis=0), jnp.concatenate([bt[rw, sl], kt[rw, sl]], axis=0))
          for rw, sl in chains]
    l_ab = [jnp.where(strict, g[:m, :m], 0.0) for g in gs]
    l_ak = [jnp.where(strict, g[:m, m:], 0.0) for g in gs]
    l_rb = [jnp.where(incl, g[m:, :m], 0.0) for g in gs]
    l_rk = [jnp.where(incl, g[m:, m:], 0.0) for g in gs]
    toff = _unit_lower_inverse_minus_eye(l_ab, ri, ci, blk)
    lv = [_dot(jnp.concatenate([la, lr], axis=0), v[rw, sl]) for la, lr, (rw, sl) in zip(l_ak, l_rk, chains)]
    ahat = [at[rw, sl] + _dot(t, at[rw, sl]) for t, (rw, sl) in zip(toff, chains)]
    uhat = [x[:m] + _dot(t, x[:m]) for t, x in zip(toff, lv)]
    qhat = [rt[rw, sl] + _dot(l, a) for l, a, (rw, sl) in zip(l_rb, ahat, chains)]
    ohat = [_dot(lb, u) + x[m:] for lb, u, x in zip(l_rb, uhat, lv)]
    terms = list(zip(ahat, uhat, qhat, ohat))
    return [terms[p * nh:(p + 1) * nh] for p in range(problems)], bc, kc, wc


def _same_block(ri, ci, size):
    sh = size.bit_length() - 1
    return (ri >> sh) == (ci >> sh)


def _split(x):
    hi = x.astype(BF16)
    return hi, (x - hi.astype(F32)).astype(BF16)


def _split3(x):
    hi, lo = _split(x)
    return hi, lo, (x - hi.astype(F32) - lo.astype(F32)).astype(BF16)


def _split3_dot(ones_bf16, x):
    hi, lo, lo2 = _split3(x)
    d = lambda y: jnp.dot(ones_bf16, y, preferred_element_type=F32)
    return d(hi) + (d(lo) + d(lo2))


def _split3_dot_right(x, ones_bf16):
    hi, lo, lo2 = _split3(x)
    d = lambda y: jnp.dot(y, ones_bf16, preferred_element_type=F32)
    return d(hi) + (d(lo) + d(lo2))


def _unit_lower_inverse_minus_eye(ls, ri, ci, blk):
    base = 8
    eye = jnp.where(ri == ci, 1.0, 0.0)
    same8 = _same_block(ri, ci, base)
    l8 = [jnp.where(same8, l, 0.0).astype(BF16) for l in ls]
    x2 = [_dot(x, x).astype(BF16) for x in l8]
    x4 = [_dot(x, x).astype(BF16) for x in x2]
    p = [eye + jnp.where(same8, l, 0.0) for l in ls]
    p = [q + _dot(q, x) for q, x in zip(p, x2)]
    t = [q + _dot(q, x) for q, x in zip(p, x4)]
    size = 2 * base
    while size <= blk:
        half = size.bit_length() - 2
        lower_left = _same_block(ri, ci, size) & (((ri >> half) & 1) == 1) & (((ci >> half) & 1) == 0)
        tb = [x.astype(BF16) for x in t]
        mt = [_dot(jnp.where(lower_left, l, 0.0), x) for l, x in zip(ls, tb)]
        t = [x + _dot(xb, m) for x, xb, m in zip(t, tb, mt)]
        size *= 2
    return [x - eye for x in t]


def _wkv_prompt_body(r_ref, k_ref, v_ref, lora_ref, w0_ref, w2_ref, a0_ref, a2_ref, kk_ref, ka_ref, rk_ref, s0_ref,
                     o_ref, bon_ref, sout_ref, s_scr, *, c, nch):
    i = pl.program_id(0)

    @pl.when(i == 0)
    def _():
        s_scr[...] = s0_ref[...]

    prm = (w0_ref[...], w2_ref[...], a0_ref[...], a2_ref[...], kk_ref[...], ka_ref[...], rk_ref[...])
    r, lw, km, v, kkn, ab, bonus = _even_mix(r_ref[...], k_ref[...], v_ref[...], lora_ref[...], prm, _head_ones())
    bon_ref[...] = bonus
    terms, bc, kc, wc = _wkv_block_terms(r, lw, km, v, kkn, ab, c, nch)
    nh = len(terms[0])
    hs = [slice(h * HEAD, (h + 1) * HEAD) for h in range(nh)]
    vk = [_dot_tn(v[ch * c:(ch + 1) * c, sl], kc[ch * c:(ch + 1) * c, sl]) for ch in range(nch) for sl in hs]
    s = [s_scr[h] for h in range(nh)]
    for ch in range(nch):
        rows = slice(ch * c, (ch + 1) * c)
        heads = terms[ch]
        mm = [_dot_nt(jnp.concatenate([ahat, qhat], axis=0), s[h])
              for h, (ahat, _, qhat, _) in enumerate(heads)]
        u = [mm[h][:c] + heads[h][1] for h in range(nh)]
        for h in range(nh):
            o_ref[rows, hs[h]] = mm[h][c:] + heads[h][3]
        s = [s[h] * wc[ch * c:ch * c + 1, hs[h]] + _dot_tn(u[h], bc[rows, hs[h]]) + vk[ch * nh + h]
             for h in range(nh)]
    for h in range(nh):
        s_scr[h] = s[h]

    @pl.when(i == pl.num_programs(0) - 1)
    def _():
        sout_ref[...] = s_scr[...]


def _mix_specs(bw):
    fix = lambda *_: (0, 0)
    vec = pl.BlockSpec((1, bw), fix)
    lora = pl.BlockSpec((LORA_PAD, bw), fix)
    return [vec, lora, vec, lora, vec, vec, vec]


def _wkv_prompt(r, k, v, lora, mix_small, s0, *, nch=4):
    t, bw = r.shape
    c = WKV_CHUNK
    nh = bw // HEAD
    row = pl.BlockSpec((nch * c, bw), lambda i: (i, 0))
    lrow = pl.BlockSpec((nch * c, LORA_PAD), lambda i: (i, 0))
    st = pl.BlockSpec((nh, HEAD, HEAD), lambda i: (0, 0, 0))
    out = jax.ShapeDtypeStruct((t, bw), F32)
    return pl.pallas_call(
        functools.partial(_wkv_prompt_body, c=c, nch=nch),
        out_shape=[out, out, jax.ShapeDtypeStruct((nh, HEAD, HEAD), F32)],
        grid=(t // (nch * c),), in_specs=[row] * 3 + [lrow] + _mix_specs(bw) + [st], out_specs=[row, row, st],
        scratch_shapes=[pltpu.VMEM((nh, HEAD, HEAD), F32)],
        compiler_params=_cparams("arbitrary"), name="wkv_prompt")(r, k, v, lora, *mix_small, s0)


def _wkv_sample_body(r_ref, k_ref, v_ref, lora_ref, w0_ref, w2_ref, a0_ref, a2_ref, kk_ref, ka_ref, rk_ref, s0_ref,
                     o_ref, bon_ref, sout_ref,
                     ah_scr, uh_scr, qh_scr, oh_scr, bc_scr, kc_scr, wc_scr, *, nb, blk, ns):
    prm = (w0_ref[...], w2_ref[...], a0_ref[...], a2_ref[...], kk_ref[...], ka_ref[...], rk_ref[...])
    r, lw, km, v, kkn, ab, bonus = _even_mix(r_ref[...], k_ref[...], v_ref[...], lora_ref[...], prm, _head_ones())
    real = (lax.broadcasted_iota(jnp.int32, lw.shape, 0) & (blk - 1)) < ns
    lw = jnp.where(real, lw, 0.0)
    bon_ref[...] = bonus
    (heads,), bc, kc, wc = _wkv_block_terms(r, lw, km, v, kkn, ab, blk, 1)
    bc_scr[...] = bc
    kc_scr[...] = kc
    wc_scr[...] = wc
    for h, (ahat, uhat, qhat, ohat) in enumerate(heads):
        sl = slice(h * HEAD, (h + 1) * HEAD)
        ah_scr[:, sl] = ahat
        uh_scr[:, sl] = uhat
        qh_scr[:, sl] = qhat
        oh_scr[:, sl] = ohat
    nh = len(heads)

    def per_seq(b, carry):
        rows = pl.ds(pl.multiple_of(b * blk, blk), blk)
        for h in range(nh):
            sl = slice(h * HEAD, (h + 1) * HEAD)
            s = s0_ref[b, h]
            mm = _dot_nt(jnp.concatenate([ah_scr[rows, sl], qh_scr[rows, sl]], axis=0), s)
            u = mm[:blk] + uh_scr[rows, sl]
            o_ref[rows, sl] = mm[blk:] + oh_scr[rows, sl]
            wrow = wc_scr[rows, sl][0:1, :]
            sout_ref[b, h] = s * wrow + _dot_tn(u, bc_scr[rows, sl]) + _dot_tn(v_ref[rows, sl], kc_scr[rows, sl])
        return carry

    lax.fori_loop(0, nb, per_seq, 0)


def _wkv_sample(r, k, v, lora, mix_small, s0, *, blk, ns):
    n, bw = r.shape
    nb = n // blk
    scr = pltpu.VMEM((n, bw), F32)
    out = jax.ShapeDtypeStruct((n, bw), F32)
    return pl.pallas_call(
        functools.partial(_wkv_sample_body, nb=nb, blk=blk, ns=ns),
        out_shape=[out, out, jax.ShapeDtypeStruct(s0.shape, F32)],
        scratch_shapes=[scr] * 7, compiler_params=_cparams(), name="wkv_sample")(r, k, v, lora, *mix_small, s0)


def _even_back_rows(o, bonus, bg, ya, x, lg, lb, w_ref, g, b, hsum, *, alpha, bw):
    om = _head_sum(o, hsum) * (1.0 / HEAD)
    oc = o - om
    ov = _head_sum(oc * oc, hsum) * (1.0 / HEAD)
    on = oc * lax.rsqrt(ov + LNX_EPS) * lg + lb
    yb = (on + bonus) * bg
    y = _dot(ya, w_ref[0:bw, :]) + _dot(yb, w_ref[bw:2 * bw, :])
    return _layer_norm(alpha * x + y, g, b, LN_EPS)


def _project_class_block(xb, w_ref, blk, cls_ref, p_scr, *, tm, gw, d, scale):
    pc = jnp.dot(xb, w_ref[:, blk * gw:(blk + 1) * gw], preferred_element_type=F32)
    if scale is not None:
        pc = pc * scale
    if d == 1:
        cls_ref[0] = pc.astype(BF16)
        return
    lt = p_scr.shape[-1]
    nlb = gw // lt
    slot = (blk % 2) * nlb
    for lb in range(nlb):
        p_scr[slot + lb] = pc[:, lb * lt:(lb + 1) * lt]
    n = tm // d
    for c in range(d):
        for lb in range(nlb):
            cls_ref[c, :, lb * lt:(lb + 1) * lt] = p_scr[slot + lb, pl.ds(c, n, stride=d), :].astype(BF16)


def _mid_body(o_ref, bon_ref, bg_ref, ya_ref, x_ref, lg_ref, lb_ref, wo_ref, g_ref, b_ref, wi_ref, x1_ref, *refs,
              tm, sub, gw, bw, dils, alpha, scale):
    ng = len(dils)
    cls_refs = refs[0:3 * ng]
    gate_ref, xb_scr, p_scr = refs[3 * ng:3 * ng + 3]
    i = pl.program_id(0)
    slot = i % 2

    @pl.when(i == 0)
    def _():
        xb_scr[1] = jnp.zeros(xb_scr.shape[1:], BF16)

    xb = xb_scr[1 - slot]
    hsum = _head_ones()
    nsub = tm // sub
    nblk = 3 * ng + 1
    for blk in range(nblk):
        if blk < 3 * ng:
            g = blk % ng
            _project_class_block(xb, wi_ref, blk, cls_refs[blk], p_scr, tm=tm, gw=gw, d=dils[g],
                                 scale=scale if blk < ng else None)
        else:
            gate_ref[...] = _silu(jnp.dot(xb, wi_ref[:, blk * gw:(blk + 1) * gw], preferred_element_type=F32))
        for s in range(nsub):
            if s * nblk // nsub != blk:
                continue
            rows = slice(s * sub, (s + 1) * sub)
            x1 = _even_back_rows(o_ref[rows, :], bon_ref[rows, :], bg_ref[rows, :], ya_ref[rows, :], x_ref[rows, :],
                                 lg_ref[...], lb_ref[...], wo_ref, g_ref[...], b_ref[...], hsum, alpha=alpha, bw=bw)
            x1_ref[rows, :] = x1
            xb_scr[slot, rows, :] = x1.astype(BF16)


def _mid(o, bonus, bg, ya, x, lnx_g, lnx_b, w_out, ln_g, ln_b, w_in, *, alpha, tm, dils, gw):
    t, d_model = x.shape
    bw = o.shape[-1]
    ng = len(dils)
    nt = t // tm
    cur = lambda i: (jnp.minimum(i, nt - 1), 0)
    prev = lambda i: (jnp.maximum(i - 1, 0), 0)
    fix = lambda i: (0, 0)
    half = pl.BlockSpec((tm, bw), cur)
    once = lambda shape: pl.BlockSpec(shape, fix, pipeline_mode=pl.Buffered(1))
    in_specs = [half] * 4 + [pl.BlockSpec((tm, d_model), cur), once((1, bw)), once((1, bw)), once((2 * bw, d_model)),
                             once((1, d_model)), once((1, d_model)), once(w_in.shape)]
    cls_specs = [pl.BlockSpec((d, tm // d, gw), lambda i: (0, jnp.maximum(i - 1, 0), 0)) for d in dils]
    cls_shapes = [jax.ShapeDtypeStruct((d, t // d, gw), BF16) for d in dils]
    outs = [jax.ShapeDtypeStruct((t, d_model), F32)] + cls_shapes * 3 + [jax.ShapeDtypeStruct((t, gw), F32)]
    out_specs = [pl.BlockSpec((tm, d_model), cur)] + cls_specs * 3 + [pl.BlockSpec((tm, gw), prev)]
    body = functools.partial(_mid_body, tm=tm, sub=min(tm, 128), gw=gw, bw=bw, dils=dils, alpha=alpha,
                             scale=HEAD ** -0.5)
    res = pl.pallas_call(
        body, out_shape=outs, grid=(nt + 1,), in_specs=in_specs, out_specs=out_specs,
        scratch_shapes=[pltpu.VMEM((2, tm, d_model), BF16), pltpu.VMEM((2 * gw // 128, tm, 128), F32)],
        compiler_params=_cparams("arbitrary"), name="mid")(o, bonus, bg, ya, x, lnx_g, lnx_b, w_out, ln_g, ln_b, w_in)
    return res[0], res[1:1 + ng], res[1 + ng:1 + 2 * ng], res[1 + 2 * ng:1 + 3 * ng], res[1 + 3 * ng]


def _kv_rows_body(x_ref, wt_ref, kv_ref):
    kv_ref[...] = lax.dot_general(wt_ref[...], x_ref[...].astype(BF16), (((1,), (1,)), ((), ())),
                                  preferred_element_type=F32)


def _kv_rows_t(x, w_kv_t, *, tm, first_row):
    t = x.shape[0] - first_row
    d_model = x.shape[1]
    ncol = w_kv_t.shape[0]
    return pl.pallas_call(
        _kv_rows_body, out_shape=jax.ShapeDtypeStruct((ncol, t), F32), grid=(t // tm,),
        in_specs=[pl.BlockSpec((tm, d_model), lambda i: (i + first_row // tm, 0)),
                  pl.BlockSpec((ncol, d_model), lambda i: (0, 0))],
        out_specs=pl.BlockSpec((ncol, tm), lambda i: (0, i)),
        compiler_params=_cparams("parallel"), name="kv_rows")(x, w_kv_t)


def _band_attn_body(q_ref, kc_ref, kp_ref, vc_ref, vp_ref, o_ref, lse_ref, *, bq, span, nb_class, rsub):
    j = pl.program_id(0)
    first_lo = jnp.where(((j * bq) % nb_class) == 0, span, 0)
    qi = lax.broadcasted_iota(jnp.int32, (rsub, 2 * span), 0)
    ki = lax.broadcasted_iota(jnp.int32, (rsub, 2 * span), 1)
    in_band = [(ki >= qi + r0) & (ki <= qi + r0 + span) for r0 in range(0, span, rsub)]
    left = lax.broadcasted_iota(jnp.int32, (rsub, 2 * HEAD), 1) < HEAD
    gw = q_ref.shape[-1]
    for b in range(bq):
        masks = [m & (ki >= first_lo) for m in in_band] if b == 0 else in_band
        for pr in range(gw // (2 * HEAD)):
            ls = slice(pr * 2 * HEAD, (pr + 1) * 2 * HEAD)
            q2 = q_ref[b * span:(b + 1) * span, ls]
            if b == 0:
                k2 = jnp.concatenate([kp_ref[:, ls], kc_ref[0:span, ls]], axis=0)
                v2 = jnp.concatenate([vp_ref[:, ls], vc_ref[0:span, ls]], axis=0)
            else:
                k2 = kc_ref[(b - 1) * span:(b + 1) * span, ls]
                v2 = vc_ref[(b - 1) * span:(b + 1) * span, ls]
            for r0 in range(0, span, rsub):
                outs, lses = [], []
                for sel in (left, jnp.logical_not(left)):
                    qh = jnp.where(sel, q2[r0:r0 + rsub], jnp.zeros((rsub, 2 * HEAD), BF16))
                    s = lax.dot_general(qh, k2, (((1,), (1,)), ((), ())), preferred_element_type=F32)
                    s = jnp.where(masks[r0 // rsub], s, -jnp.inf)
                    m = jnp.max(s, axis=-1, keepdims=True)
                    p = jnp.exp(s - m)
                    l = jnp.sum(p, axis=-1, keepdims=True)
                    outs.append(jnp.dot(p.astype(BF16), v2, preferred_element_type=F32) / l)
                    lses.append(m + jnp.log(l))
                rows = slice(b * span + r0, b * span + r0 + rsub)
                o_ref[rows, ls] = jnp.where(left, outs[0], outs[1])
                lse_ref[rows, ls] = jnp.where(left, lses[0], lses[1])


def _band_attn(q, k, v, *, span, nb_class, bq=4):
    t, gw = q.shape
    cur = pl.BlockSpec((bq * span, gw), lambda j: (j, 0))
    prev = pl.BlockSpec((span, gw), lambda j: (jnp.maximum(j * bq - 1, 0), 0))
    out = jax.ShapeDtypeStruct((t, gw), F32)
    return pl.pallas_call(
        functools.partial(_band_attn_body, bq=bq, span=span, nb_class=nb_class, rsub=span),
        out_shape=[out, out], grid=(t // (bq * span),), in_specs=[cur, cur, prev, cur, prev],
        out_specs=[cur, cur], compiler_params=_cparams("parallel"), name="band_attn")(q, k, k, v, v)


def _cache_attn_body(q_ref, nt_ref, c_ref, o_ref, lse_ref, cout_ref, *, w, d, window, ns, gw, rchunk):
    nh = gw // HEAD
    rp = 8
    zrow = jnp.zeros((rp - ns, gw), F32)
    q8 = jnp.concatenate([q_ref[0], zrow], axis=0)
    ncol = nt_ref.shape[1]
    ci = lax.broadcasted_iota(jnp.int32, (ncol, rp), 0)
    si = lax.broadcasted_iota(jnp.int32, (ncol, rp), 1)
    pick = jnp.where((ci == pl.program_id(0) * ns + si) & (si < ns), 1.0, 0.0).astype(BF16)
    new_t = _split3_dot_right(nt_ref[...], pick)
    kn_t = new_t[0:gw].astype(BF16)
    vn_t = new_t[gw:2 * gw].astype(BF16)
    qm = jnp.concatenate([q8] * nh, axis=0)
    rowh = lax.broadcasted_iota(jnp.int32, (nh * rp, gw), 0) >> 3
    laneh = lax.broadcasted_iota(jnp.int32, (nh * rp, gw), 1) >> 6
    own = rowh == laneh
    qm = jnp.where(own, qm, 0.0).astype(BF16)
    s_c = jnp.dot(qm, c_ref[0, 0:gw, :].astype(BF16), preferred_element_type=F32)
    s_n = jnp.dot(qm, kn_t, preferred_element_type=F32)
    srow = lax.broadcasted_iota(jnp.int32, s_c.shape, 0) & (rp - 1)
    dist = w + srow - lax.broadcasted_iota(jnp.int32, s_c.shape, 1)
    valid_c = (dist <= window) & ((dist & (d - 1)) == 0) & (srow < ns)
    srow_n = lax.broadcasted_iota(jnp.int32, s_n.shape, 0) & (rp - 1)
    col_n = lax.broadcasted_iota(jnp.int32, s_n.shape, 1)
    dist_n = srow_n - col_n
    valid_n = (dist_n >= 0) & ((dist_n & (d - 1)) == 0) & (srow_n < ns)
    valid_n = valid_n | ((srow_n >= ns) & (col_n == 0))
    s_c = jnp.where(valid_c, s_c, -jnp.inf)
    s_n = jnp.where(valid_n, s_n, -jnp.inf)
    m = jnp.maximum(jnp.max(s_c, axis=-1, keepdims=True), jnp.max(s_n, axis=-1, keepdims=True))
    p_c = jnp.exp(s_c - m)
    p_n = jnp.exp(s_n - m)
    l = jnp.sum(p_c, axis=-1, keepdims=True) + jnp.sum(p_n, axis=-1, keepdims=True)
    o = lax.dot_general(p_c.astype(BF16), c_ref[0, gw:2 * gw, :].astype(BF16), (((1,), (1,)), ((), ())),
                        preferred_element_type=F32)
    o = (o + lax.dot_general(p_n.astype(BF16), vn_t, (((1,), (1,)), ((), ())), preferred_element_type=F32)) / l
    lse = jnp.broadcast_to(m + jnp.log(l), o.shape)
    o = jnp.where(own, o, 0.0)
    lse = jnp.where(own, lse, 0.0)
    o8 = o[0:rp]
    l8 = lse[0:rp]
    for h in range(1, nh):
        o8 = o8 + o[h * rp:(h + 1) * rp]
        l8 = l8 + lse[h * rp:(h + 1) * rp]
    o_ref[0] = o8[0:ns]
    lse_ref[0] = l8[0:ns]
    for rb in range(2 * gw // rchunk):
        rows = slice(rb * rchunk, (rb + 1) * rchunk)
        cout_ref[0, rows, :] = jnp.concatenate([c_ref[0, rows, ns:w], new_t[rows, 0:ns]], axis=1)


def _cache_attn(q, new_t, cache_t, *, d, window):
    nbat, ns, gw = q.shape
    w = cache_t.shape[2]
    new = pl.BlockSpec((1, ns, gw), lambda b: (b, 0, 0))
    ntb = pl.BlockSpec(new_t.shape, lambda b: (0, 0))
    cb = pl.BlockSpec((1, 2 * gw, w), lambda b: (b, 0, 0))
    out = jax.ShapeDtypeStruct((nbat, ns, gw), F32)
    return pl.pallas_call(
        functools.partial(_cache_attn_body, w=w, d=d, window=window, ns=ns, gw=gw, rchunk=64),
        out_shape=[out, out, jax.ShapeDtypeStruct(cache_t.shape, F32)], grid=(nbat,),
        in_specs=[new, ntb, cb], out_specs=[new, new, cb],
        compiler_params=_cparams("parallel"), name="cache_attn")(q, new_t, cache_t)


def _odd_back_body(*refs, tm, dils, alpha):
    ng = len(dils)
    o_refs, l_refs = refs[0:ng], refs[ng:2 * ng]
    gate_ref, x_ref, w_ref, g_ref, b_ref, y_ref = refs[2 * ng:2 * ng + 6]
    scr = refs[2 * ng + 6:]
    os_, ls_ = [], []
    for g, d in enumerate(dils):
        if d == 1:
            os_.append(o_refs[g][0])
            ls_.append(l_refs[g][0])
        else:
            n = tm // d
            nlb, _, lt = scr[2 * g].shape
            for c in range(d):
                for lb in range(nlb):
                    scr[2 * g][lb, pl.ds(c, n, stride=d), :] = o_refs[g][c, :, lb * lt:(lb + 1) * lt]
                    scr[2 * g + 1][lb, pl.ds(c, n, stride=d), :] = l_refs[g][c, :, lb * lt:(lb + 1) * lt]
            os_.append(jnp.concatenate([scr[2 * g][lb] for lb in range(nlb)], axis=1))
            ls_.append(jnp.concatenate([scr[2 * g + 1][lb] for lb in range(nlb)], axis=1))
    m = ls_[0]
    for l in ls_[1:]:
        m = jnp.maximum(m, l)
    es = [jnp.exp(l - m) for l in ls_]
    den = es[0]
    for e in es[1:]:
        den = den + e
    o = es[0] * os_[0]
    for e, og in zip(es[1:], os_[1:]):
        o = o + e * og
    o = o / den
    y = _dot(o * gate_ref[...], w_ref[...])
    y_ref[...] = _layer_norm(alpha * x_ref[...] + y, g_ref[...], b_ref[...], LN_EPS)


def _odd_back(os_, ls_, gate, x, w_out, ln_g, ln_b, *, dils, alpha, tm):
    t, d_model = x.shape
    gw = gate.shape[-1]
    row = lambda i: (i, 0)
    fix = lambda i: (0, 0)
    cls_specs = [pl.BlockSpec((d, tm // d, gw), lambda i: (0, i, 0)) for d in dils]
    in_specs = cls_specs * 2 + [pl.BlockSpec((tm, gw), row), pl.BlockSpec((tm, d_model), row),
                                pl.BlockSpec((gw, d_model), fix), pl.BlockSpec((1, d_model), fix),
                                pl.BlockSpec((1, d_model), fix)]
    return pl.pallas_call(
        functools.partial(_odd_back_body, tm=tm, dils=dils, alpha=alpha),
        out_shape=jax.ShapeDtypeStruct((t, d_model), F32), grid=(t // tm,), in_specs=in_specs,
        out_specs=pl.BlockSpec((tm, d_model), row),
        scratch_shapes=[pltpu.VMEM((gw // 128, tm, 128), F32)] * (2 * len(dils)),
        compiler_params=_cparams("parallel"), name="odd_back")(*os_, *ls_, gate, x, w_out, ln_g, ln_b)


def kernel(x_prompt, x_sample, state_conv, state_shift, state_wkv, cache_kv_w128, cache_kv_w512, cache_kv_w2048,
           w_in_even, conv_w, conv_b, conv_ln_g, conv_ln_b, mu_shift, w0, w2, a0, a2, k_k, k_a, r_k, lnx_g, lnx_b,
           w_out_even, w_in_odd, w_out_odd, ln_g, ln_b):
    nbp, seq, d_model = x_prompt.shape
    nbs, ns, _ = x_sample.shape
    assert nbp == 1
    depth = ln_g.shape[0]
    assert depth == 2 and w_in_even.shape[0] == 1 and w_in_odd.shape[0] == 1
    alpha = (2.0 * depth) ** 0.25
    bw = w0.shape[-1]
    lora_d, lora_a = w2.shape[1], a2.shape[1]
    nh = bw // HEAD
    nhist = CONV_W - 1
    caches = (cache_kv_w128, cache_kv_w512, cache_kv_w2048)
    dils = tuple(d for _, d in C_GROUPS)
    gw = w_out_odd.shape[1]
    ng = len(C_GROUPS)

    wi = w_in_even[0]
    sh_cols = 3 * bw + lora_d + lora_a
    zpad = jnp.zeros((d_model, LORA_PAD - lora_d - lora_a), F32)
    w_re = jnp.concatenate([wi[:, :sh_cols], zpad, wi[:, sh_cols:]], axis=1).astype(BF16)
    mu_re = jnp.concatenate([mu_shift[0], jnp.zeros((LORA_PAD - lora_d - lora_a,), F32)])[None, :]
    w2p = jnp.zeros((LORA_PAD, bw), F32).at[0:lora_d].set(w2[0])
    a2p = jnp.zeros((LORA_PAD, bw), F32).at[lora_d:lora_d + lora_a].set(a2[0])
    cw = jnp.repeat(conv_w[0], 8, axis=0)
    vec = lambda z: z.reshape(1, -1)
    conv_small = (vec(conv_b[0]), vec(conv_ln_g[0]), vec(conv_ln_b[0]))
    mix_small = (vec(w0[0]), w2p, vec(a0[0]), a2p, vec(k_k[0]), vec(k_a[0]), vec(r_k[0]))
    w_out_e = w_out_even[0].astype(BF16)
    w_in_o = w_in_odd[0].astype(BF16)
    w_out_o = w_out_odd[0].astype(BF16)
    lng = [vec(ln_g[l]) for l in range(depth)]
    lnb = [vec(ln_b[l]) for l in range(depth)]

    xp = x_prompt[0]
    ya, bg, r, k, v, lora, conv_tail = _even_front_prompt(xp, w_re, mu_re, cw, conv_small)
    o_wkv, bonus, wkv_p = _wkv_prompt(r, k, v, lora, mix_small, jnp.zeros((nh, HEAD, HEAD), F32))
    xp1, qc, kc, vc, gate_p = _mid(o_wkv, bonus, bg, ya, xp, vec(lnx_g[0]), vec(lnx_b[0]), w_out_e, lng[0], lnb[0],
                                   w_in_o, alpha=alpha, tm=512, dils=dils, gw=gw)
    conv_p = conv_tail[CONV_HIST - nhist:][None, None]
    shift_p = x_prompt[:, -1][None]

    xs_t = jnp.transpose(x_sample, (1, 0, 2)).reshape(ns * nbs, d_model)
    x_in = jnp.concatenate([state_shift[0], xs_t], axis=0)
    hist = jnp.transpose(state_conv[0], (1, 0, 2)).reshape(nhist * nbs, bw)
    res = _even_front_sample(x_in, hist, w_re, mu_re, cw, conv_small, nb=nbs, ns=ns)
    ya_s, bg_s, conv_s_t = res[0], res[1], res[6]
    blk = 8

    def seq_major_pad(z):
        z = jnp.transpose(z.reshape(ns, nbs, z.shape[-1]), (1, 0, 2))
        return jnp.pad(z, ((0, 0), (0, blk - ns), (0, 0))).reshape(nbs * blk, z.shape[-1])

    def step_major(z):
        return jnp.transpose(z.reshape(nbs, blk, z.shape[-1])[:, :ns], (1, 0, 2)).reshape(ns * nbs, z.shape[-1])

    wkv_in = [seq_major_pad(z) for z in res[2:6]]
    o_s_pad, bon_s_pad, wkv_s = _wkv_sample(*wkv_in, mix_small, state_wkv[0], blk=blk, ns=ns)
    o_s, bon_s = step_major(o_s_pad), step_major(bon_s_pad)
    ones = (1,) * ng
    xs1_t, qs_t, _, _, gate_s_t = _mid(o_s, bon_s, bg_s, ya_s, xs_t, vec(lnx_g[0]), vec(lnx_b[0]), w_out_e, lng[0],
                                       lnb[0], w_in_o, alpha=alpha, tm=ns * nbs, dils=ones, gw=gw)
    conv_s = jnp.transpose(conv_s_t.reshape(nhist, nbs, bw), (1, 0, 2))[None]
    shift_s = x_sample[:, -1][None]

    def seq_major(z):
        return jnp.transpose(z.reshape(ns, nbs, z.shape[-1]), (1, 0, 2)).reshape(nbs * ns, z.shape[-1])

    xs1 = seq_major(xs1_t)
    gate_s = seq_major(gate_s_t)
    qs = [seq_major(q[0]) for q in qs_t]

    tm1 = 512
    tail =min(max(win for win, _ in C_GROUPS), seq)
    w_kv_t = jnp.concatenate([w_in_odd[0][:, (part * ng + g) * gw:(part * ng + g + 1) * gw].T
                              for g in range(ng) for part in (1, 2)], axis=0).astype(BF16)
    kvt = _kv_rows_t(xp1, w_kv_t, tm=512, first_row=seq - tail).reshape(ng, 2, gw // HEAD, HEAD, tail)
    os_, ls_ = [], []
    for g, (win, d) in enumerate(C_GROUPS):
        span = win // d
        flat = lambda z: z.reshape(seq, gw)
        o_g, l_g = _band_attn(flat(qc[g]), flat(kc[g]), flat(vc[g]), span=span, nb_class=(seq // d) // span)
        os_.append(o_g.reshape(d, seq // d, gw))
        ls_.append(l_g.reshape(d, seq // d, gw))
    xp2 = _odd_back(os_, ls_, gate_p, xp1, w_out_o, lng[1], lnb[1], dils=dils, alpha=alpha, tm=tm1)
    kv_p = []
    for g, (win, d) in enumerate(C_GROUPS):
        n = min(win, seq)
        kv_p.append(jnp.transpose(kvt[g, :, :, :, tail - n:], (3, 0, 1, 2))[None, None])

    kvt_s = _kv_rows_t(xs1, w_kv_t, tm=nbs * ns, first_row=0).reshape(ng, 2 * gw, nbs * ns)
    os_s, ls_s, kv_s_out = [], [], []
    for g, (win, d) in enumerate(C_GROUPS):
        cache = caches[g][0]
        w = cache.shape[1]
        q_g = qs[g].astype(F32).reshape(nbs, ns, gw)
        cache_t = jnp.transpose(cache.reshape(nbs, w, 2 * gw), (0, 2, 1))
        o_g, l_g, cnew_t = _cache_attn(q_g, kvt_s[g], cache_t, d=d, window=win)
        os_s.append(o_g.reshape(1, nbs * ns, gw))
        ls_s.append(l_g.reshape(1, nbs * ns, gw))
        kv_s_out.append(jnp.transpose(cnew_t, (0, 2, 1)).reshape(nbs, w, 2, gw // HEAD, HEAD)[None])
    xs2 = _odd_back(os_s, ls_s, gate_s, xs1, w_out_o, lng[1], lnb[1], dils=ones, alpha=alpha, tm=nbs * ns)

    return (xp2[None], xs2.reshape(nbs, ns, d_model), conv_p, conv_s, shift_p, shift_s,
            wkv_p[None, None], wkv_s[None],
            kv_p[0], kv_s_out[0], kv_p[1], kv_s_out[1], kv_p[2], kv_s_out[2])
```

```python
import functools

import jax
import jax.numpy as jnp
from jax import lax
from jax.experimental import pallas as pl
from jax.experimental.pallas import tpu as pltpu

F32 = jnp.float32
BF16 = jnp.bfloat16

LN_EPS = 1e-5
LNX_EPS = 64e-5
HEAD = 64
LORA_PAD = 128
CONV_W = 31
CONV_HIST = 32
C_GROUPS = ((128, 1), (512, 4), (2048, 16))
WKV_CHUNK = 64
VMEM_LIMIT = 56 * 1024 * 1024


def _cparams(*sem):
    return pltpu.CompilerParams(dimension_semantics=sem or None, vmem_limit_bytes=VMEM_LIMIT)


def _dot(a, b):
    return jnp.dot(a.astype(BF16), b.astype(BF16), preferred_element_type=F32)


def _dot_nt(a, b):
    return lax.dot_general(a.astype(BF16), b.astype(BF16), (((1,), (1,)), ((), ())), preferred_element_type=F32)


def _dot_tn(a, b):
    return lax.dot_general(a.astype(BF16), b.astype(BF16), (((0,), (0,)), ((), ())), preferred_element_type=F32)


def _head_sum(a, ones_bf16):
    ah = a.astype(BF16)
    al = (a - ah.astype(F32)).astype(BF16)
    lt = ones_bf16.shape[0]
    parts = []
    for lb in range(a.shape[1] // lt):
        sl = slice(lb * lt, (lb + 1) * lt)
        parts.append(jnp.dot(ah[:, sl], ones_bf16, preferred_element_type=F32) +
                     jnp.dot(al[:, sl], ones_bf16, preferred_element_type=F32))
    return jnp.concatenate(parts, axis=1)


def _head_ones():
    n = 2 * HEAD
    r = lax.broadcasted_iota(jnp.int32, (n, n), 0) >> 6
    c = lax.broadcasted_iota(jnp.int32, (n, n), 1) >> 6
    return jnp.where(r == c, 1.0, 0.0).astype(BF16)


def _sigmoid(x):
    return 1.0 / (1.0 + jnp.exp(-x))


def _silu(x):
    return x * _sigmoid(x)


def _softplus(y):
    return jnp.maximum(y, 0.0) + jnp.log(1.0 + jnp.exp(-jnp.abs(y)))


def _layer_norm(x, g, b, eps):
    mu = jnp.mean(x, axis=-1, keepdims=True)
    xc = x - mu
    var = jnp.mean(xc * xc, axis=-1, keepdims=True)
    return xc * lax.rsqrt(var + eps) * g + b


def _even_mix(r, k, v, lora, prm, hsum):
    w0, w2p, a0, a2p, k_k, k_a, r_k = prm
    w_log = -_softplus(-(w0 + _dot(jnp.tanh(lora), w2p))) - 0.5
    lw = -jnp.exp(w_log)
    a = _sigmoid(a0 + _dot(lora, a2p))
    kk = k * k_k
    nrm = jnp.sqrt(_head_sum(kk * kk, hsum))
    kkn = kk / jnp.maximum(nrm, 1e-12)
    km = k * (1.0 + (a - 1.0) * k_a)
    bonus = _head_sum(r * km * r_k, hsum) * v
    return r, lw, km, v, kkn, kkn * a, bonus


def _tap_weight(cw_ref, j, n):
    return jnp.concatenate([cw_ref[8 * j:8 * j + 8, :]] * (n // 8), axis=0)


def _conv_ln_gate(u_ref, sh_ref, row0, n, cw_ref, cb, clg, clb, a_gate):
    acc = jnp.zeros((n, cb.shape[-1]), F32) + cb
    for j in range(CONV_W):
        a, b = divmod(CONV_HIST - (CONV_W - 1) + j, 8)
        win = u_ref[pl.ds(row0 + 8 * a, n), :] if b == 0 else sh_ref[b - 1, pl.ds(row0 + 8 * a, n), :]
        acc = acc + _tap_weight(cw_ref, j, n) * win
    return _silu(_layer_norm(acc, clg, clb, LN_EPS)) * _silu(a_gate)


def _even_front_prompt_body(x_ref, w_ref, mu_ref, cw_ref, cb_ref, clg_ref, clb_ref,
                            ya_ref, bg_ref, r_ref, k_ref, v_ref, lora_ref, conv_ref,
                            p_scr, u_scr, sh_scr, *, tm, bw, sub):
    i = pl.program_id(0)
    shw = 3 * bw + LORA_PAD

    @pl.when(i == 0)
    def _():
        p_scr[0:8, :] = jnp.zeros((8, shw), F32)
        u_scr[0:CONV_HIST, :] = jnp.zeros((CONV_HIST, bw), F32)

    xb = x_ref[...].astype(BF16)
    proj = lambda c0, c1: jnp.dot(xb, w_ref[:, c0:c1], preferred_element_type=F32)
    cur = proj(0, shw)
    p_scr[8:8 + tm, :] = cur
    prev = p_scr[pl.ds(7, tm), :]
    sh = cur + (prev - cur) * mu_ref[...]
    p_scr[7:8, :] = cur[tm - 1:tm, :]
    r_ref[...] = sh[:, 0:bw]
    k_ref[...] = sh[:, bw:2 * bw]
    v_ref[...] = sh[:, 2 * bw:3 * bw]
    lora_ref[...] = sh[:, 3 * bw:shw]

    bg_ref[...] = _silu(proj(shw + 3 * bw, shw + 4 * bw))
    u_scr[CONV_HIST:CONV_HIST + tm, :] = proj(shw, shw + bw) * _sigmoid(proj(shw + bw, shw + 2 * bw))
    a_gate = proj(shw + 2 * bw, shw + 3 * bw)
    for b in range(1, 8):
        sh_scr[b - 1] = u_scr[pl.ds(b, sh_scr.shape[1]), :]
    cb, clg, clb = cb_ref[...], clg_ref[...], clb_ref[...]
    for s in range(tm // sub):
        ya_ref[s * sub:(s + 1) * sub, :] = _conv_ln_gate(u_scr, sh_scr, s * sub, sub, cw_ref, cb, clg, clb,
                                                         a_gate[s * sub:(s + 1) * sub, :]).astype(BF16)
    tail = u_scr[tm:tm + CONV_HIST, :]
    conv_ref[...] = tail
    u_scr[0:CONV_HIST, :] = tail


def _even_front_prompt(x, w_re, mu_re, cw, conv_small, *, tm=256):
    t, d = x.shape
    bw = conv_small[0].shape[-1]
    shw = 3 * bw + LORA_PAD
    ncol = w_re.shape[1]
    row = lambda i: (i, 0)
    fix = lambda i: (0, 0)
    vec = pl.BlockSpec((1, bw), fix)
    in_specs = [pl.BlockSpec((tm, d), row), pl.BlockSpec((d, ncol), fix), pl.BlockSpec((1, shw), fix),
                pl.BlockSpec((8 * CONV_W, bw), fix), vec, vec, vec]
    out = jax.ShapeDtypeStruct((t, bw), F32)
    outs = [jax.ShapeDtypeStruct((t, bw), BF16)] + [out] * 4 + [jax.ShapeDtypeStruct((t, LORA_PAD), F32),
                                                                jax.ShapeDtypeStruct((CONV_HIST, bw), F32)]
    out_specs = [pl.BlockSpec((tm, bw), row)] * 5 + [pl.BlockSpec((tm, LORA_PAD), row),
                                                     pl.BlockSpec((CONV_HIST, bw), fix)]
    body = functools.partial(_even_front_prompt_body, tm=tm, bw=bw, sub=32)
    return pl.pallas_call(
        body, out_shape=outs, grid=(t // tm,), in_specs=in_specs, out_specs=out_specs,
        scratch_shapes=[pltpu.VMEM((8 + tm, shw), F32), pltpu.VMEM((CONV_HIST + tm, bw), F32),
                        pltpu.VMEM((7, CONV_HIST - 8 + tm, bw), F32)],
        compiler_params=_cparams("arbitrary"), name="even_front_prompt")(x, w_re, mu_re, cw, *conv_small)


def _even_front_sample_body(x_ref, hist_ref, w_ref, mu_ref, cw_ref, cb_ref, clg_ref, clb_ref,
                            ya_ref, bg_ref, r_ref, k_ref, v_ref, lora_ref, conv_ref,
                            u_scr, *, nb, ns, bw):
    n = nb * ns
    shw = 3 * bw + LORA_PAD
    nh = CONV_W - 1
    p = jnp.dot(x_ref[...].astype(BF16), w_ref[...], preferred_element_type=F32)
    cur = p[nb:, :shw]
    prev = p[:n, :shw]
    sh = cur + (prev - cur) * mu_ref[...]
    r_ref[...] = sh[:, 0:bw]
    k_ref[...] = sh[:, bw:2 * bw]
    v_ref[...] = sh[:, 2 * bw:3 * bw]
    lora_ref[...] = sh[:, 3 * bw:shw]
    rest = p[nb:, shw:]
    a_val, a_glu = rest[:, 0:bw], rest[:, bw:2 * bw]
    a_gate, b_gate = rest[:, 2 * bw:3 * bw], rest[:, 3 * bw:4 * bw]
    bg_ref[...] = _silu(b_gate)
    u_scr[0:nh * nb, :] = hist_ref[...]
    u_scr[nh * nb:(nh + ns) * nb, :] = a_val * _sigmoid(a_glu)
    cb, clg, clb = cb_ref[...], clg_ref[...], clb_ref[...]
    for s in range(ns):
        acc = jnp.zeros((nb, bw), F32) + cb
        for j in range(CONV_W):
            acc = acc + _tap_weight(cw_ref, j, nb) * u_scr[(s + j) * nb:(s + j + 1) * nb, :]
        ya_ref[s * nb:(s + 1) * nb, :] = (_silu(_layer_norm(acc, clg, clb, LN_EPS)) *
                                           _silu(a_gate[s * nb:(s + 1) * nb, :])).astype(BF16)
    conv_ref[...] = u_scr[ns * nb:(ns + nh) * nb, :]


def _even_front_sample(x_in, hist, w_re, mu_re, cw, conv_small, *, nb, ns):
    bw = conv_small[0].shape[-1]
    n = nb * ns
    nh = CONV_W - 1
    out = jax.ShapeDtypeStruct((n, bw), F32)
    outs = [jax.ShapeDtypeStruct((n, bw), BF16)] + [out] * 4 + [jax.ShapeDtypeStruct((n, LORA_PAD), F32),
                                                                jax.ShapeDtypeStruct((nh * nb, bw), F32)]
    body = functools.partial(_even_front_sample_body, nb=nb, ns=ns, bw=bw)
    return pl.pallas_call(
        body, out_shape=outs, scratch_shapes=[pltpu.VMEM(((nh + ns) * nb, bw), F32)],
        compiler_params=_cparams(), name="even_front_sample")(x_in, hist, w_re, mu_re, cw, *conv_small)


def _wkv_block_terms(r, lw, km, v, kkn, ab, blk, problems):
    n, width = r.shape
    nh = width // HEAD
    ri = lax.broadcasted_iota(jnp.int32, (n, n), 0)
    ci = lax.broadcasted_iota(jnp.int32, (n, n), 1)
    same = _same_block(ri, ci, blk)
    incl = (ri >= ci) & same
    strict = (ri > ci) & same
    cum = _split3_dot(jnp.where(incl, 1.0, 0.0).astype(BF16), lw)
    tot = _split3_dot(jnp.where(same, 1.0, 0.0).astype(BF16), lw)
    dn = jnp.exp(-cum)
    at = -kkn * jnp.exp(cum - lw)
    bt = ab * dn
    kt = km * dn
    rt = r * jnp.exp(cum)
    rem = jnp.exp(tot - cum)
    bc = ab * rem
    kc = km * rem
    wc = jnp.exp(tot)
    m = n // problems
    if m != n:
        ri = lax.broadcasted_iota(jnp.int32, (m, m), 0)
        ci = lax.broadcasted_iota(jnp.int32, (m, m), 1)
        same = _same_block(ri, ci, blk)
        incl = (ri >= ci) & same
        strict = (ri > ci) & same
    chains = [(slice(p * m, (p + 1) * m), slice(h * HEAD, (h + 1) * HEAD)) for p in range(problems) for h in range(nh)]
    gs = [_dot_nt(jnp.concatenate([at[rw, sl], rt[rw, sl]], axis=0), jnp.concatenate([bt[rw, sl], kt[rw, sl]], axis=0))
          for rw, sl in chains]
    l_ab = [jnp.where(strict, g[:m, :m], 0.0) for g in gs]
    l_ak = [jnp.where(strict, g[:m, m:], 0.0) for g in gs]
    l_rb = [jnp.where(incl, g[m:, :m], 0.0) for g in gs]
    l_rk = [jnp.where(incl, g[m:, m:], 0.0) for g in gs]
    toff = _unit_lower_inverse_minus_eye(l_ab, ri, ci, blk)
    lv = [_dot(jnp.concatenate([la, lr], axis=0), v[rw, sl]) for la, lr, (rw, sl) in zip(l_ak, l_rk, chains)]
    ahat = [at[rw, sl] + _dot(t, at[rw, sl]) for t, (rw, sl) in zip(toff, chains)]
    uhat = [x[:m] + _dot(t, x[:m]) for t, x in zip(toff, lv)]
    qhat = [rt[rw, sl] + _dot(l, a) for l, a, (rw, sl) in zip(l_rb, ahat, chains)]
    ohat = [_dot(lb, u) + x[m:] for lb, u, x in zip(l_rb, uhat, lv)]
    terms = list(zip(ahat, uhat, qhat, ohat))
    return [terms[p * nh:(p + 1) * nh] for p in range(problems)], bc, kc, wc


def _same_block(ri, ci, size):
    sh = size.bit_length() - 1
    return (ri >> sh) == (ci >> sh)


def _split(x):
    hi = x.astype(BF16)
    return hi, (x - hi.astype(F32)).astype(BF16)


def _split3(x):
    hi, lo = _split(x)
    return hi, lo, (x - hi.astype(F32) - lo.astype(F32)).astype(BF16)


def _split3_dot(ones_bf16, x):
    hi, lo, lo2 = _split3(x)
    d = lambda y: jnp.dot(ones_bf16, y, preferred_element_type=F32)
    return d(hi) + (d(lo) + d(lo2))


def _split3_dot_right(x, ones_bf16):
    hi, lo, lo2 = _split3(x)
    d = lambda y: jnp.dot(y, ones_bf16, preferred_element_type=F32)
    return d(hi) + (d(lo) + d(lo2))


def _unit_lower_inverse_minus_eye(ls, ri, ci, blk):
    base = 8
    eye = jnp.where(ri == ci, 1.0, 0.0)
    same8 = _same_block(ri, ci, base)
    l8 = [jnp.where(same8, l, 0.0).astype(BF16) for l in ls]
    x2 = [_dot(x, x).astype(BF16) for x in l8]
    x4 = [_dot(x, x).astype(BF16) for x in x2]
    p = [eye + jnp.where(same8, l, 0.0) for l in ls]
    p = [q + _dot(q, x) for q, x in zip(p, x2)]
    t = [q + _dot(q, x) for q, x in zip(p, x4)]
    size = 2 * base
    while size <= blk:
        half = size.bit_length() - 2
        lower_left = _same_block(ri, ci, size) & (((ri >> half) & 1) == 1) & (((ci >> half) & 1) == 0)
        tb = [x.astype(BF16) for x in t]
        mt = [_dot(jnp.where(lower_left, l, 0.0), x) for l, x in zip(ls, tb)]
        t = [x + _dot(xb, m) for x, xb, m in zip(t, tb, mt)]
        size *= 2
    return [x - eye for x in t]


def _wkv_prompt_body(r_ref, k_ref, v_ref, lora_ref, w0_ref, w2_ref, a0_ref, a2_ref, kk_ref, ka_ref, rk_ref, s0_ref,
                     o_ref, bon_ref, sout_ref, s_scr, *, c, nch):
    i = pl.program_id(0)

    @pl.when(i == 0)
    def _():
        s_scr[...] = s0_ref[...]

    prm = (w0_ref[...], w2_ref[...], a0_ref[...], a2_ref[...], kk_ref[...], ka_ref[...], rk_ref[...])
    r, lw, km, v, kkn, ab, bonus = _even_mix(r_ref[...], k_ref[...], v_ref[...], lora_ref[...], prm, _head_ones())
    bon_ref[...] = bonus
    terms, bc, kc, wc = _wkv_block_terms(r, lw, km, v, kkn, ab, c, nch)
    nh = len(terms[0])
    hs = [slice(h * HEAD, (h + 1) * HEAD) for h in range(nh)]
    vk = [_dot_tn(v[ch * c:(ch + 1) * c, sl], kc[ch * c:(ch + 1) * c, sl]) for ch in range(nch) for sl in hs]
    s = [s_scr[h] for h in range(nh)]
    for ch in range(nch):
        rows = slice(ch * c, (ch + 1) * c)
        heads = terms[ch]
        mm = [_dot_nt(jnp.concatenate([ahat, qhat], axis=0), s[h])
              for h, (ahat, _, qhat, _) in enumerate(heads)]
        u = [mm[h][:c] + heads[h][1] for h in range(nh)]
        for h in range(nh):
            o_ref[rows, hs[h]] = mm[h][c:] + heads[h][3]
        s = [s[h] * wc[ch * c:ch * c + 1, hs[h]] + _dot_tn(u[h], bc[rows, hs[h]]) + vk[ch * nh + h]
             for h in range(nh)]
    for h in range(nh):
        s_scr[h] = s[h]

    @pl.when(i == pl.num_programs(0) - 1)
    def _():
        sout_ref[...] = s_scr[...]


def _mix_specs(bw):
    fix = lambda *_: (0, 0)
    vec = pl.BlockSpec((1, bw), fix)
    lora = pl.BlockSpec((LORA_PAD, bw), fix)
    return [vec, lora, vec, lora, vec, vec, vec]


def _wkv_prompt(r, k, v, lora, mix_small, s0, *, nch=4):
    t, bw = r.shape
    c = WKV_CHUNK
    nh = bw // HEAD
    row = pl.BlockSpec((nch * c, bw), lambda i: (i, 0))
    lrow = pl.BlockSpec((nch * c, LORA_PAD), lambda i: (i, 0))
    st = pl.BlockSpec((nh, HEAD, HEAD), lambda i: (0, 0, 0))
    out = jax.ShapeDtypeStruct((t, bw), F32)
    return pl.pallas_call(
        functools.partial(_wkv_prompt_body, c=c, nch=nch),
        out_shape=[out, out, jax.ShapeDtypeStruct((nh, HEAD, HEAD), F32)],
        grid=(t // (nch * c),), in_specs=[row] * 3 + [lrow] + _mix_specs(bw) + [st], out_specs=[row, row, st],
        scratch_shapes=[pltpu.VMEM((nh, HEAD, HEAD), F32)],
        compiler_params=_cparams("arbitrary"), name="wkv_prompt")(r, k, v, lora, *mix_small, s0)


def _wkv_sample_body(r_ref, k_ref, v_ref, lora_ref, w0_ref, w2_ref, a0_ref, a2_ref, kk_ref, ka_ref, rk_ref, s0_ref,
                     o_ref, bon_ref, sout_ref,
                     ah_scr, uh_scr, qh_scr, oh_scr, bc_scr, kc_scr, wc_scr, *, nb, blk, ns):
    prm = (w0_ref[...], w2_ref[...], a0_ref[...], a2_ref[...], kk_ref[...], ka_ref[...], rk_ref[...])
    r, lw, km, v, kkn, ab, bonus = _even_mix(r_ref[...], k_ref[...], v_ref[...], lora_ref[...], prm, _head_ones())
    real = (lax.broadcasted_iota(jnp.int32, lw.shape, 0) & (blk - 1)) < ns
    lw = jnp.where(real, lw, 0.0)
    bon_ref[...] = bonus
    (heads,), bc, kc, wc = _wkv_block_terms(r, lw, km, v, kkn, ab, blk, 1)
    bc_scr[...] = bc
    kc_scr[...] = kc
    wc_scr[...] = wc
    for h, (ahat, uhat, qhat, ohat) in enumerate(heads):
        sl = slice(h * HEAD, (h + 1) * HEAD)
        ah_scr[:, sl] = ahat
        uh_scr[:, sl] = uhat
        qh_scr[:, sl] = qhat
        oh_scr[:, sl] = ohat
    nh = len(heads)

    def per_seq(b, carry):
        rows = pl.ds(pl.multiple_of(b * blk, blk), blk)
        for h in range(nh):
            sl = slice(h * HEAD, (h + 1) * HEAD)
            s = s0_ref[b, h]
            mm = _dot_nt(jnp.concatenate([ah_scr[rows, sl], qh_scr[rows, sl]], axis=0), s)
            u = mm[:blk] + uh_scr[rows, sl]
            o_ref[rows, sl] = mm[blk:] + oh_scr[rows, sl]
            wrow = wc_scr[rows, sl][0:1, :]
            sout_ref[b, h] = s * wrow + _dot_tn(u, bc_scr[rows, sl]) + _dot_tn(v_ref[rows, sl], kc_scr[rows, sl])
        return carry

    lax.fori_loop(0, nb, per_seq, 0)


def _wkv_sample(r, k, v, lora, mix_small, s0, *, blk, ns):
    n, bw = r.shape
    nb = n // blk
    scr = pltpu.VMEM((n, bw), F32)
    out = jax.ShapeDtypeStruct((n, bw), F32)
    return pl.pallas_call(
        functools.partial(_wkv_sample_body, nb=nb, blk=blk, ns=ns),
        out_shape=[out, out, jax.ShapeDtypeStruct(s0.shape, F32)],
        scratch_shapes=[scr] * 7, compiler_params=_cparams(), name="wkv_sample")(r, k, v, lora, *mix_small, s0)


def _even_back_rows(o, bonus, bg, ya, x, lg, lb, w_ref, g, b, hsum, *, alpha, bw):
    om = _head_sum(o, hsum) * (1.0 / HEAD)
    oc = o - om
    ov = _head_sum(oc * oc, hsum) * (1.0 / HEAD)
    on = oc * lax.rsqrt(ov + LNX_EPS) * lg + lb
    yb = (on + bonus) * bg
    y = _dot(ya, w_ref[0:bw, :]) + _dot(yb, w_ref[bw:2 * bw, :])
    return _layer_norm(alpha * x + y, g, b, LN_EPS)


def _project_class_block(xb, w_ref, blk, cls_ref, p_scr, *, tm, gw, d, scale):
    pc = jnp.dot(xb, w_ref[:, blk * gw:(blk + 1) * gw], preferred_element_type=F32)
    if scale is not None:
        pc = pc * scale
    if d == 1:
        cls_ref[0] = pc.astype(BF16)
        return
    lt = p_scr.shape[-1]
    nlb = gw // lt
    slot = (blk % 2) * nlb
    for lb in range(nlb):
        p_scr[slot + lb] = pc[:, lb * lt:(lb + 1) * lt]
    n = tm // d
    for c in range(d):
        for lb in range(nlb):
            cls_ref[c, :, lb * lt:(lb + 1) * lt] = p_scr[slot + lb, pl.ds(c, n, stride=d), :].astype(BF16)


def _mid_body(o_ref, bon_ref, bg_ref, ya_ref, x_ref, lg_ref, lb_ref, wo_ref, g_ref, b_ref, wi_ref, x1_ref, *refs,
              tm, sub, gw, bw, dils, alpha, scale):
    ng = len(dils)
    cls_refs = refs[0:3 * ng]
    gate_ref, xb_scr, p_scr = refs[3 * ng:3 * ng + 3]
    i = pl.program_id(0)
    slot = i % 2

    @pl.when(i == 0)
    def _():
        xb_scr[1] = jnp.zeros(xb_scr.shape[1:], BF16)

    xb = xb_scr[1 - slot]
    hsum = _head_ones()
    nsub = tm // sub
    nblk = 3 * ng + 1
    for blk in range(nblk):
        if blk < 3 * ng:
            g = blk % ng
            _project_class_block(xb, wi_ref, blk, cls_refs[blk], p_scr, tm=tm, gw=gw, d=dils[g],
                                 scale=scale if blk < ng else None)
        else:
            gate_ref[...] = _silu(jnp.dot(xb, wi_ref[:, blk * gw:(blk + 1) * gw], preferred_element_type=F32))
        for s in range(nsub):
            if s * nblk // nsub != blk:
                continue
            rows = slice(s * sub, (s + 1) * sub)
            x1 = _even_back_rows(o_ref[rows, :], bon_ref[rows, :], bg_ref[rows, :], ya_ref[rows, :], x_ref[rows, :],
                                 lg_ref[...], lb_ref[...], wo_ref, g_ref[...], b_ref[...], hsum, alpha=alpha, bw=bw)
            x1_ref[rows, :] = x1
            xb_scr[slot, rows, :] = x1.astype(BF16)


def _mid(o, bonus, bg, ya, x, lnx_g, lnx_b, w_out, ln_g, ln_b, w_in, *, alpha, tm, dils, gw):
    t, d_model = x.shape
    bw = o.shape[-1]
    ng = len(dils)
    nt = t // tm
    cur = lambda i: (jnp.minimum(i, nt - 1), 0)
    prev = lambda i: (jnp.maximum(i - 1, 0), 0)
    fix = lambda i: (0, 0)
    half = pl.BlockSpec((tm, bw), cur)
    once = lambda shape: pl.BlockSpec(shape, fix, pipeline_mode=pl.Buffered(1))
    in_specs = [half] * 4 + [pl.BlockSpec((tm, d_model), cur), once((1, bw)), once((1, bw)), once((2 * bw, d_model)),
                             once((1, d_model)), once((1, d_model)), once(w_in.shape)]
    cls_specs = [pl.BlockSpec((d, tm // d, gw), lambda i: (0, jnp.maximum(i - 1, 0), 0)) for d in dils]
    cls_shapes = [jax.ShapeDtypeStruct((d, t // d, gw), BF16) for d in dils]
    outs = [jax.ShapeDtypeStruct((t, d_model), F32)] + cls_shapes * 3 + [jax.ShapeDtypeStruct((t, gw), F32)]
    out_specs = [pl.BlockSpec((tm, d_model), cur)] + cls_specs * 3 + [pl.BlockSpec((tm, gw), prev)]
    body = functools.partial(_mid_body, tm=tm, sub=min(tm, 128), gw=gw, bw=bw, dils=dils, alpha=alpha,
                             scale=HEAD ** -0.5)
    res = pl.pallas_call(
        body, out_shape=outs, grid=(nt + 1,), in_specs=in_specs, out_specs=out_specs,
        scratch_shapes=[pltpu.VMEM((2, tm, d_model), BF16), pltpu.VMEM((2 * gw // 128, tm, 128), F32)],
        compiler_params=_cparams("arbitrary"), name="mid")(o, bonus, bg, ya, x, lnx_g, lnx_b, w_out, ln_g, ln_b, w_in)
    return res[0], res[1:1 + ng], res[1 + ng:1 + 2 * ng], res[1 + 2 * ng:1 + 3 * ng], res[1 + 3 * ng]


def _kv_rows_body(x_ref, wt_ref, kv_ref):
    kv_ref[...] = lax.dot_general(wt_ref[...], x_ref[...].astype(BF16), (((1,), (1,)), ((), ())),
                                  preferred_element_type=F32)


def _kv_rows_t(x, w_kv_t, *, tm, first_row):
    t = x.shape[0] - first_row
    d_model = x.shape[1]
    ncol = w_kv_t.shape[0]
    return pl.pallas_call(
        _kv_rows_body, out_shape=jax.ShapeDtypeStruct((ncol, t), F32), grid=(t // tm,),
        in_specs=[pl.BlockSpec((tm, d_model), lambda i: (i + first_row // tm, 0)),
                  pl.BlockSpec((ncol, d_model), lambda i: (0, 0))],
        out_specs=pl.BlockSpec((ncol, tm), lambda i: (0, i)),
        compiler_params=_cparams("parallel"), name="kv_rows")(x, w_kv_t)


def _band_attn_body(q_ref, kc_ref, kp_ref, vc_ref, vp_ref, o_ref, lse_ref, *, bq, span, nb_class, rsub):
    j = pl.program_id(0)
    first_lo = jnp.where(((j * bq) % nb_class) == 0, span, 0)
    qi = lax.broadcasted_iota(jnp.int32, (rsub, 2 * span), 0)
    ki = lax.broadcasted_iota(jnp.int32, (rsub, 2 * span), 1)
    in_band = [(ki >= qi + r0) & (ki <= qi + r0 + span) for r0 in range(0, span, rsub)]
    left = lax.broadcasted_iota(jnp.int32, (rsub, 2 * HEAD), 1) < HEAD
    gw = q_ref.shape[-1]
    for b in range(bq):
        masks = [m & (ki >= first_lo) for m in in_band] if b == 0 else in_band
        for pr in range(gw // (2 * HEAD)):
            ls = slice(pr * 2 * HEAD, (pr + 1) * 2 * HEAD)
            q2 = q_ref[b * span:(b + 1) * span, ls]
            if b == 0:
                k2 = jnp.concatenate([kp_ref[:, ls], kc_ref[0:span, ls]], axis=0)
                v2 = jnp.concatenate([vp_ref[:, ls], vc_ref[0:span, ls]], axis=0)
            else:
                k2 = kc_ref[(b - 1) * span:(b + 1) * span, ls]
                v2 = vc_ref[(b - 1) * span:(b + 1) * span, ls]
            for r0 in range(0, span, rsub):
                outs, lses = [], []
                for sel in (left, jnp.logical_not(left)):
                    qh = jnp.where(sel, q2[r0:r0 + rsub], jnp.zeros((rsub, 2 * HEAD), BF16))
                    s = lax.dot_general(qh, k2, (((1,), (1,)), ((), ())), preferred_element_type=F32)
                    s = jnp.where(masks[r0 // rsub], s, -jnp.inf)
                    m = jnp.max(s, axis=-1, keepdims=True)
                    p = jnp.exp(s - m)
                    l = jnp.sum(p, axis=-1, keepdims=True)
                    outs.append(jnp.dot(p.astype(BF16), v2, preferred_element_type=F32) / l)
                    lses.append(m + jnp.log(l))
                rows = slice(b * span + r0, b * span + r0 + rsub)
                o_ref[rows, ls] = jnp.where(left, outs[0], outs[1])
                lse_ref[rows, ls] = jnp.where(left, lses[0], lses[1])


def _band_attn(q, k, v, *, span, nb_class, bq=4):
    t, gw = q.shape
    cur = pl.BlockSpec((bq * span, gw), lambda j: (j, 0))
    prev = pl.BlockSpec((span, gw), lambda j: (jnp.maximum(j * bq - 1, 0), 0))
    out = jax.ShapeDtypeStruct((t, gw), F32)
    return pl.pallas_call(
        functools.partial(_band_attn_body, bq=bq, span=span, nb_class=nb_class, rsub=span),
        out_shape=[out, out], grid=(t // (bq * span),), in_specs=[cur, cur, prev, cur, prev],
        out_specs=[cur, cur], compiler_params=_cparams("parallel"), name="band_attn")(q, k, k, v, v)


def _cache_attn_group(q_ref, slab, c_ref, o_ref, lse_ref, cout_ref, *, d, window, ns, rchunk):
    gw = q_ref.shape[-1]
    w = c_ref.shape[-1]
    nh = gw // HEAD
    rp = 8
    zrow = jnp.zeros((rp - ns, gw), F32)
    q8 = jnp.concatenate([q_ref[0], zrow], axis=0)
    ncol = slab.shape[1]
    ci = lax.broadcasted_iota(jnp.int32, (ncol, rp), 0)
    si = lax.broadcasted_iota(jnp.int32, (ncol, rp), 1)
    pick = jnp.where((ci == pl.program_id(0) * ns + si) & (si < ns), 1.0, 0.0).astype(BF16)
    new_t = _split3_dot_right(slab, pick)
    kn_t = new_t[0:gw].astype(BF16)
    vn_t = new_t[gw:2 * gw].astype(BF16)
    qm = jnp.concatenate([q8] * nh, axis=0)
    rowh = lax.broadcasted_iota(jnp.int32, (nh * rp, gw), 0) >> 3
    laneh = lax.broadcasted_iota(jnp.int32, (nh * rp, gw), 1) >> 6
    own = rowh == laneh
    qm = jnp.where(own, qm, 0.0).astype(BF16)
    s_c = jnp.dot(qm, c_ref[0, 0:gw, :].astype(BF16), preferred_element_type=F32)
    s_n = jnp.dot(qm, kn_t, preferred_element_type=F32)
    srow = lax.broadcasted_iota(jnp.int32, s_c.shape, 0) & (rp - 1)
    dist = w + srow - lax.broadcasted_iota(jnp.int32, s_c.shape, 1)
    valid_c = (dist <= window) & ((dist & (d - 1)) == 0) & (srow < ns)
    srow_n = lax.broadcasted_iota(jnp.int32, s_n.shape, 0) & (rp - 1)
    col_n = lax.broadcasted_iota(jnp.int32, s_n.shape, 1)
    dist_n = srow_n - col_n
    valid_n = (dist_n >= 0) & ((dist_n & (d - 1)) == 0) & (srow_n < ns)
    valid_n = valid_n | ((srow_n >= ns) & (col_n == 0))
    s_c = jnp.where(valid_c, s_c, -jnp.inf)
    s_n = jnp.where(valid_n, s_n, -jnp.inf)
    m = jnp.maximum(jnp.max(s_c, axis=-1, keepdims=True), jnp.max(s_n, axis=-1, keepdims=True))
    p_c = jnp.exp(s_c - m)
    p_n = jnp.exp(s_n - m)
    l = jnp.sum(p_c, axis=-1, keepdims=True) + jnp.sum(p_n, axis=-1, keepdims=True)
    o = lax.dot_general(p_c.astype(BF16), c_ref[0, gw:2 * gw, :].astype(BF16), (((1,), (1,)), ((), ())),
                        preferred_element_type=F32)
    o = (o + lax.dot_general(p_n.astype(BF16), vn_t, (((1,), (1,)), ((), ())), preferred_element_type=F32)) / l
    lse = jnp.broadcast_to(m + jnp.log(l), o.shape)
    o = jnp.where(own, o, 0.0)
    lse = jnp.where(own, lse, 0.0)
    o8 = o[0:rp]
    l8 = lse[0:rp]
    for h in range(1, nh):
        o8 = o8 + o[h * rp:(h + 1) * rp]
        l8 = l8 + lse[h * rp:(h + 1) * rp]
    o_ref[0] = o8[0:ns]
    lse_ref[0] = l8[0:ns]
    for rb in range(2 * gw // rchunk):
        rows = slice(rb * rchunk, (rb + 1) * rchunk)
        cout_ref[0, rows, :] = jnp.concatenate([c_ref[0, rows, ns:w], new_t[rows, 0:ns]], axis=1)


def _cache_attn_body(*refs, groups, ns):
    ng = len(groups)
    q_refs, nt_ref, c_refs = refs[0:ng], refs[ng], refs[ng + 1:2 * ng + 1]
    outs = refs[2 * ng + 1:]
    for g, (window, d) in enumerate(groups):
        _cache_attn_group(q_refs[g], nt_ref[g], c_refs[g], outs[3 * g], outs[3 * g + 1], outs[3 * g + 2],
                          d=d, window=window, ns=ns, rchunk=64)


def _cache_attn(qs, new_t, caches_t, *, groups):
    nbat, ns, gw = qs[0].shape
    new = pl.BlockSpec((1, ns, gw), lambda b: (b, 0, 0))
    out = jax.ShapeDtypeStruct((nbat, ns, gw), F32)
    cbs = [pl.BlockSpec((1, 2 * gw, c.shape[2]), lambda b: (b, 0, 0)) for c in caches_t]
    in_specs = [new] * len(groups) + [pl.BlockSpec(new_t.shape, lambda b: (0, 0, 0))] + cbs
    out_shape, out_specs = [], []
    for c, cb in zip(caches_t, cbs):
        out_shape += [out, out, jax.ShapeDtypeStruct(c.shape, F32)]
        out_specs += [new, new, cb]
    res = pl.pallas_call(
        functools.partial(_cache_attn_body, groups=groups, ns=ns),
        out_shape=out_shape, grid=(nbat,), in_specs=in_specs, out_specs=out_specs,
        compiler_params=_cparams("parallel"), name="cache_attn")(*qs, new_t, *caches_t)
    return [res[3 * g:3 * g + 3] for g in range(len(groups))]


def _odd_back_body(*refs, tm, dils, alpha):
    ng = len(dils)
    o_refs, l_refs = refs[0:ng], refs[ng:2 * ng]
    gate_ref, x_ref, w_ref, g_ref, b_ref, y_ref = refs[2 * ng:2 * ng + 6]
    scr = refs[2 * ng + 6:]
    os_, ls_ = [], []
    for g, d in enumerate(dils):
        if d == 1:
            os_.append(o_refs[g][0])
            ls_.append(l_refs[g][0])
        else:
            n = tm // d
            nlb, _, lt = scr[2 * g].shape
            for c in range(d):
                for lb in range(nlb):
                    scr[2 * g][lb, pl.ds(c, n, stride=d), :] = o_refs[g][c, :, lb * lt:(lb + 1) * lt]
                    scr[2 * g + 1][lb, pl.ds(c, n, stride=d), :] = l_refs[g][c, :, lb * lt:(lb + 1) * lt]
            os_.append(jnp.concatenate([scr[2 * g][lb] for lb in range(nlb)], axis=1))
            ls_.append(jnp.concatenate([scr[2 * g + 1][lb] for lb in range(nlb)], axis=1))
    m = ls_[0]
    for l in ls_[1:]:
        m = jnp.maximum(m, l)
    es = [jnp.exp(l - m) for l in ls_]
    den = es[0]
    for e in es[1:]:
        den = den + e
    o = es[0] * os_[0]
    for e, og in zip(es[1:], os_[1:]):
        o = o + e * og
    o = o / den
    y = _dot(o * gate_ref[...], w_ref[...])
    y_ref[...] = _layer_norm(alpha * x_ref[...] + y, g_ref[...], b_ref[...], LN_EPS)


def _odd_back(os_, ls_, gate, x, w_out, ln_g, ln_b, *, dils, alpha, tm):
    t, d_model = x.shape
    gw = gate.shape[-1]
    row = lambda i: (i, 0)
    fix = lambda i: (0, 0)
    cls_specs = [pl.BlockSpec((d, tm // d, gw), lambda i: (0, i, 0)) for d in dils]
    in_specs = cls_specs * 2 + [pl.BlockSpec((tm, gw), row), pl.BlockSpec((tm, d_model), row),
                                pl.BlockSpec((gw, d_model), fix), pl.BlockSpec((1, d_model), fix),
                                pl.BlockSpec((1, d_model), fix)]
    return pl.pallas_call(
        functools.partial(_odd_back_body, tm=tm, dils=dils, alpha=alpha),
        out_shape=jax.ShapeDtypeStruct((t, d_model), F32), grid=(t // tm,), in_specs=in_specs,
        out_specs=pl.BlockSpec((tm, d_model), row),
        scratch_shapes=[pltpu.VMEM((gw // 128, tm, 128), F32)] * (2 * len(dils)),
        compiler_params=_cparams("parallel"), name="odd_back")(*os_, *ls_, gate, x, w_out, ln_g, ln_b)


def kernel(x_prompt, x_sample, state_conv, state_shift, state_wkv, cache_kv_w128, cache_kv_w512, cache_kv_w2048,
           w_in_even, conv_w, conv_b, conv_ln_g, conv_ln_b, mu_shift, w0, w2, a0, a2, k_k, k_a, r_k, lnx_g, lnx_b,
           w_out_even, w_in_odd, w_out_odd, ln_g, ln_b):
    nbp, seq, d_model = x_prompt.shape
    nbs, ns, _ = x_sample.shape
    assert nbp == 1
    depth = ln_g.shape[0]
    assert depth == 2 and w_in_even.shape[0] == 1 and w_in_odd.shape[0] == 1
    alpha = (2.0 * depth) ** 0.25
    bw = w0.shape[-1]
    lora_d, lora_a = w2.shape[1], a2.shape[1]
    nh = bw // HEAD
    nhist = CONV_W - 1
    caches = (cache_kv_w128, cache_kv_w512, cache_kv_w2048)
    dils = tuple(d for _, d in C_GROUPS)
    gw = w_out_odd.shape[1]
    ng = len(C_GROUPS)

    wi = w_in_even[0]
    sh_cols = 3 * bw + lora_d + lora_a
    zpad = jnp.zeros((d_model, LORA_PAD - lora_d - lora_a), F32)
    w_re = jnp.concatenate([wi[:, :sh_cols], zpad, wi[:, sh_cols:]], axis=1).astype(BF16)
    mu_re = jnp.concatenate([mu_shift[0], jnp.zeros((LORA_PAD - lora_d - lora_a,), F32)])[None, :]
    w2p = jnp.zeros((LORA_PAD, bw), F32).at[0:lora_d].set(w2[0])
    a2p = jnp.zeros((LORA_PAD, bw), F32).at[lora_d:lora_d + lora_a].set(a2[0])
    cw = jnp.repeat(conv_w[0], 8, axis=0)
    vec = lambda z: z.reshape(1, -1)
    conv_small = (vec(conv_b[0]), vec(conv_ln_g[0]), vec(conv_ln_b[0]))
    mix_small = (vec(w0[0]), w2p, vec(a0[0]), a2p, vec(k_k[0]), vec(k_a[0]), vec(r_k[0]))
    w_out_e = w_out_even[0].astype(BF16)
    w_in_o = w_in_odd[0].astype(BF16)
    w_out_o = w_out_odd[0].astype(BF16)
    lng = [vec(ln_g[l]) for l in range(depth)]
    lnb = [vec(ln_b[l]) for l in range(depth)]

    xp = x_prompt[0]
    ya, bg, r, k, v, lora, conv_tail = _even_front_prompt(xp, w_re, mu_re, cw, conv_small)
    o_wkv, bonus, wkv_p = _wkv_prompt(r, k, v, lora, mix_small, jnp.zeros((nh, HEAD, HEAD), F32))
    xp1, qc, kc, vc, gate_p = _mid(o_wkv, bonus, bg, ya, xp, vec(lnx_g[0]), vec(lnx_b[0]), w_out_e, lng[0], lnb[0],
                                   w_in_o, alpha=alpha, tm=512, dils=dils, gw=gw)
    conv_p = conv_tail[CONV_HIST - nhist:][None, None]
    shift_p = x_prompt[:, -1][None]

    xs_t = jnp.transpose(x_sample, (1, 0, 2)).reshape(ns * nbs, d_model)
    x_in = jnp.concatenate([state_shift[0], xs_t], axis=0)
    hist = jnp.transpose(state_conv[0], (1, 0, 2)).reshape(nhist * nbs, bw)
    res = _even_front_sample(x_in, hist, w_re, mu_re, cw, conv_small, nb=nbs, ns=ns)
    ya_s, bg_s, conv_s_t = res[0], res[1], res[6]
    blk = 8

    def seq_major_pad(z):
        z = jnp.transpose(z.reshape(ns, nbs, z.shape[-1]), (1, 0, 2))
        return jnp.pad(z, ((0, 0), (0, blk - ns), (0, 0))).reshape(nbs * blk, z.shape[-1])

    def step_major(z):
        return jnp.transpose(z.reshape(nbs, blk, z.shape[-1])[:, :ns], (1, 0, 2)).reshape(ns * nbs, z.shape[-1])

    wkv_in = [seq_major_pad(z) for z in res[2:6]]
    o_s_pad, bon_s_pad, wkv_s = _wkv_sample(*wkv_in, mix_small, state_wkv[0], blk=blk, ns=ns)
    o_s, bon_s = step_major(o_s_pad), step_major(bon_s_pad)
    ones = (1,) * ng
    xs1_t, qs_t, _, _, gate_s_t = _mid(o_s, bon_s, bg_s, ya_s, xs_t, vec(lnx_g[0]), vec(lnx_b[0]), w_out_e, lng[0],
                                       lnb[0], w_in_o, alpha=alpha, tm=ns * nbs, dils=ones, gw=gw)
    conv_s = jnp.transpose(conv_s_t.reshape(nhist, nbs, bw), (1, 0, 2))[None]
    shift_s = x_sample[:, -1][None]

    def seq_major(z):
        return jnp.transpose(z.reshape(ns, nbs, z.shape[-1]), (1, 0, 2)).reshape(nbs * ns, z.shape[-1])

    xs1 = seq_major(xs1_t)
    gate_s = seq_major(gate_s_t)
    qs = [seq_major(q[0]) for q in qs_t]

    tm1 = 512
    tail =min(max(win for win, _ in C_GROUPS), seq)
    w_kv_t = jnp.concatenate([w_in_odd[0][:, (part * ng + g) * gw:(part * ng + g + 1) * gw].T
                              for g in range(ng) for part in (1, 2)], axis=0).astype(BF16)
    kvt = _kv_rows_t(xp1, w_kv_t, tm=512, first_row=seq - tail).reshape(ng, 2, gw // HEAD, HEAD, tail)
    os_, ls_ = [], []
    for g, (win, d) in enumerate(C_GROUPS):
        span = win // d
        flat = lambda z: z.reshape(seq, gw)
        o_g, l_g = _band_attn(flat(qc[g]), flat(kc[g]), flat(vc[g]), span=span, nb_class=(seq // d) // span)
        os_.append(o_g.reshape(d, seq // d, gw))
        ls_.append(l_g.reshape(d, seq // d, gw))
    xp2 = _odd_back(os_, ls_, gate_p, xp1, w_out_o, lng[1], lnb[1], dils=dils, alpha=alpha, tm=tm1)
    kv_p = []
    for g, (win, d) in enumerate(C_GROUPS):
        n = min(win, seq)
        kv_p.append(jnp.transpose(kvt[g, :, :, :, tail - n:], (3, 0, 1, 2))[None, None])

    kvt_s = _kv_rows_t(xs1, w_kv_t, tm=nbs * ns, first_row=0).reshape(ng, 2 * gw, nbs * ns)
    caches_t = [jnp.transpose(c[0].reshape(nbs, c.shape[2], 2 * gw), (0, 2, 1)) for c in caches]
    q_gs = [q.astype(F32).reshape(nbs, ns, gw) for q in qs]
    os_s, ls_s, kv_s_out = [], [], []
    for o_g, l_g, cnew_t in _cache_attn(q_gs, kvt_s, caches_t, groups=C_GROUPS):
        os_s.append(o_g.reshape(1, nbs * ns, gw))
        ls_s.append(l_g.reshape(1, nbs * ns, gw))
        kv_s_out.append(jnp.transpose(cnew_t, (0, 2, 1)).reshape(nbs, cnew_t.shape[2], 2, gw // HEAD, HEAD)[None])
    xs2 = _odd_back(os_s, ls_s, gate_s, xs1, w_out_o, lng[1], lnb[1], dils=ones, alpha=alpha, tm=nbs * ns)

    return (xp2[None], xs2.reshape(nbs, ns, d_model), conv_p, conv_s, shift_p, shift_s,
            wkv_p[None, None], wkv_s[None],
            kv_p[0], kv_s_out[0], kv_p[1], kv_s_out[1], kv_p[2], kv_s_out[2])
```

```python
import functools

import jax
import jax.numpy as jnp
from jax import lax
from jax.experimental import pallas as pl
from jax.experimental.pallas import tpu as pltpu

F32 = jnp.float32
BF16 = jnp.bfloat16

LN_EPS = 1e-5
LNX_EPS = 64e-5
HEAD = 64
LORA_PAD = 128
CONV_W = 31
CONV_HIST = 32
C_GROUPS = ((128, 1), (512, 4), (2048, 16))
WKV_CHUNK = 64
VMEM_LIMIT = 56 * 1024 * 1024


def _cparams(*sem):
    return pltpu.CompilerParams(dimension_semantics=sem or None, vmem_limit_bytes=VMEM_LIMIT)


def _dot(a, b):
    return jnp.dot(a.astype(BF16), b.astype(BF16), preferred_element_type=F32)


def _dot_nt(a, b):
    return lax.dot_general(a.astype(BF16), b.astype(BF16), (((1,), (1,)), ((), ())), preferred_element_type=F32)


def _dot_tn(a, b):
    return lax.dot_general(a.astype(BF16), b.astype(BF16), (((0,), (0,)), ((), ())), preferred_element_type=F32)


def _head_sum(a, ones_bf16):
    ah = a.astype(BF16)
    al = (a - ah.astype(F32)).astype(BF16)
    lt = ones_bf16.shape[0]
    parts = []
    for lb in range(a.shape[1] // lt):
        sl = slice(lb * lt, (lb + 1) * lt)
        parts.append(jnp.dot(ah[:, sl], ones_bf16, preferred_element_type=F32) +
                     jnp.dot(al[:, sl], ones_bf16, preferred_element_type=F32))
    return jnp.concatenate(parts, axis=1)


def _head_ones():
    n = 2 * HEAD
    r = lax.broadcasted_iota(jnp.int32, (n, n), 0) >> 6
    c = lax.broadcasted_iota(jnp.int32, (n, n), 1) >> 6
    return jnp.where(r == c, 1.0, 0.0).astype(BF16)


def _sigmoid(x):
    return 1.0 / (1.0 + jnp.exp(-x))


def _silu(x):
    return x * _sigmoid(x)


def _softplus(y):
    return jnp.maximum(y, 0.0) + jnp.log(1.0 + jnp.exp(-jnp.abs(y)))


def _layer_norm(x, g, b, eps):
    mu = jnp.mean(x, axis=-1, keepdims=True)
    xc = x - mu
    var = jnp.mean(xc * xc, axis=-1, keepdims=True)
    return xc * lax.rsqrt(var + eps) * g + b


def _even_mix(r, k, v, lora, prm, hsum):
    w0, w2p, a0, a2p, k_k, k_a, r_k = prm
    w_log = -_softplus(-(w0 + _dot(jnp.tanh(lora), w2p))) - 0.5
    lw = -jnp.exp(w_log)
    a = _sigmoid(a0 + _dot(lora, a2p))
    kk = k * k_k
    nrm = jnp.sqrt(_head_sum(kk * kk, hsum))
    kkn = kk / jnp.maximum(nrm, 1e-12)
    km = k * (1.0 + (a - 1.0) * k_a)
    bonus = _head_sum(r * km * r_k, hsum) * v
    return r, lw, km, v, kkn, kkn * a, bonus


def _tap_weight(cw_ref, j, n):
    return jnp.concatenate([cw_ref[8 * j:8 * j + 8, :]] * (n // 8), axis=0)


def _conv_ln_gate(u_ref, sh_ref, row0, n, cw_ref, cb, clg, clb, a_gate):
    acc = jnp.zeros((n, cb.shape[-1]), F32) + cb
    for j in range(CONV_W):
        a, b = divmod(CONV_HIST - (CONV_W - 1) + j, 8)
        win = u_ref[pl.ds(row0 + 8 * a, n), :] if b == 0 else sh_ref[b - 1, pl.ds(row0 + 8 * a, n), :]
        acc = acc + _tap_weight(cw_ref, j, n) * win
    return _silu(_layer_norm(acc, clg, clb, LN_EPS)) * _silu(a_gate)


def _even_front_prompt_body(x_ref, w_ref, mu_ref, cw_ref, cb_ref, clg_ref, clb_ref,
                            ya_ref, bg_ref, r_ref, k_ref, v_ref, lora_ref, conv_ref,
                            p_scr, u_scr, sh_scr, *, tm, bw, sub):
    i = pl.program_id(0)
    shw = 3 * bw + LORA_PAD

    @pl.when(i == 0)
    def _():
        p_scr[0:8, :] = jnp.zeros((8, shw), F32)
        u_scr[0:CONV_HIST, :] = jnp.zeros((CONV_HIST, bw), F32)

    xb = x_ref[...].astype(BF16)
    proj = lambda c0, c1: jnp.dot(xb, w_ref[:, c0:c1], preferred_element_type=F32)
    cur = proj(0, shw)
    p_scr[8:8 + tm, :] = cur
    prev = p_scr[pl.ds(7, tm), :]
    sh = cur + (prev - cur) * mu_ref[...]
    p_scr[7:8, :] = cur[tm - 1:tm, :]
    r_ref[...] = sh[:, 0:bw]
    k_ref[...] = sh[:, bw:2 * bw]
    v_ref[...] = sh[:, 2 * bw:3 * bw]
    lora_ref[...] = sh[:, 3 * bw:shw]

    bg_ref[...] = _silu(proj(shw + 3 * bw, shw + 4 * bw))
    u_scr[CONV_HIST:CONV_HIST + tm, :] = proj(shw, shw + bw) * _sigmoid(proj(shw + bw, shw + 2 * bw))
    a_gate = proj(shw + 2 * bw, shw + 3 * bw)
    for b in range(1, 8):
        sh_scr[b - 1] = u_scr[pl.ds(b, sh_scr.shape[1]), :]
    cb, clg, clb = cb_ref[...], clg_ref[...], clb_ref[...]
    for s in range(tm // sub):
        ya_ref[s * sub:(s + 1) * sub, :] = _conv_ln_gate(u_scr, sh_scr, s * sub, sub, cw_ref, cb, clg, clb,
                                                         a_gate[s * sub:(s + 1) * sub, :]).astype(BF16)
    tail = u_scr[tm:tm + CONV_HIST, :]
    conv_ref[...] = tail
    u_scr[0:CONV_HIST, :] = tail


def _even_front_prompt(x, w_re, mu_re, cw, conv_small, *, tm=512):
    t, d = x.shape
    bw = conv_small[0].shape[-1]
    shw = 3 * bw + LORA_PAD
    ncol = w_re.shape[1]
    row = lambda i: (i, 0)
    fix = lambda i: (0, 0)
    vec = pl.BlockSpec((1, bw), fix)
    in_specs = [pl.BlockSpec((tm, d), row), pl.BlockSpec((d, ncol), fix), pl.BlockSpec((1, shw), fix),
                pl.BlockSpec((8 * CONV_W, bw), fix), vec, vec, vec]
    out = jax.ShapeDtypeStruct((t, bw), F32)
    outs = [jax.ShapeDtypeStruct((t, bw), BF16)] + [out] * 4 + [jax.ShapeDtypeStruct((t, LORA_PAD), F32),
                                                                jax.ShapeDtypeStruct((CONV_HIST, bw), F32)]
    out_specs = [pl.BlockSpec((tm, bw), row)] * 5 + [pl.BlockSpec((tm, LORA_PAD), row),
                                                     pl.BlockSpec((CONV_HIST, bw), fix)]
    body = functools.partial(_even_front_prompt_body, tm=tm, bw=bw, sub=32)
    return pl.pallas_call(
        body, out_shape=outs, grid=(t // tm,), in_specs=in_specs, out_specs=out_specs,
        scratch_shapes=[pltpu.VMEM((8 + tm, shw), F32), pltpu.VMEM((CONV_HIST + tm, bw), F32),
                        pltpu.VMEM((7, CONV_HIST - 8 + tm, bw), F32)],
        compiler_params=_cparams("arbitrary"), name="even_front_prompt")(x, w_re, mu_re, cw, *conv_small)


def _even_front_sample_body(x_ref, hist_ref, w_ref, mu_ref, cw_ref, cb_ref, clg_ref, clb_ref,
                            ya_ref, bg_ref, r_ref, k_ref, v_ref, lora_ref, conv_ref,
                            u_scr, *, nb, ns, bw):
    n = nb * ns
    shw = 3 * bw + LORA_PAD
    nh = CONV_W - 1
    p = jnp.dot(x_ref[...].astype(BF16), w_ref[...], preferred_element_type=F32)
    cur = p[nb:, :shw]
    prev = p[:n, :shw]
    sh = cur + (prev - cur) * mu_ref[...]
    r_ref[...] = sh[:, 0:bw]
    k_ref[...] = sh[:, bw:2 * bw]
    v_ref[...] = sh[:, 2 * bw:3 * bw]
    lora_ref[...] = sh[:, 3 * bw:shw]
    rest = p[nb:, shw:]
    a_val, a_glu = rest[:, 0:bw], rest[:, bw:2 * bw]
    a_gate, b_gate = rest[:, 2 * bw:3 * bw], rest[:, 3 * bw:4 * bw]
    bg_ref[...] = _silu(b_gate)
    u_scr[0:nh * nb, :] = hist_ref[...]
    u_scr[nh * nb:(nh + ns) * nb, :] = a_val * _sigmoid(a_glu)
    cb, clg, clb = cb_ref[...], clg_ref[...], clb_ref[...]
    for s in range(ns):
        acc = jnp.zeros((nb, bw), F32) + cb
        for j in range(CONV_W):
            acc = acc + _tap_weight(cw_ref, j, nb) * u_scr[(s + j) * nb:(s + j + 1) * nb, :]
        ya_ref[s * nb:(s + 1) * nb, :] = (_silu(_layer_norm(acc, clg, clb, LN_EPS)) *
                                           _silu(a_gate[s * nb:(s + 1) * nb, :])).astype(BF16)
    conv_ref[...] = u_scr[ns * nb:(ns + nh) * nb, :]


def _even_front_sample(x_in, hist, w_re, mu_re, cw, conv_small, *, nb, ns):
    bw = conv_small[0].shape[-1]
    n = nb * ns
    nh = CONV_W - 1
    out = jax.ShapeDtypeStruct((n, bw), F32)
    outs = [jax.ShapeDtypeStruct((n, bw), BF16)] + [out] * 4 + [jax.ShapeDtypeStruct((n, LORA_PAD), F32),
                                                                jax.ShapeDtypeStruct((nh * nb, bw), F32)]
    body = functools.partial(_even_front_sample_body, nb=nb, ns=ns, bw=bw)
    return pl.pallas_call(
        body, out_shape=outs, scratch_shapes=[pltpu.VMEM(((nh + ns) * nb, bw), F32)],
        compiler_params=_cparams(), name="even_front_sample")(x_in, hist, w_re, mu_re, cw, *conv_small)


def _wkv_block_terms(r, lw, km, v, kkn, ab, blk, problems):
    n, width = r.shape
    nh = width // HEAD
    ri = lax.broadcasted_iota(jnp.int32, (n, n), 0)
    ci = lax.broadcasted_iota(jnp.int32, (n, n), 1)
    same = _same_block(ri, ci, blk)
    incl = (ri >= ci) & same
    strict = (ri > ci) & same
    cum = _split3_dot(jnp.where(incl, 1.0, 0.0).astype(BF16), lw)
    tot = _split3_dot(jnp.where(same, 1.0, 0.0).astype(BF16), lw)
    dn = jnp.exp(-cum)
    at = -kkn * jnp.exp(cum - lw)
    bt = ab * dn
    kt = km * dn
    rt = r * jnp.exp(cum)
    rem = jnp.exp(tot - cum)
    bc = ab * rem
    kc = km * rem
    wc = jnp.exp(tot)
    m = n // problems
    if m != n:
        ri = lax.broadcasted_iota(jnp.int32, (m, m), 0)
        ci = lax.broadcasted_iota(jnp.int32, (m, m), 1)
        same = _same_block(ri, ci, blk)
        incl = (ri >= ci) & same
        strict = (ri > ci) & same
    chains = [(slice(p * m, (p + 1) * m), slice(h * HEAD, (h + 1) * HEAD)) for p in range(problems) for h in range(nh)]
    gs = [_dot_nt(jnp.concatenate([at[rw, sl], rt[rw, sl]], axis=0), jnp.concatenate([bt[rw, sl], kt[rw, sl]], axis=0))
          for rw, sl in chains]
    l_ab = [jnp.where(strict, g[:m, :m], 0.0) for g in gs]
    l_ak = [jnp.where(strict, g[:m, m:], 0.0) for g in gs]
    l_rb = [jnp.where(incl, g[m:, :m], 0.0) for g in gs]
    l_rk = [jnp.where(incl, g[m:, m:], 0.0) for g in gs]
    toff = _unit_lower_inverse_minus_eye(l_ab, ri, ci, blk)
    lv = [_dot(jnp.concatenate([la, lr], axis=0), v[rw, sl]) for la, lr, (rw, sl) in zip(l_ak, l_rk, chains)]
    ahat = [at[rw, sl] + _dot(t, at[rw, sl]) for t, (rw, sl) in zip(toff, chains)]
    uhat = [x[:m] + _dot(t, x[:m]) for t, x in zip(toff, lv)]
    qhat = [rt[rw, sl] + _dot(l, a) for l, a, (rw, sl) in zip(l_rb, ahat, chains)]
    ohat = [_dot(lb, u) + x[m:] for lb, u, x in zip(l_rb, uhat, lv)]
    terms = list(zip(ahat, uhat, qhat, ohat))
    return [terms[p * nh:(p + 1) * nh] for p in range(problems)], bc, kc, wc


def _same_block(ri, ci, size):
    sh = size.bit_length() - 1
    return (ri >> sh) == (ci >> sh)


def _split(x):
    hi = x.astype(BF16)
    return hi, (x - hi.astype(F32)).astype(BF16)


def _split3(x):
    hi, lo = _split(x)
    return hi, lo, (x - hi.astype(F32) - lo.astype(F32)).astype(BF16)


def _split3_dot(ones_bf16, x):
    hi, lo, lo2 = _split3(x)
    d = lambda y: jnp.dot(ones_bf16, y, preferred_element_type=F32)
    return d(hi) + (d(lo) + d(lo2))


def _split3_dot_right(x, ones_bf16):
    hi, lo, lo2 = _split3(x)
    d = lambda y: jnp.dot(y, ones_bf16, preferred_element_type=F32)
    return d(hi) + (d(lo) + d(lo2))


def _unit_lower_inverse_minus_eye(ls, ri, ci, blk):
    base = 8
    eye = jnp.where(ri == ci, 1.0, 0.0)
    same8 = _same_block(ri, ci, base)
    l8 = [jnp.where(same8, l, 0.0).astype(BF16) for l in ls]
    x2 = [_dot(x, x).astype(BF16) for x in l8]
    x4 = [_dot(x, x).astype(BF16) for x in x2]
    p = [eye + jnp.where(same8, l, 0.0) for l in ls]
    p = [q + _dot(q, x) for q, x in zip(p, x2)]
    t = [q + _dot(q, x) for q, x in zip(p, x4)]
    size = 2 * base
    while size <= blk:
        half = size.bit_length() - 2
        lower_left = _same_block(ri, ci, size) & (((ri >> half) & 1) == 1) & (((ci >> half) & 1) == 0)
        tb = [x.astype(BF16) for x in t]
        mt = [_dot(jnp.where(lower_left, l, 0.0), x) for l, x in zip(ls, tb)]
        t = [x + _dot(xb, m) for x, xb, m in zip(t, tb, mt)]
        size *= 2
    return [x - eye for x in t]


def _wkv_prompt_body(r_ref, k_ref, v_ref, lora_ref, w0_ref, w2_ref, a0_ref, a2_ref, kk_ref, ka_ref, rk_ref, s0_ref,
                     o_ref, bon_ref, sout_ref, s_scr, *, c, nch):
    i = pl.program_id(0)

    @pl.when(i == 0)
    def _():
        s_scr[...] = s0_ref[...]

    prm = (w0_ref[...], w2_ref[...], a0_ref[...], a2_ref[...], kk_ref[...], ka_ref[...], rk_ref[...])
    r, lw, km, v, kkn, ab, bonus = _even_mix(r_ref[...], k_ref[...], v_ref[...], lora_ref[...], prm, _head_ones())
    bon_ref[...] = bonus
    terms, bc, kc, wc = _wkv_block_terms(r, lw, km, v, kkn, ab, c, nch)
    nh = len(terms[0])
    hs = [slice(h * HEAD, (h + 1) * HEAD) for h in range(nh)]
    vk = [_dot_tn(v[ch * c:(ch + 1) * c, sl], kc[ch * c:(ch + 1) * c, sl]) for ch in range(nch) for sl in hs]
    s = [s_scr[h] for h in range(nh)]
    for ch in range(nch):
        rows = slice(ch * c, (ch + 1) * c)
        heads = terms[ch]
        mm = [_dot_nt(jnp.concatenate([ahat, qhat], axis=0), s[h])
              for h, (ahat, _, qhat, _) in enumerate(heads)]
        u = [mm[h][:c] + heads[h][1] for h in range(nh)]
        for h in range(nh):
            o_ref[rows, hs[h]] = mm[h][c:] + heads[h][3]
        s = [s[h] * wc[ch * c:ch * c + 1, hs[h]] + _dot_tn(u[h], bc[rows, hs[h]]) + vk[ch * nh + h]
             for h in range(nh)]
    for h in range(nh):
        s_scr[h] = s[h]

    @pl.when(i == pl.num_programs(0) - 1)
    def _():
        sout_ref[...] = s_scr[...]


def _mix_specs(bw):
    fix = lambda *_: (0, 0)
    vec = pl.BlockSpec((1, bw), fix)
    lora = pl.BlockSpec((LORA_PAD, bw), fix)
    return [vec, lora, vec, lora, vec, vec, vec]


def _wkv_prompt(r, k, v, lora, mix_small, s0, *, nch=4):
    t, bw = r.shape
    c = WKV_CHUNK
    nh = bw // HEAD
    row = pl.BlockSpec((nch * c, bw), lambda i: (i, 0))
    lrow = pl.BlockSpec((nch * c, LORA_PAD), lambda i: (i, 0))
    st = pl.BlockSpec((nh, HEAD, HEAD), lambda i: (0, 0, 0))
    out = jax.ShapeDtypeStruct((t, bw), F32)
    return pl.pallas_call(
        functools.partial(_wkv_prompt_body, c=c, nch=nch),
        out_shape=[out, out, jax.ShapeDtypeStruct((nh, HEAD, HEAD), F32)],
        grid=(t // (nch * c),), in_specs=[row] * 3 + [lrow] + _mix_specs(bw) + [st], out_specs=[row, row, st],
        scratch_shapes=[pltpu.VMEM((nh, HEAD, HEAD), F32)],
        compiler_params=_cparams("arbitrary"), name="wkv_prompt")(r, k, v, lora, *mix_small, s0)


def _wkv_sample_body(r_ref, k_ref, v_ref, lora_ref, w0_ref, w2_ref, a0_ref, a2_ref, kk_ref, ka_ref, rk_ref, s0_ref,
                     o_ref, bon_ref, sout_ref,
                     ah_scr, uh_scr, qh_scr, oh_scr, bc_scr, kc_scr, wc_scr, *, nb, blk, ns):
    prm = (w0_ref[...], w2_ref[...], a0_ref[...], a2_ref[...], kk_ref[...], ka_ref[...], rk_ref[...])
    r, lw, km, v, kkn, ab, bonus = _even_mix(r_ref[...], k_ref[...], v_ref[...], lora_ref[...], prm, _head_ones())
    real = (lax.broadcasted_iota(jnp.int32, lw.shape, 0) & (blk - 1)) < ns
    lw = jnp.where(real, lw, 0.0)
    bon_ref[...] = bonus
    (heads,), bc, kc, wc = _wkv_block_terms(r, lw, km, v, kkn, ab, blk, 1)
    bc_scr[...] = bc
    kc_scr[...] = kc
    wc_scr[...] = wc
    for h, (ahat, uhat, qhat, ohat) in enumerate(heads):
        sl = slice(h * HEAD, (h + 1) * HEAD)
        ah_scr[:, sl] = ahat
        uh_scr[:, sl] = uhat
        qh_scr[:, sl] = qhat
        oh_scr[:, sl] = ohat
    nh = len(heads)

    def per_seq(b, carry):
        rows = pl.ds(pl.multiple_of(b * blk, blk), blk)
        for h in range(nh):
            sl = slice(h * HEAD, (h + 1) * HEAD)
            s = s0_ref[b, h]
            mm = _dot_nt(jnp.concatenate([ah_scr[rows, sl], qh_scr[rows, sl]], axis=0), s)
            u = mm[:blk] + uh_scr[rows, sl]
            o_ref[rows, sl] = mm[blk:] + oh_scr[rows, sl]
            wrow = wc_scr[rows, sl][0:1, :]
            sout_ref[b, h] = s * wrow + _dot_tn(u, bc_scr[rows, sl]) + _dot_tn(v_ref[rows, sl], kc_scr[rows, sl])
        return carry

    lax.fori_loop(0, nb, per_seq, 0, unroll=4)


def _wkv_sample(r, k, v, lora, mix_small, s0, *, blk, ns):
    n, bw = r.shape
    nb = n // blk
    scr = pltpu.VMEM((n, bw), F32)
    out = jax.ShapeDtypeStruct((n, bw), F32)
    return pl.pallas_call(
        functools.partial(_wkv_sample_body, nb=nb, blk=blk, ns=ns),
        out_shape=[out, out, jax.ShapeDtypeStruct(s0.shape, F32)],
        scratch_shapes=[scr] * 7, compiler_params=_cparams(), name="wkv_sample")(r, k, v, lora, *mix_small, s0)


def _even_back_rows(o, bonus, bg, ya, x, lg, lb, w_ref, g, b, hsum, *, alpha, bw):
    om = _head_sum(o, hsum) * (1.0 / HEAD)
    oc = o - om
    ov = _head_sum(oc * oc, hsum) * (1.0 / HEAD)
    on = oc * lax.rsqrt(ov + LNX_EPS) * lg + lb
    yb = (on + bonus) * bg
    y = _dot(ya, w_ref[0:bw, :]) + _dot(yb, w_ref[bw:2 * bw, :])
    return _layer_norm(alpha * x + y, g, b, LN_EPS)


def _project_class_block(xb, w_ref, blk, cls_ref, p_scr, *, tm, gw, d, scale):
    pc = jnp.dot(xb, w_ref[:, blk * gw:(blk + 1) * gw], preferred_element_type=F32)
    if scale is not None:
        pc = pc * scale
    if d == 1:
        cls_ref[0] = pc.astype(BF16)
        return
    lt = p_scr.shape[-1]
    nlb = gw // lt
    slot = (blk % 2) * nlb
    for lb in range(nlb):
        p_scr[slot + lb] = pc[:, lb * lt:(lb + 1) * lt]
    n = tm // d
    for c in range(d):
        for lb in range(nlb):
            cls_ref[c, :, lb * lt:(lb + 1) * lt] = p_scr[slot + lb, pl.ds(c, n, stride=d), :].astype(BF16)


def _mid_body(o_ref, bon_ref, bg_ref, ya_ref, x_ref, lg_ref, lb_ref, wo_ref, g_ref, b_ref, wi_ref, x1_ref, *refs,
              tm, sub, gw, bw, dils, alpha, scale):
    ng = len(dils)
    cls_refs = refs[0:3 * ng]
    gate_ref, xb_scr, p_scr = refs[3 * ng:3 * ng + 3]
    i = pl.program_id(0)
    slot = i % 2

    @pl.when(i == 0)
    def _():
        xb_scr[1] = jnp.zeros(xb_scr.shape[1:], BF16)

    xb = xb_scr[1 - slot]
    hsum = _head_ones()
    nsub = tm // sub
    nblk = 3 * ng + 1
    for blk in range(nblk):
        if blk < 3 * ng:
            g = blk % ng
            _project_class_block(xb, wi_ref, blk, cls_refs[blk], p_scr, tm=tm, gw=gw, d=dils[g],
                                 scale=scale if blk < ng else None)
        else:
            gate_ref[...] = _silu(jnp.dot(xb, wi_ref[:, blk * gw:(blk + 1) * gw], preferred_element_type=F32))
        for s in range(nsub):
            if s * nblk // nsub != blk:
                continue
            rows = slice(s * sub, (s + 1) * sub)
            x1 = _even_back_rows(o_ref[rows, :], bon_ref[rows, :], bg_ref[rows, :], ya_ref[rows, :], x_ref[rows, :],
                                 lg_ref[...], lb_ref[...], wo_ref, g_ref[...], b_ref[...], hsum, alpha=alpha, bw=bw)
            x1_ref[rows, :] = x1
            xb_scr[slot, rows, :] = x1.astype(BF16)


def _mid(o, bonus, bg, ya, x, lnx_g, lnx_b, w_out, ln_g, ln_b, w_in, *, alpha, tm, dils, gw):
    t, d_model = x.shape
    bw = o.shape[-1]
    ng = len(dils)
    nt = t // tm
    cur = lambda i: (jnp.minimum(i, nt - 1), 0)
    prev = lambda i: (jnp.maximum(i - 1, 0), 0)
    fix = lambda i: (0, 0)
    half = pl.BlockSpec((tm, bw), cur)
    once = lambda shape: pl.BlockSpec(shape, fix, pipeline_mode=pl.Buffered(1))
    in_specs = [half] * 4 + [pl.BlockSpec((tm, d_model), cur), once((1, bw)), once((1, bw)), once((2 * bw, d_model)),
                             once((1, d_model)), once((1, d_model)), once(w_in.shape)]
    cls_specs = [pl.BlockSpec((d, tm // d, gw), lambda i: (0, jnp.maximum(i - 1, 0), 0)) for d in dils]
    cls_shapes = [jax.ShapeDtypeStruct((d, t // d, gw), BF16) for d in dils]
    outs = [jax.ShapeDtypeStruct((t, d_model), F32)] + cls_shapes * 3 + [jax.ShapeDtypeStruct((t, gw), F32)]
    out_specs = [pl.BlockSpec((tm, d_model), cur)] + cls_specs * 3 + [pl.BlockSpec((tm, gw), prev)]
    body = functools.partial(_mid_body, tm=tm, sub=min(tm, 128), gw=gw, bw=bw, dils=dils, alpha=alpha,
                             scale=HEAD ** -0.5)
    res = pl.pallas_call(
        body, out_shape=outs, grid=(nt + 1,), in_specs=in_specs, out_specs=out_specs,
        scratch_shapes=[pltpu.VMEM((2, tm, d_model), BF16), pltpu.VMEM((2 * gw // 128, tm, 128), F32)],
        compiler_params=_cparams("arbitrary"), name="mid")(o, bonus, bg, ya, x, lnx_g, lnx_b, w_out, ln_g, ln_b, w_in)
    return res[0], res[1:1 + ng], res[1 + ng:1 + 2 * ng], res[1 + 2 * ng:1 + 3 * ng], res[1 + 3 * ng]


def _kv_rows_body(x_ref, wt_ref, kv_ref):
    kv_ref[...] = lax.dot_general(wt_ref[...], x_ref[...].astype(BF16), (((1,), (1,)), ((), ())),
                                  preferred_element_type=F32)


def _kv_rows_t(x, w_kv_t, *, tm, first_row):
    t = x.shape[0] - first_row
    d_model = x.shape[1]
    ncol = w_kv_t.shape[0]
    return pl.pallas_call(
        _kv_rows_body, out_shape=jax.ShapeDtypeStruct((ncol, t), F32), grid=(t // tm,),
        in_specs=[pl.BlockSpec((tm, d_model), lambda i: (i + first_row // tm, 0)),
                  pl.BlockSpec((ncol, d_model), lambda i: (0, 0))],
        out_specs=pl.BlockSpec((ncol, tm), lambda i: (0, i)),
        compiler_params=_cparams("parallel"), name="kv_rows")(x, w_kv_t)


def _band_attn_body(q_ref, kc_ref, kp_ref, vc_ref, vp_ref, o_ref, lse_ref, *, bq, span, nb_class):
    j = pl.program_id(0)
    first_lo = jnp.where(((j * bq) % nb_class) == 0, span, 0)
    qi = lax.broadcasted_iota(jnp.int32, (span, 2 * span), 0)
    ki = lax.broadcasted_iota(jnp.int32, (span, 2 * span), 1)
    in_band = (ki >= qi) & (ki <= qi + span)
    left = lax.broadcasted_iota(jnp.int32, (span, 2 * HEAD), 1) < HEAD
    gw = q_ref.shape[-1]
    for b in range(bq):
        mask = in_band & (ki >= first_lo) if b == 0 else in_band
        for pr in range(gw // (2 * HEAD)):
            ls = slice(pr * 2 * HEAD, (pr + 1) * 2 * HEAD)
            q2 = q_ref[b * span:(b + 1) * span, ls]
            if b == 0:
                k2 = jnp.concatenate([kp_ref[:, ls], kc_ref[0:span, ls]], axis=0)
                v2 = jnp.concatenate([vp_ref[:, ls], vc_ref[0:span, ls]], axis=0)
            else:
                k2 = kc_ref[(b - 1) * span:(b + 1) * span, ls]
                v2 = vc_ref[(b - 1) * span:(b + 1) * span, ls]
            outs, lses = [], []
            for sel in (left, jnp.logical_not(left)):
                qh = jnp.where(sel, q2, jnp.zeros_like(q2))
                s = lax.dot_general(qh, k2, (((1,), (1,)), ((), ())), preferred_element_type=F32)
                s = jnp.where(mask, s, -jnp.inf)
                m = jnp.max(s, axis=-1, keepdims=True)
                p = jnp.exp(s - m)
                l = jnp.sum(p, axis=-1, keepdims=True)
                outs.append(jnp.dot(p.astype(BF16), v2, preferred_element_type=F32) / l)
                lses.append(m + jnp.log(l))
            o_ref[b * span:(b + 1) * span, ls] = jnp.where(left, outs[0], outs[1])
            lse_ref[b * span:(b + 1) * span, ls] = jnp.where(left, lses[0], lses[1])


def _band_attn(q, k, v, *, span, nb_class, bq=4):
    t, gw = q.shape
    cur = pl.BlockSpec((bq * span, gw), lambda j: (j, 0))
    prev = pl.BlockSpec((span, gw), lambda j: (jnp.maximum(j * bq - 1, 0), 0))
    out = jax.ShapeDtypeStruct((t, gw), F32)
    return pl.pallas_call(
        functools.partial(_band_attn_body, bq=bq, span=span, nb_class=nb_class),
        out_shape=[out, out],
        grid=(t // (bq * span),), in_specs=[cur, cur, prev, cur, prev], out_specs=[cur, cur],
        compiler_params=_cparams("parallel"), name="band_attn")(q, k, k, v, v)


def _cache_attn_group(q_ref, slab, c_ref, o_ref, lse_ref, cout_ref, *, d, window, ns, rchunk):
    gw = q_ref.shape[-1]
    w = c_ref.shape[-1]
    nh = gw // HEAD
    rp = 8
    zrow = jnp.zeros((rp - ns, gw), F32)
    q8 = jnp.concatenate([q_ref[0], zrow], axis=0)
    ncol = slab.shape[1]
    ci = lax.broadcasted_iota(jnp.int32, (ncol, rp), 0)
    si = lax.broadcasted_iota(jnp.int32, (ncol, rp), 1)
    pick = jnp.where((ci == pl.program_id(0) * ns + si) & (si < ns), 1.0, 0.0).astype(BF16)
    new_t = _split3_dot_right(slab, pick)
    kn_t = new_t[0:gw].astype(BF16)
    vn_t = new_t[gw:2 * gw].astype(BF16)
    qm = jnp.concatenate([q8] * nh, axis=0)
    rowh = lax.broadcasted_iota(jnp.int32, (nh * rp, gw), 0) >> 3
    laneh = lax.broadcasted_iota(jnp.int32, (nh * rp, gw), 1) >> 6
    own = rowh == laneh
    qm = jnp.where(own, qm, 0.0).astype(BF16)
    s_c = jnp.dot(qm, c_ref[0, 0:gw, :].astype(BF16), preferred_element_type=F32)
    s_n = jnp.dot(qm, kn_t, preferred_element_type=F32)
    srow = lax.broadcasted_iota(jnp.int32, s_c.shape, 0) & (rp - 1)
    dist = w + srow - lax.broadcasted_iota(jnp.int32, s_c.shape, 1)
    valid_c = (dist <= window) & ((dist & (d - 1)) == 0) & (srow < ns)
    srow_n = lax.broadcasted_iota(jnp.int32, s_n.shape, 0) & (rp - 1)
    col_n = lax.broadcasted_iota(jnp.int32, s_n.shape, 1)
    dist_n = srow_n - col_n
    valid_n = (dist_n >= 0) & ((dist_n & (d - 1)) == 0) & (srow_n < ns)
    valid_n = valid_n | ((srow_n >= ns) & (col_n == 0))
    s_c = jnp.where(valid_c, s_c, -jnp.inf)
    s_n = jnp.where(valid_n, s_n, -jnp.inf)
    m = jnp.maximum(jnp.max(s_c, axis=-1, keepdims=True), jnp.max(s_n, axis=-1, keepdims=True))
    p_c = jnp.exp(s_c - m)
    p_n = jnp.exp(s_n - m)
    l = jnp.sum(p_c, axis=-1, keepdims=True) + jnp.sum(p_n, axis=-1, keepdims=True)
    o = lax.dot_general(p_c.astype(BF16), c_ref[0, gw:2 * gw, :].astype(BF16), (((1,), (1,)), ((), ())),
                        preferred_element_type=F32)
    o = (o + lax.dot_general(p_n.astype(BF16), vn_t, (((1,), (1,)), ((), ())), preferred_element_type=F32)) / l
    lse = jnp.broadcast_to(m + jnp.log(l), o.shape)
    o = jnp.where(own, o, 0.0)
    lse = jnp.where(own, lse, 0.0)
    o8 = o[0:rp]
    l8 = lse[0:rp]
    for h in range(1, nh):
        o8 = o8 + o[h * rp:(h + 1) * rp]
        l8 = l8 + lse[h * rp:(h + 1) * rp]
    o_ref[0] = o8[0:ns]
    lse_ref[0] = l8[0:ns]
    for rb in range(2 * gw // rchunk):
        rows = slice(rb * rchunk, (rb + 1) * rchunk)
        cout_ref[0, rows, :] = jnp.concatenate([c_ref[0, rows, ns:w], new_t[rows, 0:ns]], axis=1)


def _cache_attn_body(*refs, groups, ns):
    ng = len(groups)
    q_refs, nt_ref, c_refs = refs[0:ng], refs[ng], refs[ng + 1:2 * ng + 1]
    outs = refs[2 * ng + 1:]
    for g, (window, d) in enumerate(groups):
        _cache_attn_group(q_refs[g], nt_ref[g], c_refs[g], outs[3 * g], outs[3 * g + 1], outs[3 * g + 2],
                          d=d, window=window, ns=ns, rchunk=64)


def _cache_attn(qs, new_t, caches_t, *, groups):
    nbat, ns, gw = qs[0].shape
    new = pl.BlockSpec((1, ns, gw), lambda b: (b, 0, 0))
    out = jax.ShapeDtypeStruct((nbat, ns, gw), F32)
    cbs = [pl.BlockSpec((1, 2 * gw, c.shape[2]), lambda b: (b, 0, 0)) for c in caches_t]
    in_specs = [new] * len(groups) + [pl.BlockSpec(new_t.shape, lambda b: (0, 0, 0))] + cbs
    out_shape, out_specs = [], []
    for c, cb in zip(caches_t, cbs):
        out_shape += [out, out, jax.ShapeDtypeStruct(c.shape, F32)]
        out_specs += [new, new, cb]
    res = pl.pallas_call(
        functools.partial(_cache_attn_body, groups=groups, ns=ns),
        out_shape=out_shape, grid=(nbat,), in_specs=in_specs, out_specs=out_specs,
        compiler_params=_cparams("parallel"), name="cache_attn")(*qs, new_t, *caches_t)
    return [res[3 * g:3 * g + 3] for g in range(len(groups))]


def _odd_back_body(*refs, tm, dils, alpha):
    ng = len(dils)
    o_refs, l_refs = refs[0:ng], refs[ng:2 * ng]
    gate_ref, x_ref, w_ref, g_ref, b_ref, y_ref = refs[2 * ng:2 * ng + 6]
    scr = refs[2 * ng + 6:]
    os_, ls_ = [], []
    for g, d in enumerate(dils):
        if d == 1:
            os_.append(o_refs[g][0])
            ls_.append(l_refs[g][0])
        else:
            n = tm // d
            nlb, _, lt = scr[2 * g].shape
            for c in range(d):
                for lb in range(nlb):
                    scr[2 * g][lb, pl.ds(c, n, stride=d), :] = o_refs[g][c, :, lb * lt:(lb + 1) * lt]
                    scr[2 * g + 1][lb, pl.ds(c, n, stride=d), :] = l_refs[g][c, :, lb * lt:(lb + 1) * lt]
            os_.append(jnp.concatenate([scr[2 * g][lb] for lb in range(nlb)], axis=1))
            ls_.append(jnp.concatenate([scr[2 * g + 1][lb] for lb in range(nlb)], axis=1))
    m = ls_[0]
    for l in ls_[1:]:
        m = jnp.maximum(m, l)
    es = [jnp.exp(l - m) for l in ls_]
    den = es[0]
    for e in es[1:]:
        den = den + e
    o = es[0] * os_[0]
    for e, og in zip(es[1:], os_[1:]):
        o = o + e * og
    o = o / den
    y = _dot(o * gate_ref[...], w_ref[...])
    y_ref[...] = _layer_norm(alpha * x_ref[...] + y, g_ref[...], b_ref[...], LN_EPS)


def _odd_back(os_, ls_, gate, x, w_out, ln_g, ln_b, *, dils, alpha, tm):
    t, d_model = x.shape
    gw = gate.shape[-1]
    row = lambda i: (i, 0)
    fix = lambda i: (0, 0)
    cls_specs = [pl.BlockSpec((d, tm // d, gw), lambda i: (0, i, 0)) for d in dils]
    in_specs = cls_specs * 2 + [pl.BlockSpec((tm, gw), row), pl.BlockSpec((tm, d_model), row),
                                pl.BlockSpec((gw, d_model), fix), pl.BlockSpec((1, d_model), fix),
                                pl.BlockSpec((1, d_model), fix)]
    return pl.pallas_call(
        functools.partial(_odd_back_body, tm=tm, dils=dils, alpha=alpha),
        out_shape=jax.ShapeDtypeStruct((t, d_model), F32), grid=(t // tm,), in_specs=in_specs,
        out_specs=pl.BlockSpec((tm, d_model), row),
        scratch_shapes=[pltpu.VMEM((gw // 128, tm, 128), F32)] * (2 * len(dils)),
        compiler_params=_cparams("parallel"), name="odd_back")(*os_, *ls_, gate, x, w_out, ln_g, ln_b)


def kernel(x_prompt, x_sample, state_conv, state_shift, state_wkv, cache_kv_w128, cache_kv_w512, cache_kv_w2048,
           w_in_even, conv_w, conv_b, conv_ln_g, conv_ln_b, mu_shift, w0, w2, a0, a2, k_k, k_a, r_k, lnx_g, lnx_b,
           w_out_even, w_in_odd, w_out_odd, ln_g, ln_b):
    nbp, seq, d_model = x_prompt.shape
    nbs, ns, _ = x_sample.shape
    assert nbp == 1
    depth = ln_g.shape[0]
    assert depth == 2 and w_in_even.shape[0] == 1 and w_in_odd.shape[0] == 1
    alpha = (2.0 * depth) ** 0.25
    bw = w0.shape[-1]
    lora_d, lora_a = w2.shape[1], a2.shape[1]
    nh = bw // HEAD
    nhist = CONV_W - 1
    caches = (cache_kv_w128, cache_kv_w512, cache_kv_w2048)
    dils = tuple(d for _, d in C_GROUPS)
    gw = w_out_odd.shape[1]
    ng = len(C_GROUPS)

    wi = w_in_even[0]
    sh_cols = 3 * bw + lora_d + lora_a
    zpad = jnp.zeros((d_model, LORA_PAD - lora_d - lora_a), F32)
    w_re = jnp.concatenate([wi[:, :sh_cols], zpad, wi[:, sh_cols:]], axis=1).astype(BF16)
    mu_re = jnp.concatenate([mu_shift[0], jnp.zeros((LORA_PAD - lora_d - lora_a,), F32)])[None, :]
    w2p = jnp.zeros((LORA_PAD, bw), F32).at[0:lora_d].set(w2[0])
    a2p = jnp.zeros((LORA_PAD, bw), F32).at[lora_d:lora_d + lora_a].set(a2[0])
    cw = jnp.repeat(conv_w[0], 8, axis=0)
    vec = lambda z: z.reshape(1, -1)
    conv_small = (vec(conv_b[0]), vec(conv_ln_g[0]), vec(conv_ln_b[0]))
    mix_small = (vec(w0[0]), w2p, vec(a0[0]), a2p, vec(k_k[0]), vec(k_a[0]), vec(r_k[0]))
    w_out_e = w_out_even[0].astype(BF16)
    w_in_o = w_in_odd[0].astype(BF16)
    w_out_o = w_out_odd[0].astype(BF16)
    lng = [vec(ln_g[l]) for l in range(depth)]
    lnb = [vec(ln_b[l]) for l in range(depth)]

    xp = x_prompt[0]
    ya, bg, r, k, v, lora, conv_tail = _even_front_prompt(xp, w_re, mu_re, cw, conv_small)
    o_wkv, bonus, wkv_p = _wkv_prompt(r, k, v, lora, mix_small, jnp.zeros((nh, HEAD, HEAD), F32))
    xp1, qc, kc, vc, gate_p = _mid(o_wkv, bonus, bg, ya, xp, vec(lnx_g[0]), vec(lnx_b[0]), w_out_e, lng[0], lnb[0],
                                   w_in_o, alpha=alpha, tm=512, dils=dils, gw=gw)
    conv_p = conv_tail[CONV_HIST - nhist:][None, None]
    shift_p = x_prompt[:, -1][None]

    xs_t = jnp.transpose(x_sample, (1, 0, 2)).reshape(ns * nbs, d_model)
    x_in = jnp.concatenate([state_shift[0], xs_t], axis=0)
    hist = jnp.transpose(state_conv[0], (1, 0, 2)).reshape(nhist * nbs, bw)
    res = _even_front_sample(x_in, hist, w_re, mu_re, cw, conv_small, nb=nbs, ns=ns)
    ya_s, bg_s, conv_s_t = res[0], res[1], res[6]
    blk = 8

    def seq_major_pad(z):
        z = jnp.transpose(z.reshape(ns, nbs, z.shape[-1]), (1, 0, 2))
        return jnp.pad(z, ((0, 0), (0, blk - ns), (0, 0))).reshape(nbs * blk, z.shape[-1])

    def step_major(z):
        return jnp.transpose(z.reshape(nbs, blk, z.shape[-1])[:, :ns], (1, 0, 2)).reshape(ns * nbs, z.shape[-1])

    wkv_in = [seq_major_pad(z) for z in res[2:6]]
    o_s_pad, bon_s_pad, wkv_s = _wkv_sample(*wkv_in, mix_small, state_wkv[0], blk=blk, ns=ns)
    o_s, bon_s = step_major(o_s_pad), step_major(bon_s_pad)
    ones = (1,) * ng
    xs1_t, qs_t, _, _, gate_s_t = _mid(o_s, bon_s, bg_s, ya_s, xs_t, vec(lnx_g[0]), vec(lnx_b[0]), w_out_e, lng[0],
                                       lnb[0], w_in_o, alpha=alpha, tm=ns * nbs, dils=ones, gw=gw)
    conv_s = jnp.transpose(conv_s_t.reshape(nhist, nbs, bw), (1, 0, 2))[None]
    shift_s = x_sample[:, -1][None]

    def seq_major(z):
        return jnp.transpose(z.reshape(ns, nbs, z.shape[-1]), (1, 0, 2)).reshape(nbs * ns, z.shape[-1])

    xs1 = seq_major(xs1_t)
    gate_s = seq_major(gate_s_t)
    qs = [seq_major(q[0]) for q in qs_t]

    tm1 = 512
    tail =min(max(win for win, _ in C_GROUPS), seq)
    w_kv_t = jnp.concatenate([w_in_odd[0][:, (part * ng + g) * gw:(part * ng + g + 1) * gw].T
                              for g in range(ng) for part in (1, 2)], axis=0).astype(BF16)
    kvt = _kv_rows_t(xp1, w_kv_t, tm=512, first_row=seq - tail).reshape(ng, 2, gw // HEAD, HEAD, tail)
    os_, ls_ = [], []
    for g, (win, d) in enumerate(C_GROUPS):
        span = win // d
        flat = lambda z: z.reshape(seq, gw)
        o_g, l_g = _band_attn(flat(qc[g]), flat(kc[g]), flat(vc[g]), span=span, nb_class=(seq // d) // span)
        os_.append(o_g.reshape(d, seq // d, gw))
        ls_.append(l_g.reshape(d, seq // d, l_g.shape[-1]))
    xp2 = _odd_back(os_, ls_, gate_p, xp1, w_out_o, lng[1], lnb[1], dils=dils, alpha=alpha, tm=tm1)
    kv_p = []
    for g, (win, d) in enumerate(C_GROUPS):
        n = min(win, seq)
        kv_p.append(jnp.transpose(kvt[g, :, :, :, tail - n:], (3, 0, 1, 2))[None, None])

    kvt_s = _kv_rows_t(xs1, w_kv_t, tm=nbs * ns, first_row=0).reshape(ng, 2 * gw, nbs * ns)
    caches_t = [jnp.transpose(c[0].reshape(nbs, c.shape[2], 2 * gw), (0, 2, 1)) for c in caches]
    q_gs = [q.astype(F32).reshape(nbs, ns, gw) for q in qs]
    os_s, ls_s, kv_s_out = [], [], []
    for o_g, l_g, cnew_t in _cache_attn(q_gs, kvt_s, caches_t, groups=C_GROUPS):
        os_s.append(o_g.reshape(1, nbs * ns, gw))
        ls_s.append(l_g.reshape(1, nbs * ns, l_g.shape[-1]))
        kv_s_out.append(jnp.transpose(cnew_t, (0, 2, 1)).reshape(nbs, cnew_t.shape[2], 2, gw // HEAD, HEAD)[None])
    xs2 = _odd_back(os_s, ls_s, gate_s, xs1, w_out_o, lng[1], lnb[1], dils=ones, alpha=alpha, tm=nbs * ns)

    return (xp2[None], xs2.reshape(nbs, ns, d_model), conv_p, conv_s, shift_p, shift_s,
            wkv_p[None, None], wkv_s[None],
            kv_p[0], kv_s_out[0], kv_p[1], kv_s_out[1], kv_p[2], kv_s_out[2])
```

```python
import functools

import jax
import jax.numpy as jnp
from jax import lax
from jax.experimental import pallas as pl
from jax.experimental.pallas import tpu as pltpu

F32 = jnp.float32
BF16 = jnp.bfloat16

LN_EPS = 1e-5
LNX_EPS = 64e-5
HEAD = 64
LORA_PAD = 128
CONV_W = 31
CONV_HIST = 32
C_GROUPS = ((128, 1), (512, 4), (2048, 16))
WKV_CHUNK = 64
VMEM_LIMIT = 56 * 1024 * 1024


def _cparams(*sem):
    return pltpu.CompilerParams(dimension_semantics=sem or None, vmem_limit_bytes=VMEM_LIMIT)


def _dot(a, b):
    return jnp.dot(a.astype(BF16), b.astype(BF16), preferred_element_type=F32)


def _dot_nt(a, b):
    return lax.dot_general(a.astype(BF16), b.astype(BF16), (((1,), (1,)), ((), ())), preferred_element_type=F32)


def _dot_tn(a, b):
    return lax.dot_general(a.astype(BF16), b.astype(BF16), (((0,), (0,)), ((), ())), preferred_element_type=F32)


def _head_sum(a, ones_bf16):
    ah = a.astype(BF16)
    al = (a - ah.astype(F32)).astype(BF16)
    lt = ones_bf16.shape[0]
    parts = []
    for lb in range(a.shape[1] // lt):
        sl = slice(lb * lt, (lb + 1) * lt)
        parts.append(jnp.dot(ah[:, sl], ones_bf16, preferred_element_type=F32) +
                     jnp.dot(al[:, sl], ones_bf16, preferred_element_type=F32))
    return jnp.concatenate(parts, axis=1)


def _head_ones():
    n = 2 * HEAD
    r = lax.broadcasted_iota(jnp.int32, (n, n), 0) >> 6
    c = lax.broadcasted_iota(jnp.int32, (n, n), 1) >> 6
    return jnp.where(r == c, 1.0, 0.0).astype(BF16)


def _sigmoid(x):
    return 1.0 / (1.0 + jnp.exp(-x))


def _silu(x):
    return x * _sigmoid(x)


def _softplus(y):
    return jnp.maximum(y, 0.0) + jnp.log(1.0 + jnp.exp(-jnp.abs(y)))


def _layer_norm(x, g, b, eps):
    mu = jnp.mean(x, axis=-1, keepdims=True)
    xc = x - mu
    var = jnp.mean(xc * xc, axis=-1, keepdims=True)
    return xc * lax.rsqrt(var + eps) * g + b


def _even_mix(r, k, v, lora, prm, hsum):
    w0, w2p, a0, a2p, k_k, k_a, r_k = prm
    w_log = -_softplus(-(w0 + _dot(jnp.tanh(lora), w2p))) - 0.5
    lw = -jnp.exp(w_log)
    a = _sigmoid(a0 + _dot(lora, a2p))
    kk = k * k_k
    nrm = jnp.sqrt(_head_sum(kk * kk, hsum))
    kkn = kk / jnp.maximum(nrm, 1e-12)
    km = k * (1.0 + (a - 1.0) * k_a)
    bonus = _head_sum(r * km * r_k, hsum) * v
    return r, lw, km, v, kkn, kkn * a, bonus


def _tap_weight(cw_ref, j, n):
    return jnp.concatenate([cw_ref[8 * j:8 * j + 8, :]] * (n // 8), axis=0)


def _conv_ln_gate(u_ref, sh_ref, row0, n, cw_ref, cb, clg, clb, a_gate):
    acc = jnp.zeros((n, cb.shape[-1]), F32) + cb
    for j in range(CONV_W):
        a, b = divmod(CONV_HIST - (CONV_W - 1) + j, 8)
        win = u_ref[pl.ds(row0 + 8 * a, n), :] if b == 0 else sh_ref[b - 1, pl.ds(row0 + 8 * a, n), :]
        acc = acc + _tap_weight(cw_ref, j, n) * win
    return _silu(_layer_norm(acc, clg, clb, LN_EPS)) * _silu(a_gate)


def _even_front_prompt_body(x_ref, w_ref, mu_ref, cw_ref, cb_ref, clg_ref, clb_ref,
                            ya_ref, bg_ref, r_ref, k_ref, v_ref, lora_ref, conv_ref,
                            p_scr, u_scr, sh_scr, *, tm, bw, sub):
    i = pl.program_id(0)
    shw = 3 * bw + LORA_PAD

    @pl.when(i == 0)
    def _():
        p_scr[0:8, :] = jnp.zeros((8, shw), F32)
        u_scr[0:CONV_HIST, :] = jnp.zeros((CONV_HIST, bw), F32)

    xb = x_ref[...].astype(BF16)
    proj = lambda c0, c1: jnp.dot(xb, w_ref[:, c0:c1], preferred_element_type=F32)
    cur = proj(0, shw)
    p_scr[8:8 + tm, :] = cur
    prev = p_scr[pl.ds(7, tm), :]
    sh = cur + (prev - cur) * mu_ref[...]
    p_scr[7:8, :] = cur[tm - 1:tm, :]
    r_ref[...] = sh[:, 0:bw]
    k_ref[...] = sh[:, bw:2 * bw]
    v_ref[...] = sh[:, 2 * bw:3 * bw]
    lora_ref[...] = sh[:, 3 * bw:shw]

    bg_ref[...] = _silu(proj(shw + 3 * bw, shw + 4 * bw))
    u_scr[CONV_HIST:CONV_HIST + tm, :] = proj(shw, shw + bw) * _sigmoid(proj(shw + bw, shw + 2 * bw))
    a_gate = proj(shw + 2 * bw, shw + 3 * bw)
    for b in range(1, 8):
        sh_scr[b - 1] = u_scr[pl.ds(b, sh_scr.shape[1]), :]
    cb, clg, clb = cb_ref[...], clg_ref[...], clb_ref[...]
    for s in range(tm // sub):
        ya_ref[s * sub:(s + 1) * sub, :] = _conv_ln_gate(u_scr, sh_scr, s * sub, sub, cw_ref, cb, clg, clb,
                                                         a_gate[s * sub:(s + 1) * sub, :]).astype(BF16)
    tail = u_scr[tm:tm + CONV_HIST, :]
    conv_ref[...] = tail
    u_scr[0:CONV_HIST, :] = tail


def _even_front_prompt(x, w_re, mu_re, cw, conv_small, *, tm=512):
    t, d = x.shape
    bw = conv_small[0].shape[-1]
    shw = 3 * bw + LORA_PAD
    ncol = w_re.shape[1]
    row = lambda i: (i, 0)
    fix = lambda i: (0, 0)
    vec = pl.BlockSpec((1, bw), fix)
    in_specs = [pl.BlockSpec((tm, d), row), pl.BlockSpec((d, ncol), fix), pl.BlockSpec((1, shw), fix),
                pl.BlockSpec((8 * CONV_W, bw), fix), vec, vec, vec]
    out = jax.ShapeDtypeStruct((t, bw), F32)
    outs = [jax.ShapeDtypeStruct((t, bw), BF16)] + [out] * 4 + [jax.ShapeDtypeStruct((t, LORA_PAD), F32),
                                                                jax.ShapeDtypeStruct((CONV_HIST, bw), F32)]
    out_specs = [pl.BlockSpec((tm, bw), row)] * 5 + [pl.BlockSpec((tm, LORA_PAD), row),
                                                     pl.BlockSpec((CONV_HIST, bw), fix)]
    body = functools.partial(_even_front_prompt_body, tm=tm, bw=bw, sub=32)
    return pl.pallas_call(
        body, out_shape=outs, grid=(t // tm,), in_specs=in_specs, out_specs=out_specs,
        scratch_shapes=[pltpu.VMEM((8 + tm, shw), F32), pltpu.VMEM((CONV_HIST + tm, bw), F32),
                        pltpu.VMEM((7, CONV_HIST - 8 + tm, bw), F32)],
        compiler_params=_cparams("arbitrary"), name="even_front_prompt")(x, w_re, mu_re, cw, *conv_small)


def _even_front_sample_body(x_ref, hist_ref, w_ref, mu_ref, cw_ref, cb_ref, clg_ref, clb_ref,
                            ya_ref, bg_ref, r_ref, k_ref, v_ref, lora_ref, conv_ref,
                            u_scr, *, nb, ns, bw):
    n = nb * ns
    shw = 3 * bw + LORA_PAD
    nh = CONV_W - 1
    p = jnp.dot(x_ref[...].astype(BF16), w_ref[...], preferred_element_type=F32)
    cur = p[nb:, :shw]
    prev = p[:n, :shw]
    sh = cur + (prev - cur) * mu_ref[...]
    r_ref[...] = sh[:, 0:bw]
    k_ref[...] = sh[:, bw:2 * bw]
    v_ref[...] = sh[:, 2 * bw:3 * bw]
    lora_ref[...] = sh[:, 3 * bw:shw]
    rest = p[nb:, shw:]
    a_val, a_glu = rest[:, 0:bw], rest[:, bw:2 * bw]
    a_gate, b_gate = rest[:, 2 * bw:3 * bw], rest[:, 3 * bw:4 * bw]
    bg_ref[...] = _silu(b_gate)
    u_scr[0:nh * nb, :] = hist_ref[...]
    u_scr[nh * nb:(nh + ns) * nb, :] = a_val * _sigmoid(a_glu)
    cb, clg, clb = cb_ref[...], clg_ref[...], clb_ref[...]
    for s in range(ns):
        acc = jnp.zeros((nb, bw), F32) + cb
        for j in range(CONV_W):
            acc = acc + _tap_weight(cw_ref, j, nb) * u_scr[(s + j) * nb:(s + j + 1) * nb, :]
        ya_ref[s * nb:(s + 1) * nb, :] = (_silu(_layer_norm(acc, clg, clb, LN_EPS)) *
                                           _silu(a_gate[s * nb:(s + 1) * nb, :])).astype(BF16)
    conv_ref[...] = u_scr[ns * nb:(ns + nh) * nb, :]


def _even_front_sample(x_in, hist, w_re, mu_re, cw, conv_small, *, nb, ns):
    bw = conv_small[0].shape[-1]
    n = nb * ns
    nh = CONV_W - 1
    out = jax.ShapeDtypeStruct((n, bw), F32)
    outs = [jax.ShapeDtypeStruct((n, bw), BF16)] + [out] * 4 + [jax.ShapeDtypeStruct((n, LORA_PAD), F32),
                                                                jax.ShapeDtypeStruct((nh * nb, bw), F32)]
    body = functools.partial(_even_front_sample_body, nb=nb, ns=ns, bw=bw)
    return pl.pallas_call(
        body, out_shape=outs, scratch_shapes=[pltpu.VMEM(((nh + ns) * nb, bw), F32)],
        compiler_params=_cparams(), name="even_front_sample")(x_in, hist, w_re, mu_re, cw, *conv_small)


def _wkv_block_terms(r, lw, km, v, kkn, ab, blk, problems):
    n, width = r.shape
    nh = width // HEAD
    ri = lax.broadcasted_iota(jnp.int32, (n, n), 0)
    ci = lax.broadcasted_iota(jnp.int32, (n, n), 1)
    same = _same_block(ri, ci, blk)
    incl = (ri >= ci) & same
    strict = (ri > ci) & same
    cum = _split3_dot(jnp.where(incl, 1.0, 0.0).astype(BF16), lw)
    tot = _split3_dot(jnp.where(same, 1.0, 0.0).astype(BF16), lw)
    dn = jnp.exp(-cum)
    at = -kkn * jnp.exp(cum - lw)
    bt = ab * dn
    kt = km * dn
    rt = r * jnp.exp(cum)
    rem = jnp.exp(tot - cum)
    bc = ab * rem
    kc = km * rem
    wc = jnp.exp(tot)
    m = n // problems
    if m != n:
        ri = lax.broadcasted_iota(jnp.int32, (m, m), 0)
        ci = lax.broadcasted_iota(jnp.int32, (m, m), 1)
        same = _same_block(ri, ci, blk)
        incl = (ri >= ci) & same
        strict = (ri > ci) & same
    chains = [(slice(p * m, (p + 1) * m), slice(h * HEAD, (h + 1) * HEAD)) for p in range(problems) for h in range(nh)]
    gs = [_dot_nt(jnp.concatenate([at[rw, sl], rt[rw, sl]], axis=0), jnp.concatenate([bt[rw, sl], kt[rw, sl]], axis=0))
          for rw, sl in chains]
    l_ab = [jnp.where(strict, g[:m, :m], 0.0) for g in gs]
    l_ak = [jnp.where(strict, g[:m, m:], 0.0) for g in gs]
    l_rb = [jnp.where(incl, g[m:, :m], 0.0) for g in gs]
    l_rk = [jnp.where(incl, g[m:, m:], 0.0) for g in gs]
    toff = _unit_lower_inverse_minus_eye(l_ab, ri, ci, blk)
    lv = [_dot(jnp.concatenate([la, lr], axis=0), v[rw, sl]) for la, lr, (rw, sl) in zip(l_ak, l_rk, chains)]
    ahat = [at[rw, sl] + _dot(t, at[rw, sl]) for t, (rw, sl) in zip(toff, chains)]
    uhat = [x[:m] + _dot(t, x[:m]) for t, x in zip(toff, lv)]
    qhat = [rt[rw, sl] + _dot(l, a) for l, a, (rw, sl) in zip(l_rb, ahat, chains)]
    ohat = [_dot(lb, u) + x[m:] for lb, u, x in zip(l_rb, uhat, lv)]
    terms = list(zip(ahat, uhat, qhat, ohat))
    return [terms[p * nh:(p + 1) * nh] for p in range(problems)], bc, kc, wc


def _same_block(ri, ci, size):
    sh = size.bit_length() - 1
    return (ri >> sh) == (ci >> sh)


def _split(x):
    hi = x.astype(BF16)
    return hi, (x - hi.astype(F32)).astype(BF16)


def _split3(x):
    hi, lo = _split(x)
    return hi, lo, (x - hi.astype(F32) - lo.astype(F32)).astype(BF16)


def _split3_dot(ones_bf16, x):
    hi, lo, lo2 = _split3(x)
    d = lambda y: jnp.dot(ones_bf16, y, preferred_element_type=F32)
    return d(hi) + (d(lo) + d(lo2))


def _split3_dot_right(x, ones_bf16):
    hi, lo, lo2 = _split3(x)
    d = lambda y: jnp.dot(y, ones_bf16, preferred_element_type=F32)
    return d(hi) + (d(lo) + d(lo2))


def _unit_lower_inverse_minus_eye(ls, ri, ci, blk):
    base = 8
    eye = jnp.where(ri == ci, 1.0, 0.0)
    same8 = _same_block(ri, ci, base)
    l8 = [jnp.where(same8, l, 0.0).astype(BF16) for l in ls]
    x2 = [_dot(x, x).astype(BF16) for x in l8]
    x4 = [_dot(x, x).astype(BF16) for x in x2]
    p = [eye + jnp.where(same8, l, 0.0) for l in ls]
    p = [q + _dot(q, x) for q, x in zip(p, x2)]
    t = [q + _dot(q, x) for q, x in zip(p, x4)]
    size = 2 * base
    while size <= blk:
        half = size.bit_length() - 2
        lower_left = _same_block(ri, ci, size) & (((ri >> half) & 1) == 1) & (((ci >> half) & 1) == 0)
        tb = [x.astype(BF16) for x in t]
        mt = [_dot(jnp.where(lower_left, l, 0.0), x) for l, x in zip(ls, tb)]
        t = [x + _dot(xb, m) for x, xb, m in zip(t, tb, mt)]
        size *= 2
    return [x - eye for x in t]


def _wkv_prompt_body(r_ref, k_ref, v_ref, lora_ref, w0_ref, w2_ref, a0_ref, a2_ref, kk_ref, ka_ref, rk_ref, s0_ref,
                     o_ref, bon_ref, sout_ref, s_scr, *, c, nch):
    i = pl.program_id(0)

    @pl.when(i == 0)
    def _():
        s_scr[...] = s0_ref[...]

    prm = (w0_ref[...], w2_ref[...], a0_ref[...], a2_ref[...], kk_ref[...], ka_ref[...], rk_ref[...])
    r, lw, km, v, kkn, ab, bonus = _even_mix(r_ref[...], k_ref[...], v_ref[...], lora_ref[...], prm, _head_ones())
    bon_ref[...] = bonus
    terms, bc, kc, wc = _wkv_block_terms(r, lw, km, v, kkn, ab, c, nch)
    nh = len(terms[0])
    hs = [slice(h * HEAD, (h + 1) * HEAD) for h in range(nh)]
    vk = [_dot_tn(v[ch * c:(ch + 1) * c, sl], kc[ch * c:(ch + 1) * c, sl]) for ch in range(nch) for sl in hs]
    s = [s_scr[h] for h in range(nh)]
    for ch in range(nch):
        rows = slice(ch * c, (ch + 1) * c)
        heads = terms[ch]
        mm = [_dot_nt(jnp.concatenate([ahat, qhat], axis=0), s[h])
              for h, (ahat, _, qhat, _) in enumerate(heads)]
        u = [mm[h][:c] + heads[h][1] for h in range(nh)]
        for h in range(nh):
            o_ref[rows, hs[h]] = mm[h][c:] + heads[h][3]
        s = [s[h] * wc[ch * c:ch * c + 1, hs[h]] + _dot_tn(u[h], bc[rows, hs[h]]) + vk[ch * nh + h]
             for h in range(nh)]
    for h in range(nh):
        s_scr[h] = s[h]

    @pl.when(i == pl.num_programs(0) - 1)
    def _():
        sout_ref[...] = s_scr[...]


def _mix_specs(bw):
    fix = lambda *_: (0, 0)
    vec = pl.BlockSpec((1, bw), fix)
    lora = pl.BlockSpec((LORA_PAD, bw), fix)
    return [vec, lora, vec, lora, vec, vec, vec]


def _wkv_prompt(r, k, v, lora, mix_small, s0, *, nch=4):
    t, bw = r.shape
    c = WKV_CHUNK
    nh = bw // HEAD
    row = pl.BlockSpec((nch * c, bw), lambda i: (i, 0))
    lrow = pl.BlockSpec((nch * c, LORA_PAD), lambda i: (i, 0))
    st = pl.BlockSpec((nh, HEAD, HEAD), lambda i: (0, 0, 0))
    out = jax.ShapeDtypeStruct((t, bw), F32)
    return pl.pallas_call(
        functools.partial(_wkv_prompt_body, c=c, nch=nch),
        out_shape=[out, out, jax.ShapeDtypeStruct((nh, HEAD, HEAD), F32)],
        grid=(t // (nch * c),), in_specs=[row] * 3 + [lrow] + _mix_specs(bw) + [st], out_specs=[row, row, st],
        scratch_shapes=[pltpu.VMEM((nh, HEAD, HEAD), F32)],
        compiler_params=_cparams("arbitrary"), name="wkv_prompt")(r, k, v, lora, *mix_small, s0)


def _wkv_sample_body(r_ref, k_ref, v_ref, lora_ref, w0_ref, w2_ref, a0_ref, a2_ref, kk_ref, ka_ref, rk_ref, s0_ref,
                     o_ref, bon_ref, sout_ref,
                     ah_scr, uh_scr, qh_scr, oh_scr, bc_scr, kc_scr, wc_scr, *, nb, blk, ns):
    prm = (w0_ref[...], w2_ref[...], a0_ref[...], a2_ref[...], kk_ref[...], ka_ref[...], rk_ref[...])
    r, lw, km, v, kkn, ab, bonus = _even_mix(r_ref[...], k_ref[...], v_ref[...], lora_ref[...], prm, _head_ones())
    real = (lax.broadcasted_iota(jnp.int32, lw.shape, 0) & (blk - 1)) < ns
    lw = jnp.where(real, lw, 0.0)
    bon_ref[...] = bonus
    (heads,), bc, kc, wc = _wkv_block_terms(r, lw, km, v, kkn, ab, blk, 1)
    bc_scr[...] = bc
    kc_scr[...] = kc
    wc_scr[...] = wc
    for h, (ahat, uhat, qhat, ohat) in enumerate(heads):
        sl = slice(h * HEAD, (h + 1) * HEAD)
        ah_scr[:, sl] = ahat
        uh_scr[:, sl] = uhat
        qh_scr[:, sl] = qhat
        oh_scr[:, sl] = ohat
    nh = len(heads)

    def per_seq(b, carry):
        rows = pl.ds(pl.multiple_of(b * blk, blk), blk)
        for h in range(nh):
            sl = slice(h * HEAD, (h + 1) * HEAD)
            s = s0_ref[b, h]
            mm = _dot_nt(jnp.concatenate([ah_scr[rows, sl], qh_scr[rows, sl]], axis=0), s)
            u = mm[:blk] + uh_scr[rows, sl]
            o_ref[rows, sl] = mm[blk:] + oh_scr[rows, sl]
            wrow = wc_scr[rows, sl][0:1, :]
            sout_ref[b, h] = s * wrow + _dot_tn(u, bc_scr[rows, sl]) + _dot_tn(v_ref[rows, sl], kc_scr[rows, sl])
        return carry

    lax.fori_loop(0, nb, per_seq, 0, unroll=4)


def _wkv_sample(r, k, v, lora, mix_small, s0, *, blk, ns):
    n, bw = r.shape
    nb = n // blk
    scr = pltpu.VMEM((n, bw), F32)
    out = jax.ShapeDtypeStruct((n, bw), F32)
    return pl.pallas_call(
        functools.partial(_wkv_sample_body, nb=nb, blk=blk, ns=ns),
        out_shape=[out, out, jax.ShapeDtypeStruct(s0.shape, F32)],
        scratch_shapes=[scr] * 7, compiler_params=_cparams(), name="wkv_sample")(r, k, v, lora, *mix_small, s0)


def _even_back_rows(o, bonus, bg, ya, x, lg, lb, w_ref, g, b, hsum, *, alpha, bw):
    om = _head_sum(o, hsum) * (1.0 / HEAD)
    oc = o - om
    ov = _head_sum(oc * oc, hsum) * (1.0 / HEAD)
    on = oc * lax.rsqrt(ov + LNX_EPS) * lg + lb
    yb = (on + bonus) * bg
    y = _dot(ya, w_ref[0:bw, :]) + _dot(yb, w_ref[bw:2 * bw, :])
    return _layer_norm(alpha * x + y, g, b, LN_EPS)


def _project_class_block(xb, w_ref, blk, cls_ref, p_scr, *, tm, gw, d, scale):
    pc = jnp.dot(xb, w_ref[:, blk * gw:(blk + 1) * gw], preferred_element_type=F32)
    if scale is not None:
        pc = pc * scale
    if d == 1:
        cls_ref[0] = pc.astype(BF16)
        return
    lt = p_scr.shape[-1]
    nlb = gw // lt
    slot = (blk % 2) * nlb
    for lb in range(nlb):
        p_scr[slot + lb] = pc[:, lb * lt:(lb + 1) * lt]
    n = tm // d
    for c in range(d):
        for lb in range(nlb):
            cls_ref[c, :, lb * lt:(lb + 1) * lt] = p_scr[slot + lb, pl.ds(c, n, stride=d), :].astype(BF16)


def _mid_body(o_ref, bon_ref, bg_ref, ya_ref, x_ref, lg_ref, lb_ref, wo_ref, g_ref, b_ref, wi_ref, x1_ref, *refs,
              tm, sub, gw, bw, dils, alpha, scale):
    ng = len(dils)
    cls_refs = refs[0:3 * ng]
    gate_ref, xb_scr, p_scr = refs[3 * ng:3 * ng + 3]
    i = pl.program_id(0)
    slot = i % 2

    @pl.when(i == 0)
    def _():
        xb_scr[1] = jnp.zeros(xb_scr.shape[1:], BF16)

    xb = xb_scr[1 - slot]
    hsum = _head_ones()
    nsub = tm // sub
    nblk = 3 * ng + 1
    for blk in range(nblk):
        if blk < 3 * ng:
            g = blk % ng
            _project_class_block(xb, wi_ref, blk, cls_refs[blk], p_scr, tm=tm, gw=gw, d=dils[g],
                                 scale=scale if blk < ng else None)
        else:
            gate_ref[...] = _silu(jnp.dot(xb, wi_ref[:, blk * gw:(blk + 1) * gw], preferred_element_type=F32))
        for s in range(nsub):
            if s * nblk // nsub != blk:
                continue
            rows = slice(s * sub, (s + 1) * sub)
            x1 = _even_back_rows(o_ref[rows, :], bon_ref[rows, :], bg_ref[rows, :], ya_ref[rows, :], x_ref[rows, :],
                                 lg_ref[...], lb_ref[...], wo_ref, g_ref[...], b_ref[...], hsum, alpha=alpha, bw=bw)
            x1_ref[rows, :] = x1
            xb_scr[slot, rows, :] = x1.astype(BF16)


def _mid(o, bonus, bg, ya, x, lnx_g, lnx_b, w_out, ln_g, ln_b, w_in, *, alpha, tm, dils, gw):
    t, d_model = x.shape
    bw = o.shape[-1]
    ng = len(dils)
    nt = t // tm
    cur = lambda i: (jnp.minimum(i, nt - 1), 0)
    prev = lambda i: (jnp.maximum(i - 1, 0), 0)
    fix = lambda i: (0, 0)
    half = pl.BlockSpec((tm, bw), cur)
    once = lambda shape: pl.BlockSpec(shape, fix, pipeline_mode=pl.Buffered(1))
    in_specs = [half] * 4 + [pl.BlockSpec((tm, d_model), cur), once((1, bw)), once((1, bw)), once((2 * bw, d_model)),
                             once((1, d_model)), once((1, d_model)), once(w_in.shape)]
    cls_specs = [pl.BlockSpec((d, tm // d, gw), lambda i: (0, jnp.maximum(i - 1, 0), 0)) for d in dils]
    cls_shapes = [jax.ShapeDtypeStruct((d, t // d, gw), BF16) for d in dils]
    outs = [jax.ShapeDtypeStruct((t, d_model), F32)] + cls_shapes * 3 + [jax.ShapeDtypeStruct((t, gw), F32)]
    out_specs = [pl.BlockSpec((tm, d_model), cur)] + cls_specs * 3 + [pl.BlockSpec((tm, gw), prev)]
    body = functools.partial(_mid_body, tm=tm, sub=min(tm, 128), gw=gw, bw=bw, dils=dils, alpha=alpha,
                             scale=HEAD ** -0.5)
    res = pl.pallas_call(
        body, out_shape=outs, grid=(nt + 1,), in_specs=in_specs, out_specs=out_specs,
        scratch_shapes=[pltpu.VMEM((2, tm, d_model), BF16), pltpu.VMEM((2 * gw // 128, tm, 128), F32)],
        compiler_params=_cparams("arbitrary"), name="mid")(o, bonus, bg, ya, x, lnx_g, lnx_b, w_out, ln_g, ln_b, w_in)
    return res[0], res[1:1 + ng], res[1 + ng:1 + 2 * ng], res[1 + 2 * ng:1 + 3 * ng], res[1 + 3 * ng]


def _kv_rows_body(x_ref, wt_ref, kv_ref):
    kv_ref[...] = lax.dot_general(wt_ref[...], x_ref[...].astype(BF16), (((1,), (1,)), ((), ())),
                                  preferred_element_type=F32)


def _kv_rows_t(x, w_kv_t, *, tm, first_row):
    t = x.shape[0] - first_row
    d_model = x.shape[1]
    ncol = w_kv_t.shape[0]
    return pl.pallas_call(
        _kv_rows_body, out_shape=jax.ShapeDtypeStruct((ncol, t), F32), grid=(t // tm,),
        in_specs=[pl.BlockSpec((tm, d_model), lambda i: (i + first_row // tm, 0)),
                  pl.BlockSpec((ncol, d_model), lambda i: (0, 0))],
        out_specs=pl.BlockSpec((ncol, tm), lambda i: (0, i)),
        compiler_params=_cparams("parallel"), name="kv_rows")(x, w_kv_t)


def _band_attn_body(q_ref, kc_ref, kp_ref, vc_ref, vp_ref, o_ref, lse_ref, *, bq, span, nb_class):
    j = pl.program_id(0)
    first_lo = jnp.where(((j * bq) % nb_class) == 0, span, 0) if nb_class >= bq else span
    qi = lax.broadcasted_iota(jnp.int32, (span, 2 * span), 0)
    ki = lax.broadcasted_iota(jnp.int32, (span, 2 * span), 1)
    in_band = (ki >= qi) & (ki <= qi + span)
    left = lax.broadcasted_iota(jnp.int32, (span, 2 * HEAD), 1) < HEAD
    gw = q_ref.shape[-1]
    for b in range(bq):
        mask = in_band & (ki >= first_lo) if b % min(nb_class, bq) == 0 else in_band
        for pr in range(gw // (2 * HEAD)):
            ls = slice(pr * 2 * HEAD, (pr + 1) * 2 * HEAD)
            q2 = q_ref[b * span:(b + 1) * span, ls]
            if b == 0:
                k2 = jnp.concatenate([kp_ref[:, ls], kc_ref[0:span, ls]], axis=0)
                v2 = jnp.concatenate([vp_ref[:, ls], vc_ref[0:span, ls]], axis=0)
            else:
                k2 = kc_ref[(b - 1) * span:(b + 1) * span, ls]
                v2 = vc_ref[(b - 1) * span:(b + 1) * span, ls]
            outs, lses = [], []
            for sel in (left, jnp.logical_not(left)):
                qh = jnp.where(sel, q2, jnp.zeros_like(q2))
                s = lax.dot_general(qh, k2, (((1,), (1,)), ((), ())), preferred_element_type=F32)
                s = jnp.where(mask, s, -jnp.inf)
                m = jnp.max(s, axis=-1, keepdims=True)
                p = jnp.exp(s - m)
                l = jnp.sum(p, axis=-1, keepdims=True)
                outs.append(jnp.dot(p.astype(BF16), v2, preferred_element_type=F32) / l)
                lses.append(m + jnp.log(l))
            o_ref[b * span:(b + 1) * span, ls] = jnp.where(left, outs[0], outs[1])
            lse_ref[b * span:(b + 1) * span, ls] = jnp.where(left, lses[0], lses[1])


def _band_attn(q, k, v, *, span, nb_class, bq=16):
    t, gw = q.shape
    cur = pl.BlockSpec((bq * span, gw), lambda j: (j, 0))
    prev = pl.BlockSpec((span, gw), lambda j: (jnp.maximum(j * bq - 1, 0), 0))
    out = jax.ShapeDtypeStruct((t, gw), F32)
    return pl.pallas_call(
        functools.partial(_band_attn_body, bq=bq, span=span, nb_class=nb_class),
        out_shape=[out, out],
        grid=(t // (bq * span),), in_specs=[cur, cur, prev, cur, prev], out_specs=[cur, cur],
        compiler_params=_cparams("parallel"), name="band_attn")(q, k, k, v, v)


def _cache_attn_group(q_ref, slab, c_ref, o_ref, lse_ref, cout_ref, *, d, window, ns, rchunk):
    gw = q_ref.shape[-1]
    w = c_ref.shape[-1]
    nh = gw // HEAD
    rp = 8
    zrow = jnp.zeros((rp - ns, gw), F32)
    q8 = jnp.concatenate([q_ref[0], zrow], axis=0)
    ncol = slab.shape[1]
    ci = lax.broadcasted_iota(jnp.int32, (ncol, rp), 0)
    si = lax.broadcasted_iota(jnp.int32, (ncol, rp), 1)
    pick = jnp.where((ci == pl.program_id(0) * ns + si) & (si < ns), 1.0, 0.0).astype(BF16)
    new_t = _split3_dot_right(slab, pick)
    kn_t = new_t[0:gw].astype(BF16)
    vn_t = new_t[gw:2 * gw].astype(BF16)
    qm = jnp.concatenate([q8] * nh, axis=0)
    rowh = lax.broadcasted_iota(jnp.int32, (nh * rp, gw), 0) >> 3
    laneh = lax.broadcasted_iota(jnp.int32, (nh * rp, gw), 1) >> 6
    own = rowh == laneh
    qm = jnp.where(own, qm, 0.0).astype(BF16)
    s_c = jnp.dot(qm, c_ref[0, 0:gw, :].astype(BF16), preferred_element_type=F32)
    s_n = jnp.dot(qm, kn_t, preferred_element_type=F32)
    srow = lax.broadcasted_iota(jnp.int32, s_c.shape, 0) & (rp - 1)
    dist = w + srow - lax.broadcasted_iota(jnp.int32, s_c.shape, 1)
    valid_c = (dist <= window) & ((dist & (d - 1)) == 0) & (srow < ns)
    srow_n = lax.broadcasted_iota(jnp.int32, s_n.shape, 0) & (rp - 1)
    col_n = lax.broadcasted_iota(jnp.int32, s_n.shape, 1)
    dist_n = srow_n - col_n
    valid_n = (dist_n >= 0) & ((dist_n & (d - 1)) == 0) & (srow_n < ns)
    valid_n = valid_n | ((srow_n >= ns) & (col_n == 0))
    s_c = jnp.where(valid_c, s_c, -jnp.inf)
    s_n = jnp.where(valid_n, s_n, -jnp.inf)
    m = jnp.maximum(jnp.max(s_c, axis=-1, keepdims=True), jnp.max(s_n, axis=-1, keepdims=True))
    p_c = jnp.exp(s_c - m)
    p_n = jnp.exp(s_n - m)
    l = jnp.sum(p_c, axis=-1, keepdims=True) + jnp.sum(p_n, axis=-1, keepdims=True)
    o = lax.dot_general(p_c.astype(BF16), c_ref[0, gw:2 * gw, :].astype(BF16), (((1,), (1,)), ((), ())),
                        preferred_element_type=F32)
    o = (o + lax.dot_general(p_n.astype(BF16), vn_t, (((1,), (1,)), ((), ())), preferred_element_type=F32)) / l
    lse = jnp.broadcast_to(m + jnp.log(l), o.shape)
    o = jnp.where(own, o, 0.0)
    lse = jnp.where(own, lse, 0.0)
    o8 = o[0:rp]
    l8 = lse[0:rp]
    for h in range(1, nh):
        o8 = o8 + o[h * rp:(h + 1) * rp]
        l8 = l8 + lse[h * rp:(h + 1) * rp]
    o_ref[0] = o8[0:ns]
    lse_ref[0] = l8[0:ns]
    for rb in range(2 * gw // rchunk):
        rows = slice(rb * rchunk, (rb + 1) * rchunk)
        cout_ref[0, rows, :] = jnp.concatenate([c_ref[0, rows, ns:w], new_t[rows, 0:ns]], axis=1)


def _cache_attn_body(*refs, groups, ns):
    ng = len(groups)
    q_refs, nt_ref, c_refs = refs[0:ng], refs[ng], refs[ng + 1:2 * ng + 1]
    outs = refs[2 * ng + 1:]
    for g, (window, d) in enumerate(groups):
        _cache_attn_group(q_refs[g], nt_ref[g], c_refs[g], outs[3 * g], outs[3 * g + 1], outs[3 * g + 2],
                          d=d, window=window, ns=ns, rchunk=64)


def _cache_attn(qs, new_t, caches_t, *, groups):
    nbat, ns, gw = qs[0].shape
    new = pl.BlockSpec((1, ns, gw), lambda b: (b, 0, 0))
    out = jax.ShapeDtypeStruct((nbat, ns, gw), F32)
    cbs = [pl.BlockSpec((1, 2 * gw, c.shape[2]), lambda b: (b, 0, 0)) for c in caches_t]
    in_specs = [new] * len(groups) + [pl.BlockSpec(new_t.shape, lambda b: (0, 0, 0))] + cbs
    out_shape, out_specs = [], []
    for c, cb in zip(caches_t, cbs):
        out_shape += [out, out, jax.ShapeDtypeStruct(c.shape, F32)]
        out_specs += [new, new, cb]
    res = pl.pallas_call(
        functools.partial(_cache_attn_body, groups=groups, ns=ns),
        out_shape=out_shape, grid=(nbat,), in_specs=in_specs, out_specs=out_specs,
        compiler_params=_cparams("parallel"), name="cache_attn")(*qs, new_t, *caches_t)
    return [res[3 * g:3 * g + 3] for g in range(len(groups))]


def _odd_back_body(*refs, tm, dils, alpha):
    ng = len(dils)
    o_refs, l_refs = refs[0:ng], refs[ng:2 * ng]
    gate_ref, x_ref, w_ref, g_ref, b_ref, y_ref = refs[2 * ng:2 * ng + 6]
    scr = refs[2 * ng + 6:]
    os_, ls_ = [], []
    for g, d in enumerate(dils):
        if d == 1:
            os_.append(o_refs[g][0])
            ls_.append(l_refs[g][0])
        else:
            n = tm // d
            nlb, _, lt = scr[2 * g].shape
            for c in range(d):
                for lb in range(nlb):
                    scr[2 * g][lb, pl.ds(c, n, stride=d), :] = o_refs[g][c, :, lb * lt:(lb + 1) * lt]
                    scr[2 * g + 1][lb, pl.ds(c, n, stride=d), :] = l_refs[g][c, :, lb * lt:(lb + 1) * lt]
            os_.append(jnp.concatenate([scr[2 * g][lb] for lb in range(nlb)], axis=1))
            ls_.append(jnp.concatenate([scr[2 * g + 1][lb] for lb in range(nlb)], axis=1))
    m = ls_[0]
    for l in ls_[1:]:
        m = jnp.maximum(m, l)
    es = [jnp.exp(l - m) for l in ls_]
    den = es[0]
    for e in es[1:]:
        den = den + e
    o = es[0] * os_[0]
    for e, og in zip(es[1:], os_[1:]):
        o = o + e * og
    o = o / den
    y = _dot(o * gate_ref[...], w_ref[...])
    y_ref[...] = _layer_norm(alpha * x_ref[...] + y, g_ref[...], b_ref[...], LN_EPS)


def _odd_back(os_, ls_, gate, x, w_out, ln_g, ln_b, *, dils, alpha, tm):
    t, d_model = x.shape
    gw = gate.shape[-1]
    row = lambda i: (i, 0)
    fix = lambda i: (0, 0)
    cls_specs = [pl.BlockSpec((d, tm // d, gw), lambda i: (0, i, 0)) for d in dils]
    in_specs = cls_specs * 2 + [pl.BlockSpec((tm, gw), row), pl.BlockSpec((tm, d_model), row),
                                pl.BlockSpec((gw, d_model), fix), pl.BlockSpec((1, d_model), fix),
                                pl.BlockSpec((1, d_model), fix)]
    return pl.pallas_call(
        functools.partial(_odd_back_body, tm=tm, dils=dils, alpha=alpha),
        out_shape=jax.ShapeDtypeStruct((t, d_model), F32), grid=(t // tm,), in_specs=in_specs,
        out_specs=pl.BlockSpec((tm, d_model), row),
        scratch_shapes=[pltpu.VMEM((gw // 128, tm, 128), F32)] * (2 * len(dils)),
        compiler_params=_cparams("parallel"), name="odd_back")(*os_, *ls_, gate, x, w_out, ln_g, ln_b)


def kernel(x_prompt, x_sample, state_conv, state_shift, state_wkv, cache_kv_w128, cache_kv_w512, cache_kv_w2048,
           w_in_even, conv_w, conv_b, conv_ln_g, conv_ln_b, mu_shift, w0, w2, a0, a2, k_k, k_a, r_k, lnx_g, lnx_b,
           w_out_even, w_in_odd, w_out_odd, ln_g, ln_b):
    nbp, seq, d_model = x_prompt.shape
    nbs, ns, _ = x_sample.shape
    assert nbp == 1
    depth = ln_g.shape[0]
    assert depth == 2 and w_in_even.shape[0] == 1 and w_in_odd.shape[0] == 1
    alpha = (2.0 * depth) ** 0.25
    bw = w0.shape[-1]
    lora_d, lora_a = w2.shape[1], a2.shape[1]
    nh = bw // HEAD
    nhist = CONV_W - 1
    caches = (cache_kv_w128, cache_kv_w512, cache_kv_w2048)
    dils = tuple(d for _, d in C_GROUPS)
    gw = w_out_odd.shape[1]
    ng = len(C_GROUPS)

    wi = w_in_even[0]
    sh_cols = 3 * bw + lora_d + lora_a
    zpad = jnp.zeros((d_model, LORA_PAD - lora_d - lora_a), F32)
    w_re = jnp.concatenate([wi[:, :sh_cols], zpad, wi[:, sh_cols:]], axis=1).astype(BF16)
    mu_re = jnp.concatenate([mu_shift[0], jnp.zeros((LORA_PAD - lora_d - lora_a,), F32)])[None, :]
    w2p = jnp.zeros((LORA_PAD, bw), F32).at[0:lora_d].set(w2[0])
    a2p = jnp.zeros((LORA_PAD, bw), F32).at[lora_d:lora_d + lora_a].set(a2[0])
    cw = jnp.repeat(conv_w[0], 8, axis=0)
    vec = lambda z: z.reshape(1, -1)
    conv_small = (vec(conv_b[0]), vec(conv_ln_g[0]), vec(conv_ln_b[0]))
    mix_small = (vec(w0[0]), w2p, vec(a0[0]), a2p, vec(k_k[0]), vec(k_a[0]), vec(r_k[0]))
    w_out_e = w_out_even[0].astype(BF16)
    w_in_o = w_in_odd[0].astype(BF16)
    w_out_o = w_out_odd[0].astype(BF16)
    lng = [vec(ln_g[l]) for l in range(depth)]
    lnb = [vec(ln_b[l]) for l in range(depth)]

    xp = x_prompt[0]
    ya, bg, r, k, v, lora, conv_tail = _even_front_prompt(xp, w_re, mu_re, cw, conv_small)
    o_wkv, bonus, wkv_p = _wkv_prompt(r, k, v, lora, mix_small, jnp.zeros((nh, HEAD, HEAD), F32))
    xp1, qc, kc, vc, gate_p = _mid(o_wkv, bonus, bg, ya, xp, vec(lnx_g[0]), vec(lnx_b[0]), w_out_e, lng[0], lnb[0],
                                   w_in_o, alpha=alpha, tm=512, dils=dils, gw=gw)
    conv_p = conv_tail[CONV_HIST - nhist:][None, None]
    shift_p = x_prompt[:, -1][None]

    xs_t = jnp.transpose(x_sample, (1, 0, 2)).reshape(ns * nbs, d_model)
    x_in = jnp.concatenate([state_shift[0], xs_t], axis=0)
    hist = jnp.transpose(state_conv[0], (1, 0, 2)).reshape(nhist * nbs, bw)
    res = _even_front_sample(x_in, hist, w_re, mu_re, cw, conv_small, nb=nbs, ns=ns)
    ya_s, bg_s, conv_s_t = res[0], res[1], res[6]
    blk = 8

    def seq_major_pad(z):
        z = jnp.transpose(z.reshape(ns, nbs, z.shape[-1]), (1, 0, 2))
        return jnp.pad(z, ((0, 0), (0, blk - ns), (0, 0))).reshape(nbs * blk, z.shape[-1])

    def step_major(z):
        return jnp.transpose(z.reshape(nbs, blk, z.shape[-1])[:, :ns], (1, 0, 2)).reshape(ns * nbs, z.shape[-1])

    wkv_in = [seq_major_pad(z) for z in res[2:6]]
    o_s_pad, bon_s_pad, wkv_s = _wkv_sample(*wkv_in, mix_small, state_wkv[0], blk=blk, ns=ns)
    o_s, bon_s = step_major(o_s_pad), step_major(bon_s_pad)
    ones = (1,) * ng
    xs1_t, qs_t, _, _, gate_s_t = _mid(o_s, bon_s, bg_s, ya_s, xs_t, vec(lnx_g[0]), vec(lnx_b[0]), w_out_e, lng[0],
                                       lnb[0], w_in_o, alpha=alpha, tm=ns * nbs, dils=ones, gw=gw)
    conv_s = jnp.transpose(conv_s_t.reshape(nhist, nbs, bw), (1, 0, 2))[None]
    shift_s = x_sample[:, -1][None]

    def seq_major(z):
        return jnp.transpose(z.reshape(ns, nbs, z.shape[-1]), (1, 0, 2)).reshape(nbs * ns, z.shape[-1])

    xs1 = seq_major(xs1_t)
    gate_s = seq_major(gate_s_t)
    qs = [seq_major(q[0]) for q in qs_t]

    tm1 = 512
    tail =min(max(win for win, _ in C_GROUPS), seq)
    w_kv_t = jnp.concatenate([w_in_odd[0][:, (part * ng + g) * gw:(part * ng + g + 1) * gw].T
                              for g in range(ng) for part in (1, 2)], axis=0).astype(BF16)
    kvt = _kv_rows_t(xp1, w_kv_t, tm=512, first_row=seq - tail).reshape(ng, 2, gw // HEAD, HEAD, tail)
    os_, ls_ = [], []
    for g, (win, d) in enumerate(C_GROUPS):
        span = win // d
        flat = lambda z: z.reshape(seq, gw)
        o_g, l_g = _band_attn(flat(qc[g]), flat(kc[g]), flat(vc[g]), span=span, nb_class=(seq // d) // span)
        os_.append(o_g.reshape(d, seq // d, gw))
        ls_.append(l_g.reshape(d, seq // d, l_g.shape[-1]))
    xp2 = _odd_back(os_, ls_, gate_p, xp1, w_out_o, lng[1], lnb[1], dils=dils, alpha=alpha, tm=tm1)
    kv_p = []
    for g, (win, d) in enumerate(C_GROUPS):
        n = min(win, seq)
        kv_p.append(jnp.transpose(kvt[g, :, :, :, tail - n:], (3, 0, 1, 2))[None, None])

    kvt_s = _kv_rows_t(xs1, w_kv_t, tm=nbs * ns, first_row=0).reshape(ng, 2 * gw, nbs * ns)
    caches_t = [jnp.transpose(c[0].reshape(nbs, c.shape[2], 2 * gw), (0, 2, 1)) for c in caches]
    q_gs = [q.astype(F32).reshape(nbs, ns, gw) for q in qs]
    os_s, ls_s, kv_s_out = [], [], []
    for o_g, l_g, cnew_t in _cache_attn(q_gs, kvt_s, caches_t, groups=C_GROUPS):
        os_s.append(o_g.reshape(1, nbs * ns, gw))
        ls_s.append(l_g.reshape(1, nbs * ns, l_g.shape[-1]))
        kv_s_out.append(jnp.transpose(cnew_t, (0, 2, 1)).reshape(nbs, cnew_t.shape[2], 2, gw // HEAD, HEAD)[None])
    xs2 = _odd_back(os_s, ls_s, gate_s, xs1, w_out_o, lng[1], lnb[1], dils=ones, alpha=alpha, tm=nbs * ns)

    return (xp2[None], xs2.reshape(nbs, ns, d_model), conv_p, conv_s, shift_p, shift_s,
            wkv_p[None, None], wkv_s[None],
            kv_p[0], kv_s_out[0], kv_p[1], kv_s_out[1], kv_p[2], kv_s_out[2])
```

```python
import functools

import jax
import jax.numpy as jnp
from jax import lax
from jax.experimental import pallas as pl
from jax.experimental.pallas import tpu as pltpu

F32 = jnp.float32
BF16 = jnp.bfloat16

LN_EPS = 1e-5
LNX_EPS = 64e-5
HEAD = 64
LORA_PAD = 128
CONV_W = 31
CONV_HIST = 32
C_GROUPS = ((128, 1), (512, 4), (2048, 16))
WKV_CHUNK = 64
TM_FRONT = 512
WKV_CHUNKS = 4
TM_MID = 512
MID_SUB = 256
BAND_BLOCKS = 16
TM_BACK = 512
TM_KV = 512
VMEM_LIMIT = 56 * 1024 * 1024


def _cparams(*sem):
    return pltpu.CompilerParams(dimension_semantics=sem or None, vmem_limit_bytes=VMEM_LIMIT)


def _dot(a, b):
    return jnp.dot(a.astype(BF16), b.astype(BF16), preferred_element_type=F32)


def _dot_nt(a, b):
    return lax.dot_general(a.astype(BF16), b.astype(BF16), (((1,), (1,)), ((), ())), preferred_element_type=F32)


def _dot_tn(a, b):
    return lax.dot_general(a.astype(BF16), b.astype(BF16), (((0,), (0,)), ((), ())), preferred_element_type=F32)


def _head_sum(a, ones_bf16):
    ah = a.astype(BF16)
    al = (a - ah.astype(F32)).astype(BF16)
    lt = ones_bf16.shape[0]
    parts = []
    for lb in range(a.shape[1] // lt):
        sl = slice(lb * lt, (lb + 1) * lt)
        parts.append(jnp.dot(ah[:, sl], ones_bf16, preferred_element_type=F32) +
                     jnp.dot(al[:, sl], ones_bf16, preferred_element_type=F32))
    return jnp.concatenate(parts, axis=1)


def _head_ones():
    n = 2 * HEAD
    r = lax.broadcasted_iota(jnp.int32, (n, n), 0) >> 6
    c = lax.broadcasted_iota(jnp.int32, (n, n), 1) >> 6
    return jnp.where(r == c, 1.0, 0.0).astype(BF16)


def _sigmoid(x):
    return 1.0 / (1.0 + jnp.exp(-x))


def _silu(x):
    return x * _sigmoid(x)


def _softplus(y):
    return jnp.maximum(y, 0.0) + jnp.log(1.0 + jnp.exp(-jnp.abs(y)))


def _layer_norm(x, g, b, eps):
    mu = jnp.mean(x, axis=-1, keepdims=True)
    xc = x - mu
    var = jnp.mean(xc * xc, axis=-1, keepdims=True)
    return xc * lax.rsqrt(var + eps) * g + b


def _even_mix(r, k, v, lora, prm, hsum):
    w0, w2p, a0, a2p, k_k, k_a, r_k = prm
    w_log = -_softplus(-(w0 + _dot(jnp.tanh(lora), w2p))) - 0.5
    lw = -jnp.exp(w_log)
    a = _sigmoid(a0 + _dot(lora, a2p))
    kk = k * k_k
    nrm = jnp.sqrt(_head_sum(kk * kk, hsum))
    kkn = kk / jnp.maximum(nrm, 1e-12)
    km = k * (1.0 + (a - 1.0) * k_a)
    bonus = _head_sum(r * km * r_k, hsum) * v
    return r, lw, km, v, kkn, kkn * a, bonus


def _tap_weight(cw_ref, j, n):
    return jnp.concatenate([cw_ref[8 * j:8 * j + 8, :]] * (n // 8), axis=0)


def _conv_ln_gate(u_ref, sh_ref, row0, n, cw_ref, cb, clg, clb, a_gate):
    acc = jnp.zeros((n, cb.shape[-1]), F32) + cb
    for j in range(CONV_W):
        a, b = divmod(CONV_HIST - (CONV_W - 1) + j, 8)
        win = u_ref[pl.ds(row0 + 8 * a, n), :] if b == 0 else sh_ref[b - 1, pl.ds(row0 + 8 * a, n), :]
        acc = acc + _tap_weight(cw_ref, j, n) * win
    return _silu(_layer_norm(acc, clg, clb, LN_EPS)) * _silu(a_gate)


def _even_front_prompt_body(x_ref, w_ref, mu_ref, cw_ref, cb_ref, clg_ref, clb_ref,
                            ya_ref, bg_ref, r_ref, k_ref, v_ref, lora_ref, conv_ref,
                            p_scr, u_scr, sh_scr, *, tm, bw, sub):
    i = pl.program_id(0)
    shw = 3 * bw + LORA_PAD

    @pl.when(i == 0)
    def _():
        p_scr[0:8, :] = jnp.zeros((8, shw), F32)
        u_scr[0:CONV_HIST, :] = jnp.zeros((CONV_HIST, bw), F32)

    xb = x_ref[...].astype(BF16)
    proj = lambda c0, c1: jnp.dot(xb, w_ref[:, c0:c1], preferred_element_type=F32)
    cur = proj(0, shw)
    p_scr[8:8 + tm, :] = cur
    prev = p_scr[pl.ds(7, tm), :]
    sh = cur + (prev - cur) * mu_ref[...]
    p_scr[7:8, :] = cur[tm - 1:tm, :]
    r_ref[...] = sh[:, 0:bw]
    k_ref[...] = sh[:, bw:2 * bw]
    v_ref[...] = sh[:, 2 * bw:3 * bw]
    lora_ref[...] = sh[:, 3 * bw:shw]

    bg_ref[...] = _silu(proj(shw + 3 * bw, shw + 4 * bw))
    u_scr[CONV_HIST:CONV_HIST + tm, :] = proj(shw, shw + bw) * _sigmoid(proj(shw + bw, shw + 2 * bw))
    a_gate = proj(shw + 2 * bw, shw + 3 * bw)
    for b in range(1, 8):
        sh_scr[b - 1] = u_scr[pl.ds(b, sh_scr.shape[1]), :]
    cb, clg, clb = cb_ref[...], clg_ref[...], clb_ref[...]
    for s in range(tm // sub):
        ya_ref[s * sub:(s + 1) * sub, :] = _conv_ln_gate(u_scr, sh_scr, s * sub, sub, cw_ref, cb, clg, clb,
                                                         a_gate[s * sub:(s + 1) * sub, :]).astype(BF16)
    tail = u_scr[tm:tm + CONV_HIST, :]
    conv_ref[...] = tail
    u_scr[0:CONV_HIST, :] = tail


def _even_front_prompt(x, w_re, mu_re, cw, conv_small, *, tm=TM_FRONT):
    t, d = x.shape
    bw = conv_small[0].shape[-1]
    shw = 3 * bw + LORA_PAD
    ncol = w_re.shape[1]
    row = lambda i: (i, 0)
    fix = lambda i: (0, 0)
    vec = pl.BlockSpec((1, bw), fix)
    in_specs = [pl.BlockSpec((tm, d), row), pl.BlockSpec((d, ncol), fix), pl.BlockSpec((1, shw), fix),
                pl.BlockSpec((8 * CONV_W, bw), fix), vec, vec, vec]
    out = jax.ShapeDtypeStruct((t, bw), F32)
    outs = [jax.ShapeDtypeStruct((t, bw), BF16)] + [out] * 4 + [jax.ShapeDtypeStruct((t, LORA_PAD), F32),
                                                                jax.ShapeDtypeStruct((CONV_HIST, bw), F32)]
    out_specs = [pl.BlockSpec((tm, bw), row)] * 5 + [pl.BlockSpec((tm, LORA_PAD), row),
                                                     pl.BlockSpec((CONV_HIST, bw), fix)]
    body = functools.partial(_even_front_prompt_body, tm=tm, bw=bw, sub=32)
    return pl.pallas_call(
        body, out_shape=outs, grid=(t // tm,), in_specs=in_specs, out_specs=out_specs,
        scratch_shapes=[pltpu.VMEM((8 + tm, shw), F32), pltpu.VMEM((CONV_HIST + tm, bw), F32),
                        pltpu.VMEM((7, CONV_HIST - 8 + tm, bw), F32)],
        compiler_params=_cparams("arbitrary"), name="even_front_prompt")(x, w_re, mu_re, cw, *conv_small)


def _even_front_sample_body(x_ref, hist_ref, w_ref, mu_ref, cw_ref, cb_ref, clg_ref, clb_ref,
                            ya_ref, bg_ref, r_ref, k_ref, v_ref, lora_ref, conv_ref,
                            u_scr, *, nb, ns, bw):
    n = nb * ns
    shw = 3 * bw + LORA_PAD
    nh = CONV_W - 1
    p = jnp.dot(x_ref[...].astype(BF16), w_ref[...], preferred_element_type=F32)
    cur = p[nb:, :shw]
    prev = p[:n, :shw]
    sh = cur + (prev - cur) * mu_ref[...]
    r_ref[...] = sh[:, 0:bw]
    k_ref[...] = sh[:, bw:2 * bw]
    v_ref[...] = sh[:, 2 * bw:3 * bw]
    lora_ref[...] = sh[:, 3 * bw:shw]
    rest = p[nb:, shw:]
    a_val, a_glu = rest[:, 0:bw], rest[:, bw:2 * bw]
    a_gate, b_gate = rest[:, 2 * bw:3 * bw], rest[:, 3 * bw:4 * bw]
    bg_ref[...] = _silu(b_gate)
    u_scr[0:nh * nb, :] = hist_ref[...]
    u_scr[nh * nb:(nh + ns) * nb, :] = a_val * _sigmoid(a_glu)
    cb, clg, clb = cb_ref[...], clg_ref[...], clb_ref[...]
    for s in range(ns):
        acc = jnp.zeros((nb, bw), F32) + cb
        for j in range(CONV_W):
            acc = acc + _tap_weight(cw_ref, j, nb) * u_scr[(s + j) * nb:(s + j + 1) * nb, :]
        ya_ref[s * nb:(s + 1) * nb, :] = (_silu(_layer_norm(acc, clg, clb, LN_EPS)) *
                                           _silu(a_gate[s * nb:(s + 1) * nb, :])).astype(BF16)
    conv_ref[...] = u_scr[ns * nb:(ns + nh) * nb, :]


def _even_front_sample(x_in, hist, w_re, mu_re, cw, conv_small, *, nb, ns):
    bw = conv_small[0].shape[-1]
    n = nb * ns
    nh = CONV_W - 1
    out = jax.ShapeDtypeStruct((n, bw), F32)
    outs = [jax.ShapeDtypeStruct((n, bw), BF16)] + [out] * 4 + [jax.ShapeDtypeStruct((n, LORA_PAD), F32),
                                                                jax.ShapeDtypeStruct((nh * nb, bw), F32)]
    body = functools.partial(_even_front_sample_body, nb=nb, ns=ns, bw=bw)
    return pl.pallas_call(
        body, out_shape=outs, scratch_shapes=[pltpu.VMEM(((nh + ns) * nb, bw), F32)],
        compiler_params=_cparams(), name="even_front_sample")(x_in, hist, w_re, mu_re, cw, *conv_small)


def _wkv_block_terms(r, lw, km, v, kkn, ab, blk, problems):
    n, width = r.shape
    nh = width // HEAD
    ri = lax.broadcasted_iota(jnp.int32, (n, n), 0)
    ci = lax.broadcasted_iota(jnp.int32, (n, n), 1)
    same = _same_block(ri, ci, blk)
    incl = (ri >= ci) & same
    strict = (ri > ci) & same
    cum = _split3_dot(jnp.where(incl, 1.0, 0.0).astype(BF16), lw)
    tot = _split3_dot(jnp.where(same, 1.0, 0.0).astype(BF16), lw)
    dn = jnp.exp(-cum)
    at = -kkn * jnp.exp(cum - lw)
    bt = ab * dn
    kt = km * dn
    rt = r * jnp.exp(cum)
    rem = jnp.exp(tot - cum)
    bc = ab * rem
    kc = km * rem
    wc = jnp.exp(tot)
    m = n // problems
    if m != n:
        ri = lax.broadcasted_iota(jnp.int32, (m, m), 0)
        ci = lax.broadcasted_iota(jnp.int32, (m, m), 1)
        same = _same_block(ri, ci, blk)
        incl = (ri >= ci) & same
        strict = (ri > ci) & same
    chains = [(slice(p * m, (p + 1) * m), slice(h * HEAD, (h + 1) * HEAD)) for p in range(problems) for h in range(nh)]
    gs = [_dot_nt(jnp.concatenate([at[rw, sl], rt[rw, sl]], axis=0), jnp.concatenate([bt[rw, sl], kt[rw, sl]], axis=0))
          for rw, sl in chains]
    l_ab = [jnp.where(strict, g[:m, :m], 0.0) for g in gs]
    l_ak = [jnp.where(strict, g[:m, m:], 0.0) for g in gs]
    l_rb = [jnp.where(incl, g[m:, :m], 0.0) for g in gs]
    l_rk = [jnp.where(incl, g[m:, m:], 0.0) for g in gs]
    toff = _unit_lower_inverse_minus_eye(l_ab, ri, ci, blk)
    lv = [_dot(jnp.concatenate([la, lr], axis=0), v[rw, sl]) for la, lr, (rw, sl) in zip(l_ak, l_rk, chains)]
    ahat = [at[rw, sl] + _dot(t, at[rw, sl]) for t, (rw, sl) in zip(toff, chains)]
    uhat = [x[:m] + _dot(t, x[:m]) for t, x in zip(toff, lv)]
    qhat = [rt[rw, sl] + _dot(l, a) for l, a, (rw, sl) in zip(l_rb, ahat, chains)]
    ohat = [_dot(lb, u) + x[m:] for lb, u, x in zip(l_rb, uhat, lv)]
    terms = list(zip(ahat, uhat, qhat, ohat))
    return [terms[p * nh:(p + 1) * nh] for p in range(problems)], bc, kc, wc


def _same_block(ri, ci, size):
    sh = size.bit_length() - 1
    return (ri >> sh) == (ci >> sh)


def _split(x):
    hi = x.astype(BF16)
    return hi, (x - hi.astype(F32)).astype(BF16)


def _split3(x):
    hi, lo = _split(x)
    return hi, lo, (x - hi.astype(F32) - lo.astype(F32)).astype(BF16)


def _split3_dot(ones_bf16, x):
    hi, lo, lo2 = _split3(x)
    d = lambda y: jnp.dot(ones_bf16, y, preferred_element_type=F32)
    return d(hi) + (d(lo) + d(lo2))


def _split3_dot_right(x, ones_bf16):
    hi, lo, lo2 = _split3(x)
    d = lambda y: jnp.dot(y, ones_bf16, preferred_element_type=F32)
    return d(hi) + (d(lo) + d(lo2))


def _unit_lower_inverse_minus_eye(ls, ri, ci, blk):
    base = 8
    eye = jnp.where(ri == ci, 1.0, 0.0)
    same8 = _same_block(ri, ci, base)
    l8 = [jnp.where(same8, l, 0.0).astype(BF16) for l in ls]
    x2 = [_dot(x, x).astype(BF16) for x in l8]
    x4 = [_dot(x, x).astype(BF16) for x in x2]
    p = [eye + jnp.where(same8, l, 0.0) for l in ls]
    p = [q + _dot(q, x) for q, x in zip(p, x2)]
    t = [q + _dot(q, x) for q, x in zip(p, x4)]
    size = 2 * base
    while size <= blk:
        half = size.bit_length() - 2
        lower_left = _same_block(ri, ci, size) & (((ri >> half) & 1) == 1) & (((ci >> half) & 1) == 0)
        tb = [x.astype(BF16) for x in t]
        mt = [_dot(jnp.where(lower_left, l, 0.0), x) for l, x in zip(ls, tb)]
        t = [x + _dot(xb, m) for x, xb, m in zip(t, tb, mt)]
        size *= 2
    return [x - eye for x in t]


def _wkv_prompt_body(r_ref, k_ref, v_ref, lora_ref, w0_ref, w2_ref, a0_ref, a2_ref, kk_ref, ka_ref, rk_ref, s0_ref,
                     o_ref, bon_ref, sout_ref, s_scr, *, c, nch):
    i = pl.program_id(0)

    @pl.when(i == 0)
    def _():
        s_scr[...] = s0_ref[...]

    prm = (w0_ref[...], w2_ref[...], a0_ref[...], a2_ref[...], kk_ref[...], ka_ref[...], rk_ref[...])
    r, lw, km, v, kkn, ab, bonus = _even_mix(r_ref[...], k_ref[...], v_ref[...], lora_ref[...], prm, _head_ones())
    bon_ref[...] = bonus
    terms, bc, kc, wc = _wkv_block_terms(r, lw, km, v, kkn, ab, c, nch)
    nh = len(terms[0])
    hs = [slice(h * HEAD, (h + 1) * HEAD) for h in range(nh)]
    vk = [_dot_tn(v[ch * c:(ch + 1) * c, sl], kc[ch * c:(ch + 1) * c, sl]) for ch in range(nch) for sl in hs]
    s = [s_scr[h] for h in range(nh)]
    for ch in range(nch):
        rows = slice(ch * c, (ch + 1) * c)
        heads = terms[ch]
        mm = [_dot_nt(jnp.concatenate([ahat, qhat], axis=0), s[h])
              for h, (ahat, _, qhat, _) in enumerate(heads)]
        u = [mm[h][:c] + heads[h][1] for h in range(nh)]
        for h in range(nh):
            o_ref[rows, hs[h]] = mm[h][c:] + heads[h][3]
        s = [s[h] * wc[ch * c:ch * c + 1, hs[h]] + _dot_tn(u[h], bc[rows, hs[h]]) + vk[ch * nh + h]
             for h in range(nh)]
    for h in range(nh):
        s_scr[h] = s[h]

    @pl.when(i == pl.num_programs(0) - 1)
    def _():
        sout_ref[...] = s_scr[...]


def _mix_specs(bw):
    fix = lambda *_: (0, 0)
    vec = pl.BlockSpec((1, bw), fix)
    lora = pl.BlockSpec((LORA_PAD, bw), fix)
    return [vec, lora, vec, lora, vec, vec, vec]


def _wkv_prompt(r, k, v, lora, mix_small, s0, *, nch=WKV_CHUNKS):
    t, bw = r.shape
    c = WKV_CHUNK
    nh = bw // HEAD
    row = pl.BlockSpec((nch * c, bw), lambda i: (i, 0))
    lrow = pl.BlockSpec((nch * c, LORA_PAD), lambda i: (i, 0))
    st = pl.BlockSpec((nh, HEAD, HEAD), lambda i: (0, 0, 0))
    out = jax.ShapeDtypeStruct((t, bw), F32)
    return pl.pallas_call(
        functools.partial(_wkv_prompt_body, c=c, nch=nch),
        out_shape=[out, out, jax.ShapeDtypeStruct((nh, HEAD, HEAD), F32)],
        grid=(t // (nch * c),), in_specs=[row] * 3 + [lrow] + _mix_specs(bw) + [st], out_specs=[row, row, st],
        scratch_shapes=[pltpu.VMEM((nh, HEAD, HEAD), F32)],
        compiler_params=_cparams("arbitrary"), name="wkv_prompt")(r, k, v, lora, *mix_small, s0)


def _wkv_sample_body(r_ref, k_ref, v_ref, lora_ref, w0_ref, w2_ref, a0_ref, a2_ref, kk_ref, ka_ref, rk_ref, s0_ref,
                     o_ref, bon_ref, sout_ref,
                     ah_scr, uh_scr, qh_scr, oh_scr, bc_scr, kc_scr, wc_scr, *, nb, blk, ns):
    prm = (w0_ref[...], w2_ref[...], a0_ref[...], a2_ref[...], kk_ref[...], ka_ref[...], rk_ref[...])
    r, lw, km, v, kkn, ab, bonus = _even_mix(r_ref[...], k_ref[...], v_ref[...], lora_ref[...], prm, _head_ones())
    real = (lax.broadcasted_iota(jnp.int32, lw.shape, 0) & (blk - 1)) < ns
    lw = jnp.where(real, lw, 0.0)
    bon_ref[...] = bonus
    (heads,), bc, kc, wc = _wkv_block_terms(r, lw, km, v, kkn, ab, blk, 1)
    bc_scr[...] = bc
    kc_scr[...] = kc
    wc_scr[...] = wc
    for h, (ahat, uhat, qhat, ohat) in enumerate(heads):
        sl = slice(h * HEAD, (h + 1) * HEAD)
        ah_scr[:, sl] = ahat
        uh_scr[:, sl] = uhat
        qh_scr[:, sl] = qhat
        oh_scr[:, sl] = ohat
    nh = len(heads)

    def per_seq(b, carry):
        rows = pl.ds(pl.multiple_of(b * blk, blk), blk)
        for h in range(nh):
            sl = slice(h * HEAD, (h + 1) * HEAD)
            s = s0_ref[b, h]
            mm = _dot_nt(jnp.concatenate([ah_scr[rows, sl], qh_scr[rows, sl]], axis=0), s)
            u = mm[:blk] + uh_scr[rows, sl]
            o_ref[rows, sl] = mm[blk:] + oh_scr[rows, sl]
            wrow = wc_scr[rows, sl][0:1, :]
            sout_ref[b, h] = s * wrow + _dot_tn(u, bc_scr[rows, sl]) + _dot_tn(v_ref[rows, sl], kc_scr[rows, sl])
        return carry

    lax.fori_loop(0, nb, per_seq, 0, unroll=4)


def _wkv_sample(r, k, v, lora, mix_small, s0, *, blk, ns):
    n, bw = r.shape
    nb = n // blk
    scr = pltpu.VMEM((n, bw), F32)
    out = jax.ShapeDtypeStruct((n, bw), F32)
    return pl.pallas_call(
        functools.partial(_wkv_sample_body, nb=nb, blk=blk, ns=ns),
        out_shape=[out, out, jax.ShapeDtypeStruct(s0.shape, F32)],
        scratch_shapes=[scr] * 7, compiler_params=_cparams(), name="wkv_sample")(r, k, v, lora, *mix_small, s0)


def _even_back_rows(o, bonus, bg, ya, x, lg, lb, w_ref, g, b, hsum, *, alpha, bw):
    om = _head_sum(o, hsum) * (1.0 / HEAD)
    oc = o - om
    ov = _head_sum(oc * oc, hsum) * (1.0 / HEAD)
    on = oc * lax.rsqrt(ov + LNX_EPS) * lg + lb
    yb = (on + bonus) * bg
    y = _dot(ya, w_ref[0:bw, :]) + _dot(yb, w_ref[bw:2 * bw, :])
    return _layer_norm(alpha * x + y, g, b, LN_EPS)


def _project_class_block(xb, w_ref, blk, cls_ref, p_scr, *, tm, gw, d, scale):
    pc = jnp.dot(xb, w_ref[:, blk * gw:(blk + 1) * gw], preferred_element_type=F32)
    if scale is not None:
        pc = pc * scale
    if d == 1:
        cls_ref[0] = pc.astype(BF16)
        return
    lt = p_scr.shape[-1]
    nlb = gw // lt
    slot = (blk % 2) * nlb
    for lb in range(nlb):
        p_scr[slot + lb] = pc[:, lb * lt:(lb + 1) * lt]
    n = tm // d
    for c in range(d):
        for lb in range(nlb):
            cls_ref[c, :, lb * lt:(lb + 1) * lt] = p_scr[slot + lb, pl.ds(c, n, stride=d), :].astype(BF16)


def _mid_body(o_ref, bon_ref, bg_ref, ya_ref, x_ref, lg_ref, lb_ref, wo_ref, g_ref, b_ref, wi_ref, x1_ref, *refs,
              tm, sub, gw, bw, dils, alpha, scale):
    ng = len(dils)
    cls_refs = refs[0:3 * ng]
    gate_ref, xb_scr, p_scr = refs[3 * ng:3 * ng + 3]
    i = pl.program_id(0)
    slot = i % 2

    @pl.when(i == 0)
    def _():
        xb_scr[1] = jnp.zeros(xb_scr.shape[1:], BF16)

    xb = xb_scr[1 - slot]
    hsum = _head_ones()
    nsub = tm // sub
    nblk = 3 * ng + 1
    for blk in range(nblk):
        if blk < 3 * ng:
            g = blk % ng
            _project_class_block(xb, wi_ref, blk, cls_refs[blk], p_scr, tm=tm, gw=gw, d=dils[g],
                                 scale=scale if blk < ng else None)
        else:
            gate_ref[...] = _silu(jnp.dot(xb, wi_ref[:, blk * gw:(blk + 1) * gw], preferred_element_type=F32))
        for s in range(nsub):
            if s * nblk // nsub != blk:
                continue
            rows = slice(s * sub, (s + 1) * sub)
            x1 = _even_back_rows(o_ref[rows, :], bon_ref[rows, :], bg_ref[rows, :], ya_ref[rows, :], x_ref[rows, :],
                                 lg_ref[...], lb_ref[...], wo_ref, g_ref[...], b_ref[...], hsum, alpha=alpha, bw=bw)
            x1_ref[rows, :] = x1
            xb_scr[slot, rows, :] = x1.astype(BF16)


def _mid(o, bonus, bg, ya, x, lnx_g, lnx_b, w_out, ln_g, ln_b, w_in, *, alpha, tm, dils, gw):
    t, d_model = x.shape
    bw = o.shape[-1]
    ng = len(dils)
    nt = t // tm
    cur = lambda i: (jnp.minimum(i, nt - 1), 0)
    prev = lambda i: (jnp.maximum(i - 1, 0), 0)
    fix = lambda i: (0, 0)
    half = pl.BlockSpec((tm, bw), cur)
    once = lambda shape: pl.BlockSpec(shape, fix, pipeline_mode=pl.Buffered(1))
    in_specs = [half] * 4 + [pl.BlockSpec((tm, d_model), cur), once((1, bw)), once((1, bw)), once((2 * bw, d_model)),
                             once((1, d_model)), once((1, d_model)), once(w_in.shape)]
    cls_specs = [pl.BlockSpec((d, tm // d, gw), lambda i: (0, jnp.maximum(i - 1, 0), 0)) for d in dils]
    cls_shapes = [jax.ShapeDtypeStruct((d, t // d, gw), BF16) for d in dils]
    outs = [jax.ShapeDtypeStruct((t, d_model), F32)] + cls_shapes * 3 + [jax.ShapeDtypeStruct((t, gw), F32)]
    out_specs = [pl.BlockSpec((tm, d_model), cur)] + cls_specs * 3 + [pl.BlockSpec((tm, gw), prev)]
    body = functools.partial(_mid_body, tm=tm, sub=min(tm, MID_SUB), gw=gw, bw=bw, dils=dils, alpha=alpha,
                             scale=HEAD ** -0.5)
    res = pl.pallas_call(
        body, out_shape=outs, grid=(nt + 1,), in_specs=in_specs, out_specs=out_specs,
        scratch_shapes=[pltpu.VMEM((2, tm, d_model), BF16), pltpu.VMEM((2 * gw // 128, tm, 128), F32)],
        compiler_params=_cparams("arbitrary"), name="mid")(o, bonus, bg, ya, x, lnx_g, lnx_b, w_out, ln_g, ln_b, w_in)
    return res[0], res[1:1 + ng], res[1 + ng:1 + 2 * ng], res[1 + 2 * ng:1 + 3 * ng], res[1 + 3 * ng]


def _kv_rows_body(x_ref, wt_ref, kv_ref):
    kv_ref[...] = lax.dot_general(wt_ref[...], x_ref[...].astype(BF16), (((1,), (1,)), ((), ())),
                                  preferred_element_type=F32)


def _kv_rows_t(x, w_kv_t, *, tm, first_row):
    t = x.shape[0] - first_row
    d_model = x.shape[1]
    ncol = w_kv_t.shape[0]
    return pl.pallas_call(
        _kv_rows_body, out_shape=jax.ShapeDtypeStruct((ncol, t), F32), grid=(t // tm,),
        in_specs=[pl.BlockSpec((tm, d_model), lambda i: (i + first_row // tm, 0)),
                  pl.BlockSpec((ncol, d_model), lambda i: (0, 0))],
        out_specs=pl.BlockSpec((ncol, tm), lambda i: (0, i)),
        compiler_params=_cparams("parallel"), name="kv_rows")(x, w_kv_t)


def _band_attn_body(q_ref, kc_ref, kp_ref, vc_ref, vp_ref, o_ref, lse_ref, *, bq, span, nb_class):
    j = pl.program_id(0)
    first_lo = jnp.where(((j * bq) % nb_class) == 0, span, 0) if nb_class >= bq else span
    qi = lax.broadcasted_iota(jnp.int32, (span, 2 * span), 0)
    ki = lax.broadcasted_iota(jnp.int32, (span, 2 * span), 1)
    in_band = (ki >= qi) & (ki <= qi + span)
    left = lax.broadcasted_iota(jnp.int32, (span, 2 * HEAD), 1) < HEAD
    gw = q_ref.shape[-1]
    for b in range(bq):
        mask = in_band & (ki >= first_lo) if b % min(nb_class, bq) == 0 else in_band
        for pr in range(gw // (2 * HEAD)):
            ls = slice(pr * 2 * HEAD, (pr + 1) * 2 * HEAD)
            q2 = q_ref[b * span:(b + 1) * span, ls]
            if b == 0:
                k2 = jnp.concatenate([kp_ref[:, ls], kc_ref[0:span, ls]], axis=0)
                v2 = jnp.concatenate([vp_ref[:, ls], vc_ref[0:span, ls]], axis=0)
            else:
                k2 = kc_ref[(b - 1) * span:(b + 1) * span, ls]
                v2 = vc_ref[(b - 1) * span:(b + 1) * span, ls]
            outs, lses = [], []
            for sel in (left, jnp.logical_not(left)):
                qh = jnp.where(sel, q2, jnp.zeros_like(q2))
                s = lax.dot_general(qh, k2, (((1,), (1,)), ((), ())), preferred_element_type=F32)
                s = jnp.where(mask, s, -jnp.inf)
                m = jnp.max(s, axis=-1, keepdims=True)
                p = jnp.exp(s - m)
                l = jnp.sum(p, axis=-1, keepdims=True)
                outs.append(jnp.dot(p.astype(BF16), v2, preferred_element_type=F32) / l)
                lses.append(m + jnp.log(l))
            o_ref[b * span:(b + 1) * span, ls] = jnp.where(left, outs[0], outs[1])
            lse_ref[b * span:(b + 1) * span, ls] = jnp.where(left, lses[0], lses[1])


def _band_attn(q, k, v, *, span, nb_class, bq=BAND_BLOCKS):
    t, gw = q.shape
    cur = pl.BlockSpec((bq * span, gw), lambda j: (j, 0))
    prev = pl.BlockSpec((span, gw), lambda j: (jnp.maximum(j * bq - 1, 0), 0))
    out = jax.ShapeDtypeStruct((t, gw), F32)
    return pl.pallas_call(
        functools.partial(_band_attn_body, bq=bq, span=span, nb_class=nb_class),
        out_shape=[out, out],
        grid=(t // (bq * span),), in_specs=[cur, cur, prev, cur, prev], out_specs=[cur, cur],
        compiler_params=_cparams("parallel"), name="band_attn")(q, k, k, v, v)


def _cache_attn_group(q_ref, slab, c_ref, o_ref, lse_ref, cout_ref, *, d, window, ns, rchunk):
    gw = q_ref.shape[-1]
    w = c_ref.shape[-1]
    nh = gw // HEAD
    rp = 8
    zrow = jnp.zeros((rp - ns, gw), F32)
    q8 = jnp.concatenate([q_ref[0], zrow], axis=0)
    ncol = slab.shape[1]
    ci = lax.broadcasted_iota(jnp.int32, (ncol, rp), 0)
    si = lax.broadcasted_iota(jnp.int32, (ncol, rp), 1)
    pick = jnp.where((ci == pl.program_id(0) * ns + si) & (si < ns), 1.0, 0.0).astype(BF16)
    new_t = _split3_dot_right(slab, pick)
    kn_t = new_t[0:gw].astype(BF16)
    vn_t = new_t[gw:2 * gw].astype(BF16)
    qm = jnp.concatenate([q8] * nh, axis=0)
    rowh = lax.broadcasted_iota(jnp.int32, (nh * rp, gw), 0) >> 3
    laneh = lax.broadcasted_iota(jnp.int32, (nh * rp, gw), 1) >> 6
    own = rowh == laneh
    qm = jnp.where(own, qm, 0.0).astype(BF16)
    s_c = jnp.dot(qm, c_ref[0, 0:gw, :].astype(BF16), preferred_element_type=F32)
    s_n = jnp.dot(qm, kn_t, preferred_element_type=F32)
    srow = lax.broadcasted_iota(jnp.int32, s_c.shape, 0) & (rp - 1)
    dist = w + srow - lax.broadcasted_iota(jnp.int32, s_c.shape, 1)
    valid_c = (dist <= window) & ((dist & (d - 1)) == 0) & (srow < ns)
    srow_n = lax.broadcasted_iota(jnp.int32, s_n.shape, 0) & (rp - 1)
    col_n = lax.broadcasted_iota(jnp.int32, s_n.shape, 1)
    dist_n = srow_n - col_n
    valid_n = (dist_n >= 0) & ((dist_n & (d - 1)) == 0) & (srow_n < ns)
    valid_n = valid_n | ((srow_n >= ns) & (col_n == 0))
    s_c = jnp.where(valid_c, s_c, -jnp.inf)
    s_n = jnp.where(valid_n, s_n, -jnp.inf)
    m = jnp.maximum(jnp.max(s_c, axis=-1, keepdims=True), jnp.max(s_n, axis=-1, keepdims=True))
    p_c = jnp.exp(s_c - m)
    p_n = jnp.exp(s_n - m)
    l = jnp.sum(p_c, axis=-1, keepdims=True) + jnp.sum(p_n, axis=-1, keepdims=True)
    o = lax.dot_general(p_c.astype(BF16), c_ref[0, gw:2 * gw, :].astype(BF16), (((1,), (1,)), ((), ())),
                        preferred_element_type=F32)
    o = (o + lax.dot_general(p_n.astype(BF16), vn_t, (((1,), (1,)), ((), ())), preferred_element_type=F32)) / l
    lse = jnp.broadcast_to(m + jnp.log(l), o.shape)
    o = jnp.where(own, o, 0.0)
    lse = jnp.where(own, lse, 0.0)
    o8 = o[0:rp]
    l8 = lse[0:rp]
    for h in range(1, nh):
        o8 = o8 + o[h * rp:(h + 1) * rp]
        l8 = l8 + lse[h * rp:(h + 1) * rp]
    o_ref[0] = o8[0:ns]
    lse_ref[0] = l8[0:ns]
    for rb in range(2 * gw // rchunk):
        rows = slice(rb * rchunk, (rb + 1) * rchunk)
        cout_ref[0, rows, :] = jnp.concatenate([c_ref[0, rows, ns:w], new_t[rows, 0:ns]], axis=1)


def _cache_attn_body(*refs, groups, ns):
    ng = len(groups)
    q_refs, nt_ref, c_refs = refs[0:ng], refs[ng], refs[ng + 1:2 * ng + 1]
    outs = refs[2 * ng + 1:]
    for g, (window, d) in enumerate(groups):
        _cache_attn_group(q_refs[g], nt_ref[g], c_refs[g], outs[3 * g], outs[3 * g + 1], outs[3 * g + 2],
                          d=d, window=window, ns=ns, rchunk=64)


def _cache_attn(qs, new_t, caches_t, *, groups):
    nbat, ns, gw = qs[0].shape
    new = pl.BlockSpec((1, ns, gw), lambda b: (b, 0, 0))
    out = jax.ShapeDtypeStruct((nbat, ns, gw), F32)
    cbs = [pl.BlockSpec((1, 2 * gw, c.shape[2]), lambda b: (b, 0, 0)) for c in caches_t]
    in_specs = [new] * len(groups) + [pl.BlockSpec(new_t.shape, lambda b: (0, 0, 0))] + cbs
    out_shape, out_specs = [], []
    for c, cb in zip(caches_t, cbs):
        out_shape += [out, out, jax.ShapeDtypeStruct(c.shape, F32)]
        out_specs += [new, new, cb]
    res = pl.pallas_call(
        functools.partial(_cache_attn_body, groups=groups, ns=ns),
        out_shape=out_shape, grid=(nbat,), in_specs=in_specs, out_specs=out_specs,
        compiler_params=_cparams("parallel"), name="cache_attn")(*qs, new_t, *caches_t)
    return [res[3 * g:3 * g + 3] for g in range(len(groups))]


def _odd_back_body(*refs, tm, dils, alpha):
    ng = len(dils)
    o_refs, l_refs = refs[0:ng], refs[ng:2 * ng]
    gate_ref, x_ref, w_ref, g_ref, b_ref, y_ref = refs[2 * ng:2 * ng + 6]
    scr = refs[2 * ng + 6:]
    os_, ls_ = [], []
    for g, d in enumerate(dils):
        if d == 1:
            os_.append(o_refs[g][0])
            ls_.append(l_refs[g][0])
        else:
            n = tm // d
            nlb, _, lt = scr[2 * g].shape
            for c in range(d):
                for lb in range(nlb):
                    scr[2 * g][lb, pl.ds(c, n, stride=d), :] = o_refs[g][c, :, lb * lt:(lb + 1) * lt]
                    scr[2 * g + 1][lb, pl.ds(c, n, stride=d), :] = l_refs[g][c, :, lb * lt:(lb + 1) * lt]
            os_.append(jnp.concatenate([scr[2 * g][lb] for lb in range(nlb)], axis=1))
            ls_.append(jnp.concatenate([scr[2 * g + 1][lb] for lb in range(nlb)], axis=1))
    m = ls_[0]
    for l in ls_[1:]:
        m = jnp.maximum(m, l)
    es = [jnp.exp(l - m) for l in ls_]
    den = es[0]
    for e in es[1:]:
        den = den + e
    o = es[0] * os_[0]
    for e, og in zip(es[1:], os_[1:]):
        o = o + e * og
    o = o / den
    y = _dot(o * gate_ref[...], w_ref[...])
    y_ref[...] = _layer_norm(alpha * x_ref[...] + y, g_ref[...], b_ref[...], LN_EPS)


def _odd_back(os_, ls_, gate, x, w_out, ln_g, ln_b, *, dils, alpha, tm):
    t, d_model = x.shape
    gw = gate.shape[-1]
    row = lambda i: (i, 0)
    fix = lambda i: (0, 0)
    cls_specs = [pl.BlockSpec((d, tm // d, gw), lambda i: (0, i, 0)) for d in dils]
    in_specs = cls_specs * 2 + [pl.BlockSpec((tm, gw), row), pl.BlockSpec((tm, d_model), row),
                                pl.BlockSpec((gw, d_model), fix), pl.BlockSpec((1, d_model), fix),
                                pl.BlockSpec((1, d_model), fix)]
    return pl.pallas_call(
        functools.partial(_odd_back_body, tm=tm, dils=dils, alpha=alpha),
        out_shape=jax.ShapeDtypeStruct((t, d_model), F32), grid=(t // tm,), in_specs=in_specs,
        out_specs=pl.BlockSpec((tm, d_model), row),
        scratch_shapes=[pltpu.VMEM((gw // 128, tm, 128), F32)] * (2 * len(dils)),
        compiler_params=_cparams("parallel"), name="odd_back")(*os_, *ls_, gate, x, w_out, ln_g, ln_b)


def kernel(x_prompt, x_sample, state_conv, state_shift, state_wkv, cache_kv_w128, cache_kv_w512, cache_kv_w2048,
           w_in_even, conv_w, conv_b, conv_ln_g, conv_ln_b, mu_shift, w0, w2, a0, a2, k_k, k_a, r_k, lnx_g, lnx_b,
           w_out_even, w_in_odd, w_out_odd, ln_g, ln_b):
    nbp, seq, d_model = x_prompt.shape
    nbs, ns, _ = x_sample.shape
    assert nbp == 1
    depth = ln_g.shape[0]
    assert depth == 2 and w_in_even.shape[0] == 1 and w_in_odd.shape[0] == 1
    alpha = (2.0 * depth) ** 0.25
    bw = w0.shape[-1]
    lora_d, lora_a = w2.shape[1], a2.shape[1]
    nh = bw // HEAD
    nhist = CONV_W - 1
    caches = (cache_kv_w128, cache_kv_w512, cache_kv_w2048)
    dils = tuple(d for _, d in C_GROUPS)
    gw = w_out_odd.shape[1]
    ng = len(C_GROUPS)

    wi = w_in_even[0]
    sh_cols = 3 * bw + lora_d + lora_a
    zpad = jnp.zeros((d_model, LORA_PAD - lora_d - lora_a), F32)
    w_re = jnp.concatenate([wi[:, :sh_cols], zpad, wi[:, sh_cols:]], axis=1).astype(BF16)
    mu_re = jnp.concatenate([mu_shift[0], jnp.zeros((LORA_PAD - lora_d - lora_a,), F32)])[None, :]
    w2p = jnp.zeros((LORA_PAD, bw), F32).at[0:lora_d].set(w2[0])
    a2p = jnp.zeros((LORA_PAD, bw), F32).at[lora_d:lora_d + lora_a].set(a2[0])
    cw = jnp.repeat(conv_w[0], 8, axis=0)
    vec = lambda z: z.reshape(1, -1)
    conv_small = (vec(conv_b[0]), vec(conv_ln_g[0]), vec(conv_ln_b[0]))
    mix_small = (vec(w0[0]), w2p, vec(a0[0]), a2p, vec(k_k[0]), vec(k_a[0]), vec(r_k[0]))
    w_out_e = w_out_even[0].astype(BF16)
    w_in_o = w_in_odd[0].astype(BF16)
    w_out_o = w_out_odd[0].astype(BF16)
    lng = [vec(ln_g[l]) for l in range(depth)]
    lnb = [vec(ln_b[l]) for l in range(depth)]

    xp = x_prompt[0]
    ya, bg, r, k, v, lora, conv_tail = _even_front_prompt(xp, w_re, mu_re, cw, conv_small)
    o_wkv, bonus, wkv_p = _wkv_prompt(r, k, v, lora, mix_small, jnp.zeros((nh, HEAD, HEAD), F32))
    xp1, qc, kc, vc, gate_p = _mid(o_wkv, bonus, bg, ya, xp, vec(lnx_g[0]), vec(lnx_b[0]), w_out_e, lng[0], lnb[0],
                                   w_in_o, alpha=alpha, tm=TM_MID, dils=dils, gw=gw)
    conv_p = conv_tail[CONV_HIST - nhist:][None, None]
    shift_p = x_prompt[:, -1][None]

    xs_t = jnp.transpose(x_sample, (1, 0, 2)).reshape(ns * nbs, d_model)
    x_in = jnp.concatenate([state_shift[0], xs_t], axis=0)
    hist = jnp.transpose(state_conv[0], (1, 0, 2)).reshape(nhist * nbs, bw)
    res = _even_front_sample(x_in, hist, w_re, mu_re, cw, conv_small, nb=nbs, ns=ns)
    ya_s, bg_s, conv_s_t = res[0], res[1], res[6]
    blk = 8

    def seq_major_pad(z):
        z = jnp.transpose(z.reshape(ns, nbs, z.shape[-1]), (1, 0, 2))
        return jnp.pad(z, ((0, 0), (0, blk - ns), (0, 0))).reshape(nbs * blk, z.shape[-1])

    def step_major(z):
        return jnp.transpose(z.reshape(nbs, blk, z.shape[-1])[:, :ns], (1, 0, 2)).reshape(ns * nbs, z.shape[-1])

    wkv_in = [seq_major_pad(z) for z in res[2:6]]
    o_s_pad, bon_s_pad, wkv_s = _wkv_sample(*wkv_in, mix_small, state_wkv[0], blk=blk, ns=ns)
    o_s, bon_s = step_major(o_s_pad), step_major(bon_s_pad)
    ones = (1,) * ng
    xs1_t, qs_t, _, _, gate_s_t = _mid(o_s, bon_s, bg_s, ya_s, xs_t, vec(lnx_g[0]), vec(lnx_b[0]), w_out_e, lng[0],
                                       lnb[0], w_in_o, alpha=alpha, tm=ns * nbs, dils=ones, gw=gw)
    conv_s = jnp.transpose(conv_s_t.reshape(nhist, nbs, bw), (1, 0, 2))[None]
    shift_s = x_sample[:, -1][None]

    def seq_major(z):
        return jnp.transpose(z.reshape(ns, nbs, z.shape[-1]), (1, 0, 2)).reshape(nbs * ns, z.shape[-1])

    xs1 = seq_major(xs1_t)
    gate_s = seq_major(gate_s_t)
    qs = [seq_major(q[0]) for q in qs_t]

    tail = min(max(win for win, _ in C_GROUPS), seq)
    w_kv_t = jnp.concatenate([w_in_odd[0][:, (part * ng + g) * gw:(part * ng + g + 1) * gw].T
                              for g in range(ng) for part in (1, 2)], axis=0).astype(BF16)
    kvt = _kv_rows_t(xp1, w_kv_t, tm=TM_KV, first_row=seq - tail).reshape(ng, 2, gw // HEAD, HEAD, tail)
    os_, ls_ = [], []
    for g, (win, d) in enumerate(C_GROUPS):
        span = win // d
        flat = lambda z: z.reshape(seq, gw)
        o_g, l_g = _band_attn(flat(qc[g]), flat(kc[g]), flat(vc[g]), span=span, nb_class=(seq // d) // span)
        os_.append(o_g.reshape(d, seq // d, gw))
        ls_.append(l_g.reshape(d, seq // d, l_g.shape[-1]))
    xp2 = _odd_back(os_, ls_, gate_p, xp1, w_out_o, lng[1], lnb[1], dils=dils, alpha=alpha, tm=TM_BACK)
    kv_p = []
    for g, (win, d) in enumerate(C_GROUPS):
        n = min(win, seq)
        kv_p.append(jnp.transpose(kvt[g, :, :, :, tail - n:], (3, 0, 1, 2))[None, None])

    kvt_s = _kv_rows_t(xs1, w_kv_t, tm=nbs * ns, first_row=0).reshape(ng, 2 * gw, nbs * ns)
    caches_t = [jnp.transpose(c[0].reshape(nbs, c.shape[2], 2 * gw), (0, 2, 1)) for c in caches]
    q_gs = [q.astype(F32).reshape(nbs, ns, gw) for q in qs]
    os_s, ls_s, kv_s_out = [], [], []
    for o_g, l_g, cnew_t in _cache_attn(q_gs, kvt_s, caches_t, groups=C_GROUPS):
        os_s.append(o_g.reshape(1, nbs * ns, gw))
        ls_s.append(l_g.reshape(1, nbs * ns, l_g.shape[-1]))
        kv_s_out.append(jnp.transpose(cnew_t, (0, 2, 1)).reshape(nbs, cnew_t.shape[2], 2, gw // HEAD, HEAD)[None])
    xs2 = _odd_back(os_s, ls_s, gate_s, xs1, w_out_o, lng[1], lnb[1], dils=ones, alpha=alpha, tm=nbs * ns)

    return (xp2[None], xs2.reshape(nbs, ns, d_model), conv_p, conv_s, shift_p, shift_s,
            wkv_p[None, None], wkv_s[None],
            kv_p[0], kv_s_out[0], kv_p[1], kv_s_out[1], kv_p[2], kv_s_out[2])
```

```python
import functools

import jax
import jax.numpy as jnp
from jax import lax
from jax.experimental import pallas as pl
from jax.experimental.pallas import tpu as pltpu

F32 = jnp.float32
BF16 = jnp.bfloat16

LN_EPS = 1e-5
LNX_EPS = 64e-5
HEAD = 64
LORA_PAD = 128
CONV_W = 31
CONV_HIST = 32
C_GROUPS = ((128, 1), (512, 4), (2048, 16))
WKV_CHUNK = 64
TM_FRONT = 512
WKV_CHUNKS = 4
TM_MID = 512
MID_SUB = 256
BAND_BLOCKS = 16
TM_BACK = 512
TM_KV = 512
VMEM_LIMIT = 56 * 1024 * 1024


def _cparams(*sem):
    return pltpu.CompilerParams(dimension_semantics=sem or None, vmem_limit_bytes=VMEM_LIMIT)


def _dot(a, b):
    return jnp.dot(a.astype(BF16), b.astype(BF16), preferred_element_type=F32)


def _dot_nt(a, b):
    return lax.dot_general(a.astype(BF16), b.astype(BF16), (((1,), (1,)), ((), ())), preferred_element_type=F32)


def _dot_tn(a, b):
    return lax.dot_general(a.astype(BF16), b.astype(BF16), (((0,), (0,)), ((), ())), preferred_element_type=F32)


def _head_sum(a, ones_bf16):
    ah = a.astype(BF16)
    al = (a - ah.astype(F32)).astype(BF16)
    lt = ones_bf16.shape[0]
    parts = []
    for lb in range(a.shape[1] // lt):
        sl = slice(lb * lt, (lb + 1) * lt)
        parts.append(jnp.dot(ah[:, sl], ones_bf16, preferred_element_type=F32) +
                     jnp.dot(al[:, sl], ones_bf16, preferred_element_type=F32))
    return jnp.concatenate(parts, axis=1)


def _head_ones():
    n = 2 * HEAD
    r = lax.broadcasted_iota(jnp.int32, (n, n), 0) >> 6
    c = lax.broadcasted_iota(jnp.int32, (n, n), 1) >> 6
    return jnp.where(r == c, 1.0, 0.0).astype(BF16)


def _sigmoid(x):
    return 1.0 / (1.0 + jnp.exp(-x))


def _silu(x):
    return x * _sigmoid(x)


def _softplus(y):
    return jnp.maximum(y, 0.0) + jnp.log(1.0 + jnp.exp(-jnp.abs(y)))


def _layer_norm(x, g, b, eps):
    mu = jnp.mean(x, axis=-1, keepdims=True)
    xc = x - mu
    var = jnp.mean(xc * xc, axis=-1, keepdims=True)
    return xc * lax.rsqrt(var + eps) * g + b


def _even_mix(r, k, v, lora, prm, hsum):
    w0, w2p, a0, a2p, k_k, k_a, r_k = prm
    w_log = -_softplus(-(w0 + _dot(jnp.tanh(lora), w2p))) - 0.5
    lw = -jnp.exp(w_log)
    a = _sigmoid(a0 + _dot(lora, a2p))
    kk = k * k_k
    nrm = jnp.sqrt(_head_sum(kk * kk, hsum))
    kkn = kk / jnp.maximum(nrm, 1e-12)
    km = k * (1.0 + (a - 1.0) * k_a)
    bonus = _head_sum(r * km * r_k, hsum) * v
    return r, lw, km, v, kkn, kkn * a, bonus


def _tap_weight(cw_ref, j, n):
    return jnp.concatenate([cw_ref[8 * j:8 * j + 8, :]] * (n // 8), axis=0)


def _conv_ln_gate(u_ref, sh_ref, row0, n, cw_ref, cb, clg, clb, a_gate):
    acc = jnp.zeros((n, cb.shape[-1]), F32) + cb
    for j in range(CONV_W):
        a, b = divmod(CONV_HIST - (CONV_W - 1) + j, 8)
        win = u_ref[pl.ds(row0 + 8 * a, n), :] if b == 0 else sh_ref[b - 1, pl.ds(row0 + 8 * a, n), :]
        acc = acc + _tap_weight(cw_ref, j, n) * win
    return _silu(_layer_norm(acc, clg, clb, LN_EPS)) * _silu(a_gate)


def _even_front_prompt_body(x_ref, w_ref, mu_ref, cw_ref, cb_ref, clg_ref, clb_ref,
                            ya_ref, bg_ref, r_ref, k_ref, v_ref, lora_ref, conv_ref,
                            p_scr, u_scr, sh_scr, *, tm, bw, sub):
    i = pl.program_id(0)
    shw = 3 * bw + LORA_PAD

    @pl.when(i == 0)
    def _():
        p_scr[0:8, :] = jnp.zeros((8, shw), F32)
        u_scr[0:CONV_HIST, :] = jnp.zeros((CONV_HIST, bw), F32)

    xb = x_ref[...].astype(BF16)
    proj = lambda c0, c1: jnp.dot(xb, w_ref[:, c0:c1], preferred_element_type=F32)
    cur = proj(0, shw)
    p_scr[8:8 + tm, :] = cur
    prev = p_scr[pl.ds(7, tm), :]
    sh = cur + (prev - cur) * mu_ref[...]
    p_scr[7:8, :] = cur[tm - 1:tm, :]
    r_ref[...] = sh[:, 0:bw]
    k_ref[...] = sh[:, bw:2 * bw]
    v_ref[...] = sh[:, 2 * bw:3 * bw]
    lora_ref[...] = sh[:, 3 * bw:shw]

    bg_ref[...] = _silu(proj(shw + 3 * bw, shw + 4 * bw))
    u_scr[CONV_HIST:CONV_HIST + tm, :] = proj(shw, shw + bw) * _sigmoid(proj(shw + bw, shw + 2 * bw))
    a_gate = proj(shw + 2 * bw, shw + 3 * bw)
    for b in range(1, 8):
        sh_scr[b - 1] = u_scr[pl.ds(b, sh_scr.shape[1]), :]
    cb, clg, clb = cb_ref[...], clg_ref[...], clb_ref[...]
    for s in range(tm // sub):
        ya_ref[s * sub:(s + 1) * sub, :] = _conv_ln_gate(u_scr, sh_scr, s * sub, sub, cw_ref, cb, clg, clb,
                                                         a_gate[s * sub:(s + 1) * sub, :]).astype(BF16)
    tail = u_scr[tm:tm + CONV_HIST, :]
    conv_ref[...] = tail
    u_scr[0:CONV_HIST, :] = tail


def _even_front_prompt(x, w_re, mu_re, cw, conv_small, *, tm=TM_FRONT):
    t, d = x.shape
    bw = conv_small[0].shape[-1]
    shw = 3 * bw + LORA_PAD
    ncol = w_re.shape[1]
    row = lambda i: (i, 0)
    fix = lambda i: (0, 0)
    vec = pl.BlockSpec((1, bw), fix)
    in_specs = [pl.BlockSpec((tm, d), row), pl.BlockSpec((d, ncol), fix), pl.BlockSpec((1, shw), fix),
                pl.BlockSpec((8 * CONV_W, bw), fix), vec, vec, vec]
    out = jax.ShapeDtypeStruct((t, bw), F32)
    outs = [jax.ShapeDtypeStruct((t, bw), BF16)] + [out] * 4 + [jax.ShapeDtypeStruct((t, LORA_PAD), F32),
                                                                jax.ShapeDtypeStruct((CONV_HIST, bw), F32)]
    out_specs = [pl.BlockSpec((tm, bw), row)] * 5 + [pl.BlockSpec((tm, LORA_PAD), row),
                                                     pl.BlockSpec((CONV_HIST, bw), fix)]
    body = functools.partial(_even_front_prompt_body, tm=tm, bw=bw, sub=32)
    return pl.pallas_call(
        body, out_shape=outs, grid=(t // tm,), in_specs=in_specs, out_specs=out_specs,
        scratch_shapes=[pltpu.VMEM((8 + tm, shw), F32), pltpu.VMEM((CONV_HIST + tm, bw), F32),
                        pltpu.VMEM((7, CONV_HIST - 8 + tm, bw), F32)],
        compiler_params=_cparams("arbitrary"), name="even_front_prompt")(x, w_re, mu_re, cw, *conv_small)


def _even_front_sample_body(x_ref, hist_ref, w_ref, mu_ref, cw_ref, cb_ref, clg_ref, clb_ref,
                            ya_ref, bg_ref, r_ref, k_ref, v_ref, lora_ref, conv_ref,
                            u_scr, *, nb, ns, bw):
    n = nb * ns
    shw = 3 * bw + LORA_PAD
    nh = CONV_W - 1
    p = jnp.dot(x_ref[...].astype(BF16), w_ref[...], preferred_element_type=F32)
    cur = p[nb:, :shw]
    prev = p[:n, :shw]
    sh = cur + (prev - cur) * mu_ref[...]
    r_ref[...] = sh[:, 0:bw]
    k_ref[...] = sh[:, bw:2 * bw]
    v_ref[...] = sh[:, 2 * bw:3 * bw]
    lora_ref[...] = sh[:, 3 * bw:shw]
    rest = p[nb:, shw:]
    a_val, a_glu = rest[:, 0:bw], rest[:, bw:2 * bw]
    a_gate, b_gate = rest[:, 2 * bw:3 * bw], rest[:, 3 * bw:4 * bw]
    bg_ref[...] = _silu(b_gate)
    u_scr[0:nh * nb, :] = hist_ref[...]
    u_scr[nh * nb:(nh + ns) * nb, :] = a_val * _sigmoid(a_glu)
    cb, clg, clb = cb_ref[...], clg_ref[...], clb_ref[...]
    for s in range(ns):
        acc = jnp.zeros((nb, bw), F32) + cb
        for j in range(CONV_W):
            acc = acc + _tap_weight(cw_ref, j, nb) * u_scr[(s + j) * nb:(s + j + 1) * nb, :]
        ya_ref[s * nb:(s + 1) * nb, :] = (_silu(_layer_norm(acc, clg, clb, LN_EPS)) *
                                           _silu(a_gate[s * nb:(s + 1) * nb, :])).astype(BF16)
    conv_ref[...] = u_scr[ns * nb:(ns + nh) * nb, :]


def _even_front_sample(x_in, hist, w_re, mu_re, cw, conv_small, *, nb, ns):
    bw = conv_small[0].shape[-1]
    n = nb * ns
    nh = CONV_W - 1
    out = jax.ShapeDtypeStruct((n, bw), F32)
    outs = [jax.ShapeDtypeStruct((n, bw), BF16)] + [out] * 4 + [jax.ShapeDtypeStruct((n, LORA_PAD), F32),
                                                                jax.ShapeDtypeStruct((nh * nb, bw), F32)]
    body = functools.partial(_even_front_sample_body, nb=nb, ns=ns, bw=bw)
    return pl.pallas_call(
        body, out_shape=outs, scratch_shapes=[pltpu.VMEM(((nh + ns) * nb, bw), F32)],
        compiler_params=_cparams(), name="even_front_sample")(x_in, hist, w_re, mu_re, cw, *conv_small)


def _wkv_block_terms(r, lw, km, v, kkn, ab, blk, problems):
    n, width = r.shape
    nh = width // HEAD
    ri = lax.broadcasted_iota(jnp.int32, (n, n), 0)
    ci = lax.broadcasted_iota(jnp.int32, (n, n), 1)
    same = _same_block(ri, ci, blk)
    incl = (ri >= ci) & same
    strict = (ri > ci) & same
    if problems > 1:
        tri = jnp.where(lax.broadcasted_iota(jnp.int32, (blk, blk), 0) >= lax.broadcasted_iota(jnp.int32, (blk, blk), 1),
                        1.0, 0.0).astype(BF16)
        cums = [_split3_dot(tri, lw[b * blk:(b + 1) * blk]) for b in range(n // blk)]
        cum = jnp.concatenate(cums, axis=0)
        tot = jnp.concatenate([jnp.broadcast_to(c[blk - 1:blk], c.shape) for c in cums], axis=0)
    else:
        cum = _split3_dot(jnp.where(incl, 1.0, 0.0).astype(BF16), lw)
        tot = _split3_dot(jnp.where(same, 1.0, 0.0).astype(BF16), lw)
    dn = jnp.exp(-cum)
    at = -kkn * jnp.exp(cum - lw)
    bt = ab * dn
    kt = km * dn
    rt = r * jnp.exp(cum)
    rem = jnp.exp(tot - cum)
    bc = ab * rem
    kc = km * rem
    wc = jnp.exp(tot)
    m = n // problems
    if m != n:
        ri = lax.broadcasted_iota(jnp.int32, (m, m), 0)
        ci = lax.broadcasted_iota(jnp.int32, (m, m), 1)
        same = _same_block(ri, ci, blk)
        incl = (ri >= ci) & same
        strict = (ri > ci) & same
    chains = [(slice(p * m, (p + 1) * m), slice(h * HEAD, (h + 1) * HEAD)) for p in range(problems) for h in range(nh)]
    gs = [_dot_nt(jnp.concatenate([at[rw, sl], rt[rw, sl]], axis=0), jnp.concatenate([bt[rw, sl], kt[rw, sl]], axis=0))
          for rw, sl in chains]
    l_ab = [jnp.where(strict, g[:m, :m], 0.0) for g in gs]
    l_ak = [jnp.where(strict, g[:m, m:], 0.0) for g in gs]
    l_rb = [jnp.where(incl, g[m:, :m], 0.0) for g in gs]
    l_rk = [jnp.where(incl, g[m:, m:], 0.0) for g in gs]
    toff = _unit_lower_inverse_minus_eye(l_ab, ri, ci, blk)
    lv = [_dot(jnp.concatenate([la, lr], axis=0), v[rw, sl]) for la, lr, (rw, sl) in zip(l_ak, l_rk, chains)]
    ahat = [at[rw, sl] + _dot(t, at[rw, sl]) for t, (rw, sl) in zip(toff, chains)]
    uhat = [x[:m] + _dot(t, x[:m]) for t, x in zip(toff, lv)]
    qhat = [rt[rw, sl] + _dot(l, a) for l, a, (rw, sl) in zip(l_rb, ahat, chains)]
    ohat = [_dot(lb, u) + x[m:] for lb, u, x in zip(l_rb, uhat, lv)]
    terms = list(zip(ahat, uhat, qhat, ohat))
    return [terms[p * nh:(p + 1) * nh] for p in range(problems)], bc, kc, wc


def _same_block(ri, ci, size):
    sh = size.bit_length() - 1
    return (ri >> sh) == (ci >> sh)


def _split(x):
    hi = x.astype(BF16)
    return hi, (x - hi.astype(F32)).astype(BF16)


def _split3(x):
    hi, lo = _split(x)
    return hi, lo, (x - hi.astype(F32) - lo.astype(F32)).astype(BF16)


def _split3_dot(ones_bf16, x):
    hi, lo, lo2 = _split3(x)
    d = lambda y: jnp.dot(ones_bf16, y, preferred_element_type=F32)
    return d(hi) + (d(lo) + d(lo2))


def _split3_dot_right(x, ones_bf16):
    hi, lo, lo2 = _split3(x)
    d = lambda y: jnp.dot(y, ones_bf16, preferred_element_type=F32)
    return d(hi) + (d(lo) + d(lo2))


def _unit_lower_inverse_minus_eye(ls, ri, ci, blk):
    base = 8
    eye = jnp.where(ri == ci, 1.0, 0.0)
    same8 = _same_block(ri, ci, base)
    l8 = [jnp.where(same8, l, 0.0).astype(BF16) for l in ls]
    x2 = [_dot(x, x).astype(BF16) for x in l8]
    x4 = [_dot(x, x).astype(BF16) for x in x2]
    p = [eye + jnp.where(same8, l, 0.0) for l in ls]
    p = [q + _dot(q, x) for q, x in zip(p, x2)]
    t = [q + _dot(q, x) for q, x in zip(p, x4)]
    size = 2 * base
    while size <= blk:
        half = size.bit_length() - 2
        lower_left = _same_block(ri, ci, size) & (((ri >> half) & 1) == 1) & (((ci >> half) & 1) == 0)
        tb = [x.astype(BF16) for x in t]
        mt = [_dot(jnp.where(lower_left, l, 0.0), x) for l, x in zip(ls, tb)]
        t = [x + _dot(xb, m) for x, xb, m in zip(t, tb, mt)]
        size *= 2
    return [x - eye for x in t]


def _wkv_prompt_body(r_ref, k_ref, v_ref, lora_ref, w0_ref, w2_ref, a0_ref, a2_ref, kk_ref, ka_ref, rk_ref, s0_ref,
                     o_ref, bon_ref, sout_ref, s_scr, *, c, nch):
    i = pl.program_id(0)

    @pl.when(i == 0)
    def _():
        s_scr[...] = s0_ref[...]

    prm = (w0_ref[...], w2_ref[...], a0_ref[...], a2_ref[...], kk_ref[...], ka_ref[...], rk_ref[...])
    r, lw, km, v, kkn, ab, bonus = _even_mix(r_ref[...], k_ref[...], v_ref[...], lora_ref[...], prm, _head_ones())
    bon_ref[...] = bonus
    terms, bc, kc, wc = _wkv_block_terms(r, lw, km, v, kkn, ab, c, nch)
    nh = len(terms[0])
    hs = [slice(h * HEAD, (h + 1) * HEAD) for h in range(nh)]
    vk = [_dot_tn(v[ch * c:(ch + 1) * c, sl], kc[ch * c:(ch + 1) * c, sl]) for ch in range(nch) for sl in hs]
    s = [s_scr[h] for h in range(nh)]
    for ch in range(nch):
        rows = slice(ch * c, (ch + 1) * c)
        heads = terms[ch]
        mm = [_dot_nt(jnp.concatenate([ahat, qhat], axis=0), s[h])
              for h, (ahat, _, qhat, _) in enumerate(heads)]
        u = [mm[h][:c] + heads[h][1] for h in range(nh)]
        for h in range(nh):
            o_ref[rows, hs[h]] = mm[h][c:] + heads[h][3]
        s = [s[h] * wc[ch * c:ch * c + 1, hs[h]] + _dot_tn(u[h], bc[rows, hs[h]]) + vk[ch * nh + h]
             for h in range(nh)]
    for h in range(nh):
        s_scr[h] = s[h]

    @pl.when(i == pl.num_programs(0) - 1)
    def _():
        sout_ref[...] = s_scr[...]


def _mix_specs(bw):
    fix = lambda *_: (0, 0)
    vec = pl.BlockSpec((1, bw), fix)
    lora = pl.BlockSpec((LORA_PAD, bw), fix)
    return [vec, lora, vec, lora, vec, vec, vec]


def _wkv_prompt(r, k, v, lora, mix_small, s0, *, nch=WKV_CHUNKS):
    t, bw = r.shape
    c = WKV_CHUNK
    nh = bw // HEAD
    row = pl.BlockSpec((nch * c, bw), lambda i: (i, 0))
    lrow = pl.BlockSpec((nch * c, LORA_PAD), lambda i: (i, 0))
    st = pl.BlockSpec((nh, HEAD, HEAD), lambda i: (0, 0, 0))
    out = jax.ShapeDtypeStruct((t, bw), F32)
    return pl.pallas_call(
        functools.partial(_wkv_prompt_body, c=c, nch=nch),
        out_shape=[out, out, jax.ShapeDtypeStruct((nh, HEAD, HEAD), F32)],
        grid=(t // (nch * c),), in_specs=[row] * 3 + [lrow] + _mix_specs(bw) + [st], out_specs=[row, row, st],
        scratch_shapes=[pltpu.VMEM((nh, HEAD, HEAD), F32)],
        compiler_params=_cparams("arbitrary"), name="wkv_prompt")(r, k, v, lora, *mix_small, s0)


def _wkv_sample_body(r_ref, k_ref, v_ref, lora_ref, w0_ref, w2_ref, a0_ref, a2_ref, kk_ref, ka_ref, rk_ref, s0_ref,
                     o_ref, bon_ref, sout_ref,
                     ah_scr, uh_scr, qh_scr, oh_scr, bc_scr, kc_scr, wc_scr, *, nb, blk, ns):
    prm = (w0_ref[...], w2_ref[...], a0_ref[...], a2_ref[...], kk_ref[...], ka_ref[...], rk_ref[...])
    r, lw, km, v, kkn, ab, bonus = _even_mix(r_ref[...], k_ref[...], v_ref[...], lora_ref[...], prm, _head_ones())
    real = (lax.broadcasted_iota(jnp.int32, lw.shape, 0) & (blk - 1)) < ns
    lw = jnp.where(real, lw, 0.0)
    bon_ref[...] = bonus
    (heads,), bc, kc, wc = _wkv_block_terms(r, lw, km, v, kkn, ab, blk, 1)
    bc_scr[...] = bc
    kc_scr[...] = kc
    wc_scr[...] = wc
    for h, (ahat, uhat, qhat, ohat) in enumerate(heads):
        sl = slice(h * HEAD, (h + 1) * HEAD)
        ah_scr[:, sl] = ahat
        uh_scr[:, sl] = uhat
        qh_scr[:, sl] = qhat
        oh_scr[:, sl] = ohat
    nh = len(heads)

    def per_seq(b, carry):
        rows = pl.ds(pl.multiple_of(b * blk, blk), blk)
        for h in range(nh):
            sl = slice(h * HEAD, (h + 1) * HEAD)
            s = s0_ref[b, h]
            mm = _dot_nt(jnp.concatenate([ah_scr[rows, sl], qh_scr[rows, sl]], axis=0), s)
            u = mm[:blk] + uh_scr[rows, sl]
            o_ref[rows, sl] = mm[blk:] + oh_scr[rows, sl]
            wrow = wc_scr[rows, sl][0:1, :]
            sout_ref[b, h] = s * wrow + _dot_tn(u, bc_scr[rows, sl]) + _dot_tn(v_ref[rows, sl], kc_scr[rows, sl])
        return carry

    lax.fori_loop(0, nb, per_seq, 0, unroll=4)


def _wkv_sample(r, k, v, lora, mix_small, s0, *, blk, ns):
    n, bw = r.shape
    nb = n // blk
    scr = pltpu.VMEM((n, bw), F32)
    out = jax.ShapeDtypeStruct((n, bw), F32)
    return pl.pallas_call(
        functools.partial(_wkv_sample_body, nb=nb, blk=blk, ns=ns),
        out_shape=[out, out, jax.ShapeDtypeStruct(s0.shape, F32)],
        scratch_shapes=[scr] * 7, compiler_params=_cparams(), name="wkv_sample")(r, k, v, lora, *mix_small, s0)


def _even_back_rows(o, bonus, bg, ya, x, lg, lb, w_ref, g, b, hsum, *, alpha, bw):
    om = _head_sum(o, hsum) * (1.0 / HEAD)
    oc = o - om
    ov = _head_sum(oc * oc, hsum) * (1.0 / HEAD)
    on = oc * lax.rsqrt(ov + LNX_EPS) * lg + lb
    yb = (on + bonus) * bg
    y = _dot(ya, w_ref[0:bw, :]) + _dot(yb, w_ref[bw:2 * bw, :])
    return _layer_norm(alpha * x + y, g, b, LN_EPS)


def _project_class_block(xb, w_ref, blk, cls_ref, p_scr, *, tm, gw, d, scale):
    pc = jnp.dot(xb, w_ref[:, blk * gw:(blk + 1) * gw], preferred_element_type=F32)
    if scale is not None:
        pc = pc * scale
    if d == 1:
        cls_ref[0] = pc.astype(BF16)
        return
    lt = p_scr.shape[-1]
    nlb = gw // lt
    slot = (blk % 2) * nlb
    for lb in range(nlb):
        p_scr[slot + lb] = pc[:, lb * lt:(lb + 1) * lt]
    n = tm // d
    for c in range(d):
        for lb in range(nlb):
            cls_ref[c, :, lb * lt:(lb + 1) * lt] = p_scr[slot + lb, pl.ds(c, n, stride=d), :].astype(BF16)


def _mid_body(o_ref, bon_ref, bg_ref, ya_ref, x_ref, lg_ref, lb_ref, wo_ref, g_ref, b_ref, wi_ref, x1_ref, *refs,
              tm, sub, gw, bw, dils, alpha, scale):
    ng = len(dils)
    cls_refs = refs[0:3 * ng]
    gate_ref, xb_scr, p_scr = refs[3 * ng:3 * ng + 3]
    i = pl.program_id(0)
    slot = i % 2

    @pl.when(i == 0)
    def _():
        xb_scr[1] = jnp.zeros(xb_scr.shape[1:], BF16)

    xb = xb_scr[1 - slot]
    hsum = _head_ones()
    nsub = tm // sub
    nblk = 3 * ng + 1
    for blk in range(nblk):
        if blk < 3 * ng:
            g = blk % ng
            _project_class_block(xb, wi_ref, blk, cls_refs[blk], p_scr, tm=tm, gw=gw, d=dils[g],
                                 scale=scale if blk < ng else None)
        else:
            gate_ref[...] = _silu(jnp.dot(xb, wi_ref[:, blk * gw:(blk + 1) * gw], preferred_element_type=F32))
        for s in range(nsub):
            if s * nblk // nsub != blk:
                continue
            rows = slice(s * sub, (s + 1) * sub)
            x1 = _even_back_rows(o_ref[rows, :], bon_ref[rows, :], bg_ref[rows, :], ya_ref[rows, :], x_ref[rows, :],
                                 lg_ref[...], lb_ref[...], wo_ref, g_ref[...], b_ref[...], hsum, alpha=alpha, bw=bw)
            x1_ref[rows, :] = x1
            xb_scr[slot, rows, :] = x1.astype(BF16)


def _mid(o, bonus, bg, ya, x, lnx_g, lnx_b, w_out, ln_g, ln_b, w_in, *, alpha, tm, dils, gw):
    t, d_model = x.shape
    bw = o.shape[-1]
    ng = len(dils)
    nt = t // tm
    cur = lambda i: (jnp.minimum(i, nt - 1), 0)
    prev = lambda i: (jnp.maximum(i - 1, 0), 0)
    fix = lambda i: (0, 0)
    half = pl.BlockSpec((tm, bw), cur)
    once = lambda shape: pl.BlockSpec(shape, fix, pipeline_mode=pl.Buffered(1))
    in_specs = [half] * 4 + [pl.BlockSpec((tm, d_model), cur), once((1, bw)), once((1, bw)), once((2 * bw, d_model)),
                             once((1, d_model)), once((1, d_model)), once(w_in.shape)]
    cls_specs = [pl.BlockSpec((d, tm // d, gw), lambda i: (0, jnp.maximum(i - 1, 0), 0)) for d in dils]
    cls_shapes = [jax.ShapeDtypeStruct((d, t // d, gw), BF16) for d in dils]
    outs = [jax.ShapeDtypeStruct((t, d_model), F32)] + cls_shapes * 3 + [jax.ShapeDtypeStruct((t, gw), F32)]
    out_specs = [pl.BlockSpec((tm, d_model), cur)] + cls_specs * 3 + [pl.BlockSpec((tm, gw), prev)]
    body = functools.partial(_mid_body, tm=tm, sub=min(tm, MID_SUB), gw=gw, bw=bw, dils=dils, alpha=alpha,
                             scale=HEAD ** -0.5)
    res = pl.pallas_call(
        body, out_shape=outs, grid=(nt + 1,), in_specs=in_specs, out_specs=out_specs,
        scratch_shapes=[pltpu.VMEM((2, tm, d_model), BF16), pltpu.VMEM((2 * gw // 128, tm, 128), F32)],
        compiler_params=_cparams("arbitrary"), name="mid")(o, bonus, bg, ya, x, lnx_g, lnx_b, w_out, ln_g, ln_b, w_in)
    return res[0], res[1:1 + ng], res[1 + ng:1 + 2 * ng], res[1 + 2 * ng:1 + 3 * ng], res[1 + 3 * ng]


def _kv_rows_body(x_ref, wt_ref, kv_ref):
    kv_ref[...] = lax.dot_general(wt_ref[...], x_ref[...].astype(BF16), (((1,), (1,)), ((), ())),
                                  preferred_element_type=F32)


def _kv_rows_t(x, w_kv_t, *, tm, first_row):
    t = x.shape[0] - first_row
    d_model = x.shape[1]
    ncol = w_kv_t.shape[0]
    return pl.pallas_call(
        _kv_rows_body, out_shape=jax.ShapeDtypeStruct((ncol, t), F32), grid=(t // tm,),
        in_specs=[pl.BlockSpec((tm, d_model), lambda i: (i + first_row // tm, 0)),
                  pl.BlockSpec((ncol, d_model), lambda i: (0, 0))],
        out_specs=pl.BlockSpec((ncol, tm), lambda i: (0, i)),
        compiler_params=_cparams("parallel"), name="kv_rows")(x, w_kv_t)


def _band_attn_body(q_ref, kc_ref, kp_ref, vc_ref, vp_ref, o_ref, lse_ref, *, bq, span, nb_class):
    j = pl.program_id(0)
    first_lo = jnp.where(((j * bq) % nb_class) == 0, span, 0) if nb_class >= bq else span
    qi = lax.broadcasted_iota(jnp.int32, (span, 2 * span), 0)
    ki = lax.broadcasted_iota(jnp.int32, (span, 2 * span), 1)
    in_band = (ki >= qi) & (ki <= qi + span)
    left = lax.broadcasted_iota(jnp.int32, (span, 2 * HEAD), 1) < HEAD
    gw = q_ref.shape[-1]
    for b in range(bq):
        mask = in_band & (ki >= first_lo) if b % min(nb_class, bq) == 0 else in_band
        for pr in range(gw // (2 * HEAD)):
            ls = slice(pr * 2 * HEAD, (pr + 1) * 2 * HEAD)
            q2 = q_ref[b * span:(b + 1) * span, ls]
            if b == 0:
                k2 = jnp.concatenate([kp_ref[:, ls], kc_ref[0:span, ls]], axis=0)
                v2 = jnp.concatenate([vp_ref[:, ls], vc_ref[0:span, ls]], axis=0)
            else:
                k2 = kc_ref[(b - 1) * span:(b + 1) * span, ls]
                v2 = vc_ref[(b - 1) * span:(b + 1) * span, ls]
            outs, lses = [], []
            for sel in (left, jnp.logical_not(left)):
                qh = jnp.where(sel, q2, jnp.zeros_like(q2))
                s = lax.dot_general(qh, k2, (((1,), (1,)), ((), ())), preferred_element_type=F32)
                s = jnp.where(mask, s, -jnp.inf)
                m = jnp.max(s, axis=-1, keepdims=True)
                p = jnp.exp(s - m)
                l = jnp.sum(p, axis=-1, keepdims=True)
                outs.append(jnp.dot(p.astype(BF16), v2, preferred_element_type=F32) / l)
                lses.append(m + jnp.log(l))
            o_ref[b * span:(b + 1) * span, ls] = jnp.where(left, outs[0], outs[1])
            lse_ref[b * span:(b + 1) * span, ls] = jnp.where(left, lses[0], lses[1])


def _band_attn(q, k, v, *, span, nb_class, bq=BAND_BLOCKS):
    t, gw = q.shape
    cur = pl.BlockSpec((bq * span, gw), lambda j: (j, 0))
    prev = pl.BlockSpec((span, gw), lambda j: (jnp.maximum(j * bq - 1, 0), 0))
    out = jax.ShapeDtypeStruct((t, gw), F32)
    return pl.pallas_call(
        functools.partial(_band_attn_body, bq=bq, span=span, nb_class=nb_class),
        out_shape=[out, out],
        grid=(t // (bq * span),), in_specs=[cur, cur, prev, cur, prev], out_specs=[cur, cur],
        compiler_params=_cparams("parallel"), name="band_attn")(q, k, k, v, v)


def _cache_attn_group(q_ref, slab, c_ref, o_ref, lse_ref, cout_ref, *, d, window, ns, rchunk):
    gw = q_ref.shape[-1]
    w = c_ref.shape[-1]
    nh = gw // HEAD
    rp = 8
    zrow = jnp.zeros((rp - ns, gw), F32)
    q8 = jnp.concatenate([q_ref[0], zrow], axis=0)
    ncol = slab.shape[1]
    ci = lax.broadcasted_iota(jnp.int32, (ncol, rp), 0)
    si = lax.broadcasted_iota(jnp.int32, (ncol, rp), 1)
    pick = jnp.where((ci == pl.program_id(0) * ns + si) & (si < ns), 1.0, 0.0).astype(BF16)
    new_t = _split3_dot_right(slab, pick)
    kn_t = new_t[0:gw].astype(BF16)
    vn_t = new_t[gw:2 * gw].astype(BF16)
    qm = jnp.concatenate([q8] * nh, axis=0)
    rowh = lax.broadcasted_iota(jnp.int32, (nh * rp, gw), 0) >> 3
    laneh = lax.broadcasted_iota(jnp.int32, (nh * rp, gw), 1) >> 6
    own = rowh == laneh
    qm = jnp.where(own, qm, 0.0).astype(BF16)
    s_c = jnp.dot(qm, c_ref[0, 0:gw, :].astype(BF16), preferred_element_type=F32)
    s_n = jnp.dot(qm, kn_t, preferred_element_type=F32)
    srow = lax.broadcasted_iota(jnp.int32, s_c.shape, 0) & (rp - 1)
    dist = w + srow - lax.broadcasted_iota(jnp.int32, s_c.shape, 1)
    valid_c = (dist <= window) & ((dist & (d - 1)) == 0) & (srow < ns)
    srow_n = lax.broadcasted_iota(jnp.int32, s_n.shape, 0) & (rp - 1)
    col_n = lax.broadcasted_iota(jnp.int32, s_n.shape, 1)
    dist_n = srow_n - col_n
    valid_n = (dist_n >= 0) & ((dist_n & (d - 1)) == 0) & (srow_n < ns)
    valid_n = valid_n | ((srow_n >= ns) & (col_n == 0))
    s_c = jnp.where(valid_c, s_c, -jnp.inf)
    s_n = jnp.where(valid_n, s_n, -jnp.inf)
    m = jnp.maximum(jnp.max(s_c, axis=-1, keepdims=True), jnp.max(s_n, axis=-1, keepdims=True))
    p_c = jnp.exp(s_c - m)
    p_n = jnp.exp(s_n - m)
    l = jnp.sum(p_c, axis=-1, keepdims=True) + jnp.sum(p_n, axis=-1, keepdims=True)
    o = lax.dot_general(p_c.astype(BF16), c_ref[0, gw:2 * gw, :].astype(BF16), (((1,), (1,)), ((), ())),
                        preferred_element_type=F32)
    o = (o + lax.dot_general(p_n.astype(BF16), vn_t, (((1,), (1,)), ((), ())), preferred_element_type=F32)) / l
    lse = jnp.broadcast_to(m + jnp.log(l), o.shape)
    o = jnp.where(own, o, 0.0)
    lse = jnp.where(own, lse, 0.0)
    o8 = o[0:rp]
    l8 = lse[0:rp]
    for h in range(1, nh):
        o8 = o8 + o[h * rp:(h + 1) * rp]
        l8 = l8 + lse[h * rp:(h + 1) * rp]
    o_ref[0] = o8[0:ns]
    lse_ref[0] = l8[0:ns]
    for rb in range(2 * gw // rchunk):
        rows = slice(rb * rchunk, (rb + 1) * rchunk)
        cout_ref[0, rows, :] = jnp.concatenate([c_ref[0, rows, ns:w], new_t[rows, 0:ns]], axis=1)


def _cache_attn_body(*refs, groups, ns):
    ng = len(groups)
    q_refs, nt_ref, c_refs = refs[0:ng], refs[ng], refs[ng + 1:2 * ng + 1]
    outs = refs[2 * ng + 1:]
    for g, (window, d) in enumerate(groups):
        _cache_attn_group(q_refs[g], nt_ref[g], c_refs[g], outs[3 * g], outs[3 * g + 1], outs[3 * g + 2],
                          d=d, window=window, ns=ns, rchunk=64)


def _cache_attn(qs, new_t, caches_t, *, groups):
    nbat, ns, gw = qs[0].shape
    new = pl.BlockSpec((1, ns, gw), lambda b: (b, 0, 0))
    out = jax.ShapeDtypeStruct((nbat, ns, gw), F32)
    cbs = [pl.BlockSpec((1, 2 * gw, c.shape[2]), lambda b: (b, 0, 0)) for c in caches_t]
    in_specs = [new] * len(groups) + [pl.BlockSpec(new_t.shape, lambda b: (0, 0, 0))] + cbs
    out_shape, out_specs = [], []
    for c, cb in zip(caches_t, cbs):
        out_shape += [out, out, jax.ShapeDtypeStruct(c.shape, F32)]
        out_specs += [new, new, cb]
    res = pl.pallas_call(
        functools.partial(_cache_attn_body, groups=groups, ns=ns),
        out_shape=out_shape, grid=(nbat,), in_specs=in_specs, out_specs=out_specs,
        compiler_params=_cparams("parallel"), name="cache_attn")(*qs, new_t, *caches_t)
    return [res[3 * g:3 * g + 3] for g in range(len(groups))]


def _odd_back_body(*refs, tm, dils, alpha):
    ng = len(dils)
    o_refs, l_refs = refs[0:ng], refs[ng:2 * ng]
    gate_ref, x_ref, w_ref, g_ref, b_ref, y_ref = refs[2 * ng:2 * ng + 6]
    scr = refs[2 * ng + 6:]
    os_, ls_ = [], []
    for g, d in enumerate(dils):
        if d == 1:
            os_.append(o_refs[g][0])
            ls_.append(l_refs[g][0])
        else:
            n = tm // d
            nlb, _, lt = scr[2 * g].shape
            for c in range(d):
                for lb in range(nlb):
                    scr[2 * g][lb, pl.ds(c, n, stride=d), :] = o_refs[g][c, :, lb * lt:(lb + 1) * lt]
                    scr[2 * g + 1][lb, pl.ds(c, n, stride=d), :] = l_refs[g][c, :, lb * lt:(lb + 1) * lt]
            os_.append(jnp.concatenate([scr[2 * g][lb] for lb in range(nlb)], axis=1))
            ls_.append(jnp.concatenate([scr[2 * g + 1][lb] for lb in range(nlb)], axis=1))
    m = ls_[0]
    for l in ls_[1:]:
        m = jnp.maximum(m, l)
    es = [jnp.exp(l - m) for l in ls_]
    den = es[0]
    for e in es[1:]:
        den = den + e
    o = es[0] * os_[0]
    for e, og in zip(es[1:], os_[1:]):
        o = o + e * og
    o = o / den
    y = _dot(o * gate_ref[...], w_ref[...])
    y_ref[...] = _layer_norm(alpha * x_ref[...] + y, g_ref[...], b_ref[...], LN_EPS)


def _odd_back(os_, ls_, gate, x, w_out, ln_g, ln_b, *, dils, alpha, tm):
    t, d_model = x.shape
    gw = gate.shape[-1]
    row = lambda i: (i, 0)
    fix = lambda i: (0, 0)
    cls_specs = [pl.BlockSpec((d, tm // d, gw), lambda i: (0, i, 0)) for d in dils]
    in_specs = cls_specs * 2 + [pl.BlockSpec((tm, gw), row), pl.BlockSpec((tm, d_model), row),
                                pl.BlockSpec((gw, d_model), fix), pl.BlockSpec((1, d_model), fix),
                                pl.BlockSpec((1, d_model), fix)]
    return pl.pallas_call(
        functools.partial(_odd_back_body, tm=tm, dils=dils, alpha=alpha),
        out_shape=jax.ShapeDtypeStruct((t, d_model), F32), grid=(t // tm,), in_specs=in_specs,
        out_specs=pl.BlockSpec((tm, d_model), row),
        scratch_shapes=[pltpu.VMEM((gw // 128, tm, 128), F32)] * (2 * len(dils)),
        compiler_params=_cparams("parallel"), name="odd_back")(*os_, *ls_, gate, x, w_out, ln_g, ln_b)


def kernel(x_prompt, x_sample, state_conv, state_shift, state_wkv, cache_kv_w128, cache_kv_w512, cache_kv_w2048,
           w_in_even, conv_w, conv_b, conv_ln_g, conv_ln_b, mu_shift, w0, w2, a0, a2, k_k, k_a, r_k, lnx_g, lnx_b,
           w_out_even, w_in_odd, w_out_odd, ln_g, ln_b):
    nbp, seq, d_model = x_prompt.shape
    nbs, ns, _ = x_sample.shape
    assert nbp == 1
    depth = ln_g.shape[0]
    assert depth == 2 and w_in_even.shape[0] == 1 and w_in_odd.shape[0] == 1
    alpha = (2.0 * depth) ** 0.25
    bw = w0.shape[-1]
    lora_d, lora_a = w2.shape[1], a2.shape[1]
    nh = bw // HEAD
    nhist = CONV_W - 1
    caches = (cache_kv_w128, cache_kv_w512, cache_kv_w2048)
    dils = tuple(d for _, d in C_GROUPS)
    gw = w_out_odd.shape[1]
    ng = len(C_GROUPS)

    wi = w_in_even[0]
    sh_cols = 3 * bw + lora_d + lora_a
    zpad = jnp.zeros((d_model, LORA_PAD - lora_d - lora_a), F32)
    w_re = jnp.concatenate([wi[:, :sh_cols], zpad, wi[:, sh_cols:]], axis=1).astype(BF16)
    mu_re = jnp.concatenate([mu_shift[0], jnp.zeros((LORA_PAD - lora_d - lora_a,), F32)])[None, :]
    w2p = jnp.zeros((LORA_PAD, bw), F32).at[0:lora_d].set(w2[0])
    a2p = jnp.zeros((LORA_PAD, bw), F32).at[lora_d:lora_d + lora_a].set(a2[0])
    cw = jnp.repeat(conv_w[0], 8, axis=0)
    vec = lambda z: z.reshape(1, -1)
    conv_small = (vec(conv_b[0]), vec(conv_ln_g[0]), vec(conv_ln_b[0]))
    mix_small = (vec(w0[0]), w2p, vec(a0[0]), a2p, vec(k_k[0]), vec(k_a[0]), vec(r_k[0]))
    w_out_e = w_out_even[0].astype(BF16)
    w_in_o = w_in_odd[0].astype(BF16)
    w_out_o = w_out_odd[0].astype(BF16)
    lng = [vec(ln_g[l]) for l in range(depth)]
    lnb = [vec(ln_b[l]) for l in range(depth)]

    xp = x_prompt[0]
    ya, bg, r, k, v, lora, conv_tail = _even_front_prompt(xp, w_re, mu_re, cw, conv_small)
    o_wkv, bonus, wkv_p = _wkv_prompt(r, k, v, lora, mix_small, jnp.zeros((nh, HEAD, HEAD), F32))
    xp1, qc, kc, vc, gate_p = _mid(o_wkv, bonus, bg, ya, xp, vec(lnx_g[0]), vec(lnx_b[0]), w_out_e, lng[0], lnb[0],
                                   w_in_o, alpha=alpha, tm=TM_MID, dils=dils, gw=gw)
    conv_p = conv_tail[CONV_HIST - nhist:][None, None]
    shift_p = x_prompt[:, -1][None]

    xs_t = jnp.transpose(x_sample, (1, 0, 2)).reshape(ns * nbs, d_model)
    x_in = jnp.concatenate([state_shift[0], xs_t], axis=0)
    hist = jnp.transpose(state_conv[0], (1, 0, 2)).reshape(nhist * nbs, bw)
    res = _even_front_sample(x_in, hist, w_re, mu_re, cw, conv_small, nb=nbs, ns=ns)
    ya_s, bg_s, conv_s_t = res[0], res[1], res[6]
    blk = 8

    def seq_major_pad(z):
        z = jnp.transpose(z.reshape(ns, nbs, z.shape[-1]), (1, 0, 2))
        return jnp.pad(z, ((0, 0), (0, blk - ns), (0, 0))).reshape(nbs * blk, z.shape[-1])

    def step_major(z):
        return jnp.transpose(z.reshape(nbs, blk, z.shape[-1])[:, :ns], (1, 0, 2)).reshape(ns * nbs, z.shape[-1])

    wkv_in = [seq_major_pad(z) for z in res[2:6]]
    o_s_pad, bon_s_pad, wkv_s = _wkv_sample(*wkv_in, mix_small, state_wkv[0], blk=blk, ns=ns)
    o_s, bon_s = step_major(o_s_pad), step_major(bon_s_pad)
    ones = (1,) * ng
    xs1_t, qs_t, _, _, gate_s_t = _mid(o_s, bon_s, bg_s, ya_s, xs_t, vec(lnx_g[0]), vec(lnx_b[0]), w_out_e, lng[0],
                                       lnb[0], w_in_o, alpha=alpha, tm=ns * nbs, dils=ones, gw=gw)
    conv_s = jnp.transpose(conv_s_t.reshape(nhist, nbs, bw), (1, 0, 2))[None]
    shift_s = x_sample[:, -1][None]

    def seq_major(z):
        return jnp.transpose(z.reshape(ns, nbs, z.shape[-1]), (1, 0, 2)).reshape(nbs * ns, z.shape[-1])

    xs1 = seq_major(xs1_t)
    gate_s = seq_major(gate_s_t)
    qs = [seq_major(q[0]) for q in qs_t]

    tail = min(max(win for win, _ in C_GROUPS), seq)
    w_kv_t = jnp.concatenate([w_in_odd[0][:, (part * ng + g) * gw:(part * ng + g + 1) * gw].T
                              for g in range(ng) for part in (1, 2)], axis=0).astype(BF16)
    kvt = _kv_rows_t(xp1, w_kv_t, tm=TM_KV, first_row=seq - tail).reshape(ng, 2, gw // HEAD, HEAD, tail)
    os_, ls_ = [], []
    for g, (win, d) in enumerate(C_GROUPS):
        span = win // d
        flat = lambda z: z.reshape(seq, gw)
        o_g, l_g = _band_attn(flat(qc[g]), flat(kc[g]), flat(vc[g]), span=span, nb_class=(seq // d) // span)
        os_.append(o_g.reshape(d, seq // d, gw))
        ls_.append(l_g.reshape(d, seq // d, l_g.shape[-1]))
    xp2 = _odd_back(os_, ls_, gate_p, xp1, w_out_o, lng[1], lnb[1], dils=dils, alpha=alpha, tm=TM_BACK)
    kv_p = []
    for g, (win, d) in enumerate(C_GROUPS):
        n = min(win, seq)
        kv_p.append(jnp.transpose(kvt[g, :, :, :, tail - n:], (3, 0, 1, 2))[None, None])

    kvt_s = _kv_rows_t(xs1, w_kv_t, tm=nbs * ns, first_row=0).reshape(ng, 2 * gw, nbs * ns)
    caches_t = [jnp.transpose(c[0].reshape(nbs, c.shape[2], 2 * gw), (0, 2, 1)) for c in caches]
    q_gs = [q.astype(F32).reshape(nbs, ns, gw) for q in qs]
    os_s, ls_s, kv_s_out = [], [], []
    for o_g, l_g, cnew_t in _cache_attn(q_gs, kvt_s, caches_t, groups=C_GROUPS):
        os_s.append(o_g.reshape(1, nbs * ns, gw))
        ls_s.append(l_g.reshape(1, nbs * ns, l_g.shape[-1]))
        kv_s_out.append(jnp.transpose(cnew_t, (0, 2, 1)).reshape(nbs, cnew_t.shape[2], 2, gw // HEAD, HEAD)[None])
    xs2 = _odd_back(os_s, ls_s, gate_s, xs1, w_out_o, lng[1], lnb[1], dils=ones, alpha=alpha, tm=nbs * ns)

    return (xp2[None], xs2.reshape(nbs, ns, d_model), conv_p, conv_s, shift_p, shift_s,
            wkv_p[None, None], wkv_s[None],
            kv_p[0], kv_s_out[0], kv_p[1], kv_s_out[1], kv_p[2], kv_s_out[2])
```

```python
import functools

import jax
import jax.numpy as jnp
from jax import lax
from jax.experimental import pallas as pl
from jax.experimental.pallas import tpu as pltpu

F32 = jnp.float32
BF16 = jnp.bfloat16

LN_EPS = 1e-5
LNX_EPS = 64e-5
HEAD = 64
LORA_PAD = 128
CONV_W = 31
CONV_HIST = 32
C_GROUPS = ((128, 1), (512, 4), (2048, 16))
WKV_CHUNK = 64
TM_FRONT = 512
WKV_CHUNKS = 4
TM_MID = 512
MID_SUB = 256
BAND_BLOCKS = 16
TM_BACK = 512
TM_KV = 512
VMEM_LIMIT = 56 * 1024 * 1024


def _cparams(*sem):
    return pltpu.CompilerParams(dimension_semantics=sem or None, vmem_limit_bytes=VMEM_LIMIT)


def _dot(a, b):
    return jnp.dot(a.astype(BF16), b.astype(BF16), preferred_element_type=F32)


def _dot_nt(a, b):
    return lax.dot_general(a.astype(BF16), b.astype(BF16), (((1,), (1,)), ((), ())), preferred_element_type=F32)


def _dot_tn(a, b):
    return lax.dot_general(a.astype(BF16), b.astype(BF16), (((0,), (0,)), ((), ())), preferred_element_type=F32)


def _head_sum(a, ones_bf16):
    ah = a.astype(BF16)
    al = (a - ah.astype(F32)).astype(BF16)
    lt = ones_bf16.shape[0]
    parts = []
    for lb in range(a.shape[1] // lt):
        sl = slice(lb * lt, (lb + 1) * lt)
        parts.append(jnp.dot(ah[:, sl], ones_bf16, preferred_element_type=F32) +
                     jnp.dot(al[:, sl], ones_bf16, preferred_element_type=F32))
    return jnp.concatenate(parts, axis=1)


def _head_ones():
    n = 2 * HEAD
    r = lax.broadcasted_iota(jnp.int32, (n, n), 0) >> 6
    c = lax.broadcasted_iota(jnp.int32, (n, n), 1) >> 6
    return jnp.where(r == c, 1.0, 0.0).astype(BF16)


def _sigmoid(x):
    return 1.0 / (1.0 + jnp.exp(-x))


def _silu(x):
    return x * _sigmoid(x)


def _softplus(y):
    return jnp.maximum(y, 0.0) + jnp.log(1.0 + jnp.exp(-jnp.abs(y)))


def _layer_norm(x, g, b, eps):
    mu = jnp.mean(x, axis=-1, keepdims=True)
    xc = x - mu
    var = jnp.mean(xc * xc, axis=-1, keepdims=True)
    return xc * lax.rsqrt(var + eps) * g + b


def _even_mix(r, k, v, lora, prm, hsum):
    w0, w2p, a0, a2p, k_k, k_a, r_k = prm
    w_log = -_softplus(-(w0 + _dot(jnp.tanh(lora), w2p))) - 0.5
    lw = -jnp.exp(w_log)
    a = _sigmoid(a0 + _dot(lora, a2p))
    kk = k * k_k
    nrm = jnp.sqrt(_head_sum(kk * kk, hsum))
    kkn = kk / jnp.maximum(nrm, 1e-12)
    km = k * (1.0 + (a - 1.0) * k_a)
    bonus = _head_sum(r * km * r_k, hsum) * v
    return r, lw, km, v, kkn, kkn * a, bonus


def _tap_weight(cw_ref, j, n):
    return jnp.concatenate([cw_ref[8 * j:8 * j + 8, :]] * (n // 8), axis=0)


def _conv_ln_gate(u_ref, sh_ref, row0, n, cw_ref, cb, clg, clb, a_gate):
    acc = jnp.zeros((n, cb.shape[-1]), F32) + cb
    for j in range(CONV_W):
        a, b = divmod(CONV_HIST - (CONV_W - 1) + j, 8)
        win = u_ref[pl.ds(row0 + 8 * a, n), :] if b == 0 else sh_ref[b - 1, pl.ds(row0 + 8 * a, n), :]
        acc = acc + _tap_weight(cw_ref, j, n) * win
    return _silu(_layer_norm(acc, clg, clb, LN_EPS)) * _silu(a_gate)


def _even_front_prompt_body(x_ref, w_ref, mu_ref, cw_ref, cb_ref, clg_ref, clb_ref,
                            ya_ref, bg_ref, r_ref, k_ref, v_ref, lora_ref, conv_ref,
                            p_scr, u_scr, sh_scr, *, tm, bw, sub):
    i = pl.program_id(0)
    shw = 3 * bw + LORA_PAD

    @pl.when(i == 0)
    def _():
        p_scr[0:8, :] = jnp.zeros((8, shw), F32)
        u_scr[0:CONV_HIST, :] = jnp.zeros((CONV_HIST, bw), F32)

    xb = x_ref[...].astype(BF16)
    proj = lambda c0, c1: jnp.dot(xb, w_ref[:, c0:c1], preferred_element_type=F32)
    cur = proj(0, shw)
    p_scr[8:8 + tm, :] = cur
    prev = p_scr[pl.ds(7, tm), :]
    sh = cur + (prev - cur) * mu_ref[...]
    p_scr[7:8, :] = cur[tm - 1:tm, :]
    r_ref[...] = sh[:, 0:bw]
    k_ref[...] = sh[:, bw:2 * bw]
    v_ref[...] = sh[:, 2 * bw:3 * bw]
    lora_ref[...] = sh[:, 3 * bw:shw]

    bg_ref[...] = _silu(proj(shw + 3 * bw, shw + 4 * bw))
    u_scr[CONV_HIST:CONV_HIST + tm, :] = proj(shw, shw + bw) * _sigmoid(proj(shw + bw, shw + 2 * bw))
    a_gate = proj(shw + 2 * bw, shw + 3 * bw)
    for b in range(1, 8):
        sh_scr[b - 1] = u_scr[pl.ds(b, sh_scr.shape[1]), :]
    cb, clg, clb = cb_ref[...], clg_ref[...], clb_ref[...]
    for s in range(tm // sub):
        ya_ref[s * sub:(s + 1) * sub, :] = _conv_ln_gate(u_scr, sh_scr, s * sub, sub, cw_ref, cb, clg, clb,
                                                         a_gate[s * sub:(s + 1) * sub, :]).astype(BF16)
    tail = u_scr[tm:tm + CONV_HIST, :]
    conv_ref[...] = tail
    u_scr[0:CONV_HIST, :] = tail


def _even_front_prompt(x, w_re, mu_re, cw, conv_small, *, tm=TM_FRONT):
    t, d = x.shape
    bw = conv_small[0].shape[-1]
    shw = 3 * bw + LORA_PAD
    ncol = w_re.shape[1]
    row = lambda i: (i, 0)
    fix = lambda i: (0, 0)
    vec = pl.BlockSpec((1, bw), fix)
    in_specs = [pl.BlockSpec((tm, d), row), pl.BlockSpec((d, ncol), fix), pl.BlockSpec((1, shw), fix),
                pl.BlockSpec((8 * CONV_W, bw), fix), vec, vec, vec]
    out = jax.ShapeDtypeStruct((t, bw), F32)
    outs = [jax.ShapeDtypeStruct((t, bw), BF16)] + [out] * 4 + [jax.ShapeDtypeStruct((t, LORA_PAD), F32),
                                                                jax.ShapeDtypeStruct((CONV_HIST, bw), F32)]
    out_specs = [pl.BlockSpec((tm, bw), row)] * 5 + [pl.BlockSpec((tm, LORA_PAD), row),
                                                     pl.BlockSpec((CONV_HIST, bw), fix)]
    body = functools.partial(_even_front_prompt_body, tm=tm, bw=bw, sub=32)
    return pl.pallas_call(
        body, out_shape=outs, grid=(t // tm,), in_specs=in_specs, out_specs=out_specs,
        scratch_shapes=[pltpu.VMEM((8 + tm, shw), F32), pltpu.VMEM((CONV_HIST + tm, bw), F32),
                        pltpu.VMEM((7, CONV_HIST - 8 + tm, bw), F32)],
        compiler_params=_cparams("arbitrary"), name="even_front_prompt")(x, w_re, mu_re, cw, *conv_small)


def _even_front_sample_body(x_ref, hist_ref, w_ref, mu_ref, cw_ref, cb_ref, clg_ref, clb_ref,
                            ya_ref, bg_ref, r_ref, k_ref, v_ref, lora_ref, conv_ref,
                            u_scr, *, nb, ns, bw):
    n = nb * ns
    shw = 3 * bw + LORA_PAD
    nh = CONV_W - 1
    p = jnp.dot(x_ref[...].astype(BF16), w_ref[...], preferred_element_type=F32)
    cur = p[nb:, :shw]
    prev = p[:n, :shw]
    sh = cur + (prev - cur) * mu_ref[...]
    r_ref[...] = sh[:, 0:bw]
    k_ref[...] = sh[:, bw:2 * bw]
    v_ref[...] = sh[:, 2 * bw:3 * bw]
    lora_ref[...] = sh[:, 3 * bw:shw]
    rest = p[nb:, shw:]
    a_val, a_glu = rest[:, 0:bw], rest[:, bw:2 * bw]
    a_gate, b_gate = rest[:, 2 * bw:3 * bw], rest[:, 3 * bw:4 * bw]
    bg_ref[...] = _silu(b_gate)
    u_scr[0:nh * nb, :] = hist_ref[...]
    u_scr[nh * nb:(nh + ns) * nb, :] = a_val * _sigmoid(a_glu)
    cb, clg, clb = cb_ref[...], clg_ref[...], clb_ref[...]
    for s in range(ns):
        acc = jnp.zeros((nb, bw), F32) + cb
        for j in range(CONV_W):
            acc = acc + _tap_weight(cw_ref, j, nb) * u_scr[(s + j) * nb:(s + j + 1) * nb, :]
        ya_ref[s * nb:(s + 1) * nb, :] = (_silu(_layer_norm(acc, clg, clb, LN_EPS)) *
                                           _silu(a_gate[s * nb:(s + 1) * nb, :])).astype(BF16)
    conv_ref[...] = u_scr[ns * nb:(ns + nh) * nb, :]


def _even_front_sample(x_in, hist, w_re, mu_re, cw, conv_small, *, nb, ns):
    bw = conv_small[0].shape[-1]
    n = nb * ns
    nh = CONV_W - 1
    out = jax.ShapeDtypeStruct((n, bw), F32)
    outs = [jax.ShapeDtypeStruct((n, bw), BF16)] + [out] * 4 + [jax.ShapeDtypeStruct((n, LORA_PAD), F32),
                                                                jax.ShapeDtypeStruct((nh * nb, bw), F32)]
    body = functools.partial(_even_front_sample_body, nb=nb, ns=ns, bw=bw)
    return pl.pallas_call(
        body, out_shape=outs, scratch_shapes=[pltpu.VMEM(((nh + ns) * nb, bw), F32)],
        compiler_params=_cparams(), name="even_front_sample")(x_in, hist, w_re, mu_re, cw, *conv_small)


def _wkv_block_terms(r, lw, km, v, kkn, ab, blk, problems):
    n, width = r.shape
    nh = width // HEAD
    ri = lax.broadcasted_iota(jnp.int32, (n, n), 0)
    ci = lax.broadcasted_iota(jnp.int32, (n, n), 1)
    same = _same_block(ri, ci, blk)
    incl = (ri >= ci) & same
    strict = (ri > ci) & same
    if problems > 1:
        tri = jnp.where(lax.broadcasted_iota(jnp.int32, (blk, blk), 0) >= lax.broadcasted_iota(jnp.int32, (blk, blk), 1),
                        1.0, 0.0).astype(BF16)
        cums = [_split3_dot(tri, lw[b * blk:(b + 1) * blk]) for b in range(n // blk)]
        cum = jnp.concatenate(cums, axis=0)
        tot = jnp.concatenate([jnp.broadcast_to(c[blk - 1:blk], c.shape) for c in cums], axis=0)
    else:
        cum = _split3_dot(jnp.where(incl, 1.0, 0.0).astype(BF16), lw)
        tot = _split3_dot(jnp.where(same, 1.0, 0.0).astype(BF16), lw)
    dn = jnp.exp(-cum)
    at = -kkn * jnp.exp(cum - lw)
    bt = ab * dn
    kt = km * dn
    rt = r * jnp.exp(cum)
    rem = jnp.exp(tot - cum)
    bc = ab * rem
    kc = km * rem
    wc = jnp.exp(tot)
    m = n // problems
    if m != n:
        ri = lax.broadcasted_iota(jnp.int32, (m, m), 0)
        ci = lax.broadcasted_iota(jnp.int32, (m, m), 1)
        same = _same_block(ri, ci, blk)
        incl = (ri >= ci) & same
        strict = (ri > ci) & same
    chains = [(slice(p * m, (p + 1) * m), slice(h * HEAD, (h + 1) * HEAD)) for p in range(problems) for h in range(nh)]
    gs = [_dot_nt(jnp.concatenate([at[rw, sl], rt[rw, sl]], axis=0), jnp.concatenate([bt[rw, sl], kt[rw, sl]], axis=0))
          for rw, sl in chains]
    l_ab = [jnp.where(strict, g[:m, :m], 0.0) for g in gs]
    l_ak = [jnp.where(strict, g[:m, m:], 0.0) for g in gs]
    l_rb = [jnp.where(incl, g[m:, :m], 0.0) for g in gs]
    l_rk = [jnp.where(incl, g[m:, m:], 0.0) for g in gs]
    toff = _unit_lower_inverse_minus_eye(l_ab, ri, ci, blk)
    lv = [_dot(jnp.concatenate([la, lr], axis=0), v[rw, sl]) for la, lr, (rw, sl) in zip(l_ak, l_rk, chains)]
    ahat = [at[rw, sl] + _dot(t, at[rw, sl]) for t, (rw, sl) in zip(toff, chains)]
    uhat = [x[:m] + _dot(t, x[:m]) for t, x in zip(toff, lv)]
    qhat = [rt[rw, sl] + _dot(l, a) for l, a, (rw, sl) in zip(l_rb, ahat, chains)]
    ohat = [_dot(lb, u) + x[m:] for lb, u, x in zip(l_rb, uhat, lv)]
    terms = list(zip(ahat, uhat, qhat, ohat))
    return [terms[p * nh:(p + 1) * nh] for p in range(problems)], bc, kc, wc


def _same_block(ri, ci, size):
    sh = size.bit_length() - 1
    return (ri >> sh) == (ci >> sh)


def _split(x):
    hi = x.astype(BF16)
    return hi, (x - hi.astype(F32)).astype(BF16)


def _split3(x):
    hi, lo = _split(x)
    return hi, lo, (x - hi.astype(F32) - lo.astype(F32)).astype(BF16)


def _split3_dot(ones_bf16, x):
    hi, lo, lo2 = _split3(x)
    d = lambda y: jnp.dot(ones_bf16, y, preferred_element_type=F32)
    return d(hi) + (d(lo) + d(lo2))


def _split3_dot_right(x, ones_bf16):
    hi, lo, lo2 = _split3(x)
    d = lambda y: jnp.dot(y, ones_bf16, preferred_element_type=F32)
    return d(hi) + (d(lo) + d(lo2))


def _unit_lower_inverse_minus_eye(ls, ri, ci, blk):
    base = 8
    eye = jnp.where(ri == ci, 1.0, 0.0)
    same8 = _same_block(ri, ci, base)
    l8 = [jnp.where(same8, l, 0.0).astype(BF16) for l in ls]
    x2 = [_dot(x, x).astype(BF16) for x in l8]
    x4 = [_dot(x, x).astype(BF16) for x in x2]
    p = [eye + jnp.where(same8, l, 0.0) for l in ls]
    p = [q + _dot(q, x) for q, x in zip(p, x2)]
    t = [q + _dot(q, x) for q, x in zip(p, x4)]
    size = 2 * base
    while size <= blk:
        half = size.bit_length() - 2
        lower_left = _same_block(ri, ci, size) & (((ri >> half) & 1) == 1) & (((ci >> half) & 1) == 0)
        tb = [x.astype(BF16) for x in t]
        mt = [_dot(jnp.where(lower_left, l, 0.0), x) for l, x in zip(ls, tb)]
        t = [x + _dot(xb, m) for x, xb, m in zip(t, tb, mt)]
        size *= 2
    return [x - eye for x in t]


def _wkv_prompt_body(r_ref, k_ref, v_ref, lora_ref, w0_ref, w2_ref, a0_ref, a2_ref, kk_ref, ka_ref, rk_ref, s0_ref,
                     o_ref, bon_ref, sout_ref, s_scr, *, c, nch):
    i = pl.program_id(0)

    @pl.when(i == 0)
    def _():
        s_scr[...] = s0_ref[...]

    prm = (w0_ref[...], w2_ref[...], a0_ref[...], a2_ref[...], kk_ref[...], ka_ref[...], rk_ref[...])
    r, lw, km, v, kkn, ab, bonus = _even_mix(r_ref[...], k_ref[...], v_ref[...], lora_ref[...], prm, _head_ones())
    bon_ref[...] = bonus
    terms, bc, kc, wc = _wkv_block_terms(r, lw, km, v, kkn, ab, c, nch)
    nh = len(terms[0])
    hs = [slice(h * HEAD, (h + 1) * HEAD) for h in range(nh)]
    s = [s_scr[h] for h in range(nh)]
    for ch in range(nch):
        rows = slice(ch * c, (ch + 1) * c)
        heads = terms[ch]
        mm = [_dot_nt(jnp.concatenate([ahat, qhat], axis=0), s[h])
              for h, (ahat, _, qhat, _) in enumerate(heads)]
        u = [mm[h][:c] + heads[h][1] for h in range(nh)]
        for h in range(nh):
            o_ref[rows, hs[h]] = mm[h][c:] + heads[h][3]
        s = [s[h] * wc[ch * c:ch * c + 1, hs[h]]
             + _dot_tn(jnp.concatenate([u[h], v[rows, hs[h]]], axis=0),
                       jnp.concatenate([bc[rows, hs[h]], kc[rows, hs[h]]], axis=0)) for h in range(nh)]
    for h in range(nh):
        s_scr[h] = s[h]

    @pl.when(i == pl.num_programs(0) - 1)
    def _():
        sout_ref[...] = s_scr[...]


def _mix_specs(bw):
    fix = lambda *_: (0, 0)
    vec = pl.BlockSpec((1, bw), fix)
    lora = pl.BlockSpec((LORA_PAD, bw), fix)
    return [vec, lora, vec, lora, vec, vec, vec]


def _wkv_prompt(r, k, v, lora, mix_small, s0, *, nch=WKV_CHUNKS):
    t, bw = r.shape
    c = WKV_CHUNK
    nh = bw // HEAD
    row = pl.BlockSpec((nch * c, bw), lambda i: (i, 0))
    lrow = pl.BlockSpec((nch * c, LORA_PAD), lambda i: (i, 0))
    st = pl.BlockSpec((nh, HEAD, HEAD), lambda i: (0, 0, 0))
    out = jax.ShapeDtypeStruct((t, bw), F32)
    return pl.pallas_call(
        functools.partial(_wkv_prompt_body, c=c, nch=nch),
        out_shape=[out, out, jax.ShapeDtypeStruct((nh, HEAD, HEAD), F32)],
        grid=(t // (nch * c),), in_specs=[row] * 3 + [lrow] + _mix_specs(bw) + [st], out_specs=[row, row, st],
        scratch_shapes=[pltpu.VMEM((nh, HEAD, HEAD), F32)],
        compiler_params=_cparams("arbitrary"), name="wkv_prompt")(r, k, v, lora, *mix_small, s0)


def _wkv_sample_body(r_ref, k_ref, v_ref, lora_ref, w0_ref, w2_ref, a0_ref, a2_ref, kk_ref, ka_ref, rk_ref, s0_ref,
                     o_ref, bon_ref, sout_ref,
                     ah_scr, uh_scr, qh_scr, oh_scr, bc_scr, kc_scr, wc_scr, *, nb, blk, ns):
    prm = (w0_ref[...], w2_ref[...], a0_ref[...], a2_ref[...], kk_ref[...], ka_ref[...], rk_ref[...])
    r, lw, km, v, kkn, ab, bonus = _even_mix(r_ref[...], k_ref[...], v_ref[...], lora_ref[...], prm, _head_ones())
    real = (lax.broadcasted_iota(jnp.int32, lw.shape, 0) & (blk - 1)) < ns
    lw = jnp.where(real, lw, 0.0)
    bon_ref[...] = bonus
    (heads,), bc, kc, wc = _wkv_block_terms(r, lw, km, v, kkn, ab, blk, 1)
    bc_scr[...] = bc
    kc_scr[...] = kc
    wc_scr[...] = wc
    for h, (ahat, uhat, qhat, ohat) in enumerate(heads):
        sl = slice(h * HEAD, (h + 1) * HEAD)
        ah_scr[:, sl] = ahat
        uh_scr[:, sl] = uhat
        qh_scr[:, sl] = qhat
        oh_scr[:, sl] = ohat
    nh = len(heads)

    def per_seq(b, carry):
        rows = pl.ds(pl.multiple_of(b * blk, blk), blk)
        for h in range(nh):
            sl = slice(h * HEAD, (h + 1) * HEAD)
            s = s0_ref[b, h]
            mm = _dot_nt(jnp.concatenate([ah_scr[rows, sl], qh_scr[rows, sl]], axis=0), s)
            u = mm[:blk] + uh_scr[rows, sl]
            o_ref[rows, sl] = mm[blk:] + oh_scr[rows, sl]
            wrow = wc_scr[rows, sl][0:1, :]
            sout_ref[b, h] = s * wrow + _dot_tn(u, bc_scr[rows, sl]) + _dot_tn(v_ref[rows, sl], kc_scr[rows, sl])
        return carry

    lax.fori_loop(0, nb, per_seq, 0, unroll=4)


def _wkv_sample(r, k, v, lora, mix_small, s0, *, blk, ns):
    n, bw = r.shape
    nb = n // blk
    scr = pltpu.VMEM((n, bw), F32)
    out = jax.ShapeDtypeStruct((n, bw), F32)
    return pl.pallas_call(
        functools.partial(_wkv_sample_body, nb=nb, blk=blk, ns=ns),
        out_shape=[out, out, jax.ShapeDtypeStruct(s0.shape, F32)],
        scratch_shapes=[scr] * 7, compiler_params=_cparams(), name="wkv_sample")(r, k, v, lora, *mix_small, s0)


def _even_back_rows(o, bonus, bg, ya, x, lg, lb, w_ref, g, b, hsum, *, alpha, bw):
    om = _head_sum(o, hsum) * (1.0 / HEAD)
    oc = o - om
    ov = _head_sum(oc * oc, hsum) * (1.0 / HEAD)
    on = oc * lax.rsqrt(ov + LNX_EPS) * lg + lb
    yb = (on + bonus) * bg
    y = _dot(ya, w_ref[0:bw, :]) + _dot(yb, w_ref[bw:2 * bw, :])
    return _layer_norm(alpha * x + y, g, b, LN_EPS)


def _project_class_block(xb, w_ref, blk, cls_ref, p_scr, *, tm, gw, d, scale):
    pc = jnp.dot(xb, w_ref[:, blk * gw:(blk + 1) * gw], preferred_element_type=F32)
    if scale is not None:
        pc = pc * scale
    if d == 1:
        cls_ref[0] = pc.astype(BF16)
        return
    lt = p_scr.shape[-1]
    nlb = gw // lt
    slot = (blk % 2) * nlb
    for lb in range(nlb):
        p_scr[slot + lb] = pc[:, lb * lt:(lb + 1) * lt]
    n = tm // d
    for c in range(d):
        for lb in range(nlb):
            cls_ref[c, :, lb * lt:(lb + 1) * lt] = p_scr[slot + lb, pl.ds(c, n, stride=d), :].astype(BF16)


def _mid_body(o_ref, bon_ref, bg_ref, ya_ref, x_ref, lg_ref, lb_ref, wo_ref, g_ref, b_ref, wi_ref, x1_ref, *refs,
              tm, sub, gw, bw, dils, alpha, scale):
    ng = len(dils)
    cls_refs = refs[0:3 * ng]
    gate_ref, xb_scr, p_scr = refs[3 * ng:3 * ng + 3]
    i = pl.program_id(0)
    slot = i % 2

    @pl.when(i == 0)
    def _():
        xb_scr[1] = jnp.zeros(xb_scr.shape[1:], BF16)

    xb = xb_scr[1 - slot]
    hsum = _head_ones()
    nsub = tm // sub
    nblk = 3 * ng + 1
    for blk in range(nblk):
        if blk < 3 * ng:
            g = blk % ng
            _project_class_block(xb, wi_ref, blk, cls_refs[blk], p_scr, tm=tm, gw=gw, d=dils[g],
                                 scale=scale if blk < ng else None)
        else:
            gate_ref[...] = _silu(jnp.dot(xb, wi_ref[:, blk * gw:(blk + 1) * gw], preferred_element_type=F32))
        for s in range(nsub):
            if s * nblk // nsub != blk:
                continue
            rows = slice(s * sub, (s + 1) * sub)
            x1 = _even_back_rows(o_ref[rows, :], bon_ref[rows, :], bg_ref[rows, :], ya_ref[rows, :], x_ref[rows, :],
                                 lg_ref[...], lb_ref[...], wo_ref, g_ref[...], b_ref[...], hsum, alpha=alpha, bw=bw)
            x1_ref[rows, :] = x1
            xb_scr[slot, rows, :] = x1.astype(BF16)


def _mid(o, bonus, bg, ya, x, lnx_g, lnx_b, w_out, ln_g, ln_b, w_in, *, alpha, tm, dils, gw):
    t, d_model = x.shape
    bw = o.shape[-1]
    ng = len(dils)
    nt = t // tm
    cur = lambda i: (jnp.minimum(i, nt - 1), 0)
    prev = lambda i: (jnp.maximum(i - 1, 0), 0)
    fix = lambda i: (0, 0)
    half = pl.BlockSpec((tm, bw), cur)
    once = lambda shape: pl.BlockSpec(shape, fix, pipeline_mode=pl.Buffered(1))
    in_specs = [half] * 4 + [pl.BlockSpec((tm, d_model), cur), once((1, bw)), once((1, bw)), once((2 * bw, d_model)),
                             once((1, d_model)), once((1, d_model)), once(w_in.shape)]
    cls_specs = [pl.BlockSpec((d, tm // d, gw), lambda i: (0, jnp.maximum(i - 1, 0), 0)) for d in dils]
    cls_shapes = [jax.ShapeDtypeStruct((d, t // d, gw), BF16) for d in dils]
    outs = [jax.ShapeDtypeStruct((t, d_model), F32)] + cls_shapes * 3 + [jax.ShapeDtypeStruct((t, gw), F32)]
    out_specs = [pl.BlockSpec((tm, d_model), cur)] + cls_specs * 3 + [pl.BlockSpec((tm, gw), prev)]
    body = functools.partial(_mid_body, tm=tm, sub=min(tm, MID_SUB), gw=gw, bw=bw, dils=dils, alpha=alpha,
                             scale=HEAD ** -0.5)
    res = pl.pallas_call(
        body, out_shape=outs, grid=(nt + 1,), in_specs=in_specs, out_specs=out_specs,
        scratch_shapes=[pltpu.VMEM((2, tm, d_model), BF16), pltpu.VMEM((2 * gw // 128, tm, 128), F32)],
        compiler_params=_cparams("arbitrary"), name="mid")(o, bonus, bg, ya, x, lnx_g, lnx_b, w_out, ln_g, ln_b, w_in)
    return res[0], res[1:1 + ng], res[1 + ng:1 + 2 * ng], res[1 + 2 * ng:1 + 3 * ng], res[1 + 3 * ng]


def _kv_rows_body(x_ref, wt_ref, kv_ref):
    kv_ref[...] = lax.dot_general(wt_ref[...], x_ref[...].astype(BF16), (((1,), (1,)), ((), ())),
                                  preferred_element_type=F32)


def _kv_rows_t(x, w_kv_t, *, tm, first_row):
    t = x.shape[0] - first_row
    d_model = x.shape[1]
    ncol = w_kv_t.shape[0]
    return pl.pallas_call(
        _kv_rows_body, out_shape=jax.ShapeDtypeStruct((ncol, t), F32), grid=(t // tm,),
        in_specs=[pl.BlockSpec((tm, d_model), lambda i: (i + first_row // tm, 0)),
                  pl.BlockSpec((ncol, d_model), lambda i: (0, 0))],
        out_specs=pl.BlockSpec((ncol, tm), lambda i: (0, i)),
        compiler_params=_cparams("parallel"), name="kv_rows")(x, w_kv_t)


def _band_attn_body(q_ref, kc_ref, kp_ref, vc_ref, vp_ref, o_ref, lse_ref, *, bq, span, nb_class):
    j = pl.program_id(0)
    first_lo = jnp.where(((j * bq) % nb_class) == 0, span, 0) if nb_class >= bq else span
    qi = lax.broadcasted_iota(jnp.int32, (span, 2 * span), 0)
    ki = lax.broadcasted_iota(jnp.int32, (span, 2 * span), 1)
    in_band = (ki >= qi) & (ki <= qi + span)
    left = lax.broadcasted_iota(jnp.int32, (span, 2 * HEAD), 1) < HEAD
    gw = q_ref.shape[-1]
    for b in range(bq):
        mask = in_band & (ki >= first_lo) if b % min(nb_class, bq) == 0 else in_band
        for pr in range(gw // (2 * HEAD)):
            ls = slice(pr * 2 * HEAD, (pr + 1) * 2 * HEAD)
            q2 = q_ref[b * span:(b + 1) * span, ls]
            if b == 0:
                k2 = jnp.concatenate([kp_ref[:, ls], kc_ref[0:span, ls]], axis=0)
                v2 = jnp.concatenate([vp_ref[:, ls], vc_ref[0:span, ls]], axis=0)
            else:
                k2 = kc_ref[(b - 1) * span:(b + 1) * span, ls]
                v2 = vc_ref[(b - 1) * span:(b + 1) * span, ls]
            outs, lses = [], []
            for sel in (left, jnp.logical_not(left)):
                qh = jnp.where(sel, q2, jnp.zeros_like(q2))
                s = lax.dot_general(qh, k2, (((1,), (1,)), ((), ())), preferred_element_type=F32)
                s = jnp.where(mask, s, -jnp.inf)
                m = jnp.max(s, axis=-1, keepdims=True)
                p = jnp.exp(s - m)
                l = jnp.sum(p, axis=-1, keepdims=True)
                outs.append(jnp.dot(p.astype(BF16), v2, preferred_element_type=F32) / l)
                lses.append(m + jnp.log(l))
            o_ref[b * span:(b + 1) * span, ls] = jnp.where(left, outs[0], outs[1])
            lse_ref[b * span:(b + 1) * span, ls] = jnp.where(left, lses[0], lses[1])


def _band_attn(q, k, v, *, span, nb_class, bq=BAND_BLOCKS):
    t, gw = q.shape
    cur = pl.BlockSpec((bq * span, gw), lambda j: (j, 0))
    prev = pl.BlockSpec((span, gw), lambda j: (jnp.maximum(j * bq - 1, 0), 0))
    out = jax.ShapeDtypeStruct((t, gw), F32)
    return pl.pallas_call(
        functools.partial(_band_attn_body, bq=bq, span=span, nb_class=nb_class),
        out_shape=[out, out],
        grid=(t // (bq * span),), in_specs=[cur, cur, prev, cur, prev], out_specs=[cur, cur],
        compiler_params=_cparams("parallel"), name="band_attn")(q, k, k, v, v)


def _cache_attn_group(q_ref, slab, c_ref, o_ref, lse_ref, cout_ref, *, d, window, ns, rchunk):
    gw = q_ref.shape[-1]
    w = c_ref.shape[-1]
    nh = gw // HEAD
    rp = 8
    zrow = jnp.zeros((rp - ns, gw), F32)
    q8 = jnp.concatenate([q_ref[0], zrow], axis=0)
    ncol = slab.shape[1]
    ci = lax.broadcasted_iota(jnp.int32, (ncol, rp), 0)
    si = lax.broadcasted_iota(jnp.int32, (ncol, rp), 1)
    pick = jnp.where((ci == pl.program_id(0) * ns + si) & (si < ns), 1.0, 0.0).astype(BF16)
    new_t = _split3_dot_right(slab, pick)
    kn_t = new_t[0:gw].astype(BF16)
    vn_t = new_t[gw:2 * gw].astype(BF16)
    qm = jnp.concatenate([q8] * nh, axis=0)
    rowh = lax.broadcasted_iota(jnp.int32, (nh * rp, gw), 0) >> 3
    laneh = lax.broadcasted_iota(jnp.int32, (nh * rp, gw), 1) >> 6
    own = rowh == laneh
    qm = jnp.where(own, qm, 0.0).astype(BF16)
    s_c = jnp.dot(qm, c_ref[0, 0:gw, :].astype(BF16), preferred_element_type=F32)
    s_n = jnp.dot(qm, kn_t, preferred_element_type=F32)
    srow = lax.broadcasted_iota(jnp.int32, s_c.shape, 0) & (rp - 1)
    dist = w + srow - lax.broadcasted_iota(jnp.int32, s_c.shape, 1)
    valid_c = (dist <= window) & ((dist & (d - 1)) == 0) & (srow < ns)
    srow_n = lax.broadcasted_iota(jnp.int32, s_n.shape, 0) & (rp - 1)
    col_n = lax.broadcasted_iota(jnp.int32, s_n.shape, 1)
    dist_n = srow_n - col_n
    valid_n = (dist_n >= 0) & ((dist_n & (d - 1)) == 0) & (srow_n < ns)
    valid_n = valid_n | ((srow_n >= ns) & (col_n == 0))
    s_c = jnp.where(valid_c, s_c, -jnp.inf)
    s_n = jnp.where(valid_n, s_n, -jnp.inf)
    m = jnp.maximum(jnp.max(s_c, axis=-1, keepdims=True), jnp.max(s_n, axis=-1, keepdims=True))
    p_c = jnp.exp(s_c - m)
    p_n = jnp.exp(s_n - m)
    l = jnp.sum(p_c, axis=-1, keepdims=True) + jnp.sum(p_n, axis=-1, keepdims=True)
    o = lax.dot_general(p_c.astype(BF16), c_ref[0, gw:2 * gw, :].astype(BF16), (((1,), (1,)), ((), ())),
                        preferred_element_type=F32)
    o = (o + lax.dot_general(p_n.astype(BF16), vn_t, (((1,), (1,)), ((), ())), preferred_element_type=F32)) / l
    lse = jnp.broadcast_to(m + jnp.log(l), o.shape)
    o = jnp.where(own, o, 0.0)
    lse = jnp.where(own, lse, 0.0)
    o8 = o[0:rp]
    l8 = lse[0:rp]
    for h in range(1, nh):
        o8 = o8 + o[h * rp:(h + 1) * rp]
        l8 = l8 + lse[h * rp:(h + 1) * rp]
    o_ref[0] = o8[0:ns]
    lse_ref[0] = l8[0:ns]
    for rb in range(2 * gw // rchunk):
        rows = slice(rb * rchunk, (rb + 1) * rchunk)
        cout_ref[0, rows, :] = jnp.concatenate([c_ref[0, rows, ns:w], new_t[rows, 0:ns]], axis=1)


def _cache_attn_body(*refs, groups, ns):
    ng = len(groups)
    q_refs, nt_ref, c_refs = refs[0:ng], refs[ng], refs[ng + 1:2 * ng + 1]
    outs = refs[2 * ng + 1:]
    for g, (window, d) in enumerate(groups):
        _cache_attn_group(q_refs[g], nt_ref[g], c_refs[g], outs[3 * g], outs[3 * g + 1], outs[3 * g + 2],
                          d=d, window=window, ns=ns, rchunk=64)


def _cache_attn(qs, new_t, caches_t, *, groups):
    nbat, ns, gw = qs[0].shape
    new = pl.BlockSpec((1, ns, gw), lambda b: (b, 0, 0))
    out = jax.ShapeDtypeStruct((nbat, ns, gw), F32)
    cbs = [pl.BlockSpec((1, 2 * gw, c.shape[2]), lambda b: (b, 0, 0)) for c in caches_t]
    in_specs = [new] * len(groups) + [pl.BlockSpec(new_t.shape, lambda b: (0, 0, 0))] + cbs
    out_shape, out_specs = [], []
    for c, cb in zip(caches_t, cbs):
        out_shape += [out, out, jax.ShapeDtypeStruct(c.shape, F32)]
        out_specs += [new, new, cb]
    res = pl.pallas_call(
        functools.partial(_cache_attn_body, groups=groups, ns=ns),
        out_shape=out_shape, grid=(nbat,), in_specs=in_specs, out_specs=out_specs,
        compiler_params=_cparams("parallel"), name="cache_attn")(*qs, new_t, *caches_t)
    return [res[3 * g:3 * g + 3] for g in range(len(groups))]


def _odd_back_body(*refs, tm, dils, alpha):
    ng = len(dils)
    o_refs, l_refs = refs[0:ng], refs[ng:2 * ng]
    gate_ref, x_ref, w_ref, g_ref, b_ref, y_ref = refs[2 * ng:2 * ng + 6]
    scr = refs[2 * ng + 6:]
    os_, ls_ = [], []
    for g, d in enumerate(dils):
        if d == 1:
            os_.append(o_refs[g][0])
            ls_.append(l_refs[g][0])
        else:
            n = tm // d
            nlb, _, lt = scr[2 * g].shape
            for c in range(d):
                for lb in range(nlb):
                    scr[2 * g][lb, pl.ds(c, n, stride=d), :] = o_refs[g][c, :, lb * lt:(lb + 1) * lt]
                    scr[2 * g + 1][lb, pl.ds(c, n, stride=d), :] = l_refs[g][c, :, lb * lt:(lb + 1) * lt]
            os_.append(jnp.concatenate([scr[2 * g][lb] for lb in range(nlb)], axis=1))
            ls_.append(jnp.concatenate([scr[2 * g + 1][lb] for lb in range(nlb)], axis=1))
    m = ls_[0]
    for l in ls_[1:]:
        m = jnp.maximum(m, l)
    es = [jnp.exp(l - m) for l in ls_]
    den = es[0]
    for e in es[1:]:
        den = den + e
    o = es[0] * os_[0]
    for e, og in zip(es[1:], os_[1:]):
        o = o + e * og
    o = o / den
    y = _dot(o * gate_ref[...], w_ref[...])
    y_ref[...] = _layer_norm(alpha * x_ref[...] + y, g_ref[...], b_ref[...], LN_EPS)


def _odd_back(os_, ls_, gate, x, w_out, ln_g, ln_b, *, dils, alpha, tm):
    t, d_model = x.shape
    gw = gate.shape[-1]
    row = lambda i: (i, 0)
    fix = lambda i: (0, 0)
    cls_specs = [pl.BlockSpec((d, tm // d, gw), lambda i: (0, i, 0)) for d in dils]
    in_specs = cls_specs * 2 + [pl.BlockSpec((tm, gw), row), pl.BlockSpec((tm, d_model), row),
                                pl.BlockSpec((gw, d_model), fix), pl.BlockSpec((1, d_model), fix),
                                pl.BlockSpec((1, d_model), fix)]
    return pl.pallas_call(
        functools.partial(_odd_back_body, tm=tm, dils=dils, alpha=alpha),
        out_shape=jax.ShapeDtypeStruct((t, d_model), F32), grid=(t // tm,), in_specs=in_specs,
        out_specs=pl.BlockSpec((tm, d_model), row),
        scratch_shapes=[pltpu.VMEM((gw // 128, tm, 128), F32)] * (2 * len(dils)),
        compiler_params=_cparams("parallel"), name="odd_back")(*os_, *ls_, gate, x, w_out, ln_g, ln_b)


def kernel(x_prompt, x_sample, state_conv, state_shift, state_wkv, cache_kv_w128, cache_kv_w512, cache_kv_w2048,
           w_in_even, conv_w, conv_b, conv_ln_g, conv_ln_b, mu_shift, w0, w2, a0, a2, k_k, k_a, r_k, lnx_g, lnx_b,
           w_out_even, w_in_odd, w_out_odd, ln_g, ln_b):
    nbp, seq, d_model = x_prompt.shape
    nbs, ns, _ = x_sample.shape
    assert nbp == 1
    depth = ln_g.shape[0]
    assert depth == 2 and w_in_even.shape[0] == 1 and w_in_odd.shape[0] == 1
    alpha = (2.0 * depth) ** 0.25
    bw = w0.shape[-1]
    lora_d, lora_a = w2.shape[1], a2.shape[1]
    nh = bw // HEAD
    nhist = CONV_W - 1
    caches = (cache_kv_w128, cache_kv_w512, cache_kv_w2048)
    dils = tuple(d for _, d in C_GROUPS)
    gw = w_out_odd.shape[1]
    ng = len(C_GROUPS)

    wi = w_in_even[0]
    sh_cols = 3 * bw + lora_d + lora_a
    zpad = jnp.zeros((d_model, LORA_PAD - lora_d - lora_a), F32)
    w_re = jnp.concatenate([wi[:, :sh_cols], zpad, wi[:, sh_cols:]], axis=1).astype(BF16)
    mu_re = jnp.concatenate([mu_shift[0], jnp.zeros((LORA_PAD - lora_d - lora_a,), F32)])[None, :]
    w2p = jnp.zeros((LORA_PAD, bw), F32).at[0:lora_d].set(w2[0])
    a2p = jnp.zeros((LORA_PAD, bw), F32).at[lora_d:lora_d + lora_a].set(a2[0])
    cw = jnp.repeat(conv_w[0], 8, axis=0)
    vec = lambda z: z.reshape(1, -1)
    conv_small = (vec(conv_b[0]), vec(conv_ln_g[0]), vec(conv_ln_b[0]))
    mix_small = (vec(w0[0]), w2p, vec(a0[0]), a2p, vec(k_k[0]), vec(k_a[0]), vec(r_k[0]))
    w_out_e = w_out_even[0].astype(BF16)
    w_in_o = w_in_odd[0].astype(BF16)
    w_out_o = w_out_odd[0].astype(BF16)
    lng = [vec(ln_g[l]) for l in range(depth)]
    lnb = [vec(ln_b[l]) for l in range(depth)]

    xp = x_prompt[0]
    ya, bg, r, k, v, lora, conv_tail = _even_front_prompt(xp, w_re, mu_re, cw, conv_small)
    o_wkv, bonus, wkv_p = _wkv_prompt(r, k, v, lora, mix_small, jnp.zeros((nh, HEAD, HEAD), F32))
    xp1, qc, kc, vc, gate_p = _mid(o_wkv, bonus, bg, ya, xp, vec(lnx_g[0]), vec(lnx_b[0]), w_out_e, lng[0], lnb[0],
                                   w_in_o, alpha=alpha, tm=TM_MID, dils=dils, gw=gw)
    conv_p = conv_tail[CONV_HIST - nhist:][None, None]
    shift_p = x_prompt[:, -1][None]

    xs_t = jnp.transpose(x_sample, (1, 0, 2)).reshape(ns * nbs, d_model)
    x_in = jnp.concatenate([state_shift[0], xs_t], axis=0)
    hist = jnp.transpose(state_conv[0], (1, 0, 2)).reshape(nhist * nbs, bw)
    res = _even_front_sample(x_in, hist, w_re, mu_re, cw, conv_small, nb=nbs, ns=ns)
    ya_s, bg_s, conv_s_t = res[0], res[1], res[6]
    blk = 8

    def seq_major_pad(z):
        z = jnp.transpose(z.reshape(ns, nbs, z.shape[-1]), (1, 0, 2))
        return jnp.pad(z, ((0, 0), (0, blk - ns), (0, 0))).reshape(nbs * blk, z.shape[-1])

    def step_major(z):
        return jnp.transpose(z.reshape(nbs, blk, z.shape[-1])[:, :ns], (1, 0, 2)).reshape(ns * nbs, z.shape[-1])

    wkv_in = [seq_major_pad(z) for z in res[2:6]]
    o_s_pad, bon_s_pad, wkv_s = _wkv_sample(*wkv_in, mix_small, state_wkv[0], blk=blk, ns=ns)
    o_s, bon_s = step_major(o_s_pad), step_major(bon_s_pad)
    ones = (1,) * ng
    xs1_t, qs_t, _, _, gate_s_t = _mid(o_s, bon_s, bg_s, ya_s, xs_t, vec(lnx_g[0]), vec(lnx_b[0]), w_out_e, lng[0],
                                       lnb[0], w_in_o, alpha=alpha, tm=ns * nbs, dils=ones, gw=gw)
    conv_s = jnp.transpose(conv_s_t.reshape(nhist, nbs, bw), (1, 0, 2))[None]
    shift_s = x_sample[:, -1][None]

    def seq_major(z):
        return jnp.transpose(z.reshape(ns, nbs, z.shape[-1]), (1, 0, 2)).reshape(nbs * ns, z.shape[-1])

    xs1 = seq_major(xs1_t)
    gate_s = seq_major(gate_s_t)
    qs = [seq_major(q[0]) for q in qs_t]

    tail = min(max(win for win, _ in C_GROUPS), seq)
    w_kv_t = jnp.concatenate([w_in_odd[0][:, (part * ng + g) * gw:(part * ng + g + 1) * gw].T
                              for g in range(ng) for part in (1, 2)], axis=0).astype(BF16)
    kvt = _kv_rows_t(xp1, w_kv_t, tm=TM_KV, first_row=seq - tail).reshape(ng, 2, gw // HEAD, HEAD, tail)
    os_, ls_ = [], []
    for g, (win, d) in enumerate(C_GROUPS):
        span = win // d
        flat = lambda z: z.reshape(seq, gw)
        o_g, l_g = _band_attn(flat(qc[g]), flat(kc[g]), flat(vc[g]), span=span, nb_class=(seq // d) // span)
        os_.append(o_g.reshape(d, seq // d, gw))
        ls_.append(l_g.reshape(d, seq // d, l_g.shape[-1]))
    xp2 = _odd_back(os_, ls_, gate_p, xp1, w_out_o, lng[1], lnb[1], dils=dils, alpha=alpha, tm=TM_BACK)
    kv_p = []
    for g, (win, d) in enumerate(C_GROUPS):
        n = min(win, seq)
        kv_p.append(jnp.transpose(kvt[g, :, :, :, tail - n:], (3, 0, 1, 2))[None, None])

    kvt_s = _kv_rows_t(xs1, w_kv_t, tm=nbs * ns, first_row=0).reshape(ng, 2 * gw, nbs * ns)
    caches_t = [jnp.transpose(c[0].reshape(nbs, c.shape[2], 2 * gw), (0, 2, 1)) for c in caches]
    q_gs = [q.astype(F32).reshape(nbs, ns, gw) for q in qs]
    os_s, ls_s, kv_s_out = [], [], []
    for o_g, l_g, cnew_t in _cache_attn(q_gs, kvt_s, caches_t, groups=C_GROUPS):
        os_s.append(o_g.reshape(1, nbs * ns, gw))
        ls_s.append(l_g.reshape(1, nbs * ns, l_g.shape[-1]))
        kv_s_out.append(jnp.transpose(cnew_t, (0, 2, 1)).reshape(nbs, cnew_t.shape[2], 2, gw // HEAD, HEAD)[None])
    xs2 = _odd_back(os_s, ls_s, gate_s, xs1, w_out_o, lng[1], lnb[1], dils=ones, alpha=alpha, tm=nbs * ns)

    return (xp2[None], xs2.reshape(nbs, ns, d_model), conv_p, conv_s, shift_p, shift_s,
            wkv_p[None, None], wkv_s[None],
            kv_p[0], kv_s_out[0], kv_p[1], kv_s_out[1], kv_p[2], kv_s_out[2])
```
